```python
import math
import jax, jax.numpy as jnp
from jax import lax
import numpy as np

D_MODEL = 2048
BATCH = 2
SEQ = 4096
DEPTH = 1
DEC_BATCH = 32
DEC_SEQ = 8
PAST_LEN = 8192
PAGE_SIZE = 128

HG_HEADS = 8
HG_KDIM = 128
HG_VDIM = 128
HG_KWIDTH = HG_HEADS * HG_KDIM
HG_WIDTH = HG_HEADS * HG_VDIM
HG_CHUNK = 64
ATT_HEAD_DIM = 128
DILATION_PAIRS = ((128, 1), (512, 4), (2048, 16))
N_GROUPS = len(DILATION_PAIRS)
ATT_KV_HEADS = 4
ATT_Q_HEADS = N_GROUPS * ATT_KV_HEADS
ATT_MAX_WINDOW = max(w for w, _ in DILATION_PAIRS)
ATT_Q_BLOCK = 128
ATT_OUT_WIDTH = ATT_KV_HEADS * ATT_HEAD_DIM
ATT_SCALE = ATT_HEAD_DIM ** -0.5
N_EXPERTS = 32
TOP_K = 4
D_FF = D_MODEL
SWIGLU_ALPHA = 1.702
SWIGLU_LIMIT = 7.0
MOE_BLOCK = 128
LN_EPS = 1e-5
RMS_EPS = 1e-5
DEEPNORM_ALPHA = (2 * DEPTH) ** 0.25
DEEPNORM_BETA = (8 * DEPTH) ** -0.25

IN_SPLITS = (
    ("hg_q", HG_KWIDTH), ("hg_f", HG_KWIDTH), ("hg_i", HG_WIDTH), ("hg_g", HG_WIDTH),
    ("att_q", ATT_Q_HEADS * ATT_HEAD_DIM), ("att_k", ATT_KV_HEADS * ATT_HEAD_DIM),
    ("att_v", ATT_KV_HEADS * ATT_HEAD_DIM), ("gate_a", D_MODEL), ("gate_b", D_MODEL),
)
IN_WIDTH = sum(w for _, w in IN_SPLITS)
SPLIT_POINTS = tuple(sum(w for _, w in IN_SPLITS[:i + 1]) for i in range(len(IN_SPLITS) - 1))
BETA_SCALED = ("hg_i", "att_v")

kernel_name = "hgrn2_dilated_attn_moe_decoder_step"


def layer_norm(x, g, b):
    xf = x.astype(jnp.float32)
    mu = jnp.mean(xf, axis=-1, keepdims=True)
    var = jnp.mean(jnp.square(xf - mu), axis=-1, keepdims=True)
    return ((xf - mu) * lax.rsqrt(var + LN_EPS) * g.astype(jnp.float32) + b.astype(jnp.float32)).astype(x.dtype)


def rms_norm(x, w):
    xf = x.astype(jnp.float32)
    return xf * lax.rsqrt(jnp.mean(jnp.square(xf), axis=-1, keepdims=True) + RMS_EPS) * w.astype(jnp.float32)


def alibi_slopes():
    n = jnp.arange(1, ATT_Q_HEADS + 1, dtype=jnp.float32)
    return jnp.power(2.0, -8.0 * n / ATT_Q_HEADS).reshape(N_GROUPS, ATT_KV_HEADS)


def hgrn2_recurrence(q, k, v, logf, s0):
    B, L, H, K = q.shape
    V = v.shape[-1]
    C = math.gcd(L, HG_CHUNK)
    n = L // C

    def chunks(a):
        return a.astype(jnp.float32).reshape(B, n, C, H, a.shape[-1]).transpose(1, 0, 3, 2, 4)

    causal = jnp.tril(jnp.ones((C, C), dtype=bool))[None, None, :, :, None]

    def step(S, inp):
        qc, kc, vc, gc = inp
        G = jnp.cumsum(gc, axis=2)
        o_inter = jnp.einsum('bhtk,bhkv->bhtv', qc * jnp.exp(G), S)
        diff = G[:, :, :, None, :] - G[:, :, None, :, :]
        decay = jnp.exp(jnp.where(causal, diff, -jnp.inf))
        scores = jnp.einsum('bhtk,bhsk,bhtsk->bhts', qc, kc, decay)
        o = o_inter + jnp.einsum('bhts,bhsv->bhtv', scores, vc)
        G_last = G[:, :, -1:, :]
        S = jnp.exp(G_last[:, :, 0, :])[..., None] * S + jnp.einsum(
            'bhsk,bhsv->bhkv', kc * jnp.exp(G_last - G), vc)
        return S, o

    S, o = lax.scan(step, s0.astype(jnp.float32), (chunks(q), chunks(k), chunks(v), chunks(logf)))
    o = o.transpose(1, 0, 3, 2, 4).reshape(B, L, H, V)
    return o, S


def dilated_window_attention(q, k_ext, v_ext, ctx_len):
    B, Lq = q.shape[:2]
    qb = Lq if Lq <= ATT_Q_BLOCK else math.gcd(Lq, ATT_Q_BLOCK)
    nb = Lq // qb
    slopes = alibi_slopes()
    qblocks = q.reshape(B, nb, qb, N_GROUPS, ATT_KV_HEADS, ATT_HEAD_DIM).transpose(1, 0, 2, 3, 4, 5)

    def block(args):
        b, qblk = args
        q_ext = ctx_len + b * qb + jnp.arange(qb)
        lses, outs = [], []
        for g, (window, dil) in enumerate(DILATION_PAIRS):
            dist = jnp.arange(window // dil + 1) * dil
            idx = q_ext[:, None] - dist[None, :]
            valid = idx >= 0
            idx = jnp.maximum(idx, 0)
            kg = k_ext[:, idx]
            vg = v_ext[:, idx]
            s = jnp.einsum('bqhd,bqjhd->bhqj', qblk[:, :, g], kg).astype(jnp.float32) * ATT_SCALE
            s = s - slopes[g][None, :, None, None] * dist.astype(jnp.float32)
            s = jnp.where(valid[None, None], s, -jnp.inf)
            lse = jax.nn.logsumexp(s, axis=-1)
            p = jnp.exp(s - lse[..., None]).astype(vg.dtype)
            outs.append(jnp.einsum('bhqj,bqjhd->bqhd', p, vg))
            lses.append(lse)
        mix = jax.nn.softmax(jnp.stack(lses), axis=0).astype(q.dtype)
        return jnp.einsum('gbhq,gbqhd->bqhd', mix, jnp.stack(outs))

    o = lax.map(block, (jnp.arange(nb), qblocks))
    return o.transpose(1, 0, 2, 3, 4).reshape(B, Lq, ATT_KV_HEADS, ATT_HEAD_DIM)


def clamped_swiglu(h):
    x_glu = jnp.minimum(h[..., ::2], SWIGLU_LIMIT)
    x_lin = jnp.clip(h[..., 1::2], -SWIGLU_LIMIT, SWIGLU_LIMIT)
    return x_glu * jax.nn.sigmoid(SWIGLU_ALPHA * x_glu) * (x_lin + 1.0)


def moe_ffn(x2d, w_router, b_router, w_gate_up, b_gate_up, w_down, b_down):
    T, D = x2d.shape
    logits = jnp.dot(x2d, w_router).astype(jnp.float32) + b_router.astype(jnp.float32)
    top_v, top_e = lax.top_k(logits, TOP_K)
    gate = jax.nn.softmax(top_v, axis=-1)
    n_assign = T * TOP_K
    e_flat = top_e.reshape(-1)
    order = jnp.argsort(e_flat)
    e_sorted = e_flat[order]
    tok_sorted = (order // TOP_K).astype(jnp.int32)
    gate_sorted = gate.reshape(-1)[order]
    counts = jnp.bincount(e_flat, length=N_EXPERTS)
    start = jnp.cumsum(counts) - counts
    padded = (counts + MOE_BLOCK - 1) // MOE_BLOCK * MOE_BLOCK
    pend = jnp.cumsum(padded)
    pstart = pend - padded
    slot = pstart[e_sorted] + jnp.arange(n_assign) - start[e_sorted]
    n_blocks = -(-n_assign // MOE_BLOCK) + N_EXPERTS
    P = n_blocks * MOE_BLOCK
    tok_pad = jnp.zeros((P,), jnp.int32).at[slot].set(tok_sorted)
    gate_pad = jnp.zeros((P,), jnp.float32).at[slot].set(gate_sorted)
    blk_expert = jnp.minimum(
        jnp.searchsorted(pend, jnp.arange(n_blocks) * MOE_BLOCK, side='right'), N_EXPERTS - 1)

    def expert_block(args):
        tok, g, e = args
        h = jnp.dot(x2d[tok], w_gate_up[e]) + b_gate_up[e]
        y = jnp.dot(clamped_swiglu(h), w_down[e]) + b_down[e]
        return y.astype(jnp.float32) * g[:, None]

    y = lax.map(expert_block, (tok_pad.reshape(n_blocks, MOE_BLOCK),
                               gate_pad.reshape(n_blocks, MOE_BLOCK), blk_expert))
    out = jax.ops.segment_sum(y.reshape(P, D), tok_pad, num_segments=T)
    return out.astype(x2d.dtype)


def decoder_layer(x, k_ctx, v_ctx, s0, lb, w_in, hg_norm_w, w_branch_a, w_branch_b, w_out,
                  ln1_g, ln1_b, w_router, b_router, w_gate_up, b_gate_up, w_down, b_down,
                  ln2_g, ln2_b):
    B, L, _ = x.shape
    hq, hf, hi, hg, aq, ak, av, ga, gb = jnp.split(jnp.dot(x, w_in), list(SPLIT_POINTS), axis=-1)
    q = jax.nn.silu(hq).reshape(B, L, HG_HEADS, HG_KDIM)
    f = lb + (1.0 - lb) * jax.nn.sigmoid(hf.astype(jnp.float32).reshape(B, L, HG_HEADS, HG_KDIM))
    o_a, s_new = hgrn2_recurrence(q, 1.0 - f, hi.reshape(B, L, HG_HEADS, HG_VDIM), jnp.log(f), s0)
    o_a = (rms_norm(o_a, hg_norm_w) * jax.nn.silu(hg.reshape(B, L, HG_HEADS, HG_VDIM).astype(jnp.float32)))
    o_a = o_a.astype(x.dtype).reshape(B, L, HG_WIDTH)
    k_b = ak.reshape(B, L, ATT_KV_HEADS, ATT_HEAD_DIM)
    v_b = av.reshape(B, L, ATT_KV_HEADS, ATT_HEAD_DIM)
    k_ext = jnp.concatenate([k_ctx.astype(k_b.dtype), k_b], axis=1)
    v_ext = jnp.concatenate([v_ctx.astype(v_b.dtype), v_b], axis=1)
    o_b = dilated_window_attention(aq.reshape(B, L, ATT_Q_HEADS, ATT_HEAD_DIM), k_ext, v_ext,
                                   k_ctx.shape[1]).reshape(B, L, ATT_OUT_WIDTH)
    merged = jax.nn.sigmoid(ga) * jnp.dot(o_a, w_branch_a) + jax.nn.sigmoid(gb) * jnp.dot(o_b, w_branch_b)
    h = layer_norm(DEEPNORM_ALPHA * x + jnp.dot(merged, w_out), ln1_g, ln1_b)
    ffn = moe_ffn(h.reshape(B * L, D_MODEL), w_router, b_router, w_gate_up, b_gate_up,
                  w_down, b_down).reshape(B, L, D_MODEL)
    y = layer_norm(DEEPNORM_ALPHA * h + ffn, ln2_g, ln2_b)
    return y, k_b, v_b, s_new


def setup_inputs(seed: int = 0) -> dict:
    key = jax.random.key(seed)
    ks = jax.random.split(key, 21)
    f32 = jnp.float32

    def nrm(k, shape, scale=1.0):
        return jax.random.normal(k, shape, f32) * scale

    att_cache = min(ATT_MAX_WINDOW, PAST_LEN)
    col_scale = jnp.concatenate([
        jnp.full((w,), DEEPNORM_BETA if name in BETA_SCALED else 1.0, f32) for name, w in IN_SPLITS])
    return {
        "x_prompt": nrm(ks[0], (BATCH, SEQ, D_MODEL)),
        "x_sample": nrm(ks[1], (DEC_BATCH, DEC_SEQ, D_MODEL)),
        "cache_attn_k": nrm(ks[2], (DEPTH, DEC_BATCH, att_cache, ATT_KV_HEADS, ATT_HEAD_DIM)),
        "cache_attn_v": nrm(ks[3], (DEPTH, DEC_BATCH, att_cache, ATT_KV_HEADS, ATT_HEAD_DIM)),
        "state_hgrn": nrm(ks[4], (DEPTH, DEC_BATCH, HG_HEADS, HG_KDIM, HG_VDIM), 0.5),
        "w_in": nrm(ks[5], (DEPTH, D_MODEL, IN_WIDTH), D_MODEL ** -0.5) * col_scale,
        "hgrn_lb_logits": nrm(ks[6], (DEPTH + 1, HG_KWIDTH), 0.5),
        "hgrn_norm_w": 1.0 + nrm(ks[7], (DEPTH, HG_VDIM), 0.02),
        "w_branch_a": nrm(ks[8], (DEPTH, HG_WIDTH, D_MODEL), HG_WIDTH ** -0.5 * DEEPNORM_BETA),
        "w_branch_b": nrm(ks[9], (DEPTH, ATT_OUT_WIDTH, D_MODEL), ATT_OUT_WIDTH ** -0.5 * DEEPNORM_BETA),
        "w_out": nrm(ks[10], (DEPTH, D_MODEL, D_MODEL), D_MODEL ** -0.5 * DEEPNORM_BETA),
        "ln1_g": 1.0 + nrm(ks[11], (DEPTH, D_MODEL), 0.02),
        "ln1_b": nrm(ks[12], (DEPTH, D_MODEL), 0.02),
        "w_router": nrm(ks[13], (DEPTH, D_MODEL, N_EXPERTS), D_MODEL ** -0.5),
        "b_router": nrm(ks[14], (DEPTH, N_EXPERTS), 0.01),
        "w_gate_up": nrm(ks[15], (DEPTH, N_EXPERTS, D_MODEL, 2 * D_FF), D_MODEL ** -0.5 * DEEPNORM_BETA),
        "b_gate_up": nrm(ks[16], (DEPTH, N_EXPERTS, 2 * D_FF), 0.01),
        "w_down": nrm(ks[17], (DEPTH, N_EXPERTS, D_FF, D_MODEL), D_FF ** -0.5 * DEEPNORM_BETA),
        "b_down": nrm(ks[18], (DEPTH, N_EXPERTS, D_MODEL), 0.01),
        "ln2_g": 1.0 + nrm(ks[19], (DEPTH, D_MODEL), 0.02),
        "ln2_b": nrm(ks[20], (DEPTH, D_MODEL), 0.02),
    }


def reference(x_prompt, x_sample, cache_attn_k, cache_attn_v, state_hgrn, w_in, hgrn_lb_logits,
              hgrn_norm_w, w_branch_a, w_branch_b, w_out, ln1_g, ln1_b, w_router, b_router,
              w_gate_up, b_gate_up, w_down, b_down, ln2_g, ln2_b):
    lower_bounds = jnp.cumsum(jax.nn.softmax(hgrn_lb_logits.astype(jnp.float32), axis=0), axis=0)
    Bp = x_prompt.shape[0]
    keep = min(ATT_MAX_WINDOW, x_prompt.shape[1])
    empty_kv = jnp.zeros((Bp, 0, ATT_KV_HEADS, ATT_HEAD_DIM), x_prompt.dtype)
    zero_state = jnp.zeros((Bp, HG_HEADS, HG_KDIM, HG_VDIM), jnp.float32)
    hp, hs = x_prompt, x_sample
    pk, pv, ps, sk, sv, ss = [], [], [], [], [], []
    for l in range(DEPTH):
        lb = lower_bounds[l].reshape(HG_HEADS, HG_KDIM)
        params = (w_in[l], hgrn_norm_w[l], w_branch_a[l], w_branch_b[l], w_out[l], ln1_g[l], ln1_b[l],
                  w_router[l], b_router[l], w_gate_up[l], b_gate_up[l], w_down[l], b_down[l],
                  ln2_g[l], ln2_b[l])
        hp, kp_l, vp_l, sp_l = decoder_layer(hp, empty_kv, empty_kv, zero_state, lb, *params)
        hs, ks_l, vs_l, ss_l = decoder_layer(hs, cache_attn_k[l], cache_attn_v[l], state_hgrn[l], lb, *params)
        pk.append(kp_l[:, -keep:])
        pv.append(vp_l[:, -keep:])
        ps.append(sp_l)
        sk.append(ks_l)
        sv.append(vs_l)
        ss.append(ss_l)
    return (hp, hs, jnp.stack(pk), jnp.stack(pv), jnp.stack(ps), jnp.stack(sk), jnp.stack(sv), jnp.stack(ss))
```

```python
import functools
import math

import jax
import jax.numpy as jnp
from jax import lax
from jax.experimental import pallas as pl
from jax.experimental.pallas import tpu as pltpu

F32 = jnp.float32
BF16 = jnp.bfloat16
I32 = jnp.int32

D_MODEL = 2048
DEPTH = 1
HG_HEADS = 8
HG_DIM = 128
HG_WIDTH = HG_HEADS * HG_DIM
HG_CHUNK = 64
ATT_HEAD_DIM = 128
DILATION_PAIRS = ((128, 1), (512, 4), (2048, 16))
N_GROUPS = len(DILATION_PAIRS)
ATT_KV_HEADS = 4
ATT_Q_HEADS = N_GROUPS * ATT_KV_HEADS
ATT_TAPS = 128
ATT_KV_WIDTH = ATT_KV_HEADS * ATT_HEAD_DIM
ATT_SCALE = ATT_HEAD_DIM ** -0.5
N_EXPERTS = 32
TOP_K = 4
D_FF = D_MODEL
SWIGLU_ALPHA = 1.702
SWIGLU_LIMIT = 7.0
LN_EPS = 1e-5
RMS_EPS = 1e-5
DEEPNORM_ALPHA = (2 * DEPTH) ** 0.25

P1_WIDTH = 4 * HG_WIDTH
P2_WIDTH = (ATT_Q_HEADS + 2 * ATT_KV_HEADS) * ATT_HEAD_DIM
P3_WIDTH = 2 * D_MODEL

LANES = 128
SUBLANES = 8
VMEM_LIMIT = 56 * 1024 * 1024

TOKEN_TILE = 256
MOE_BLOCK = 256
MOE_FT = 512
NEG = -1e30

_NT = (((1,), (1,)), ((), ()))
_TN = (((0,), (0,)), ((), ()))


def _cparams(sem):
    return pltpu.CompilerParams(dimension_semantics=sem, vmem_limit_bytes=VMEM_LIMIT)


def _sigmoid(x):
    return 1.0 / (1.0 + jnp.exp(-x))


def _mm_kernel(x_ref, w_ref, o_ref):
    o_ref[...] = jnp.dot(x_ref[...], w_ref[...], preferred_element_type=F32).astype(o_ref.dtype)


def _matmul(x, w, tm, tn, out_dtype, name):
    m, k = x.shape
    n = w.shape[1]
    assert m % tm == 0 and n % tn == 0
    return pl.pallas_call(
        _mm_kernel,
        grid=(n // tn, m // tm),
        in_specs=[pl.BlockSpec((tm, k), lambda j, i: (i, 0)),
                  pl.BlockSpec((k, tn), lambda j, i: (0, j))],
        out_specs=pl.BlockSpec((tm, tn), lambda j, i: (i, j)),
        out_shape=jax.ShapeDtypeStruct((m, n), out_dtype),
        compiler_params=_cparams(("parallel", "parallel")),
        name=name,
    )(x, w)


def _level_ref(g_scr, m, sl, chunk, sub):
    pieces = []
    for j in range(chunk // SUBLANES):
        if 2 * m >= SUBLANES:
            row = ((SUBLANES * j) // (2 * m)) * (2 * m) + m - 1
            pieces.append(jnp.broadcast_to(g_scr[row:row + 1, sl], (SUBLANES, LANES)))
        else:
            acc = None
            for u in range(SUBLANES // (2 * m)):
                row = SUBLANES * j + 2 * m * u + m - 1
                b = jnp.broadcast_to(g_scr[row:row + 1, sl], (SUBLANES, LANES))
                acc = b if acc is None else jnp.where(sub // (2 * m) == u, b, acc)
            pieces.append(acc)
    return pieces[0] if len(pieces) == 1 else jnp.concatenate(pieces, axis=0)


def _hgrn_kernel(lbl_ref, nw_ref, hq_ref, hf_ref, hi_ref, hg_ref, *rest, chunk, has_s0):
    if has_s0:
        s0_ref, o_ref, sout_ref, st_scr, g_scr = rest
    else:
        o_ref, sout_ref, st_scr, g_scr = rest
    c = pl.program_id(1)
    nc = pl.num_programs(1)

    @pl.when(c == 0)
    def _init():
        for h in range(HG_HEADS):
            if has_s0:
                st_scr[h] = s0_ref[0, h].T
            else:
                st_scr[h] = jnp.zeros((HG_DIM, HG_DIM), F32)

    lbl = lbl_ref[...]
    ex = jnp.exp(lbl - jnp.max(lbl, axis=0, keepdims=True))
    lb = ex[0:1] / jnp.sum(ex, axis=0, keepdims=True)

    f = lb + (1.0 - lb) * _sigmoid(hf_ref[...])
    logf = jnp.log(f)
    kall = 1.0 - f

    r_cc = lax.broadcasted_iota(I32, (chunk, chunk), 0)
    c_cc = lax.broadcasted_iota(I32, (chunk, chunk), 1)
    tri = jnp.where(r_cc >= c_cc, 1.0, 0.0).astype(BF16)
    p0 = logf.astype(BF16)
    r1 = logf - p0.astype(F32)
    p1 = r1.astype(BF16)
    p2 = (r1 - p1.astype(F32)).astype(BF16)
    g_all = (jnp.dot(tri, p0, preferred_element_type=F32)
             + jnp.dot(tri, p1, preferred_element_type=F32)
             + jnp.dot(tri, p2, preferred_element_type=F32))
    g_scr[...] = g_all

    sub = lax.broadcasted_iota(I32, (SUBLANES, LANES), 0)
    row_c = lax.broadcasted_iota(I32, (chunk, LANES), 0)
    nw = nw_ref[...]

    for h in range(HG_HEADS):
        sl = slice(h * HG_DIM, (h + 1) * HG_DIM)
        hq = hq_ref[:, sl]
        q = hq * _sigmoid(hq)
        k = kall[:, sl]
        g = g_all[:, sl]
        v = hi_ref[:, sl]
        vb = v.astype(BF16)

        p = lax.dot_general(q.astype(BF16), k.astype(BF16), _NT, preferred_element_type=F32)
        a = jnp.where(r_cc == c_cc, p, 0.0)
        m = 1
        while m < chunk:
            ref = _level_ref(g_scr, m, sl, chunk, sub)
            e = jnp.exp(-jnp.abs(g - ref))
            odd = ((row_c // m) & 1) == 1
            x = (jnp.where(odd, q, k) * e).astype(BF16)
            p = lax.dot_general(x, x, _NT, preferred_element_type=F32)
            pair = (r_cc // (2 * m)) == (c_cc // (2 * m))
            sel = jnp.where(pair, ((r_cc // m) & 1) - ((c_cc // m) & 1), 0) == 1
            a = jnp.where(sel, p, a)
            m *= 2

        st = st_scr[h]
        o = jnp.dot(a.astype(BF16), vb, preferred_element_type=F32)
        qe = (q * jnp.exp(g)).astype(BF16)
        o = o + lax.dot_general(qe, st.astype(BF16), _NT, preferred_element_type=F32)

        ms = jnp.mean(o * o, axis=-1, keepdims=True)
        hg = hg_ref[:, sl]
        out = o * lax.rsqrt(ms + RMS_EPS) * nw * (hg * _sigmoid(hg))
        o_ref[0, :, sl] = out.astype(o_ref.dtype)

        g_last = g[chunk - 1:chunk, :]
        kd = (k * jnp.exp(g_last - g)).astype(BF16)
        st_scr[h] = st * jnp.exp(g_last) + lax.dot_general(vb, kd, _TN, preferred_element_type=F32)

    @pl.when(c == nc - 1)
    def _fin():
        for h in range(HG_HEADS):
            sout_ref[0, h] = st_scr[h].T


def _hgrn(p1, lb_logits, norm_w, s0, batch, seq, row0):
    chunk = math.gcd(seq, HG_CHUNK)
    nchunk = seq // chunk
    assert row0 % chunk == 0
    rb0 = row0 // chunk

    def col(idx):
        return pl.BlockSpec((chunk, HG_WIDTH), lambda b, c: (rb0 + b * nchunk + c, idx))

    in_specs = [pl.BlockSpec(lb_logits.shape, lambda b, c: (0, 0)),
                pl.BlockSpec((1, HG_DIM), lambda b, c: (0, 0)),
                col(0), col(1), col(2), col(3)]
    args = [lb_logits, norm_w.reshape(1, HG_DIM), p1, p1, p1, p1]
    if s0 is not None:
        in_specs.append(pl.BlockSpec((1, HG_HEADS, HG_DIM, HG_DIM), lambda b, c: (b, 0, 0, 0)))
        args.append(s0)
    return pl.pallas_call(
        functools.partial(_hgrn_kernel, chunk=chunk, has_s0=s0 is not None),
        grid=(batch, nchunk),
        in_specs=in_specs,
        out_specs=[pl.BlockSpec((1, chunk, HG_WIDTH), lambda b, c: (b, c, 0)),
                   pl.BlockSpec((1, HG_HEADS, HG_DIM, HG_DIM), lambda b, c: (b, 0, 0, 0))],
        out_shape=[jax.ShapeDtypeStruct((batch, seq, HG_WIDTH), BF16),
                   jax.ShapeDtypeStruct((batch, HG_HEADS, HG_DIM, HG_DIM), F32)],
        scratch_shapes=[pltpu.VMEM((HG_HEADS, HG_DIM, HG_DIM), F32),
                        pltpu.VMEM((chunk, HG_WIDTH), F32)],
        compiler_params=_cparams(("parallel", "arbitrary")),
        name="hgrn_seq%d" % seq,
    )(*args)


QB = 128


def _attn_prompt_kernel(slopes_ref, q0_ref, q1_ref, q2_ref, k_ref, v_ref, o_ref,
                        og_scr, lse_scr, bias_scr, *, seq):
    h = pl.program_id(1)
    il = lax.broadcasted_iota(I32, (QB, 2 * QB), 0)
    jl = lax.broadcasted_iota(I32, (QB, 2 * QB), 1)
    delta = il + QB - jl
    band = jnp.where(delta >= 0, delta, ATT_TAPS + 1) <= ATT_TAPS
    prev_half = jl < QB
    q_refs = (q0_ref, q1_ref, q2_ref)

    for g, (window, dil) in enumerate(DILATION_PAIRS):
        assert window // dil == ATT_TAPS
        slope = slopes_ref[g, h]
        bias_scr[g] = jnp.where(band, (-slope * dil) * delta.astype(F32), NEG)
        nbr = (seq // dil) // QB
        q_ref = q_refs[g]

        def rows(start, dil=dil):
            return pl.ds(start, QB) if dil == 1 else pl.ds(start, QB, stride=dil)

        def body(ib, carry, g=g, dil=dil, nbr=nbr, q_ref=q_ref, rows=rows):
            res = ib // nbr
            jb = ib - res * nbr
            qs = res + dil * (jb * QB)
            ps = res + dil * jnp.maximum(jb * QB - QB, 0)
            qv = q_ref[rows(qs), :].astype(BF16)
            kb = jnp.concatenate([k_ref[rows(ps), :], k_ref[rows(qs), :]], axis=0).astype(BF16)
            vb = jnp.concatenate([v_ref[rows(ps), :], v_ref[rows(qs), :]], axis=0).astype(BF16)
            s = lax.dot_general(qv, kb, _NT, preferred_element_type=F32) * ATT_SCALE + bias_scr[g]
            s = jnp.where(jnp.where(prev_half, jb, 1) == 0, NEG, s)
            mx = jnp.max(s, axis=-1, keepdims=True)
            p = jnp.exp(s - mx)
            l = jnp.sum(p, axis=-1, keepdims=True)
            o = jnp.dot(p.astype(BF16), vb, preferred_element_type=F32) / l
            og_scr[g, rows(qs), :] = o
            lse_scr[g, rows(qs), :] = jnp.broadcast_to(mx + jnp.log(l), (QB, LANES))
            return carry

        lax.fori_loop(0, seq // QB, body, 0)

    def merge(tb, carry):
        r = pl.ds(pl.multiple_of(tb * QB, QB), QB)
        l0, l1, l2 = lse_scr[0, r, :], lse_scr[1, r, :], lse_scr[2, r, :]
        mx = jnp.maximum(jnp.maximum(l0, l1), l2)
        w0, w1, w2 = jnp.exp(l0 - mx), jnp.exp(l1 - mx), jnp.exp(l2 - mx)
        o = (w0 * og_scr[0, r, :] + w1 * og_scr[1, r, :] + w2 * og_scr[2, r, :]) / (w0 + w1 + w2)
        o_ref[r, :] = o.astype(o_ref.dtype)
        return carry

    lax.fori_loop(0, seq // QB, merge, 0)


def _attn_prompt(p2, slopes, batch, seq):
    assert seq % (QB * max(d for _, d in DILATION_PAIRS)) == 0

    def col(fn):
        return pl.BlockSpec((seq, ATT_HEAD_DIM), lambda b, h: (b, fn(h)))

    return pl.pallas_call(
        functools.partial(_attn_prompt_kernel, seq=seq),
        grid=(batch, ATT_KV_HEADS),
        in_specs=[pl.BlockSpec(memory_space=pltpu.SMEM),
                  col(lambda h: h), col(lambda h: ATT_KV_HEADS + h), col(lambda h: 2 * ATT_KV_HEADS + h),
                  col(lambda h: ATT_Q_HEADS + h), col(lambda h: ATT_Q_HEADS + ATT_KV_HEADS + h)],
        out_specs=pl.BlockSpec((seq, ATT_HEAD_DIM), lambda b, h: (b, h)),
        out_shape=jax.ShapeDtypeStruct((batch * seq, ATT_KV_WIDTH), BF16),
        scratch_shapes=[pltpu.VMEM((N_GROUPS, seq, ATT_HEAD_DIM), F32),
                        pltpu.VMEM((N_GROUPS, seq, LANES), F32),
                        pltpu.VMEM((N_GROUPS, QB, 2 * QB), F32)],
        compiler_params=_cparams(("parallel", "parallel")),
        name="attn_prompt",
    )(slopes, p2, p2, p2, p2, p2)


def _attn_sample_kernel(slopes_ref, qa_ref, qb_ref, qc_ref, kn_ref, vn_ref, kc_ref, vc_ref, o_ref,
                        *, ctx, nq):
    rows = N_GROUPS * nq
    gi = lax.broadcasted_iota(I32, (rows, 1), 0) // nq
    dil = jnp.where(gi == 0, DILATION_PAIRS[0][1], jnp.where(gi == 1, DILATION_PAIRS[1][1], DILATION_PAIRS[2][1]))
    win = jnp.where(gi == 0, DILATION_PAIRS[0][0], jnp.where(gi == 1, DILATION_PAIRS[1][0], DILATION_PAIRS[2][0]))

    def dist_valid(shape, key0, nkeys):
        r = lax.broadcasted_iota(I32, shape, 0)
        t = r - (r // nq) * nq
        col = lax.broadcasted_iota(I32, shape, 1)
        dist = ctx + t - (col + key0)
        bad = jnp.where(dist >= 0, dist & (dil - 1), 1)
        bad = jnp.where(dist <= win, bad, 1)
        bad = jnp.where(col < nkeys, bad, 1)
        return dist.astype(F32), bad == 0

    dist_c, ok_c = dist_valid((rows, ctx), 0, ctx)
    dist_n, ok_n = dist_valid((rows, LANES), ctx, nq)
    zpad = jnp.zeros((LANES - nq, ATT_HEAD_DIM), F32)

    for h in range(ATT_KV_HEADS):
        hs = slice(h * ATT_HEAD_DIM, (h + 1) * ATT_HEAD_DIM)
        slope = jnp.where(gi == 0, slopes_ref[0, h], jnp.where(gi == 1, slopes_ref[1, h], slopes_ref[2, h]))
        qh = jnp.concatenate([qa_ref[:, hs], qb_ref[:, hs], qc_ref[:, hs]], axis=0).astype(BF16)
        kc = kc_ref[0, :, hs].astype(BF16)
        vc = vc_ref[0, :, hs].astype(BF16)
        kn = jnp.concatenate([kn_ref[:, hs], zpad], axis=0).astype(BF16)
        vn = jnp.concatenate([vn_ref[:, hs], zpad], axis=0).astype(BF16)
        sc = lax.dot_general(qh, kc, _NT, preferred_element_type=F32) * ATT_SCALE - slope * dist_c
        sn = lax.dot_general(qh, kn, _NT, preferred_element_type=F32) * ATT_SCALE - slope * dist_n
        sc = jnp.where(ok_c, sc, NEG)
        sn = jnp.where(ok_n, sn, NEG)
        mx = jnp.maximum(jnp.max(sc, axis=-1, keepdims=True), jnp.max(sn, axis=-1, keepdims=True))
        pc = jnp.exp(sc - mx)
        pn = jnp.exp(sn - mx)
        l = jnp.sum(pc, axis=-1, keepdims=True) + jnp.sum(pn, axis=-1, keepdims=True)
        o = (jnp.dot(pc.astype(BF16), vc, preferred_element_type=F32)
             + jnp.dot(pn.astype(BF16), vn, preferred_element_type=F32)) / l
        lse = mx + jnp.log(l)
        l0, l1, l2 = lse[0:nq], lse[nq:2 * nq], lse[2 * nq:3 * nq]
        m3 = jnp.maximum(jnp.maximum(l0, l1), l2)
        w0, w1, w2 = jnp.exp(l0 - m3), jnp.exp(l1 - m3), jnp.exp(l2 - m3)
        out = (w0 * o[0:nq] + w1 * o[nq:2 * nq] + w2 * o[2 * nq:3 * nq]) / (w0 + w1 + w2)
        o_ref[0, :, hs] = out.astype(o_ref.dtype)


def _attn_sample(p2, cache_k, cache_v, slopes, batch, nq, row0):
    ctx = cache_k.shape[1]
    assert row0 % nq == 0 and nq == SUBLANES and ctx >= max(w for w, _ in DILATION_PAIRS)
    rb0 = row0 // nq

    def col(idx):
        return pl.BlockSpec((nq, ATT_KV_WIDTH), lambda b: (rb0 + b, idx))

    cache_spec = pl.BlockSpec((1, ctx, ATT_KV_WIDTH), lambda b: (b, 0, 0))
    return pl.pallas_call(
        functools.partial(_attn_sample_kernel, ctx=ctx, nq=nq),
        grid=(batch,),
        in_specs=[pl.BlockSpec(memory_space=pltpu.SMEM),
                  col(0), col(1), col(2), col(3), col(4), cache_spec, cache_spec],
        out_specs=pl.BlockSpec((1, nq, ATT_KV_WIDTH), lambda b: (b, 0, 0)),
        out_shape=jax.ShapeDtypeStruct((batch, nq, ATT_KV_WIDTH), BF16),
        compiler_params=_cparams(("parallel",)),
        name="attn_sample",
    )(slopes, p2, p2, p2, p2, p2, cache_k, cache_v)


def _layer_norm(y, g, b):
    mu = jnp.mean(y, axis=-1, keepdims=True)
    yc = y - mu
    var = jnp.mean(yc * yc, axis=-1, keepdims=True)
    return yc * lax.rsqrt(var + LN_EPS) * g + b


def _merge_kernel(x_ref, oa_ref, ob_ref, ga_ref, gb_ref, wa_ref, wb_ref, wo_ref, g1_ref, b1_ref,
                  wr0_ref, wr1_ref, br_ref,
                  h_ref, tope_ref, gate_ref, lrank_ref, cnt_ref):
    tm = x_ref.shape[0]
    a = jnp.dot(oa_ref[...], wa_ref[...], preferred_element_type=F32)
    b = jnp.dot(ob_ref[...], wb_ref[...], preferred_element_type=F32)
    merged = _sigmoid(ga_ref[...]) * a + _sigmoid(gb_ref[...]) * b
    z = jnp.dot(merged.astype(BF16), wo_ref[...], preferred_element_type=F32)
    hh = _layer_norm(DEEPNORM_ALPHA * x_ref[...] + z, g1_ref[...], b1_ref[...])
    h_ref[...] = hh

    h0 = hh.astype(BF16)
    h1 = (hh - h0.astype(F32)).astype(BF16)
    logits = (jnp.dot(h0, wr0_ref[...], preferred_element_type=F32)
              + jnp.dot(h1, wr0_ref[...], preferred_element_type=F32)
              + jnp.dot(h0, wr1_ref[...], preferred_element_type=F32)) + br_ref[...]

    lane = lax.broadcasted_iota(I32, (tm, N_EXPERTS), 1)
    work = logits
    vals, idxs = [], []
    onehot = jnp.zeros((tm, N_EXPERTS), F32)
    for _ in range(TOP_K):
        mx = jnp.max(work, axis=-1, keepdims=True)
        idx = jnp.min(jnp.where(work == mx, lane, N_EXPERTS), axis=-1, keepdims=True)
        hit = lane == idx
        vals.append(mx)
        idxs.append(idx)
        onehot = jnp.where(hit, 1.0, onehot)
        work = jnp.where(hit, -jnp.inf, work)
    ex = [jnp.exp(v - vals[0]) for v in vals]
    den = ex[0] + ex[1] + ex[2] + ex[3]

    r_tt = lax.broadcasted_iota(I32, (tm, tm), 0)
    c_tt = lax.broadcasted_iota(I32, (tm, tm), 1)
    before = jnp.where(r_tt > c_tt, 1.0, 0.0).astype(BF16)
    prefix = jnp.dot(before, onehot.astype(BF16), preferred_element_type=F32)
    for j in range(TOP_K):
        tope_ref[:, j:j + 1] = idxs[j]
        gate_ref[:, j:j + 1] = ex[j] / den
        lrank_ref[:, j:j + 1] = jnp.sum(jnp.where(lane == idxs[j], prefix, 0.0), axis=-1,
                                        keepdims=True).astype(I32)
    cnt_ref[0] = jnp.sum(onehot, axis=0, keepdims=True).astype(I32)


def _merge(xa, oa, ob, p3, wa, wb, wo, g1, b1, wr0, wr1, br):
    t = xa.shape[0]
    tm = TOKEN_TILE
    nt = t // tm
    assert t % tm == 0

    def full(a):
        return pl.BlockSpec(a.shape, lambda i: (0,) * a.ndim)

    def rowblk(width, idx=0):
        return pl.BlockSpec((tm, width), lambda i: (i, idx))

    return pl.pallas_call(
        _merge_kernel,
        grid=(nt,),
        in_specs=[rowblk(D_MODEL), rowblk(HG_WIDTH), rowblk(ATT_KV_WIDTH), rowblk(D_MODEL, 0),
                  rowblk(D_MODEL, 1), full(wa), full(wb), full(wo), full(g1), full(b1),
                  full(wr0), full(wr1), full(br)],
        out_specs=[rowblk(D_MODEL), rowblk(TOP_K), rowblk(TOP_K), rowblk(TOP_K),
                   pl.BlockSpec((1, 1, N_EXPERTS), lambda i: (i, 0, 0))],
        out_shape=[jax.ShapeDtypeStruct((t, D_MODEL), F32),
                   jax.ShapeDtypeStruct((t, TOP_K), I32),
                   jax.ShapeDtypeStruct((t, TOP_K), F32),
                   jax.ShapeDtypeStruct((t, TOP_K), I32),
                   jax.ShapeDtypeStruct((nt, 1, N_EXPERTS), I32)],
        compiler_params=_cparams(("parallel",)),
        name="merge_ln1_router",
    )(xa, oa, ob, p3, p3, wa, wb, wo, g1, b1, wr0, wr1, br)


def _gather_kernel(tok_ref, h_ref, o_ref, buf_ref):
    m = o_ref.shape[0]

    def body(r, carry):
        buf_ref[pl.ds(r, 1), :] = h_ref[pl.ds(tok_ref[0, 0, r], 1), :]
        return carry

    lax.fori_loop(0, m, body, 0)
    o_ref[...] = buf_ref[...].astype(o_ref.dtype)


def _moe_gather(h, tok_pad):
    t, d = h.shape
    p = tok_pad.shape[0]
    m = MOE_BLOCK
    half = d // 2
    return pl.pallas_call(
        _gather_kernel,
        grid=(2, p // m),
        in_specs=[pl.BlockSpec((1, 1, m), lambda c, i: (i, 0, 0), memory_space=pltpu.SMEM),
                  pl.BlockSpec((t, half), lambda c, i: (0, c), pipeline_mode=pl.Buffered(1))],
        out_specs=pl.BlockSpec((m, half), lambda c, i: (i, c)),
        out_shape=jax.ShapeDtypeStruct((p, d), BF16),
        scratch_shapes=[pltpu.VMEM((m, half), F32)],
        compiler_params=_cparams(("arbitrary", "arbitrary")),
        name="moe_gather",
    )(tok_pad.reshape(p // m, 1, m), h)


def _moe_up_kernel(be_ref, na_ref, xs_ref, wg_ref, wl_ref, bg_ref, bl_ref, o_ref):
    i = pl.program_id(1)

    @pl.when(i < na_ref[0])
    def _():
        x = xs_ref[...]
        hg = jnp.dot(x, wg_ref[0], preferred_element_type=F32) + bg_ref[0]
        hl = jnp.dot(x, wl_ref[0], preferred_element_type=F32) + bl_ref[0]
        glu = jnp.minimum(hg, SWIGLU_LIMIT)
        lin = jnp.clip(hl, -SWIGLU_LIMIT, SWIGLU_LIMIT)
        o_ref[...] = (glu * _sigmoid(SWIGLU_ALPHA * glu) * (lin + 1.0)).astype(o_ref.dtype)

    @pl.when(i >= na_ref[0])
    def _():
        o_ref[...] = jnp.zeros_like(o_ref)


def _moe_up(xs, w_glu, w_lin, b_glu, b_lin, blk_expert, n_active):
    p, d = xs.shape
    m = MOE_BLOCK
    ft = MOE_FT
    nf = w_glu.shape[2] // ft
    grid_spec = pltpu.PrefetchScalarGridSpec(
        num_scalar_prefetch=2,
        grid=(nf, p // m),
        in_specs=[pl.BlockSpec((m, d), lambda j, i, be, na: (i, 0)),
                  pl.BlockSpec((1, d, ft), lambda j, i, be, na: (be[i], 0, j)),
                  pl.BlockSpec((1, d, ft), lambda j, i, be, na: (be[i], 0, j)),
                  pl.BlockSpec((1, 1, ft), lambda j, i, be, na: (be[i], 0, j)),
                  pl.BlockSpec((1, 1, ft), lambda j, i, be, na: (be[i], 0, j))],
        out_specs=pl.BlockSpec((m, ft), lambda j, i, be, na: (i, j)),
    )
    return pl.pallas_call(
        _moe_up_kernel,
        grid_spec=grid_spec,
        out_shape=jax.ShapeDtypeStruct((p, w_glu.shape[2]), BF16),
        compiler_params=_cparams(("arbitrary", "arbitrary")),
        name="moe_up",
    )(blk_expert, n_active, xs, w_glu, w_lin, b_glu, b_lin)


def _moe_down_kernel(be_ref, na_ref, nreal_ref, dest_ref, a_ref, wd_ref, bd_ref, yt_ref,
                     wbf_scr, y_scr, sem):
    i = pl.program_id(0)
    nb = pl.num_programs(0)
    m = a_ref.shape[0]
    slot = i % 2

    def row_copy(s, r, d):
        return pltpu.make_async_copy(y_scr.at[s, pl.ds(r, 1)], yt_ref.at[pl.ds(d, 1)], sem.at[s])

    def drain(s, count):
        def wbody(_, carry):
            row_copy(s, 0, 0).wait()
            return carry
        lax.fori_loop(0, count, wbody, 0)

    @pl.when(i >= 2)
    def _():
        drain(slot, nreal_ref[i - 2])

    changed = jnp.where(i == 0, 1, be_ref[i] - be_ref[jnp.maximum(i - 1, 0)])

    @pl.when(changed != 0)
    def _():
        wbf_scr[...] = wd_ref[0].astype(BF16)

    @pl.when(i < na_ref[0])
    def _():
        y_scr[slot] = jnp.dot(a_ref[...], wbf_scr[...], preferred_element_type=F32) + bd_ref[0]

        def sbody(r, carry):
            d = dest_ref[0, 0, r]

            @pl.when(d >= 0)
            def _():
                row_copy(slot, r, d).start()
            return carry

        lax.fori_loop(0, m, sbody, 0)

    @pl.when(i == nb - 1)
    def _():
        @pl.when(i >= 1)
        def _():
            drain(1 - slot, nreal_ref[i - 1])
        drain(slot, nreal_ref[i])


def _moe_down(act, w_down, b_down, blk_expert, n_active, nreal, dest, n_rows):
    p, f = act.shape
    m = MOE_BLOCK
    d = w_down.shape[2]
    grid_spec = pltpu.PrefetchScalarGridSpec(
        num_scalar_prefetch=3,
        grid=(p // m,),
        in_specs=[pl.BlockSpec((1, 1, m), lambda i, be, na, nr: (i, 0, 0), memory_space=pltpu.SMEM),
                  pl.BlockSpec((m, f), lambda i, be, na, nr: (i, 0)),
                  pl.BlockSpec((1, f, d), lambda i, be, na, nr: (be[i], 0, 0)),
                  pl.BlockSpec((1, 1, d), lambda i, be, na, nr: (be[i], 0, 0))],
        out_specs=pl.BlockSpec(memory_space=pl.ANY),
        scratch_shapes=[pltpu.VMEM((f, d), BF16),
                        pltpu.VMEM((2, m, d), F32),
                        pltpu.SemaphoreType.DMA((2,))],
    )
    return pl.pallas_call(
        _moe_down_kernel,
        grid_spec=grid_spec,
        out_shape=jax.ShapeDtypeStruct((n_rows, d), F32),
        compiler_params=_cparams(("arbitrary",)),
        name="moe_down",
    )(blk_expert, n_active, nreal, dest.reshape(p // m, 1, m), act, w_down, b_down)


def _combine_kernel(h_ref, y0_ref, y1_ref, y2_ref, y3_ref, gate_ref, g2_ref, b2_ref, o_ref):
    ffn = gate_ref[:, 0:1] * y0_ref[...]
    for j, y_ref in enumerate((y1_ref, y2_ref, y3_ref), start=1):
        ffn = ffn + gate_ref[:, j:j + 1] * y_ref[...]
    o_ref[...] = _layer_norm(DEEPNORM_ALPHA * h_ref[...] + ffn, g2_ref[...], b2_ref[...])


def _combine(h, yt, gate, g2, b2):
    t, d = h.shape
    tm = TOKEN_TILE
    nt = t // tm

    def choice(j):
        return pl.BlockSpec((tm, d), lambda i: (j * nt + i, 0))

    return pl.pallas_call(
        _combine_kernel,
        grid=(nt,),
        in_specs=[pl.BlockSpec((tm, d), lambda i: (i, 0)),
                  choice(0), choice(1), choice(2), choice(3),
                  pl.BlockSpec((tm, TOP_K), lambda i: (i, 0)),
                  pl.BlockSpec((1, d), lambda i: (0, 0)),
                  pl.BlockSpec((1, d), lambda i: (0, 0))],
        out_specs=pl.BlockSpec((tm, d), lambda i: (i, 0)),
        out_shape=jax.ShapeDtypeStruct((t, d), F32),
        compiler_params=_cparams(("parallel",)),
        name="combine_ln2",
    )(h, yt, yt, yt, yt, gate, g2, b2)


def _route(top_e, lrank, tilecnt):
    t = top_e.shape[0]
    m = MOE_BLOCK
    n_assign = t * TOP_K
    n_blocks = n_assign // m + N_EXPERTS
    p = n_blocks * m
    tc = tilecnt.reshape(-1, N_EXPERTS)
    base = jnp.cumsum(tc, axis=0) - tc
    counts = jnp.sum(tc, axis=0)
    padded = (counts + m - 1) // m * m
    pend = jnp.cumsum(padded)
    pstart = pend - padded
    base_tok = jnp.repeat(base, TOKEN_TILE, axis=0)
    rank = lrank + jnp.take_along_axis(base_tok, top_e, axis=1)
    slot = (pstart[top_e] + rank).reshape(-1)
    tok_pad = jnp.zeros((p,), I32).at[slot].set(jnp.arange(n_assign, dtype=I32) // TOP_K)
    a_idx = jnp.arange(n_assign, dtype=I32)
    dest = jnp.full((p,), -1, I32).at[slot].set((a_idx % TOP_K) * t + a_idx // TOP_K)
    blk_expert = jnp.minimum(
        jnp.searchsorted(pend, jnp.arange(n_blocks, dtype=I32) * m, side="right"),
        N_EXPERTS - 1).astype(I32)
    n_active = (pend[-1:] // m).astype(I32)
    nreal = jnp.sum((dest >= 0).reshape(n_blocks, m), axis=1).astype(I32)
    return tok_pad, dest, blk_expert, n_active, nreal


def kernel(x_prompt, x_sample, cache_attn_k, cache_attn_v, state_hgrn, w_in, hgrn_lb_logits,
           hgrn_norm_w, w_branch_a, w_branch_b, w_out, ln1_g, ln1_b, w_router, b_router,
           w_gate_up, b_gate_up, w_down, b_down, ln2_g, ln2_b):
    assert w_in.shape[0] == DEPTH == 1
    bp, lp, d = x_prompt.shape
    bs, ls, _ = x_sample.shape
    tp, ts = bp * lp, bs * ls
    t = tp + ts
    keep = min(max(w for w, _ in DILATION_PAIRS), lp)

    xa = jnp.concatenate([x_prompt.reshape(tp, d), x_sample.reshape(ts, d)], axis=0)
    xb = xa.astype(BF16)
    wi = w_in[0]
    c1, c2 = P1_WIDTH, P1_WIDTH + P2_WIDTH
    p1 = _matmul(xb, wi[:, :c1].astype(BF16), 768, 1024, F32, "inproj_hgrn")
    p2 = _matmul(xb, wi[:, c1:c2].astype(BF16), 768, 512, F32, "inproj_attn")
    p3 = _matmul(xb, wi[:, c2:].astype(BF16), 768, 1024, F32, "inproj_gate")

    oa_p, st_p = _hgrn(p1, hgrn_lb_logits, hgrn_norm_w[0], None, bp, lp, 0)
    oa_s, st_s = _hgrn(p1, hgrn_lb_logits, hgrn_norm_w[0], state_hgrn[0], bs, ls, tp)

    n = jnp.arange(1, ATT_Q_HEADS + 1, dtype=F32)
    slopes = jnp.power(2.0, -8.0 * n / ATT_Q_HEADS).reshape(N_GROUPS, ATT_KV_HEADS)
    ctx = cache_attn_k.shape[2]
    ob_p = _attn_prompt(p2, slopes, bp, lp)
    ob_s = _attn_sample(p2, cache_attn_k[0].reshape(bs, ctx, ATT_KV_WIDTH),
                        cache_attn_v[0].reshape(bs, ctx, ATT_KV_WIDTH), slopes, bs, ls, tp)

    oa = jnp.concatenate([oa_p.reshape(tp, HG_WIDTH), oa_s.reshape(ts, HG_WIDTH)], axis=0)
    ob = jnp.concatenate([ob_p, ob_s.reshape(ts, ATT_KV_WIDTH)], axis=0)
    wr = w_router[0]
    wr0 = wr.astype(BF16)
    wr1 = (wr - wr0.astype(F32)).astype(BF16)
    h, top_e, gate, lrank, tilecnt = _merge(
        xa, oa, ob, p3, w_branch_a[0].astype(BF16), w_branch_b[0].astype(BF16),
        w_out[0].astype(BF16), ln1_g, ln1_b, wr0, wr1, b_router)

    tok_pad, dest, blk_expert, n_active, nreal = _route(top_e, lrank, tilecnt)
    xs = _moe_gather(h, tok_pad)
    wgu = w_gate_up[0]
    bgu = b_gate_up[0]
    act = _moe_up(xs, wgu[:, :, 0::2].astype(BF16), wgu[:, :, 1::2].astype(BF16),
                  bgu[:, None, 0::2], bgu[:, None, 1::2], blk_expert, n_active)
    yt = _moe_down(act, w_down[0], b_down[0][:, None, :], blk_expert, n_active, nreal, dest,
                   t * TOP_K)
    y = _combine(h, yt, gate, ln2_g, ln2_b)

    k_new = p2[:, ATT_Q_HEADS * ATT_HEAD_DIM:(ATT_Q_HEADS + ATT_KV_HEADS) * ATT_HEAD_DIM]
    v_new = p2[:, (ATT_Q_HEADS + ATT_KV_HEADS) * ATT_HEAD_DIM:]

    def kv_prompt(a):
        return a[:tp].reshape(bp, lp, ATT_KV_HEADS, ATT_HEAD_DIM)[:, lp - keep:][None]

    def kv_sample(a):
        return a[tp:].reshape(bs, ls, ATT_KV_HEADS, ATT_HEAD_DIM)[None]

    return (y[:tp].reshape(bp, lp, d), y[tp:].reshape(bs, ls, d),
            kv_prompt(k_new), kv_prompt(v_new), st_p[None],
            kv_sample(k_new), kv_sample(v_new), st_s[None])
```

```python
import functools
import math

import jax
import jax.numpy as jnp
from jax import lax
from jax.experimental import pallas as pl
from jax.experimental.pallas import tpu as pltpu

F32 = jnp.float32
BF16 = jnp.bfloat16
I32 = jnp.int32

D_MODEL = 2048
DEPTH = 1
HG_HEADS = 8
HG_DIM = 128
HG_WIDTH = HG_HEADS * HG_DIM
HG_CHUNK = 64
ATT_HEAD_DIM = 128
DILATION_PAIRS = ((128, 1), (512, 4), (2048, 16))
N_GROUPS = len(DILATION_PAIRS)
ATT_KV_HEADS = 4
ATT_Q_HEADS = N_GROUPS * ATT_KV_HEADS
ATT_TAPS = 128
ATT_KV_WIDTH = ATT_KV_HEADS * ATT_HEAD_DIM
ATT_SCALE = ATT_HEAD_DIM ** -0.5
N_EXPERTS = 32
TOP_K = 4
D_FF = D_MODEL
SWIGLU_ALPHA = 1.702
SWIGLU_LIMIT = 7.0
LN_EPS = 1e-5
RMS_EPS = 1e-5
DEEPNORM_ALPHA = (2 * DEPTH) ** 0.25

P1_WIDTH = 4 * HG_WIDTH
P2_WIDTH = (ATT_Q_HEADS + 2 * ATT_KV_HEADS) * ATT_HEAD_DIM
P3_WIDTH = 2 * D_MODEL

LANES = 128
SUBLANES = 8
VMEM_LIMIT = 56 * 1024 * 1024

TOKEN_TILE = 256
MOE_BLOCK = 256
MOE_FT = 1024
NEG = -1e30

_NT = (((1,), (1,)), ((), ()))
_TN = (((0,), (0,)), ((), ()))


def _cparams(sem):
    return pltpu.CompilerParams(dimension_semantics=sem, vmem_limit_bytes=VMEM_LIMIT)


def _sigmoid(x):
    return 1.0 / (1.0 + jnp.exp(-x))


def _mm_kernel(x_ref, w_ref, o_ref):
    o_ref[...] = jnp.dot(x_ref[...], w_ref[...], preferred_element_type=F32).astype(o_ref.dtype)


def _matmul(x, w, tm, tn, out_dtype, name):
    m, k = x.shape
    n = w.shape[1]
    assert m % tm == 0 and n % tn == 0
    return pl.pallas_call(
        _mm_kernel,
        grid=(n // tn, m // tm),
        in_specs=[pl.BlockSpec((tm, k), lambda j, i: (i, 0)),
                  pl.BlockSpec((k, tn), lambda j, i: (0, j))],
        out_specs=pl.BlockSpec((tm, tn), lambda j, i: (i, j)),
        out_shape=jax.ShapeDtypeStruct((m, n), out_dtype),
        compiler_params=_cparams(("parallel", "parallel")),
        name=name,
    )(x, w)


def _level_ref(g_scr, m, sl, chunk, sub):
    pieces = []
    for j in range(chunk // SUBLANES):
        if 2 * m >= SUBLANES:
            row = ((SUBLANES * j) // (2 * m)) * (2 * m) + m - 1
            pieces.append(jnp.broadcast_to(g_scr[row:row + 1, sl], (SUBLANES, LANES)))
        else:
            acc = None
            for u in range(SUBLANES // (2 * m)):
                row = SUBLANES * j + 2 * m * u + m - 1
                b = jnp.broadcast_to(g_scr[row:row + 1, sl], (SUBLANES, LANES))
                acc = b if acc is None else jnp.where(sub // (2 * m) == u, b, acc)
            pieces.append(acc)
    return pieces[0] if len(pieces) == 1 else jnp.concatenate(pieces, axis=0)


def _hgrn_kernel(lbl_ref, nw_ref, hq_ref, hf_ref, hi_ref, hg_ref, *rest, chunk, has_s0):
    if has_s0:
        s0_ref, o_ref, sout_ref, st_scr, g_scr = rest
    else:
        o_ref, sout_ref, st_scr, g_scr = rest
    c = pl.program_id(1)
    nc = pl.num_programs(1)

    @pl.when(c == 0)
    def _init():
        for h in range(HG_HEADS):
            if has_s0:
                st_scr[h] = s0_ref[0, h].T
            else:
                st_scr[h] = jnp.zeros((HG_DIM, HG_DIM), F32)

    lbl = lbl_ref[...]
    ex = jnp.exp(lbl - jnp.max(lbl, axis=0, keepdims=True))
    lb = ex[0:1] / jnp.sum(ex, axis=0, keepdims=True)

    f = lb + (1.0 - lb) * _sigmoid(hf_ref[...])
    logf = jnp.log(f)
    kall = 1.0 - f

    r_cc = lax.broadcasted_iota(I32, (chunk, chunk), 0)
    c_cc = lax.broadcasted_iota(I32, (chunk, chunk), 1)
    tri = jnp.where(r_cc >= c_cc, 1.0, 0.0).astype(BF16)
    p0 = logf.astype(BF16)
    r1 = logf - p0.astype(F32)
    p1 = r1.astype(BF16)
    p2 = (r1 - p1.astype(F32)).astype(BF16)
    g_all = (jnp.dot(tri, p0, preferred_element_type=F32)
             + jnp.dot(tri, p1, preferred_element_type=F32)
             + jnp.dot(tri, p2, preferred_element_type=F32))
    g_scr[...] = g_all

    sub = lax.broadcasted_iota(I32, (SUBLANES, LANES), 0)
    row_c = lax.broadcasted_iota(I32, (chunk, LANES), 0)
    nw = nw_ref[...]

    for h in range(HG_HEADS):
        sl = slice(h * HG_DIM, (h + 1) * HG_DIM)
        hq = hq_ref[:, sl]
        q = hq * _sigmoid(hq)
        k = kall[:, sl]
        g = g_all[:, sl]
        v = hi_ref[:, sl]
        vb = v.astype(BF16)

        p = lax.dot_general(q.astype(BF16), k.astype(BF16), _NT, preferred_element_type=F32)
        a = jnp.where(r_cc == c_cc, p, 0.0)
        m = 1
        while m < chunk:
            ref = _level_ref(g_scr, m, sl, chunk, sub)
            e = jnp.exp(-jnp.abs(g - ref))
            odd = ((row_c // m) & 1) == 1
            x = (jnp.where(odd, q, k) * e).astype(BF16)
            p = lax.dot_general(x, x, _NT, preferred_element_type=F32)
            pair = (r_cc // (2 * m)) == (c_cc // (2 * m))
            sel = jnp.where(pair, ((r_cc // m) & 1) - ((c_cc // m) & 1), 0) == 1
            a = jnp.where(sel, p, a)
            m *= 2

        st = st_scr[h]
        o = jnp.dot(a.astype(BF16), vb, preferred_element_type=F32)
        qe = (q * jnp.exp(g)).astype(BF16)
        o = o + lax.dot_general(qe, st.astype(BF16), _NT, preferred_element_type=F32)

        ms = jnp.mean(o * o, axis=-1, keepdims=True)
        hg = hg_ref[:, sl]
        out = o * lax.rsqrt(ms + RMS_EPS) * nw * (hg * _sigmoid(hg))
        o_ref[0, :, sl] = out.astype(o_ref.dtype)

        g_last = g[chunk - 1:chunk, :]
        kd = (k * jnp.exp(g_last - g)).astype(BF16)
        st_scr[h] = st * jnp.exp(g_last) + lax.dot_general(vb, kd, _TN, preferred_element_type=F32)

    @pl.when(c == nc - 1)
    def _fin():
        for h in range(HG_HEADS):
            sout_ref[0, h] = st_scr[h].T


def _hgrn(p1, lb_logits, norm_w, s0, batch, seq, row0):
    chunk = math.gcd(seq, HG_CHUNK)
    nchunk = seq // chunk
    assert row0 % chunk == 0
    rb0 = row0 // chunk

    def col(idx):
        return pl.BlockSpec((chunk, HG_WIDTH), lambda b, c: (rb0 + b * nchunk + c, idx))

    in_specs = [pl.BlockSpec(lb_logits.shape, lambda b, c: (0, 0)),
                pl.BlockSpec((1, HG_DIM), lambda b, c: (0, 0)),
                col(0), col(1), col(2), col(3)]
    args = [lb_logits, norm_w.reshape(1, HG_DIM), p1, p1, p1, p1]
    if s0 is not None:
        in_specs.append(pl.BlockSpec((1, HG_HEADS, HG_DIM, HG_DIM), lambda b, c: (b, 0, 0, 0)))
        args.append(s0)
    return pl.pallas_call(
        functools.partial(_hgrn_kernel, chunk=chunk, has_s0=s0 is not None),
        grid=(batch, nchunk),
        in_specs=in_specs,
        out_specs=[pl.BlockSpec((1, chunk, HG_WIDTH), lambda b, c: (b, c, 0)),
                   pl.BlockSpec((1, HG_HEADS, HG_DIM, HG_DIM), lambda b, c: (b, 0, 0, 0))],
        out_shape=[jax.ShapeDtypeStruct((batch, seq, HG_WIDTH), BF16),
                   jax.ShapeDtypeStruct((batch, HG_HEADS, HG_DIM, HG_DIM), F32)],
        scratch_shapes=[pltpu.VMEM((HG_HEADS, HG_DIM, HG_DIM), F32),
                        pltpu.VMEM((chunk, HG_WIDTH), F32)],
        compiler_params=_cparams(("parallel", "arbitrary")),
        name="hgrn_seq%d" % seq,
    )(*args)


QB = 128


def _attn_prompt_kernel(slopes_ref, q0_ref, q1_ref, q2_ref, k_ref, v_ref, o_ref,
                        og_scr, lse_scr, bias_scr, *, seq):
    h = pl.program_id(1)
    il = lax.broadcasted_iota(I32, (QB, 2 * QB), 0)
    jl = lax.broadcasted_iota(I32, (QB, 2 * QB), 1)
    delta = il + QB - jl
    band = jnp.where(delta >= 0, delta, ATT_TAPS + 1) <= ATT_TAPS
    prev_half = jl < QB
    q_refs = (q0_ref, q1_ref, q2_ref)

    for g, (window, dil) in enumerate(DILATION_PAIRS):
        assert window // dil == ATT_TAPS
        slope = slopes_ref[g, h]
        bias_scr[g] = jnp.where(band, (-slope * dil) * delta.astype(F32), NEG)
        nbr = (seq // dil) // QB
        q_ref = q_refs[g]

        def rows(start, dil=dil):
            return pl.ds(start, QB) if dil == 1 else pl.ds(start, QB, stride=dil)

        def body(ib, carry, g=g, dil=dil, nbr=nbr, q_ref=q_ref, rows=rows):
            res = ib // nbr
            jb = ib - res * nbr
            qs = res + dil * (jb * QB)
            ps = res + dil * jnp.maximum(jb * QB - QB, 0)
            qv = q_ref[rows(qs), :].astype(BF16)
            kb = jnp.concatenate([k_ref[rows(ps), :], k_ref[rows(qs), :]], axis=0).astype(BF16)
            vb = jnp.concatenate([v_ref[rows(ps), :], v_ref[rows(qs), :]], axis=0).astype(BF16)
            s = lax.dot_general(qv, kb, _NT, preferred_element_type=F32) * ATT_SCALE + bias_scr[g]
            s = jnp.where(jnp.where(prev_half, jb, 1) == 0, NEG, s)
            mx = jnp.max(s, axis=-1, keepdims=True)
            p = jnp.exp(s - mx)
            l = jnp.sum(p, axis=-1, keepdims=True)
            o = jnp.dot(p.astype(BF16), vb, preferred_element_type=F32) / l
            og_scr[g, rows(qs), :] = o
            lse_scr[g, rows(qs), :] = jnp.broadcast_to(mx + jnp.log(l), (QB, LANES))
            return carry

        lax.fori_loop(0, seq // QB, body, 0)

    def merge(tb, carry):
        r = pl.ds(pl.multiple_of(tb * QB, QB), QB)
        l0, l1, l2 = lse_scr[0, r, :], lse_scr[1, r, :], lse_scr[2, r, :]
        mx = jnp.maximum(jnp.maximum(l0, l1), l2)
        w0, w1, w2 = jnp.exp(l0 - mx), jnp.exp(l1 - mx), jnp.exp(l2 - mx)
        o = (w0 * og_scr[0, r, :] + w1 * og_scr[1, r, :] + w2 * og_scr[2, r, :]) / (w0 + w1 + w2)
        o_ref[r, :] = o.astype(o_ref.dtype)
        return carry

    lax.fori_loop(0, seq // QB, merge, 0)


def _attn_prompt(p2, slopes, batch, seq):
    assert seq % (QB * max(d for _, d in DILATION_PAIRS)) == 0

    def col(fn):
        return pl.BlockSpec((seq, ATT_HEAD_DIM), lambda b, h: (b, fn(h)))

    return pl.pallas_call(
        functools.partial(_attn_prompt_kernel, seq=seq),
        grid=(batch, ATT_KV_HEADS),
        in_specs=[pl.BlockSpec(memory_space=pltpu.SMEM),
                  col(lambda h: h), col(lambda h: ATT_KV_HEADS + h), col(lambda h: 2 * ATT_KV_HEADS + h),
                  col(lambda h: ATT_Q_HEADS + h), col(lambda h: ATT_Q_HEADS + ATT_KV_HEADS + h)],
        out_specs=pl.BlockSpec((seq, ATT_HEAD_DIM), lambda b, h: (b, h)),
        out_shape=jax.ShapeDtypeStruct((batch * seq, ATT_KV_WIDTH), BF16),
        scratch_shapes=[pltpu.VMEM((N_GROUPS, seq, ATT_HEAD_DIM), F32),
                        pltpu.VMEM((N_GROUPS, seq, LANES), F32),
                        pltpu.VMEM((N_GROUPS, QB, 2 * QB), F32)],
        compiler_params=_cparams(("parallel", "parallel")),
        name="attn_prompt",
    )(slopes, p2, p2, p2, p2, p2)


def _attn_sample_kernel(slopes_ref, qa_ref, qb_ref, qc_ref, kn_ref, vn_ref, kc_ref, vc_ref, o_ref,
                        *, ctx, nq):
    rows = N_GROUPS * nq
    gi = lax.broadcasted_iota(I32, (rows, 1), 0) // nq
    dil = jnp.where(gi == 0, DILATION_PAIRS[0][1], jnp.where(gi == 1, DILATION_PAIRS[1][1], DILATION_PAIRS[2][1]))
    win = jnp.where(gi == 0, DILATION_PAIRS[0][0], jnp.where(gi == 1, DILATION_PAIRS[1][0], DILATION_PAIRS[2][0]))

    def dist_valid(shape, key0, nkeys):
        r = lax.broadcasted_iota(I32, shape, 0)
        t = r - (r // nq) * nq
        col = lax.broadcasted_iota(I32, shape, 1)
        dist = ctx + t - (col + key0)
        bad = jnp.where(dist >= 0, dist & (dil - 1), 1)
        bad = jnp.where(dist <= win, bad, 1)
        bad = jnp.where(col < nkeys, bad, 1)
        return dist.astype(F32), bad == 0

    dist_c, ok_c = dist_valid((rows, ctx), 0, ctx)
    dist_n, ok_n = dist_valid((rows, LANES), ctx, nq)
    zpad = jnp.zeros((LANES - nq, ATT_HEAD_DIM), F32)

    for h in range(ATT_KV_HEADS):
        hs = slice(h * ATT_HEAD_DIM, (h + 1) * ATT_HEAD_DIM)
        slope = jnp.where(gi == 0, slopes_ref[0, h], jnp.where(gi == 1, slopes_ref[1, h], slopes_ref[2, h]))
        qh = jnp.concatenate([qa_ref[:, hs], qb_ref[:, hs], qc_ref[:, hs]], axis=0).astype(BF16)
        kc = kc_ref[0, :, hs].astype(BF16)
        vc = vc_ref[0, :, hs].astype(BF16)
        kn = jnp.concatenate([kn_ref[:, hs], zpad], axis=0).astype(BF16)
        vn = jnp.concatenate([vn_ref[:, hs], zpad], axis=0).astype(BF16)
        sc = lax.dot_general(qh, kc, _NT, preferred_element_type=F32) * ATT_SCALE - slope * dist_c
        sn = lax.dot_general(qh, kn, _NT, preferred_element_type=F32) * ATT_SCALE - slope * dist_n
        sc = jnp.where(ok_c, sc, NEG)
        sn = jnp.where(ok_n, sn, NEG)
        mx = jnp.maximum(jnp.max(sc, axis=-1, keepdims=True), jnp.max(sn, axis=-1, keepdims=True))
        pc = jnp.exp(sc - mx)
        pn = jnp.exp(sn - mx)
        l = jnp.sum(pc, axis=-1, keepdims=True) + jnp.sum(pn, axis=-1, keepdims=True)
        o = (jnp.dot(pc.astype(BF16), vc, preferred_element_type=F32)
             + jnp.dot(pn.astype(BF16), vn, preferred_element_type=F32)) / l
        lse = mx + jnp.log(l)
        l0, l1, l2 = lse[0:nq], lse[nq:2 * nq], lse[2 * nq:3 * nq]
        m3 = jnp.maximum(jnp.maximum(l0, l1), l2)
        w0, w1, w2 = jnp.exp(l0 - m3), jnp.exp(l1 - m3), jnp.exp(l2 - m3)
        out = (w0 * o[0:nq] + w1 * o[nq:2 * nq] + w2 * o[2 * nq:3 * nq]) / (w0 + w1 + w2)
        o_ref[0, :, hs] = out.astype(o_ref.dtype)


def _attn_sample(p2, cache_k, cache_v, slopes, batch, nq, row0):
    ctx = cache_k.shape[1]
    assert row0 % nq == 0 and nq == SUBLANES and ctx >= max(w for w, _ in DILATION_PAIRS)
    rb0 = row0 // nq

    def col(idx):
        return pl.BlockSpec((nq, ATT_KV_WIDTH), lambda b: (rb0 + b, idx))

    cache_spec = pl.BlockSpec((1, ctx, ATT_KV_WIDTH), lambda b: (b, 0, 0))
    return pl.pallas_call(
        functools.partial(_attn_sample_kernel, ctx=ctx, nq=nq),
        grid=(batch,),
        in_specs=[pl.BlockSpec(memory_space=pltpu.SMEM),
                  col(0), col(1), col(2), col(3), col(4), cache_spec, cache_spec],
        out_specs=pl.BlockSpec((1, nq, ATT_KV_WIDTH), lambda b: (b, 0, 0)),
        out_shape=jax.ShapeDtypeStruct((batch, nq, ATT_KV_WIDTH), BF16),
        compiler_params=_cparams(("parallel",)),
        name="attn_sample",
    )(slopes, p2, p2, p2, p2, p2, cache_k, cache_v)


def _layer_norm(y, g, b):
    mu = jnp.mean(y, axis=-1, keepdims=True)
    yc = y - mu
    var = jnp.mean(yc * yc, axis=-1, keepdims=True)
    return yc * lax.rsqrt(var + LN_EPS) * g + b


def _merge_kernel(x_ref, oa_ref, ob_ref, ga_ref, gb_ref, wa_ref, wb_ref, wo_ref, g1_ref, b1_ref,
                  wr0_ref, wr1_ref, br_ref,
                  h_ref, tope_ref, gate_ref, lrank_ref, cnt_ref):
    tm = x_ref.shape[0]
    a = jnp.dot(oa_ref[...], wa_ref[...], preferred_element_type=F32)
    b = jnp.dot(ob_ref[...], wb_ref[...], preferred_element_type=F32)
    merged = _sigmoid(ga_ref[...]) * a + _sigmoid(gb_ref[...]) * b
    z = jnp.dot(merged.astype(BF16), wo_ref[...], preferred_element_type=F32)
    hh = _layer_norm(DEEPNORM_ALPHA * x_ref[...] + z, g1_ref[...], b1_ref[...])
    h_ref[...] = hh

    h0 = hh.astype(BF16)
    h1 = (hh - h0.astype(F32)).astype(BF16)
    logits = (jnp.dot(h0, wr0_ref[...], preferred_element_type=F32)
              + jnp.dot(h1, wr0_ref[...], preferred_element_type=F32)
              + jnp.dot(h0, wr1_ref[...], preferred_element_type=F32)) + br_ref[...]

    lane = lax.broadcasted_iota(I32, (tm, N_EXPERTS), 1)
    work = logits
    vals, idxs = [], []
    onehot = jnp.zeros((tm, N_EXPERTS), F32)
    for _ in range(TOP_K):
        mx = jnp.max(work, axis=-1, keepdims=True)
        idx = jnp.min(jnp.where(work == mx, lane, N_EXPERTS), axis=-1, keepdims=True)
        hit = lane == idx
        vals.append(mx)
        idxs.append(idx)
        onehot = jnp.where(hit, 1.0, onehot)
        work = jnp.where(hit, -jnp.inf, work)
    ex = [jnp.exp(v - vals[0]) for v in vals]
    den = ex[0] + ex[1] + ex[2] + ex[3]

    r_tt = lax.broadcasted_iota(I32, (tm, tm), 0)
    c_tt = lax.broadcasted_iota(I32, (tm, tm), 1)
    before = jnp.where(r_tt > c_tt, 1.0, 0.0).astype(BF16)
    prefix = jnp.dot(before, onehot.astype(BF16), preferred_element_type=F32)
    for j in range(TOP_K):
        tope_ref[:, j:j + 1] = idxs[j]
        gate_ref[:, j:j + 1] = ex[j] / den
        lrank_ref[:, j:j + 1] = jnp.sum(jnp.where(lane == idxs[j], prefix, 0.0), axis=-1,
                                        keepdims=True).astype(I32)
    cnt_ref[0] = jnp.sum(onehot, axis=0, keepdims=True).astype(I32)


def _merge(xa, oa, ob, p3, wa, wb, wo, g1, b1, wr0, wr1, br):
    t = xa.shape[0]
    tm = TOKEN_TILE
    nt = t // tm
    assert t % tm == 0

    def full(a):
        return pl.BlockSpec(a.shape, lambda i: (0,) * a.ndim)

    def rowblk(width, idx=0):
        return pl.BlockSpec((tm, width), lambda i: (i, idx))

    return pl.pallas_call(
        _merge_kernel,
        grid=(nt,),
        in_specs=[rowblk(D_MODEL), rowblk(HG_WIDTH), rowblk(ATT_KV_WIDTH), rowblk(D_MODEL, 0),
                  rowblk(D_MODEL, 1), full(wa), full(wb), full(wo), full(g1), full(b1),
                  full(wr0), full(wr1), full(br)],
        out_specs=[rowblk(D_MODEL), rowblk(TOP_K), rowblk(TOP_K), rowblk(TOP_K),
                   pl.BlockSpec((1, 1, N_EXPERTS), lambda i: (i, 0, 0))],
        out_shape=[jax.ShapeDtypeStruct((t, D_MODEL), F32),
                   jax.ShapeDtypeStruct((t, TOP_K), I32),
                   jax.ShapeDtypeStruct((t, TOP_K), F32),
                   jax.ShapeDtypeStruct((t, TOP_K), I32),
                   jax.ShapeDtypeStruct((nt, 1, N_EXPERTS), I32)],
        compiler_params=_cparams(("parallel",)),
        name="merge_ln1_router",
    )(xa, oa, ob, p3, p3, wa, wb, wo, g1, b1, wr0, wr1, br)


def _gather_kernel(tok_ref, h_ref, o_ref, buf_ref):
    m = o_ref.shape[0]

    def body(r, carry):
        buf_ref[pl.ds(r, 1), :] = h_ref[pl.ds(tok_ref[0, 0, r], 1), :]
        return carry

    lax.fori_loop(0, m, body, 0)
    o_ref[...] = buf_ref[...].astype(o_ref.dtype)


def _moe_gather(h, tok_pad):
    t, d = h.shape
    p = tok_pad.shape[0]
    m = MOE_BLOCK
    half = d // 2
    return pl.pallas_call(
        _gather_kernel,
        grid=(2, p // m),
        in_specs=[pl.BlockSpec((1, 1, m), lambda c, i: (i, 0, 0), memory_space=pltpu.SMEM),
                  pl.BlockSpec((t, half), lambda c, i: (0, c), pipeline_mode=pl.Buffered(1))],
        out_specs=pl.BlockSpec((m, half), lambda c, i: (i, c)),
        out_shape=jax.ShapeDtypeStruct((p, d), BF16),
        scratch_shapes=[pltpu.VMEM((m, half), F32)],
        compiler_params=_cparams(("arbitrary", "arbitrary")),
        name="moe_gather",
    )(tok_pad.reshape(p // m, 1, m), h)


def _moe_up_kernel(be_ref, na_ref, xs_ref, w_ref, b_ref, o_ref, wbf_scr):
    i = pl.program_id(1)
    changed = jnp.where(i == 0, 1, be_ref[i] - be_ref[jnp.maximum(i - 1, 0)])

    @pl.when(changed != 0)
    def _():
        wbf_scr[...] = w_ref[0].astype(BF16)

    @pl.when(i < na_ref[0])
    def _():
        m, ft = o_ref.shape[0], w_ref.shape[2]
        h = jnp.dot(xs_ref[...], wbf_scr[...], preferred_element_type=F32) + b_ref[0]
        even = (lax.broadcasted_iota(I32, (m, LANES), 1) & 1) == 0
        for g in range(ft // (2 * LANES)):
            halves = []
            for s in range(2):
                hs = h[:, (2 * g + s) * LANES:(2 * g + s + 1) * LANES]
                glu = jnp.minimum(hs, SWIGLU_LIMIT)
                glu = glu * _sigmoid(SWIGLU_ALPHA * glu)
                lin = jnp.clip(hs, -SWIGLU_LIMIT, SWIGLU_LIMIT) + 1.0
                u = jnp.where(even, glu, lin)
                halves.append(u * pltpu.roll(u, LANES - 1, 1))
            merged = jnp.where(even, halves[0], pltpu.roll(halves[1], 1, 1))
            o_ref[:, g * LANES:(g + 1) * LANES] = merged.astype(o_ref.dtype)

    @pl.when(i >= na_ref[0])
    def _():
        o_ref[...] = jnp.zeros_like(o_ref)


def _moe_up(xs, w_gate_up, b_gate_up, blk_expert, n_active):
    p, d = xs.shape
    m = MOE_BLOCK
    ft = MOE_FT
    nf = w_gate_up.shape[2] // ft
    grid_spec = pltpu.PrefetchScalarGridSpec(
        num_scalar_prefetch=2,
        grid=(nf, p // m),
        in_specs=[pl.BlockSpec((m, d), lambda j, i, be, na: (i, 0)),
                  pl.BlockSpec((1, d, ft), lambda j, i, be, na: (be[i], 0, j)),
                  pl.BlockSpec((1, 1, ft), lambda j, i, be, na: (be[i], 0, j))],
        out_specs=pl.BlockSpec((m, ft // 2), lambda j, i, be, na: (i, j)),
        scratch_shapes=[pltpu.VMEM((d, ft), BF16)],
    )
    return pl.pallas_call(
        _moe_up_kernel,
        grid_spec=grid_spec,
        out_shape=jax.ShapeDtypeStruct((p, w_gate_up.shape[2] // 2), BF16),
        compiler_params=_cparams(("arbitrary", "arbitrary")),
        name="moe_up",
    )(blk_expert, n_active, xs, w_gate_up, b_gate_up)


def _hidden_perm():
    r = lax.broadcasted_iota(I32, (LANES, LANES), 0)
    c = lax.broadcasted_iota(I32, (LANES, LANES), 1)
    src = (r >> 1) + (r & 1) * (LANES // 2)
    return jnp.where(c == src, 1.0, 0.0).astype(BF16)


def _moe_down_kernel(be_ref, na_ref, nreal_ref, dest_ref, a_ref, wd_ref, bd_ref, yt_ref,
                     wbf_scr, y_scr, sem):
    i = pl.program_id(0)
    nb = pl.num_programs(0)
    m = a_ref.shape[0]
    slot = i % 2

    def row_copy(s, r, d):
        return pltpu.make_async_copy(y_scr.at[s, pl.ds(r, 1)], yt_ref.at[pl.ds(d, 1)], sem.at[s])

    def drain(s, count):
        def wbody(_, carry):
            row_copy(s, 0, 0).wait()
            return carry
        lax.fori_loop(0, count, wbody, 0)

    @pl.when(i >= 2)
    def _():
        drain(slot, nreal_ref[i - 2])

    changed = jnp.where(i == 0, 1, be_ref[i] - be_ref[jnp.maximum(i - 1, 0)])

    @pl.when(changed != 0)
    def _():
        perm = _hidden_perm()
        for g in range(wd_ref.shape[1] // LANES):
            rows = slice(g * LANES, (g + 1) * LANES)
            wbf_scr[rows, :] = jnp.dot(perm, wd_ref[0, rows, :].astype(BF16),
                                       preferred_element_type=F32).astype(BF16)

    @pl.when(i < na_ref[0])
    def _():
        y_scr[slot] = jnp.dot(a_ref[...], wbf_scr[...], preferred_element_type=F32) + bd_ref[0]

        def sbody(r, carry):
            d = dest_ref[0, 0, r]

            @pl.when(d >= 0)
            def _():
                row_copy(slot, r, d).start()
            return carry

        lax.fori_loop(0, m, sbody, 0)

    @pl.when(i == nb - 1)
    def _():
        @pl.when(i >= 1)
        def _():
            drain(1 - slot, nreal_ref[i - 1])
        drain(slot, nreal_ref[i])


def _moe_down(act, w_down, b_down, blk_expert, n_active, nreal, dest, n_rows):
    p, f = act.shape
    m = MOE_BLOCK
    d = w_down.shape[2]
    grid_spec = pltpu.PrefetchScalarGridSpec(
        num_scalar_prefetch=3,
        grid=(p // m,),
        in_specs=[pl.BlockSpec((1, 1, m), lambda i, be, na, nr: (i, 0, 0), memory_space=pltpu.SMEM),
                  pl.BlockSpec((m, f), lambda i, be, na, nr: (i, 0)),
                  pl.BlockSpec((1, f, d), lambda i, be, na, nr: (be[i], 0, 0)),
                  pl.BlockSpec((1, 1, d), lambda i, be, na, nr: (be[i], 0, 0))],
        out_specs=pl.BlockSpec(memory_space=pl.ANY),
        scratch_shapes=[pltpu.VMEM((f, d), BF16),
                        pltpu.VMEM((2, m, d), F32),
                        pltpu.SemaphoreType.DMA((2,))],
    )
    return pl.pallas_call(
        _moe_down_kernel,
        grid_spec=grid_spec,
        out_shape=jax.ShapeDtypeStruct((n_rows, d), F32),
        compiler_params=_cparams(("arbitrary",)),
        name="moe_down",
    )(blk_expert, n_active, nreal, dest.reshape(p // m, 1, m), act, w_down, b_down)


def _combine_kernel(h_ref, y0_ref, y1_ref, y2_ref, y3_ref, gate_ref, g2_ref, b2_ref, o_ref):
    ffn = gate_ref[:, 0:1] * y0_ref[...]
    for j, y_ref in enumerate((y1_ref, y2_ref, y3_ref), start=1):
        ffn = ffn + gate_ref[:, j:j + 1] * y_ref[...]
    o_ref[...] = _layer_norm(DEEPNORM_ALPHA * h_ref[...] + ffn, g2_ref[...], b2_ref[...])


def _combine(h, yt, gate, g2, b2):
    t, d = h.shape
    tm = TOKEN_TILE
    nt = t // tm

    def choice(j):
        return pl.BlockSpec((tm, d), lambda i: (j * nt + i, 0))

    return pl.pallas_call(
        _combine_kernel,
        grid=(nt,),
        in_specs=[pl.BlockSpec((tm, d), lambda i: (i, 0)),
                  choice(0), choice(1), choice(2), choice(3),
                  pl.BlockSpec((tm, TOP_K), lambda i: (i, 0)),
                  pl.BlockSpec((1, d), lambda i: (0, 0)),
                  pl.BlockSpec((1, d), lambda i: (0, 0))],
        out_specs=pl.BlockSpec((tm, d), lambda i: (i, 0)),
        out_shape=jax.ShapeDtypeStruct((t, d), F32),
        compiler_params=_cparams(("parallel",)),
        name="combine_ln2",
    )(h, yt, yt, yt, yt, gate, g2, b2)


def _route(top_e, lrank, tilecnt):
    t = top_e.shape[0]
    m = MOE_BLOCK
    n_assign = t * TOP_K
    n_blocks = n_assign // m + N_EXPERTS
    p = n_blocks * m
    tc = tilecnt.reshape(-1, N_EXPERTS)
    base = jnp.cumsum(tc, axis=0) - tc
    counts = jnp.sum(tc, axis=0)
    padded = (counts + m - 1) // m * m
    pend = jnp.cumsum(padded)
    pstart = pend - padded
    base_tok = jnp.repeat(base, TOKEN_TILE, axis=0)
    rank = lrank + jnp.take_along_axis(base_tok, top_e, axis=1)
    slot = (pstart[top_e] + rank).reshape(-1)
    tok_pad = jnp.zeros((p,), I32).at[slot].set(jnp.arange(n_assign, dtype=I32) // TOP_K)
    a_idx = jnp.arange(n_assign, dtype=I32)
    dest = jnp.full((p,), -1, I32).at[slot].set((a_idx % TOP_K) * t + a_idx // TOP_K)
    blk_expert = jnp.minimum(
        jnp.searchsorted(pend, jnp.arange(n_blocks, dtype=I32) * m, side="right"),
        N_EXPERTS - 1).astype(I32)
    n_active = (pend[-1:] // m).astype(I32)
    nreal = jnp.sum((dest >= 0).reshape(n_blocks, m), axis=1).astype(I32)
    return tok_pad, dest, blk_expert, n_active, nreal


def kernel(x_prompt, x_sample, cache_attn_k, cache_attn_v, state_hgrn, w_in, hgrn_lb_logits,
           hgrn_norm_w, w_branch_a, w_branch_b, w_out, ln1_g, ln1_b, w_router, b_router,
           w_gate_up, b_gate_up, w_down, b_down, ln2_g, ln2_b):
    assert w_in.shape[0] == DEPTH == 1
    bp, lp, d = x_prompt.shape
    bs, ls, _ = x_sample.shape
    tp, ts = bp * lp, bs * ls
    t = tp + ts
    keep = min(max(w for w, _ in DILATION_PAIRS), lp)

    xa = jnp.concatenate([x_prompt.reshape(tp, d), x_sample.reshape(ts, d)], axis=0)
    xb = xa.astype(BF16)
    wi = w_in[0]
    c1, c2 = P1_WIDTH, P1_WIDTH + P2_WIDTH
    p1 = _matmul(xb, wi[:, :c1].astype(BF16), 768, 1024, F32, "inproj_hgrn")
    p2 = _matmul(xb, wi[:, c1:c2].astype(BF16), 768, 512, F32, "inproj_attn")
    p3 = _matmul(xb, wi[:, c2:].astype(BF16), 768, 1024, F32, "inproj_gate")

    oa_p, st_p = _hgrn(p1, hgrn_lb_logits, hgrn_norm_w[0], None, bp, lp, 0)
    oa_s, st_s = _hgrn(p1, hgrn_lb_logits, hgrn_norm_w[0],
                       state_hgrn.reshape(bs, HG_HEADS, HG_DIM, HG_DIM), bs, ls, tp)

    n = jnp.arange(1, ATT_Q_HEADS + 1, dtype=F32)
    slopes = jnp.power(2.0, -8.0 * n / ATT_Q_HEADS).reshape(N_GROUPS, ATT_KV_HEADS)
    ctx = cache_attn_k.shape[2]
    ob_p = _attn_prompt(p2, slopes, bp, lp)
    ob_s = _attn_sample(p2, cache_attn_k.reshape(bs, ctx, ATT_KV_WIDTH),
                        cache_attn_v.reshape(bs, ctx, ATT_KV_WIDTH), slopes, bs, ls, tp)

    oa = jnp.concatenate([oa_p.reshape(tp, HG_WIDTH), oa_s.reshape(ts, HG_WIDTH)], axis=0)
    ob = jnp.concatenate([ob_p, ob_s.reshape(ts, ATT_KV_WIDTH)], axis=0)
    wr = w_router[0]
    wr0 = wr.astype(BF16)
    wr1 = (wr - wr0.astype(F32)).astype(BF16)
    h, top_e, gate, lrank, tilecnt = _merge(
        xa, oa, ob, p3, w_branch_a[0].astype(BF16), w_branch_b[0].astype(BF16),
        w_out[0].astype(BF16), ln1_g, ln1_b, wr0, wr1, b_router)

    tok_pad, dest, blk_expert, n_active, nreal = _route(top_e, lrank, tilecnt)
    xs = _moe_gather(h, tok_pad)
    act = _moe_up(xs, w_gate_up.reshape(N_EXPERTS, d, 2 * D_FF),
                  b_gate_up.reshape(N_EXPERTS, 1, 2 * D_FF), blk_expert, n_active)
    yt = _moe_down(act, w_down.reshape(N_EXPERTS, D_FF, d), b_down.reshape(N_EXPERTS, 1, d),
                   blk_expert, n_active, nreal, dest, t * TOP_K)
    y = _combine(h, yt, gate, ln2_g, ln2_b)

    k_new = p2[:, ATT_Q_HEADS * ATT_HEAD_DIM:(ATT_Q_HEADS + ATT_KV_HEADS) * ATT_HEAD_DIM]
    v_new = p2[:, (ATT_Q_HEADS + ATT_KV_HEADS) * ATT_HEAD_DIM:]

    def kv_prompt(a):
        return a[:tp].reshape(bp, lp, ATT_KV_HEADS, ATT_HEAD_DIM)[:, lp - keep:][None]

    def kv_sample(a):
        return a[tp:].reshape(bs, ls, ATT_KV_HEADS, ATT_HEAD_DIM)[None]

    return (y[:tp].reshape(bp, lp, d), y[tp:].reshape(bs, ls, d),
            kv_prompt(k_new), kv_prompt(v_new), st_p[None],
            kv_sample(k_new), kv_sample(v_new), st_s[None])
```

```python
import functools
import math

import jax
import jax.numpy as jnp
from jax import lax
from jax.experimental import pallas as pl
from jax.experimental.pallas import tpu as pltpu

F32 = jnp.float32
BF16 = jnp.bfloat16
I32 = jnp.int32

D_MODEL = 2048
DEPTH = 1
HG_HEADS = 8
HG_DIM = 128
HG_WIDTH = HG_HEADS * HG_DIM
HG_CHUNK = 64
ATT_HEAD_DIM = 128
DILATION_PAIRS = ((128, 1), (512, 4), (2048, 16))
N_GROUPS = len(DILATION_PAIRS)
ATT_KV_HEADS = 4
ATT_Q_HEADS = N_GROUPS * ATT_KV_HEADS
ATT_TAPS = 128
ATT_KV_WIDTH = ATT_KV_HEADS * ATT_HEAD_DIM
ATT_SCALE = ATT_HEAD_DIM ** -0.5
N_EXPERTS = 32
TOP_K = 4
D_FF = D_MODEL
SWIGLU_ALPHA = 1.702
SWIGLU_LIMIT = 7.0
LN_EPS = 1e-5
RMS_EPS = 1e-5
DEEPNORM_ALPHA = (2 * DEPTH) ** 0.25

P1_WIDTH = 4 * HG_WIDTH
P2_WIDTH = (ATT_Q_HEADS + 2 * ATT_KV_HEADS) * ATT_HEAD_DIM
P3_WIDTH = 2 * D_MODEL

LANES = 128
SUBLANES = 8
VMEM_LIMIT = 56 * 1024 * 1024

TOKEN_TILE = 256
MOE_BLOCK = 256
MOE_FT = 1024
NEG = -1e30

_NT = (((1,), (1,)), ((), ()))
_TN = (((0,), (0,)), ((), ()))


def _cparams(sem):
    return pltpu.CompilerParams(dimension_semantics=sem, vmem_limit_bytes=VMEM_LIMIT)


def _sigmoid(x):
    return 1.0 / (1.0 + jnp.exp(-x))


def _mm_kernel(x_ref, w_ref, o_ref):
    o_ref[...] = jnp.dot(x_ref[...], w_ref[...], preferred_element_type=F32).astype(o_ref.dtype)


def _matmul(x, w, tm, tn, out_dtype, name):
    m, k = x.shape
    n = w.shape[1]
    assert m % tm == 0 and n % tn == 0
    return pl.pallas_call(
        _mm_kernel,
        grid=(n // tn, m // tm),
        in_specs=[pl.BlockSpec((tm, k), lambda j, i: (i, 0)),
                  pl.BlockSpec((k, tn), lambda j, i: (0, j))],
        out_specs=pl.BlockSpec((tm, tn), lambda j, i: (i, j)),
        out_shape=jax.ShapeDtypeStruct((m, n), out_dtype),
        compiler_params=_cparams(("parallel", "parallel")),
        name=name,
    )(x, w)


def _level_ref(g_scr, m, sl, chunk, sub):
    pieces = []
    for j in range(chunk // SUBLANES):
        if 2 * m >= SUBLANES:
            row = ((SUBLANES * j) // (2 * m)) * (2 * m) + m - 1
            pieces.append(jnp.broadcast_to(g_scr[row:row + 1, sl], (SUBLANES, LANES)))
        else:
            acc = None
            for u in range(SUBLANES // (2 * m)):
                row = SUBLANES * j + 2 * m * u + m - 1
                b = jnp.broadcast_to(g_scr[row:row + 1, sl], (SUBLANES, LANES))
                acc = b if acc is None else jnp.where(sub // (2 * m) == u, b, acc)
            pieces.append(acc)
    return pieces[0] if len(pieces) == 1 else jnp.concatenate(pieces, axis=0)


def _hgrn_kernel(lbl_ref, nw_ref, hq_ref, hf_ref, hi_ref, hg_ref, *rest, chunk, has_s0):
    if has_s0:
        s0_ref, o_ref, sout_ref, st_scr, g_scr = rest
    else:
        o_ref, sout_ref, st_scr, g_scr = rest
    c = pl.program_id(1)
    nc = pl.num_programs(1)

    @pl.when(c == 0)
    def _init():
        for h in range(HG_HEADS):
            if has_s0:
                st_scr[h] = s0_ref[0, h].T
            else:
                st_scr[h] = jnp.zeros((HG_DIM, HG_DIM), F32)

    lbl = lbl_ref[...]
    ex = jnp.exp(lbl - jnp.max(lbl, axis=0, keepdims=True))
    lb = ex[0:1] / jnp.sum(ex, axis=0, keepdims=True)

    f = lb + (1.0 - lb) * _sigmoid(hf_ref[...])
    logf = jnp.log(f)
    kall = 1.0 - f

    r_cc = lax.broadcasted_iota(I32, (chunk, chunk), 0)
    c_cc = lax.broadcasted_iota(I32, (chunk, chunk), 1)
    tri = jnp.where(r_cc >= c_cc, 1.0, 0.0).astype(BF16)
    p0 = logf.astype(BF16)
    r1 = logf - p0.astype(F32)
    p1 = r1.astype(BF16)
    p2 = (r1 - p1.astype(F32)).astype(BF16)
    g_all = (jnp.dot(tri, p0, preferred_element_type=F32)
             + jnp.dot(tri, p1, preferred_element_type=F32)
             + jnp.dot(tri, p2, preferred_element_type=F32))
    g_scr[...] = g_all

    sub = lax.broadcasted_iota(I32, (SUBLANES, LANES), 0)
    row_c = lax.broadcasted_iota(I32, (chunk, LANES), 0)
    nw = nw_ref[...]

    for h in range(HG_HEADS):
        sl = slice(h * HG_DIM, (h + 1) * HG_DIM)
        hq = hq_ref[:, sl]
        q = hq * _sigmoid(hq)
        k = kall[:, sl]
        g = g_all[:, sl]
        v = hi_ref[:, sl]
        vb = v.astype(BF16)

        p = lax.dot_general(q.astype(BF16), k.astype(BF16), _NT, preferred_element_type=F32)
        a = jnp.where(r_cc == c_cc, p, 0.0)
        m = 1
        while m < chunk:
            ref = _level_ref(g_scr, m, sl, chunk, sub)
            e = jnp.exp(-jnp.abs(g - ref))
            odd = ((row_c // m) & 1) == 1
            x = (jnp.where(odd, q, k) * e).astype(BF16)
            p = lax.dot_general(x, x, _NT, preferred_element_type=F32)
            pair = (r_cc // (2 * m)) == (c_cc // (2 * m))
            sel = jnp.where(pair, ((r_cc // m) & 1) - ((c_cc // m) & 1), 0) == 1
            a = jnp.where(sel, p, a)
            m *= 2

        st = st_scr[h]
        o = jnp.dot(a.astype(BF16), vb, preferred_element_type=F32)
        qe = (q * jnp.exp(g)).astype(BF16)
        o = o + lax.dot_general(qe, st.astype(BF16), _NT, preferred_element_type=F32)

        ms = jnp.mean(o * o, axis=-1, keepdims=True)
        hg = hg_ref[:, sl]
        out = o * lax.rsqrt(ms + RMS_EPS) * nw * (hg * _sigmoid(hg))
        o_ref[0, :, sl] = out.astype(o_ref.dtype)

        g_last = g[chunk - 1:chunk, :]
        kd = (k * jnp.exp(g_last - g)).astype(BF16)
        st_scr[h] = st * jnp.exp(g_last) + lax.dot_general(vb, kd, _TN, preferred_element_type=F32)

    @pl.when(c == nc - 1)
    def _fin():
        for h in range(HG_HEADS):
            sout_ref[0, h] = st_scr[h].T


def _hgrn(p1, lb_logits, norm_w, s0, batch, seq, row0):
    chunk = math.gcd(seq, HG_CHUNK)
    nchunk = seq // chunk
    assert row0 % chunk == 0
    rb0 = row0 // chunk

    def col(idx):
        return pl.BlockSpec((chunk, HG_WIDTH), lambda b, c: (rb0 + b * nchunk + c, idx))

    in_specs = [pl.BlockSpec(lb_logits.shape, lambda b, c: (0, 0)),
                pl.BlockSpec((1, HG_DIM), lambda b, c: (0, 0)),
                col(0), col(1), col(2), col(3)]
    args = [lb_logits, norm_w.reshape(1, HG_DIM), p1, p1, p1, p1]
    if s0 is not None:
        in_specs.append(pl.BlockSpec((1, HG_HEADS, HG_DIM, HG_DIM), lambda b, c: (b, 0, 0, 0)))
        args.append(s0)
    return pl.pallas_call(
        functools.partial(_hgrn_kernel, chunk=chunk, has_s0=s0 is not None),
        grid=(batch, nchunk),
        in_specs=in_specs,
        out_specs=[pl.BlockSpec((1, chunk, HG_WIDTH), lambda b, c: (b, c, 0)),
                   pl.BlockSpec((1, HG_HEADS, HG_DIM, HG_DIM), lambda b, c: (b, 0, 0, 0))],
        out_shape=[jax.ShapeDtypeStruct((batch, seq, HG_WIDTH), BF16),
                   jax.ShapeDtypeStruct((batch, HG_HEADS, HG_DIM, HG_DIM), F32)],
        scratch_shapes=[pltpu.VMEM((HG_HEADS, HG_DIM, HG_DIM), F32),
                        pltpu.VMEM((chunk, HG_WIDTH), F32)],
        compiler_params=_cparams(("parallel", "arbitrary")),
        name="hgrn_seq%d" % seq,
    )(*args)


QB = 128


def _attn_prompt_kernel(slopes_ref, q0_ref, q1_ref, q2_ref, k_ref, v_ref, o_ref,
                        og_scr, lse_scr, bias_scr, *, seq):
    h = pl.program_id(1)
    il = lax.broadcasted_iota(I32, (QB, 2 * QB), 0)
    jl = lax.broadcasted_iota(I32, (QB, 2 * QB), 1)
    delta = il + QB - jl
    band = jnp.where(delta >= 0, delta, ATT_TAPS + 1) <= ATT_TAPS
    prev_half = jl < QB
    q_refs = (q0_ref, q1_ref, q2_ref)

    for g, (window, dil) in enumerate(DILATION_PAIRS):
        assert window // dil == ATT_TAPS
        slope = slopes_ref[g, h]
        bias_scr[g] = jnp.where(band, (-slope * dil) * delta.astype(F32), NEG)
        nbr = (seq // dil) // QB
        q_ref = q_refs[g]

        def rows(start, dil=dil):
            return pl.ds(start, QB) if dil == 1 else pl.ds(start, QB, stride=dil)

        def body(ib, carry, g=g, dil=dil, nbr=nbr, q_ref=q_ref, rows=rows):
            res = ib // nbr
            jb = ib - res * nbr
            qs = res + dil * (jb * QB)
            ps = res + dil * jnp.maximum(jb * QB - QB, 0)
            qv = q_ref[rows(qs), :].astype(BF16)
            kb = jnp.concatenate([k_ref[rows(ps), :], k_ref[rows(qs), :]], axis=0).astype(BF16)
            vb = jnp.concatenate([v_ref[rows(ps), :], v_ref[rows(qs), :]], axis=0).astype(BF16)
            s = lax.dot_general(qv, kb, _NT, preferred_element_type=F32) * ATT_SCALE + bias_scr[g]
            s = jnp.where(jnp.where(prev_half, jb, 1) == 0, NEG, s)
            mx = jnp.max(s, axis=-1, keepdims=True)
            p = jnp.exp(s - mx)
            l = jnp.sum(p, axis=-1, keepdims=True)
            o = jnp.dot(p.astype(BF16), vb, preferred_element_type=F32) / l
            og_scr[g, rows(qs), :] = o
            lse_scr[g, rows(qs), :] = jnp.broadcast_to(mx + jnp.log(l), (QB, LANES))
            return carry

        lax.fori_loop(0, seq // QB, body, 0)

    def merge(tb, carry):
        r = pl.ds(pl.multiple_of(tb * QB, QB), QB)
        l0, l1, l2 = lse_scr[0, r, :], lse_scr[1, r, :], lse_scr[2, r, :]
        mx = jnp.maximum(jnp.maximum(l0, l1), l2)
        w0, w1, w2 = jnp.exp(l0 - mx), jnp.exp(l1 - mx), jnp.exp(l2 - mx)
        o = (w0 * og_scr[0, r, :] + w1 * og_scr[1, r, :] + w2 * og_scr[2, r, :]) / (w0 + w1 + w2)
        o_ref[r, :] = o.astype(o_ref.dtype)
        return carry

    lax.fori_loop(0, seq // QB, merge, 0)


def _attn_prompt(p2, slopes, batch, seq):
    assert seq % (QB * max(d for _, d in DILATION_PAIRS)) == 0

    def col(fn):
        return pl.BlockSpec((seq, ATT_HEAD_DIM), lambda b, h: (b, fn(h)))

    return pl.pallas_call(
        functools.partial(_attn_prompt_kernel, seq=seq),
        grid=(batch, ATT_KV_HEADS),
        in_specs=[pl.BlockSpec(memory_space=pltpu.SMEM),
                  col(lambda h: h), col(lambda h: ATT_KV_HEADS + h), col(lambda h: 2 * ATT_KV_HEADS + h),
                  col(lambda h: ATT_Q_HEADS + h), col(lambda h: ATT_Q_HEADS + ATT_KV_HEADS + h)],
        out_specs=pl.BlockSpec((seq, ATT_HEAD_DIM), lambda b, h: (b, h)),
        out_shape=jax.ShapeDtypeStruct((batch * seq, ATT_KV_WIDTH), BF16),
        scratch_shapes=[pltpu.VMEM((N_GROUPS, seq, ATT_HEAD_DIM), F32),
                        pltpu.VMEM((N_GROUPS, seq, LANES), F32),
                        pltpu.VMEM((N_GROUPS, QB, 2 * QB), F32)],
        compiler_params=_cparams(("parallel", "parallel")),
        name="attn_prompt",
    )(slopes, p2, p2, p2, p2, p2)


def _attn_sample_kernel(slopes_ref, qa_ref, qb_ref, qc_ref, kn_ref, vn_ref, kc_ref, vc_ref, o_ref,
                        *, ctx, nq):
    rows = N_GROUPS * nq
    gi = lax.broadcasted_iota(I32, (rows, 1), 0) // nq
    dil = jnp.where(gi == 0, DILATION_PAIRS[0][1], jnp.where(gi == 1, DILATION_PAIRS[1][1], DILATION_PAIRS[2][1]))
    win = jnp.where(gi == 0, DILATION_PAIRS[0][0], jnp.where(gi == 1, DILATION_PAIRS[1][0], DILATION_PAIRS[2][0]))

    def dist_valid(shape, key0, nkeys):
        r = lax.broadcasted_iota(I32, shape, 0)
        t = r - (r // nq) * nq
        col = lax.broadcasted_iota(I32, shape, 1)
        dist = ctx + t - (col + key0)
        bad = jnp.where(dist >= 0, dist & (dil - 1), 1)
        bad = jnp.where(dist <= win, bad, 1)
        bad = jnp.where(col < nkeys, bad, 1)
        return dist.astype(F32), bad == 0

    dist_c, ok_c = dist_valid((rows, ctx), 0, ctx)
    dist_n, ok_n = dist_valid((rows, LANES), ctx, nq)
    zpad = jnp.zeros((LANES - nq, ATT_HEAD_DIM), F32)

    for h in range(ATT_KV_HEADS):
        hs = slice(h * ATT_HEAD_DIM, (h + 1) * ATT_HEAD_DIM)
        slope = jnp.where(gi == 0, slopes_ref[0, h], jnp.where(gi == 1, slopes_ref[1, h], slopes_ref[2, h]))
        qh = jnp.concatenate([qa_ref[:, hs], qb_ref[:, hs], qc_ref[:, hs]], axis=0).astype(BF16)
        kc = kc_ref[0, pl.ds(h, ctx, stride=ATT_KV_HEADS), :].astype(BF16)
        vc = vc_ref[0, pl.ds(h, ctx, stride=ATT_KV_HEADS), :].astype(BF16)
        kn = jnp.concatenate([kn_ref[:, hs], zpad], axis=0).astype(BF16)
        vn = jnp.concatenate([vn_ref[:, hs], zpad], axis=0).astype(BF16)
        sc = lax.dot_general(qh, kc, _NT, preferred_element_type=F32) * ATT_SCALE - slope * dist_c
        sn = lax.dot_general(qh, kn, _NT, preferred_element_type=F32) * ATT_SCALE - slope * dist_n
        sc = jnp.where(ok_c, sc, NEG)
        sn = jnp.where(ok_n, sn, NEG)
        mx = jnp.maximum(jnp.max(sc, axis=-1, keepdims=True), jnp.max(sn, axis=-1, keepdims=True))
        pc = jnp.exp(sc - mx)
        pn = jnp.exp(sn - mx)
        l = jnp.sum(pc, axis=-1, keepdims=True) + jnp.sum(pn, axis=-1, keepdims=True)
        o = (jnp.dot(pc.astype(BF16), vc, preferred_element_type=F32)
             + jnp.dot(pn.astype(BF16), vn, preferred_element_type=F32)) / l
        lse = mx + jnp.log(l)
        l0, l1, l2 = lse[0:nq], lse[nq:2 * nq], lse[2 * nq:3 * nq]
        m3 = jnp.maximum(jnp.maximum(l0, l1), l2)
        w0, w1, w2 = jnp.exp(l0 - m3), jnp.exp(l1 - m3), jnp.exp(l2 - m3)
        out = (w0 * o[0:nq] + w1 * o[nq:2 * nq] + w2 * o[2 * nq:3 * nq]) / (w0 + w1 + w2)
        o_ref[0, :, hs] = out.astype(o_ref.dtype)


def _attn_sample(p2, cache_k, cache_v, slopes, batch, nq, row0):
    ctx = cache_k.shape[1] // ATT_KV_HEADS
    assert row0 % nq == 0 and nq == SUBLANES and ctx >= max(w for w, _ in DILATION_PAIRS)
    rb0 = row0 // nq

    def col(idx):
        return pl.BlockSpec((nq, ATT_KV_WIDTH), lambda b: (rb0 + b, idx))

    cache_spec = pl.BlockSpec((1, ctx * ATT_KV_HEADS, ATT_HEAD_DIM), lambda b: (b, 0, 0))
    return pl.pallas_call(
        functools.partial(_attn_sample_kernel, ctx=ctx, nq=nq),
        grid=(batch,),
        in_specs=[pl.BlockSpec(memory_space=pltpu.SMEM),
                  col(0), col(1), col(2), col(3), col(4), cache_spec, cache_spec],
        out_specs=pl.BlockSpec((1, nq, ATT_KV_WIDTH), lambda b: (b, 0, 0)),
        out_shape=jax.ShapeDtypeStruct((batch, nq, ATT_KV_WIDTH), BF16),
        compiler_params=_cparams(("parallel",)),
        name="attn_sample",
    )(slopes, p2, p2, p2, p2, p2, cache_k, cache_v)


def _layer_norm(y, g, b):
    mu = jnp.mean(y, axis=-1, keepdims=True)
    yc = y - mu
    var = jnp.mean(yc * yc, axis=-1, keepdims=True)
    return yc * lax.rsqrt(var + LN_EPS) * g + b


def _merge_kernel(x_ref, oa_ref, ob_ref, ga_ref, gb_ref, wa_ref, wb_ref, wo_ref, g1_ref, b1_ref,
                  wr0_ref, wr1_ref, br_ref,
                  h_ref, tope_ref, gate_ref, lrank_ref, cnt_ref):
    tm = x_ref.shape[0]
    a = jnp.dot(oa_ref[...], wa_ref[...], preferred_element_type=F32)
    b = jnp.dot(ob_ref[...], wb_ref[...], preferred_element_type=F32)
    merged = _sigmoid(ga_ref[...]) * a + _sigmoid(gb_ref[...]) * b
    z = jnp.dot(merged.astype(BF16), wo_ref[...], preferred_element_type=F32)
    hh = _layer_norm(DEEPNORM_ALPHA * x_ref[...] + z, g1_ref[...], b1_ref[...])
    h_ref[...] = hh

    h0 = hh.astype(BF16)
    h1 = (hh - h0.astype(F32)).astype(BF16)
    logits = (jnp.dot(h0, wr0_ref[...], preferred_element_type=F32)
              + jnp.dot(h1, wr0_ref[...], preferred_element_type=F32)
              + jnp.dot(h0, wr1_ref[...], preferred_element_type=F32)) + br_ref[...]

    lane = lax.broadcasted_iota(I32, (tm, N_EXPERTS), 1)
    work = logits
    vals, idxs = [], []
    onehot = jnp.zeros((tm, N_EXPERTS), F32)
    for _ in range(TOP_K):
        mx = jnp.max(work, axis=-1, keepdims=True)
        idx = jnp.min(jnp.where(work == mx, lane, N_EXPERTS), axis=-1, keepdims=True)
        hit = lane == idx
        vals.append(mx)
        idxs.append(idx)
        onehot = jnp.where(hit, 1.0, onehot)
        work = jnp.where(hit, -jnp.inf, work)
    ex = [jnp.exp(v - vals[0]) for v in vals]
    den = ex[0] + ex[1] + ex[2] + ex[3]

    r_tt = lax.broadcasted_iota(I32, (tm, tm), 0)
    c_tt = lax.broadcasted_iota(I32, (tm, tm), 1)
    before = jnp.where(r_tt > c_tt, 1.0, 0.0).astype(BF16)
    prefix = jnp.dot(before, onehot.astype(BF16), preferred_element_type=F32)
    for j in range(TOP_K):
        tope_ref[:, j:j + 1] = idxs[j]
        gate_ref[:, j:j + 1] = ex[j] / den
        lrank_ref[:, j:j + 1] = jnp.sum(jnp.where(lane == idxs[j], prefix, 0.0), axis=-1,
                                        keepdims=True).astype(I32)
    cnt_ref[0] = jnp.sum(onehot, axis=0, keepdims=True).astype(I32)


def _merge(xa, oa, ob, p3, wa, wb, wo, g1, b1, wr0, wr1, br):
    t = xa.shape[0]
    tm = TOKEN_TILE
    nt = t // tm
    assert t % tm == 0

    def full(a):
        return pl.BlockSpec(a.shape, lambda i: (0,) * a.ndim)

    def rowblk(width, idx=0):
        return pl.BlockSpec((tm, width), lambda i: (i, idx))

    return pl.pallas_call(
        _merge_kernel,
        grid=(nt,),
        in_specs=[rowblk(D_MODEL), rowblk(HG_WIDTH), rowblk(ATT_KV_WIDTH), rowblk(D_MODEL, 0),
                  rowblk(D_MODEL, 1), full(wa), full(wb), full(wo), full(g1), full(b1),
                  full(wr0), full(wr1), full(br)],
        out_specs=[rowblk(D_MODEL), rowblk(TOP_K), rowblk(TOP_K), rowblk(TOP_K),
                   pl.BlockSpec((1, 1, N_EXPERTS), lambda i: (i, 0, 0))],
        out_shape=[jax.ShapeDtypeStruct((t, D_MODEL), F32),
                   jax.ShapeDtypeStruct((t, TOP_K), I32),
                   jax.ShapeDtypeStruct((t, TOP_K), F32),
                   jax.ShapeDtypeStruct((t, TOP_K), I32),
                   jax.ShapeDtypeStruct((nt, 1, N_EXPERTS), I32)],
        compiler_params=_cparams(("parallel",)),
        name="merge_ln1_router",
    )(xa, oa, ob, p3, p3, wa, wb, wo, g1, b1, wr0, wr1, br)


def _gather_kernel(tok_ref, h_ref, o_ref, buf_ref):
    m = o_ref.shape[0]

    def body(r, carry):
        buf_ref[pl.ds(r, 1), :] = h_ref[pl.ds(tok_ref[0, 0, r], 1), :]
        return carry

    lax.fori_loop(0, m, body, 0)
    o_ref[...] = buf_ref[...].astype(o_ref.dtype)


def _moe_gather(h, tok_pad):
    t, d = h.shape
    p = tok_pad.shape[0]
    m = MOE_BLOCK
    half = d // 2
    return pl.pallas_call(
        _gather_kernel,
        grid=(2, p // m),
        in_specs=[pl.BlockSpec((1, 1, m), lambda c, i: (i, 0, 0), memory_space=pltpu.SMEM),
                  pl.BlockSpec((t, half), lambda c, i: (0, c), pipeline_mode=pl.Buffered(1))],
        out_specs=pl.BlockSpec((m, half), lambda c, i: (i, c)),
        out_shape=jax.ShapeDtypeStruct((p, d), BF16),
        scratch_shapes=[pltpu.VMEM((m, half), F32)],
        compiler_params=_cparams(("arbitrary", "arbitrary")),
        name="moe_gather",
    )(tok_pad.reshape(p // m, 1, m), h)


def _moe_up_kernel(be_ref, na_ref, xs_ref, w_ref, b_ref, o_ref, wbf_scr):
    i = pl.program_id(1)
    changed = jnp.where(i == 0, 1, be_ref[i] - be_ref[jnp.maximum(i - 1, 0)])

    @pl.when(changed != 0)
    def _():
        wbf_scr[...] = w_ref[0].astype(BF16)

    @pl.when(i < na_ref[0])
    def _():
        m, ft = o_ref.shape[0], w_ref.shape[2]
        x = xs_ref[...]
        even = (lax.broadcasted_iota(I32, (m, LANES), 1) & 1) == 0
        for g in range(ft // (2 * LANES)):
            cols = slice(2 * g * LANES, (2 * g + 2) * LANES)
            h = jnp.dot(x, wbf_scr[:, cols], preferred_element_type=F32) + b_ref[0, :, cols]
            halves = []
            for s in range(2):
                hs = jnp.minimum(h[:, s * LANES:(s + 1) * LANES], SWIGLU_LIMIT)
                glu = hs * _sigmoid(SWIGLU_ALPHA * hs)
                lin = jnp.maximum(hs, -SWIGLU_LIMIT) + 1.0
                u = jnp.where(even, glu, lin)
                halves.append(u * pltpu.roll(u, LANES - 1, 1))
            merged = jnp.where(even, halves[0], pltpu.roll(halves[1], 1, 1))
            o_ref[:, g * LANES:(g + 1) * LANES] = merged.astype(o_ref.dtype)

    @pl.when(i >= na_ref[0])
    def _():
        o_ref[...] = jnp.zeros_like(o_ref)


def _moe_up(xs, w_gate_up, b_gate_up, blk_expert, n_active):
    p, d = xs.shape
    m = MOE_BLOCK
    ft = MOE_FT
    nf = w_gate_up.shape[2] // ft
    grid_spec = pltpu.PrefetchScalarGridSpec(
        num_scalar_prefetch=2,
        grid=(nf, p // m),
        in_specs=[pl.BlockSpec((m, d), lambda j, i, be, na: (i, 0)),
                  pl.BlockSpec((1, d, ft), lambda j, i, be, na: (be[i], 0, j)),
                  pl.BlockSpec((1, 1, ft), lambda j, i, be, na: (be[i], 0, j))],
        out_specs=pl.BlockSpec((m, ft // 2), lambda j, i, be, na: (i, j)),
        scratch_shapes=[pltpu.VMEM((d, ft), BF16)],
    )
    return pl.pallas_call(
        _moe_up_kernel,
        grid_spec=grid_spec,
        out_shape=jax.ShapeDtypeStruct((p, w_gate_up.shape[2] // 2), BF16),
        compiler_params=_cparams(("arbitrary", "arbitrary")),
        name="moe_up",
    )(blk_expert, n_active, xs, w_gate_up, b_gate_up)


def _hidden_perm():
    r = lax.broadcasted_iota(I32, (LANES, LANES), 0)
    c = lax.broadcasted_iota(I32, (LANES, LANES), 1)
    src = (r >> 1) + (r & 1) * (LANES // 2)
    return jnp.where(c == src, 1.0, 0.0).astype(BF16)


def _moe_down_kernel(be_ref, na_ref, nreal_ref, dest_ref, a_ref, wd_ref, bd_ref, yt_ref,
                     wbf_scr, y_scr, sem):
    i = pl.program_id(0)
    nb = pl.num_programs(0)
    m = a_ref.shape[0]
    slot = i % 2

    def row_copy(s, r, d):
        return pltpu.make_async_copy(y_scr.at[s, pl.ds(r, 1)], yt_ref.at[pl.ds(d, 1)], sem.at[s])

    def drain(s, count):
        @pl.when(count == m)
        def _():
            pltpu.make_async_copy(y_scr.at[s], yt_ref.at[pl.ds(0, m)], sem.at[s]).wait()

        @pl.when(count != m)
        def _():
            def wbody(_, carry):
                row_copy(s, 0, 0).wait()
                return carry
            lax.fori_loop(0, count, wbody, 0)

    @pl.when(i >= 2)
    def _():
        drain(slot, nreal_ref[i - 2])

    changed = jnp.where(i == 0, 1, be_ref[i] - be_ref[jnp.maximum(i - 1, 0)])

    @pl.when(changed != 0)
    def _():
        perm = _hidden_perm()
        for g in range(wd_ref.shape[1] // LANES):
            rows = slice(g * LANES, (g + 1) * LANES)
            wbf_scr[rows, :] = jnp.dot(perm, wd_ref[0, rows, :].astype(BF16),
                                       preferred_element_type=F32).astype(BF16)

    @pl.when(i < na_ref[0])
    def _():
        y_scr[slot] = jnp.dot(a_ref[...], wbf_scr[...], preferred_element_type=F32) + bd_ref[0]

        n = nreal_ref[i]
        unroll = 8

        def sbody8(q, carry):
            for u in range(unroll):
                r = q * unroll + u
                row_copy(slot, r, dest_ref[0, 0, r]).start()
            return carry

        def sbody1(r, carry):
            row_copy(slot, r, dest_ref[0, 0, r]).start()
            return carry

        lax.fori_loop(0, n // unroll, sbody8, 0)
        lax.fori_loop((n // unroll) * unroll, n, sbody1, 0)

    @pl.when(i == nb - 1)
    def _():
        @pl.when(i >= 1)
        def _():
            drain(1 - slot, nreal_ref[i - 1])
        drain(slot, nreal_ref[i])


def _moe_down(act, w_down, b_down, blk_expert, n_active, nreal, dest, n_rows):
    p, f = act.shape
    m = MOE_BLOCK
    d = w_down.shape[2]
    grid_spec = pltpu.PrefetchScalarGridSpec(
        num_scalar_prefetch=3,
        grid=(p // m,),
        in_specs=[pl.BlockSpec((1, 1, m), lambda i, be, na, nr: (i, 0, 0), memory_space=pltpu.SMEM),
                  pl.BlockSpec((m, f), lambda i, be, na, nr: (i, 0)),
                  pl.BlockSpec((1, f, d), lambda i, be, na, nr: (be[i], 0, 0)),
                  pl.BlockSpec((1, 1, d), lambda i, be, na, nr: (be[i], 0, 0))],
        out_specs=pl.BlockSpec(memory_space=pl.ANY),
        scratch_shapes=[pltpu.VMEM((f, d), BF16),
                        pltpu.VMEM((2, m, d), F32),
                        pltpu.SemaphoreType.DMA((2,))],
    )
    return pl.pallas_call(
        _moe_down_kernel,
        grid_spec=grid_spec,
        out_shape=jax.ShapeDtypeStruct((n_rows, d), F32),
        compiler_params=_cparams(("arbitrary",)),
        name="moe_down",
    )(blk_expert, n_active, nreal, dest.reshape(p // m, 1, m), act, w_down, b_down)


def _combine_kernel(h_ref, y0_ref, y1_ref, y2_ref, y3_ref, gate_ref, g2_ref, b2_ref, o_ref):
    ffn = gate_ref[:, 0:1] * y0_ref[...]
    for j, y_ref in enumerate((y1_ref, y2_ref, y3_ref), start=1):
        ffn = ffn + gate_ref[:, j:j + 1] * y_ref[...]
    o_ref[...] = _layer_norm(DEEPNORM_ALPHA * h_ref[...] + ffn, g2_ref[...], b2_ref[...])


def _combine(h, yt, gate, g2, b2):
    t, d = h.shape
    tm = TOKEN_TILE
    nt = t // tm

    def choice(j):
        return pl.BlockSpec((tm, d), lambda i: (j * nt + i, 0))

    return pl.pallas_call(
        _combine_kernel,
        grid=(nt,),
        in_specs=[pl.BlockSpec((tm, d), lambda i: (i, 0)),
                  choice(0), choice(1), choice(2), choice(3),
                  pl.BlockSpec((tm, TOP_K), lambda i: (i, 0)),
                  pl.BlockSpec((1, d), lambda i: (0, 0)),
                  pl.BlockSpec((1, d), lambda i: (0, 0))],
        out_specs=pl.BlockSpec((tm, d), lambda i: (i, 0)),
        out_shape=jax.ShapeDtypeStruct((t, d), F32),
        compiler_params=_cparams(("parallel",)),
        name="combine_ln2",
    )(h, yt, yt, yt, yt, gate, g2, b2)


def _route(top_e, lrank, tilecnt):
    t = top_e.shape[0]
    m = MOE_BLOCK
    n_assign = t * TOP_K
    n_blocks = n_assign // m + N_EXPERTS
    p = n_blocks * m
    tc = tilecnt.reshape(-1, N_EXPERTS)
    base = jnp.cumsum(tc, axis=0) - tc
    counts = jnp.sum(tc, axis=0)
    padded = (counts + m - 1) // m * m
    pend = jnp.cumsum(padded)
    pstart = pend - padded
    base_tok = jnp.repeat(base, TOKEN_TILE, axis=0)
    rank = lrank + jnp.take_along_axis(base_tok, top_e, axis=1)
    slot = (pstart[top_e] + rank).reshape(-1)
    a_idx = jnp.arange(n_assign, dtype=I32)
    dest = jnp.full((p,), -1, I32).at[slot].set((a_idx % TOP_K) * t + a_idx // TOP_K)
    tok_pad = jnp.where(dest >= 0, dest % t, 0)
    blk_start = jnp.arange(n_blocks, dtype=I32) * m
    blk_expert = jnp.minimum(jnp.sum((pend[None, :] <= blk_start[:, None]).astype(I32), axis=1),
                             N_EXPERTS - 1)
    n_active = (pend[-1:] // m).astype(I32)
    real_end = (pstart + counts)[blk_expert]
    nreal = jnp.where(blk_start < pend[-1], jnp.clip(real_end - blk_start, 0, m), 0).astype(I32)
    return tok_pad, dest, blk_expert, n_active, nreal


def kernel(x_prompt, x_sample, cache_attn_k, cache_attn_v, state_hgrn, w_in, hgrn_lb_logits,
           hgrn_norm_w, w_branch_a, w_branch_b, w_out, ln1_g, ln1_b, w_router, b_router,
           w_gate_up, b_gate_up, w_down, b_down, ln2_g, ln2_b):
    assert w_in.shape[0] == DEPTH == 1
    bp, lp, d = x_prompt.shape
    bs, ls, _ = x_sample.shape
    tp, ts = bp * lp, bs * ls
    t = tp + ts
    keep = min(max(w for w, _ in DILATION_PAIRS), lp)

    xa = jnp.concatenate([x_prompt.reshape(tp, d), x_sample.reshape(ts, d)], axis=0)
    xb = xa.astype(BF16)
    wi = w_in[0]
    c1, c2 = P1_WIDTH, P1_WIDTH + P2_WIDTH
    p1 = _matmul(xb, wi[:, :c1].astype(BF16), 768, 1024, F32, "inproj_hgrn")
    p2 = _matmul(xb, wi[:, c1:c2].astype(BF16), 768, 512, F32, "inproj_attn")
    p3 = _matmul(xb, wi[:, c2:].astype(BF16), 768, 1024, F32, "inproj_gate")

    oa_p, st_p = _hgrn(p1, hgrn_lb_logits, hgrn_norm_w[0], None, bp, lp, 0)
    oa_s, st_s = _hgrn(p1, hgrn_lb_logits, hgrn_norm_w[0],
                       state_hgrn.reshape(bs, HG_HEADS, HG_DIM, HG_DIM), bs, ls, tp)

    n = jnp.arange(1, ATT_Q_HEADS + 1, dtype=F32)
    slopes = jnp.power(2.0, -8.0 * n / ATT_Q_HEADS).reshape(N_GROUPS, ATT_KV_HEADS)
    ctx = cache_attn_k.shape[2]
    ob_p = _attn_prompt(p2, slopes, bp, lp)
    ob_s = _attn_sample(p2, cache_attn_k.reshape(bs, ctx * ATT_KV_HEADS, ATT_HEAD_DIM),
                        cache_attn_v.reshape(bs, ctx * ATT_KV_HEADS, ATT_HEAD_DIM), slopes, bs, ls, tp)

    oa = jnp.concatenate([oa_p.reshape(tp, HG_WIDTH), oa_s.reshape(ts, HG_WIDTH)], axis=0)
    ob = jnp.concatenate([ob_p, ob_s.reshape(ts, ATT_KV_WIDTH)], axis=0)
    wr = w_router[0]
    wr0 = wr.astype(BF16)
    wr1 = (wr - wr0.astype(F32)).astype(BF16)
    h, top_e, gate, lrank, tilecnt = _merge(
        xa, oa, ob, p3, w_branch_a[0].astype(BF16), w_branch_b[0].astype(BF16),
        w_out[0].astype(BF16), ln1_g, ln1_b, wr0, wr1, b_router)

    tok_pad, dest, blk_expert, n_active, nreal = _route(top_e, lrank, tilecnt)
    xs = _moe_gather(h, tok_pad)
    act = _moe_up(xs, w_gate_up.reshape(N_EXPERTS, d, 2 * D_FF),
                  b_gate_up.reshape(N_EXPERTS, 1, 2 * D_FF), blk_expert, n_active)
    yt = _moe_down(act, w_down.reshape(N_EXPERTS, D_FF, d), b_down.reshape(N_EXPERTS, 1, d),
                   blk_expert, n_active, nreal, dest, t * TOP_K)
    y = _combine(h, yt, gate, ln2_g, ln2_b)

    k_new = p2[:, ATT_Q_HEADS * ATT_HEAD_DIM:(ATT_Q_HEADS + ATT_KV_HEADS) * ATT_HEAD_DIM]
    v_new = p2[:, (ATT_Q_HEADS + ATT_KV_HEADS) * ATT_HEAD_DIM:]

    def kv_prompt(a):
        return a[:tp].reshape(bp, lp, ATT_KV_HEADS, ATT_HEAD_DIM)[:, lp - keep:][None]

    def kv_sample(a):
        return a[tp:].reshape(bs, ls, ATT_KV_HEADS, ATT_HEAD_DIM)[None]

    return (y[:tp].reshape(bp, lp, d), y[tp:].reshape(bs, ls, d),
            kv_prompt(k_new), kv_prompt(v_new), st_p[None],
            kv_sample(k_new), kv_sample(v_new), st_s[None])
```

```python
import functools
import math

import jax
import jax.numpy as jnp
from jax import lax
from jax.experimental import pallas as pl
from jax.experimental.pallas import tpu as pltpu

F32 = jnp.float32
BF16 = jnp.bfloat16
I32 = jnp.int32

D_MODEL = 2048
DEPTH = 1
HG_HEADS = 8
HG_DIM = 128
HG_WIDTH = HG_HEADS * HG_DIM
HG_CHUNK = 64
ATT_HEAD_DIM = 128
DILATION_PAIRS = ((128, 1), (512, 4), (2048, 16))
N_GROUPS = len(DILATION_PAIRS)
ATT_KV_HEADS = 4
ATT_Q_HEADS = N_GROUPS * ATT_KV_HEADS
ATT_TAPS = 128
ATT_KV_WIDTH = ATT_KV_HEADS * ATT_HEAD_DIM
ATT_SCALE = ATT_HEAD_DIM ** -0.5
N_EXPERTS = 32
TOP_K = 4
D_FF = D_MODEL
SWIGLU_ALPHA = 1.702
SWIGLU_LIMIT = 7.0
LN_EPS = 1e-5
RMS_EPS = 1e-5
DEEPNORM_ALPHA = (2 * DEPTH) ** 0.25

P1_WIDTH = 4 * HG_WIDTH
P2_WIDTH = (ATT_Q_HEADS + 2 * ATT_KV_HEADS) * ATT_HEAD_DIM
P3_WIDTH = 2 * D_MODEL

LANES = 128
SUBLANES = 8
VMEM_LIMIT = 56 * 1024 * 1024

TOKEN_TILE = 256
MOE_BLOCK = 256
MOE_FT = 1024
NEG = -1e30

_NT = (((1,), (1,)), ((), ()))
_TN = (((0,), (0,)), ((), ()))


def _cparams(sem):
    return pltpu.CompilerParams(dimension_semantics=sem, vmem_limit_bytes=VMEM_LIMIT)


def _sigmoid(x):
    return 1.0 / (1.0 + jnp.exp(-x))


def _mm_kernel(x_ref, w_ref, o_ref):
    o_ref[...] = jnp.dot(x_ref[...], w_ref[...], preferred_element_type=F32).astype(o_ref.dtype)


def _matmul(x, w, tm, tn, out_dtype, name):
    m, k = x.shape
    n = w.shape[1]
    assert m % tm == 0 and n % tn == 0
    return pl.pallas_call(
        _mm_kernel,
        grid=(n // tn, m // tm),
        in_specs=[pl.BlockSpec((tm, k), lambda j, i: (i, 0)),
                  pl.BlockSpec((k, tn), lambda j, i: (0, j))],
        out_specs=pl.BlockSpec((tm, tn), lambda j, i: (i, j)),
        out_shape=jax.ShapeDtypeStruct((m, n), out_dtype),
        compiler_params=_cparams(("parallel", "parallel")),
        name=name,
    )(x, w)


def _level_ref(g_scr, m, sl, chunk, sub):
    pieces = []
    for j in range(chunk // SUBLANES):
        if 2 * m >= SUBLANES:
            row = ((SUBLANES * j) // (2 * m)) * (2 * m) + m - 1
            pieces.append(jnp.broadcast_to(g_scr[row:row + 1, sl], (SUBLANES, LANES)))
        else:
            acc = None
            for u in range(SUBLANES // (2 * m)):
                row = SUBLANES * j + 2 * m * u + m - 1
                b = jnp.broadcast_to(g_scr[row:row + 1, sl], (SUBLANES, LANES))
                acc = b if acc is None else jnp.where(sub // (2 * m) == u, b, acc)
            pieces.append(acc)
    return pieces[0] if len(pieces) == 1 else jnp.concatenate(pieces, axis=0)


def _hgrn_kernel(lbl_ref, nw_ref, hq_ref, hf_ref, hi_ref, hg_ref, *rest, chunk, has_s0):
    if has_s0:
        s0_ref, o_ref, sout_ref, st_scr, g_scr = rest
    else:
        o_ref, sout_ref, st_scr, g_scr = rest
    c = pl.program_id(1)
    nc = pl.num_programs(1)

    @pl.when(c == 0)
    def _init():
        for h in range(HG_HEADS):
            if has_s0:
                st_scr[h] = s0_ref[0, h].T
            else:
                st_scr[h] = jnp.zeros((HG_DIM, HG_DIM), F32)

    lbl = lbl_ref[...]
    ex = jnp.exp(lbl - jnp.max(lbl, axis=0, keepdims=True))
    lb = ex[0:1] / jnp.sum(ex, axis=0, keepdims=True)

    f = lb + (1.0 - lb) * _sigmoid(hf_ref[...])
    logf = jnp.log(f)
    kall = 1.0 - f

    r_cc = lax.broadcasted_iota(I32, (chunk, chunk), 0)
    c_cc = lax.broadcasted_iota(I32, (chunk, chunk), 1)
    tri = jnp.where(r_cc >= c_cc, 1.0, 0.0).astype(BF16)
    p0 = logf.astype(BF16)
    r1 = logf - p0.astype(F32)
    p1 = r1.astype(BF16)
    p2 = (r1 - p1.astype(F32)).astype(BF16)
    g_all = (jnp.dot(tri, p0, preferred_element_type=F32)
             + jnp.dot(tri, p1, preferred_element_type=F32)
             + jnp.dot(tri, p2, preferred_element_type=F32))
    g_scr[...] = g_all

    sub = lax.broadcasted_iota(I32, (SUBLANES, LANES), 0)
    row_c = lax.broadcasted_iota(I32, (chunk, LANES), 0)
    nw = nw_ref[...]

    for h in range(HG_HEADS):
        sl = slice(h * HG_DIM, (h + 1) * HG_DIM)
        hq = hq_ref[:, sl]
        q = hq * _sigmoid(hq)
        k = kall[:, sl]
        g = g_all[:, sl]
        v = hi_ref[:, sl]
        vb = v.astype(BF16)

        p = lax.dot_general(q.astype(BF16), k.astype(BF16), _NT, preferred_element_type=F32)
        a = jnp.where(r_cc == c_cc, p, 0.0)
        m = 1
        while m < chunk:
            ref = _level_ref(g_scr, m, sl, chunk, sub)
            e = jnp.exp(-jnp.abs(g - ref))
            odd = ((row_c // m) & 1) == 1
            x = (jnp.where(odd, q, k) * e).astype(BF16)
            p = lax.dot_general(x, x, _NT, preferred_element_type=F32)
            pair = (r_cc // (2 * m)) == (c_cc // (2 * m))
            sel = jnp.where(pair, ((r_cc // m) & 1) - ((c_cc // m) & 1), 0) == 1
            a = jnp.where(sel, p, a)
            m *= 2

        st = st_scr[h]
        o = jnp.dot(a.astype(BF16), vb, preferred_element_type=F32)
        qe = (q * jnp.exp(g)).astype(BF16)
        o = o + lax.dot_general(qe, st.astype(BF16), _NT, preferred_element_type=F32)

        ms = jnp.mean(o * o, axis=-1, keepdims=True)
        hg = hg_ref[:, sl]
        out = o * lax.rsqrt(ms + RMS_EPS) * nw * (hg * _sigmoid(hg))
        o_ref[0, :, sl] = out.astype(o_ref.dtype)

        g_last = g[chunk - 1:chunk, :]
        kd = (k * jnp.exp(g_last - g)).astype(BF16)
        st_scr[h] = st * jnp.exp(g_last) + lax.dot_general(vb, kd, _TN, preferred_element_type=F32)

    @pl.when(c == nc - 1)
    def _fin():
        for h in range(HG_HEADS):
            sout_ref[0, h] = st_scr[h].T


def _hgrn(p1, lb_logits, norm_w, s0, batch, seq, row0):
    chunk = math.gcd(seq, HG_CHUNK)
    nchunk = seq // chunk
    assert row0 % chunk == 0
    rb0 = row0 // chunk

    def col(idx):
        return pl.BlockSpec((chunk, HG_WIDTH), lambda b, c: (rb0 + b * nchunk + c, idx))

    in_specs = [pl.BlockSpec(lb_logits.shape, lambda b, c: (0, 0)),
                pl.BlockSpec((1, HG_DIM), lambda b, c: (0, 0)),
                col(0), col(1), col(2), col(3)]
    args = [lb_logits, norm_w.reshape(1, HG_DIM), p1, p1, p1, p1]
    if s0 is not None:
        in_specs.append(pl.BlockSpec((1, HG_HEADS, HG_DIM, HG_DIM), lambda b, c: (b, 0, 0, 0)))
        args.append(s0)
    return pl.pallas_call(
        functools.partial(_hgrn_kernel, chunk=chunk, has_s0=s0 is not None),
        grid=(batch, nchunk),
        in_specs=in_specs,
        out_specs=[pl.BlockSpec((1, chunk, HG_WIDTH), lambda b, c: (b, c, 0)),
                   pl.BlockSpec((1, HG_HEADS, HG_DIM, HG_DIM), lambda b, c: (b, 0, 0, 0))],
        out_shape=[jax.ShapeDtypeStruct((batch, seq, HG_WIDTH), BF16),
                   jax.ShapeDtypeStruct((batch, HG_HEADS, HG_DIM, HG_DIM), F32)],
        scratch_shapes=[pltpu.VMEM((HG_HEADS, HG_DIM, HG_DIM), F32),
                        pltpu.VMEM((chunk, HG_WIDTH), F32)],
        compiler_params=_cparams(("parallel", "arbitrary")),
        name="hgrn_seq%d" % seq,
    )(*args)


QB = 128


def _attn_prompt_kernel(slopes_ref, q0_ref, q1_ref, q2_ref, k_ref, v_ref, o_ref,
                        og_scr, lse_scr, bias_scr, *, seq):
    h = pl.program_id(1)
    il = lax.broadcasted_iota(I32, (QB, 2 * QB), 0)
    jl = lax.broadcasted_iota(I32, (QB, 2 * QB), 1)
    delta = il + QB - jl
    band = jnp.where(delta >= 0, delta, ATT_TAPS + 1) <= ATT_TAPS
    prev_half = jl < QB
    q_refs = (q0_ref, q1_ref, q2_ref)

    for g, (window, dil) in enumerate(DILATION_PAIRS):
        assert window // dil == ATT_TAPS
        slope = slopes_ref[g, h]
        bias_scr[g] = jnp.where(band, (-slope * dil) * delta.astype(F32), NEG)
        nbr = (seq // dil) // QB
        q_ref = q_refs[g]

        def rows(start, dil=dil):
            return pl.ds(start, QB) if dil == 1 else pl.ds(start, QB, stride=dil)

        def body(ib, carry, g=g, dil=dil, nbr=nbr, q_ref=q_ref, rows=rows):
            res = ib // nbr
            jb = ib - res * nbr
            qs = res + dil * (jb * QB)
            ps = res + dil * jnp.maximum(jb * QB - QB, 0)
            qv = q_ref[rows(qs), :].astype(BF16)
            kb = jnp.concatenate([k_ref[rows(ps), :], k_ref[rows(qs), :]], axis=0).astype(BF16)
            vb = jnp.concatenate([v_ref[rows(ps), :], v_ref[rows(qs), :]], axis=0).astype(BF16)
            s = lax.dot_general(qv, kb, _NT, preferred_element_type=F32) * ATT_SCALE + bias_scr[g]
            s = jnp.where(jnp.where(prev_half, jb, 1) == 0, NEG, s)
            mx = jnp.max(s, axis=-1, keepdims=True)
            p = jnp.exp(s - mx)
            l = jnp.sum(p, axis=-1, keepdims=True)
            o = jnp.dot(p.astype(BF16), vb, preferred_element_type=F32) / l
            og_scr[g, rows(qs), :] = o
            lse_scr[g, rows(qs), :] = jnp.broadcast_to(mx + jnp.log(l), (QB, LANES))
            return carry

        lax.fori_loop(0, seq // QB, body, 0)

    def merge(tb, carry):
        r = pl.ds(pl.multiple_of(tb * QB, QB), QB)
        l0, l1, l2 = lse_scr[0, r, :], lse_scr[1, r, :], lse_scr[2, r, :]
        mx = jnp.maximum(jnp.maximum(l0, l1), l2)
        w0, w1, w2 = jnp.exp(l0 - mx), jnp.exp(l1 - mx), jnp.exp(l2 - mx)
        o = (w0 * og_scr[0, r, :] + w1 * og_scr[1, r, :] + w2 * og_scr[2, r, :]) / (w0 + w1 + w2)
        o_ref[r, :] = o.astype(o_ref.dtype)
        return carry

    lax.fori_loop(0, seq // QB, merge, 0)


def _attn_prompt(p2, slopes, batch, seq):
    assert seq % (QB * max(d for _, d in DILATION_PAIRS)) == 0

    def col(fn):
        return pl.BlockSpec((seq, ATT_HEAD_DIM), lambda b, h: (b, fn(h)))

    return pl.pallas_call(
        functools.partial(_attn_prompt_kernel, seq=seq),
        grid=(batch, ATT_KV_HEADS),
        in_specs=[pl.BlockSpec(memory_space=pltpu.SMEM),
                  col(lambda h: h), col(lambda h: ATT_KV_HEADS + h), col(lambda h: 2 * ATT_KV_HEADS + h),
                  col(lambda h: ATT_Q_HEADS + h), col(lambda h: ATT_Q_HEADS + ATT_KV_HEADS + h)],
        out_specs=pl.BlockSpec((seq, ATT_HEAD_DIM), lambda b, h: (b, h)),
        out_shape=jax.ShapeDtypeStruct((batch * seq, ATT_KV_WIDTH), BF16),
        scratch_shapes=[pltpu.VMEM((N_GROUPS, seq, ATT_HEAD_DIM), F32),
                        pltpu.VMEM((N_GROUPS, seq, LANES), F32),
                        pltpu.VMEM((N_GROUPS, QB, 2 * QB), F32)],
        compiler_params=_cparams(("parallel", "parallel")),
        name="attn_prompt",
    )(slopes, p2, p2, p2, p2, p2)


def _attn_sample_kernel(slopes_ref, qa_ref, qb_ref, qc_ref, kn_ref, vn_ref, kc_ref, vc_ref, o_ref,
                        *, ctx, nq):
    rows = N_GROUPS * nq
    gi = lax.broadcasted_iota(I32, (rows, 1), 0) // nq
    dil = jnp.where(gi == 0, DILATION_PAIRS[0][1], jnp.where(gi == 1, DILATION_PAIRS[1][1], DILATION_PAIRS[2][1]))
    win = jnp.where(gi == 0, DILATION_PAIRS[0][0], jnp.where(gi == 1, DILATION_PAIRS[1][0], DILATION_PAIRS[2][0]))

    def dist_valid(shape, key0, nkeys):
        r = lax.broadcasted_iota(I32, shape, 0)
        t = r - (r // nq) * nq
        col = lax.broadcasted_iota(I32, shape, 1)
        dist = ctx + t - (col + key0)
        bad = jnp.where(dist >= 0, dist & (dil - 1), 1)
        bad = jnp.where(dist <= win, bad, 1)
        bad = jnp.where(col < nkeys, bad, 1)
        return dist.astype(F32), bad == 0

    dist_c, ok_c = dist_valid((rows, ctx), 0, ctx)
    dist_n, ok_n = dist_valid((rows, LANES), ctx, nq)
    zpad = jnp.zeros((LANES - nq, ATT_HEAD_DIM), F32)

    for h in range(ATT_KV_HEADS):
        hs = slice(h * ATT_HEAD_DIM, (h + 1) * ATT_HEAD_DIM)
        slope = jnp.where(gi == 0, slopes_ref[0, h], jnp.where(gi == 1, slopes_ref[1, h], slopes_ref[2, h]))
        qh = jnp.concatenate([qa_ref[:, hs], qb_ref[:, hs], qc_ref[:, hs]], axis=0).astype(BF16)
        kc = kc_ref[0, pl.ds(h, ctx, stride=ATT_KV_HEADS), :].astype(BF16)
        vc = vc_ref[0, pl.ds(h, ctx, stride=ATT_KV_HEADS), :].astype(BF16)
        kn = jnp.concatenate([kn_ref[:, hs], zpad], axis=0).astype(BF16)
        vn = jnp.concatenate([vn_ref[:, hs], zpad], axis=0).astype(BF16)
        sc = lax.dot_general(qh, kc, _NT, preferred_element_type=F32) * ATT_SCALE - slope * dist_c
        sn = lax.dot_general(qh, kn, _NT, preferred_element_type=F32) * ATT_SCALE - slope * dist_n
        sc = jnp.where(ok_c, sc, NEG)
        sn = jnp.where(ok_n, sn, NEG)
        mx = jnp.maximum(jnp.max(sc, axis=-1, keepdims=True), jnp.max(sn, axis=-1, keepdims=True))
        pc = jnp.exp(sc - mx)
        pn = jnp.exp(sn - mx)
        l = jnp.sum(pc, axis=-1, keepdims=True) + jnp.sum(pn, axis=-1, keepdims=True)
        o = (jnp.dot(pc.astype(BF16), vc, preferred_element_type=F32)
             + jnp.dot(pn.astype(BF16), vn, preferred_element_type=F32)) / l
        lse = mx + jnp.log(l)
        l0, l1, l2 = lse[0:nq], lse[nq:2 * nq], lse[2 * nq:3 * nq]
        m3 = jnp.maximum(jnp.maximum(l0, l1), l2)
        w0, w1, w2 = jnp.exp(l0 - m3), jnp.exp(l1 - m3), jnp.exp(l2 - m3)
        out = (w0 * o[0:nq] + w1 * o[nq:2 * nq] + w2 * o[2 * nq:3 * nq]) / (w0 + w1 + w2)
        o_ref[0, :, hs] = out.astype(o_ref.dtype)


def _attn_sample(p2, cache_k, cache_v, slopes, batch, nq, row0):
    ctx = cache_k.shape[1] // ATT_KV_HEADS
    assert row0 % nq == 0 and nq == SUBLANES and ctx >= max(w for w, _ in DILATION_PAIRS)
    rb0 = row0 // nq

    def col(idx):
        return pl.BlockSpec((nq, ATT_KV_WIDTH), lambda b: (rb0 + b, idx))

    cache_spec = pl.BlockSpec((1, ctx * ATT_KV_HEADS, ATT_HEAD_DIM), lambda b: (b, 0, 0))
    return pl.pallas_call(
        functools.partial(_attn_sample_kernel, ctx=ctx, nq=nq),
        grid=(batch,),
        in_specs=[pl.BlockSpec(memory_space=pltpu.SMEM),
                  col(0), col(1), col(2), col(3), col(4), cache_spec, cache_spec],
        out_specs=pl.BlockSpec((1, nq, ATT_KV_WIDTH), lambda b: (b, 0, 0)),
        out_shape=jax.ShapeDtypeStruct((batch, nq, ATT_KV_WIDTH), BF16),
        compiler_params=_cparams(("parallel",)),
        name="attn_sample",
    )(slopes, p2, p2, p2, p2, p2, cache_k, cache_v)


def _layer_norm(y, g, b):
    mu = jnp.mean(y, axis=-1, keepdims=True)
    yc = y - mu
    var = jnp.mean(yc * yc, axis=-1, keepdims=True)
    return yc * lax.rsqrt(var + LN_EPS) * g + b


def _merge_kernel(x_ref, oa_ref, ob_ref, ga_ref, gb_ref, wa_ref, wb_ref, wo_ref, g1_ref, b1_ref,
                  wr0_ref, wr1_ref, br_ref,
                  h_ref, tope_ref, gate_ref, lrank_ref, cnt_ref):
    tm = x_ref.shape[0]
    a = jnp.dot(oa_ref[...], wa_ref[...], preferred_element_type=F32)
    b = jnp.dot(ob_ref[...], wb_ref[...], preferred_element_type=F32)
    merged = _sigmoid(ga_ref[...]) * a + _sigmoid(gb_ref[...]) * b
    z = jnp.dot(merged.astype(BF16), wo_ref[...], preferred_element_type=F32)
    hh = _layer_norm(DEEPNORM_ALPHA * x_ref[...] + z, g1_ref[...], b1_ref[...])
    h_ref[...] = hh

    h0 = hh.astype(BF16)
    h1 = (hh - h0.astype(F32)).astype(BF16)
    logits = (jnp.dot(h0, wr0_ref[...], preferred_element_type=F32)
              + jnp.dot(h1, wr0_ref[...], preferred_element_type=F32)
              + jnp.dot(h0, wr1_ref[...], preferred_element_type=F32)) + br_ref[...]

    lane = lax.broadcasted_iota(I32, (tm, N_EXPERTS), 1)
    work = logits
    vals, idxs = [], []
    onehot = jnp.zeros((tm, N_EXPERTS), F32)
    for _ in range(TOP_K):
        mx = jnp.max(work, axis=-1, keepdims=True)
        idx = jnp.min(jnp.where(work == mx, lane, N_EXPERTS), axis=-1, keepdims=True)
        hit = lane == idx
        vals.append(mx)
        idxs.append(idx)
        onehot = jnp.where(hit, 1.0, onehot)
        work = jnp.where(hit, -jnp.inf, work)
    ex = [jnp.exp(v - vals[0]) for v in vals]
    den = ex[0] + ex[1] + ex[2] + ex[3]

    r_tt = lax.broadcasted_iota(I32, (tm, tm), 0)
    c_tt = lax.broadcasted_iota(I32, (tm, tm), 1)
    before = jnp.where(r_tt > c_tt, 1.0, 0.0).astype(BF16)
    prefix = jnp.dot(before, onehot.astype(BF16), preferred_element_type=F32)
    for j in range(TOP_K):
        tope_ref[:, j:j + 1] = idxs[j]
        gate_ref[:, j:j + 1] = ex[j] / den
        lrank_ref[:, j:j + 1] = jnp.sum(jnp.where(lane == idxs[j], prefix, 0.0), axis=-1,
                                        keepdims=True).astype(I32)
    cnt_ref[0] = jnp.sum(onehot, axis=0, keepdims=True).astype(I32)


def _merge(xa, oa, ob, p3, wa, wb, wo, g1, b1, wr0, wr1, br):
    t = xa.shape[0]
    tm = TOKEN_TILE
    nt = t // tm
    assert t % tm == 0

    def full(a):
        return pl.BlockSpec(a.shape, lambda i: (0,) * a.ndim)

    def rowblk(width, idx=0):
        return pl.BlockSpec((tm, width), lambda i: (i, idx))

    return pl.pallas_call(
        _merge_kernel,
        grid=(nt,),
        in_specs=[rowblk(D_MODEL), rowblk(HG_WIDTH), rowblk(ATT_KV_WIDTH), rowblk(D_MODEL, 0),
                  rowblk(D_MODEL, 1), full(wa), full(wb), full(wo), full(g1), full(b1),
                  full(wr0), full(wr1), full(br)],
        out_specs=[rowblk(D_MODEL), rowblk(TOP_K), rowblk(TOP_K), rowblk(TOP_K),
                   pl.BlockSpec((1, 1, N_EXPERTS), lambda i: (i, 0, 0))],
        out_shape=[jax.ShapeDtypeStruct((t, D_MODEL), F32),
                   jax.ShapeDtypeStruct((t, TOP_K), I32),
                   jax.ShapeDtypeStruct((t, TOP_K), F32),
                   jax.ShapeDtypeStruct((t, TOP_K), I32),
                   jax.ShapeDtypeStruct((nt, 1, N_EXPERTS), I32)],
        compiler_params=_cparams(("parallel",)),
        name="merge_ln1_router",
    )(xa, oa, ob, p3, p3, wa, wb, wo, g1, b1, wr0, wr1, br)


def _gather_kernel(tok_ref, h_ref, o_ref, buf_ref):
    m = o_ref.shape[0]

    def body(q, carry):
        for u in range(SUBLANES):
            r = q * SUBLANES + u
            buf_ref[pl.ds(r, 1), :] = h_ref[pl.ds(tok_ref[0, 0, r], 1), :]
        return carry

    lax.fori_loop(0, m // SUBLANES, body, 0)
    o_ref[...] = buf_ref[...].astype(o_ref.dtype)


def _moe_gather(h, tok_pad):
    t, d = h.shape
    p = tok_pad.shape[0]
    m = MOE_BLOCK
    half = d // 2
    return pl.pallas_call(
        _gather_kernel,
        grid=(2, p // m),
        in_specs=[pl.BlockSpec((1, 1, m), lambda c, i: (i, 0, 0), memory_space=pltpu.SMEM),
                  pl.BlockSpec((t, half), lambda c, i: (0, c), pipeline_mode=pl.Buffered(1))],
        out_specs=pl.BlockSpec((m, half), lambda c, i: (i, c)),
        out_shape=jax.ShapeDtypeStruct((p, d), BF16),
        scratch_shapes=[pltpu.VMEM((m, half), F32)],
        compiler_params=_cparams(("arbitrary", "arbitrary")),
        name="moe_gather",
    )(tok_pad.reshape(p // m, 1, m), h)


def _moe_up_kernel(be_ref, na_ref, xs_ref, w_ref, b_ref, o_ref, wbf_scr):
    i = pl.program_id(1)
    changed = jnp.where(i == 0, 1, be_ref[i] - be_ref[jnp.maximum(i - 1, 0)])

    @pl.when(changed != 0)
    def _():
        wbf_scr[...] = w_ref[0].astype(BF16)

    @pl.when(i < na_ref[0])
    def _():
        m, ft = o_ref.shape[0], w_ref.shape[2]
        x = xs_ref[...]
        even = (lax.broadcasted_iota(I32, (m, LANES), 1) & 1) == 0
        for g in range(ft // (2 * LANES)):
            cols = slice(2 * g * LANES, (2 * g + 2) * LANES)
            h = jnp.dot(x, wbf_scr[:, cols], preferred_element_type=F32) + b_ref[0, :, cols]
            ha, hb = h[:, :LANES], h[:, LANES:]
            glu = jnp.minimum(jnp.where(even, ha, pltpu.roll(hb, 1, 1)), SWIGLU_LIMIT)
            lin = jnp.where(even, pltpu.roll(ha, LANES - 1, 1), hb)
            lin = jnp.clip(lin, -SWIGLU_LIMIT, SWIGLU_LIMIT) + 1.0
            act = glu * _sigmoid(SWIGLU_ALPHA * glu) * lin
            o_ref[:, g * LANES:(g + 1) * LANES] = act.astype(o_ref.dtype)

    @pl.when(i >= na_ref[0])
    def _():
        o_ref[...] = jnp.zeros_like(o_ref)


def _moe_up(xs, w_gate_up, b_gate_up, blk_expert, n_active):
    p, d = xs.shape
    m = MOE_BLOCK
    ft = MOE_FT
    nf = w_gate_up.shape[2] // ft
    grid_spec = pltpu.PrefetchScalarGridSpec(
        num_scalar_prefetch=2,
        grid=(nf, p // m),
        in_specs=[pl.BlockSpec((m, d), lambda j, i, be, na: (i, 0)),
                  pl.BlockSpec((1, d, ft), lambda j, i, be, na: (be[i], 0, j)),
                  pl.BlockSpec((1, 1, ft), lambda j, i, be, na: (be[i], 0, j))],
        out_specs=pl.BlockSpec((m, ft // 2), lambda j, i, be, na: (i, j)),
        scratch_shapes=[pltpu.VMEM((d, ft), BF16)],
    )
    return pl.pallas_call(
        _moe_up_kernel,
        grid_spec=grid_spec,
        out_shape=jax.ShapeDtypeStruct((p, w_gate_up.shape[2] // 2), BF16),
        compiler_params=_cparams(("arbitrary", "arbitrary")),
        name="moe_up",
    )(blk_expert, n_active, xs, w_gate_up, b_gate_up)


def _hidden_perm():
    r = lax.broadcasted_iota(I32, (LANES, LANES), 0)
    c = lax.broadcasted_iota(I32, (LANES, LANES), 1)
    src = (r >> 1) + (r & 1) * (LANES // 2)
    return jnp.where(c == src, 1.0, 0.0).astype(BF16)


def _moe_down_kernel(be_ref, na_ref, nreal_ref, dest_ref, a_ref, wd_ref, bd_ref, yt_ref,
                     wbf_scr, y_scr, sem):
    i = pl.program_id(0)
    nb = pl.num_programs(0)
    m = a_ref.shape[0]
    slot = i % 2

    def row_copy(s, r, d):
        return pltpu.make_async_copy(y_scr.at[s, pl.ds(r, 1)], yt_ref.at[pl.ds(d, 1)], sem.at[s])

    def drain(s, count):
        @pl.when(count == m)
        def _():
            pltpu.make_async_copy(y_scr.at[s], yt_ref.at[pl.ds(0, m)], sem.at[s]).wait()

        @pl.when(count != m)
        def _():
            def wbody(_, carry):
                row_copy(s, 0, 0).wait()
                return carry
            lax.fori_loop(0, count, wbody, 0)

    @pl.when(i >= 2)
    def _():
        drain(slot, nreal_ref[i - 2])

    changed = jnp.where(i == 0, 1, be_ref[i] - be_ref[jnp.maximum(i - 1, 0)])

    @pl.when(changed != 0)
    def _():
        perm = _hidden_perm()
        for g in range(wd_ref.shape[1] // LANES):
            rows = slice(g * LANES, (g + 1) * LANES)
            wbf_scr[rows, :] = jnp.dot(perm, wd_ref[0, rows, :].astype(BF16),
                                       preferred_element_type=F32).astype(BF16)

    @pl.when(i < na_ref[0])
    def _():
        y_scr[slot] = jnp.dot(a_ref[...], wbf_scr[...], preferred_element_type=F32) + bd_ref[0]

        n = nreal_ref[i]
        unroll = 8

        def sbody8(q, carry):
            for u in range(unroll):
                r = q * unroll + u
                row_copy(slot, r, dest_ref[0, 0, r]).start()
            return carry

        def sbody1(r, carry):
            row_copy(slot, r, dest_ref[0, 0, r]).start()
            return carry

        lax.fori_loop(0, n // unroll, sbody8, 0)
        lax.fori_loop((n // unroll) * unroll, n, sbody1, 0)

    @pl.when(i == nb - 1)
    def _():
        @pl.when(i >= 1)
        def _():
            drain(1 - slot, nreal_ref[i - 1])
        drain(slot, nreal_ref[i])


def _moe_down(act, w_down, b_down, blk_expert, n_active, nreal, dest, n_rows):
    p, f = act.shape
    m = MOE_BLOCK
    d = w_down.shape[2]
    grid_spec = pltpu.PrefetchScalarGridSpec(
        num_scalar_prefetch=3,
        grid=(p // m,),
        in_specs=[pl.BlockSpec((1, 1, m), lambda i, be, na, nr: (i, 0, 0), memory_space=pltpu.SMEM),
                  pl.BlockSpec((m, f), lambda i, be, na, nr: (i, 0)),
                  pl.BlockSpec((1, f, d), lambda i, be, na, nr: (be[i], 0, 0)),
                  pl.BlockSpec((1, 1, d), lambda i, be, na, nr: (be[i], 0, 0))],
        out_specs=pl.BlockSpec(memory_space=pl.ANY),
        scratch_shapes=[pltpu.VMEM((f, d), BF16),
                        pltpu.VMEM((2, m, d), F32),
                        pltpu.SemaphoreType.DMA((2,))],
    )
    return pl.pallas_call(
        _moe_down_kernel,
        grid_spec=grid_spec,
        out_shape=jax.ShapeDtypeStruct((n_rows, d), F32),
        compiler_params=_cparams(("arbitrary",)),
        name="moe_down",
    )(blk_expert, n_active, nreal, dest.reshape(p // m, 1, m), act, w_down, b_down)


def _combine_kernel(h_ref, y0_ref, y1_ref, y2_ref, y3_ref, gate_ref, g2_ref, b2_ref, o_ref):
    ffn = gate_ref[:, 0:1] * y0_ref[...]
    for j, y_ref in enumerate((y1_ref, y2_ref, y3_ref), start=1):
        ffn = ffn + gate_ref[:, j:j + 1] * y_ref[...]
    o_ref[...] = _layer_norm(DEEPNORM_ALPHA * h_ref[...] + ffn, g2_ref[...], b2_ref[...])


def _combine(h, yt, gate, g2, b2):
    t, d = h.shape
    tm = TOKEN_TILE
    nt = t // tm

    def choice(j):
        return pl.BlockSpec((tm, d), lambda i: (j * nt + i, 0))

    return pl.pallas_call(
        _combine_kernel,
        grid=(nt,),
        in_specs=[pl.BlockSpec((tm, d), lambda i: (i, 0)),
                  choice(0), choice(1), choice(2), choice(3),
                  pl.BlockSpec((tm, TOP_K), lambda i: (i, 0)),
                  pl.BlockSpec((1, d), lambda i: (0, 0)),
                  pl.BlockSpec((1, d), lambda i: (0, 0))],
        out_specs=pl.BlockSpec((tm, d), lambda i: (i, 0)),
        out_shape=jax.ShapeDtypeStruct((t, d), F32),
        compiler_params=_cparams(("parallel",)),
        name="combine_ln2",
    )(h, yt, yt, yt, yt, gate, g2, b2)


def _route(top_e, lrank, tilecnt):
    t = top_e.shape[0]
    m = MOE_BLOCK
    n_assign = t * TOP_K
    n_blocks = n_assign // m + N_EXPERTS
    p = n_blocks * m
    tc = tilecnt.reshape(-1, N_EXPERTS)
    base = jnp.cumsum(tc, axis=0) - tc
    counts = jnp.sum(tc, axis=0)
    padded = (counts + m - 1) // m * m
    pend = jnp.cumsum(padded)
    pstart = pend - padded
    base_tok = jnp.repeat(base, TOKEN_TILE, axis=0)
    rank = lrank + jnp.take_along_axis(base_tok, top_e, axis=1)
    slot = (pstart[top_e] + rank).reshape(-1)
    a_idx = jnp.arange(n_assign, dtype=I32)
    dest = jnp.full((p,), -1, I32).at[slot].set((a_idx % TOP_K) * t + a_idx // TOP_K)
    tok_pad = jnp.where(dest >= 0, dest % t, 0)
    blk_start = jnp.arange(n_blocks, dtype=I32) * m
    blk_expert = jnp.minimum(jnp.sum((pend[None, :] <= blk_start[:, None]).astype(I32), axis=1),
                             N_EXPERTS - 1)
    n_active = (pend[-1:] // m).astype(I32)
    real_end = (pstart + counts)[blk_expert]
    nreal = jnp.where(blk_start < pend[-1], jnp.clip(real_end - blk_start, 0, m), 0).astype(I32)
    return tok_pad, dest, blk_expert, n_active, nreal


def kernel(x_prompt, x_sample, cache_attn_k, cache_attn_v, state_hgrn, w_in, hgrn_lb_logits,
           hgrn_norm_w, w_branch_a, w_branch_b, w_out, ln1_g, ln1_b, w_router, b_router,
           w_gate_up, b_gate_up, w_down, b_down, ln2_g, ln2_b):
    assert w_in.shape[0] == DEPTH == 1
    bp, lp, d = x_prompt.shape
    bs, ls, _ = x_sample.shape
    tp, ts = bp * lp, bs * ls
    t = tp + ts
    keep = min(max(w for w, _ in DILATION_PAIRS), lp)

    xa = jnp.concatenate([x_prompt.reshape(tp, d), x_sample.reshape(ts, d)], axis=0)
    xb = xa.astype(BF16)
    wi = w_in[0]
    c1, c2 = P1_WIDTH, P1_WIDTH + P2_WIDTH
    p1 = _matmul(xb, wi[:, :c1].astype(BF16), 768, 1024, F32, "inproj_hgrn")
    p2 = _matmul(xb, wi[:, c1:c2].astype(BF16), 768, 512, F32, "inproj_attn")
    p3 = _matmul(xb, wi[:, c2:].astype(BF16), 768, 1024, F32, "inproj_gate")

    oa_p, st_p = _hgrn(p1, hgrn_lb_logits, hgrn_norm_w[0], None, bp, lp, 0)
    oa_s, st_s = _hgrn(p1, hgrn_lb_logits, hgrn_norm_w[0],
                       state_hgrn.reshape(bs, HG_HEADS, HG_DIM, HG_DIM), bs, ls, tp)

    n = jnp.arange(1, ATT_Q_HEADS + 1, dtype=F32)
    slopes = jnp.power(2.0, -8.0 * n / ATT_Q_HEADS).reshape(N_GROUPS, ATT_KV_HEADS)
    ctx = cache_attn_k.shape[2]
    ob_p = _attn_prompt(p2, slopes, bp, lp)
    ob_s = _attn_sample(p2, cache_attn_k.reshape(bs, ctx * ATT_KV_HEADS, ATT_HEAD_DIM),
                        cache_attn_v.reshape(bs, ctx * ATT_KV_HEADS, ATT_HEAD_DIM), slopes, bs, ls, tp)

    oa = jnp.concatenate([oa_p.reshape(tp, HG_WIDTH), oa_s.reshape(ts, HG_WIDTH)], axis=0)
    ob = jnp.concatenate([ob_p, ob_s.reshape(ts, ATT_KV_WIDTH)], axis=0)
    wr = w_router[0]
    wr0 = wr.astype(BF16)
    wr1 = (wr - wr0.astype(F32)).astype(BF16)
    h, top_e, gate, lrank, tilecnt = _merge(
        xa, oa, ob, p3, w_branch_a[0].astype(BF16), w_branch_b[0].astype(BF16),
        w_out[0].astype(BF16), ln1_g, ln1_b, wr0, wr1, b_router)

    tok_pad, dest, blk_expert, n_active, nreal = _route(top_e, lrank, tilecnt)
    xs = _moe_gather(h, tok_pad)
    act = _moe_up(xs, w_gate_up.reshape(N_EXPERTS, d, 2 * D_FF),
                  b_gate_up.reshape(N_EXPERTS, 1, 2 * D_FF), blk_expert, n_active)
    yt = _moe_down(act, w_down.reshape(N_EXPERTS, D_FF, d), b_down.reshape(N_EXPERTS, 1, d),
                   blk_expert, n_active, nreal, dest, t * TOP_K)
    y = _combine(h, yt, gate, ln2_g, ln2_b)

    k_new = p2[:, ATT_Q_HEADS * ATT_HEAD_DIM:(ATT_Q_HEADS + ATT_KV_HEADS) * ATT_HEAD_DIM]
    v_new = p2[:, (ATT_Q_HEADS + ATT_KV_HEADS) * ATT_HEAD_DIM:]

    def kv_prompt(a):
        return a[:tp].reshape(bp, lp, ATT_KV_HEADS, ATT_HEAD_DIM)[:, lp - keep:][None]

    def kv_sample(a):
        return a[tp:].reshape(bs, ls, ATT_KV_HEADS, ATT_HEAD_DIM)[None]

    return (y[:tp].reshape(bp, lp, d), y[tp:].reshape(bs, ls, d),
            kv_prompt(k_new), kv_prompt(v_new), st_p[None],
            kv_sample(k_new), kv_sample(v_new), st_s[None])
```

```python
import functools
import math

import jax
import jax.numpy as jnp
from jax import lax
from jax.experimental import pallas as pl
from jax.experimental.pallas import tpu as pltpu

F32 = jnp.float32
BF16 = jnp.bfloat16
I32 = jnp.int32

D_MODEL = 2048
DEPTH = 1
HG_HEADS = 8
HG_DIM = 128
HG_WIDTH = HG_HEADS * HG_DIM
HG_CHUNK = 64
ATT_HEAD_DIM = 128
DILATION_PAIRS = ((128, 1), (512, 4), (2048, 16))
N_GROUPS = len(DILATION_PAIRS)
ATT_KV_HEADS = 4
ATT_Q_HEADS = N_GROUPS * ATT_KV_HEADS
ATT_TAPS = 128
ATT_KV_WIDTH = ATT_KV_HEADS * ATT_HEAD_DIM
ATT_SCALE = ATT_HEAD_DIM ** -0.5
N_EXPERTS = 32
TOP_K = 4
D_FF = D_MODEL
SWIGLU_ALPHA = 1.702
SWIGLU_LIMIT = 7.0
LN_EPS = 1e-5
RMS_EPS = 1e-5
DEEPNORM_ALPHA = (2 * DEPTH) ** 0.25

P1_WIDTH = 4 * HG_WIDTH
P2_WIDTH = (ATT_Q_HEADS + 2 * ATT_KV_HEADS) * ATT_HEAD_DIM
P3_WIDTH = 2 * D_MODEL

LANES = 128
SUBLANES = 8
VMEM_LIMIT = 56 * 1024 * 1024

TOKEN_TILE = 256
MOE_BLOCK = 256
MOE_FT = 1024
NEG = -1e30

_NT = (((1,), (1,)), ((), ()))
_TN = (((0,), (0,)), ((), ()))


def _cparams(sem):
    return pltpu.CompilerParams(dimension_semantics=sem, vmem_limit_bytes=VMEM_LIMIT)


def _sigmoid(x):
    return 1.0 / (1.0 + jnp.exp(-x))


def _mm_kernel(x_ref, w_ref, o_ref):
    o_ref[...] = jnp.dot(x_ref[...], w_ref[...], preferred_element_type=F32).astype(o_ref.dtype)


def _matmul(x, w, tm, tn, out_dtype, name):
    m, k = x.shape
    n = w.shape[1]
    assert m % tm == 0 and n % tn == 0
    return pl.pallas_call(
        _mm_kernel,
        grid=(n // tn, m // tm),
        in_specs=[pl.BlockSpec((tm, k), lambda j, i: (i, 0)),
                  pl.BlockSpec((k, tn), lambda j, i: (0, j))],
        out_specs=pl.BlockSpec((tm, tn), lambda j, i: (i, j)),
        out_shape=jax.ShapeDtypeStruct((m, n), out_dtype),
        compiler_params=_cparams(("parallel", "parallel")),
        name=name,
    )(x, w)


def _level_ref(g_scr, m, sl, chunk, sub):
    pieces = []
    for j in range(chunk // SUBLANES):
        if 2 * m >= SUBLANES:
            row = ((SUBLANES * j) // (2 * m)) * (2 * m) + m - 1
            pieces.append(jnp.broadcast_to(g_scr[row:row + 1, sl], (SUBLANES, LANES)))
        else:
            acc = None
            for u in range(SUBLANES // (2 * m)):
                row = SUBLANES * j + 2 * m * u + m - 1
                b = jnp.broadcast_to(g_scr[row:row + 1, sl], (SUBLANES, LANES))
                acc = b if acc is None else jnp.where(sub // (2 * m) == u, b, acc)
            pieces.append(acc)
    return pieces[0] if len(pieces) == 1 else jnp.concatenate(pieces, axis=0)


def _hgrn_kernel(lbl_ref, nw_ref, hq_ref, hf_ref, hi_ref, hg_ref, *rest, chunk, has_s0):
    if has_s0:
        s0_ref, o_ref, sout_ref, st_scr, g_scr = rest
    else:
        o_ref, sout_ref, st_scr, g_scr = rest
    c = pl.program_id(1)
    nc = pl.num_programs(1)

    @pl.when(c == 0)
    def _init():
        for h in range(HG_HEADS):
            if has_s0:
                st_scr[h] = s0_ref[0, h].T
            else:
                st_scr[h] = jnp.zeros((HG_DIM, HG_DIM), F32)

    lbl = lbl_ref[...]
    ex = jnp.exp(lbl - jnp.max(lbl, axis=0, keepdims=True))
    lb = ex[0:1] / jnp.sum(ex, axis=0, keepdims=True)

    f = lb + (1.0 - lb) * _sigmoid(hf_ref[...])
    logf = jnp.log(f)
    kall = 1.0 - f

    r_cc = lax.broadcasted_iota(I32, (chunk, chunk), 0)
    c_cc = lax.broadcasted_iota(I32, (chunk, chunk), 1)
    tri = jnp.where(r_cc >= c_cc, 1.0, 0.0).astype(BF16)
    p0 = logf.astype(BF16)
    r1 = logf - p0.astype(F32)
    p1 = r1.astype(BF16)
    p2 = (r1 - p1.astype(F32)).astype(BF16)
    g_all = (jnp.dot(tri, p0, preferred_element_type=F32)
             + jnp.dot(tri, p1, preferred_element_type=F32)
             + jnp.dot(tri, p2, preferred_element_type=F32))
    g_scr[...] = g_all

    sub = lax.broadcasted_iota(I32, (SUBLANES, LANES), 0)
    row_c = lax.broadcasted_iota(I32, (chunk, LANES), 0)
    nw = nw_ref[...]

    for h in range(HG_HEADS):
        sl = slice(h * HG_DIM, (h + 1) * HG_DIM)
        hq = hq_ref[:, sl]
        q = hq * _sigmoid(hq)
        k = kall[:, sl]
        g = g_all[:, sl]
        v = hi_ref[:, sl]
        vb = v.astype(BF16)

        p = lax.dot_general(q.astype(BF16), k.astype(BF16), _NT, preferred_element_type=F32)
        a = jnp.where(r_cc == c_cc, p, 0.0)
        m = 1
        while m < chunk:
            ref = _level_ref(g_scr, m, sl, chunk, sub)
            e = jnp.exp(-jnp.abs(g - ref))
            odd = ((row_c // m) & 1) == 1
            x = (jnp.where(odd, q, k) * e).astype(BF16)
            p = lax.dot_general(x, x, _NT, preferred_element_type=F32)
            pair = (r_cc // (2 * m)) == (c_cc // (2 * m))
            sel = jnp.where(pair, ((r_cc // m) & 1) - ((c_cc // m) & 1), 0) == 1
            a = jnp.where(sel, p, a)
            m *= 2

        st = st_scr[h]
        o = jnp.dot(a.astype(BF16), vb, preferred_element_type=F32)
        qe = (q * jnp.exp(g)).astype(BF16)
        o = o + lax.dot_general(qe, st.astype(BF16), _NT, preferred_element_type=F32)

        ms = jnp.mean(o * o, axis=-1, keepdims=True)
        hg = hg_ref[:, sl]
        out = o * lax.rsqrt(ms + RMS_EPS) * nw * (hg * _sigmoid(hg))
        o_ref[0, :, sl] = out.astype(o_ref.dtype)

        g_last = g[chunk - 1:chunk, :]
        kd = (k * jnp.exp(g_last - g)).astype(BF16)
        st_scr[h] = st * jnp.exp(g_last) + lax.dot_general(vb, kd, _TN, preferred_element_type=F32)

    @pl.when(c == nc - 1)
    def _fin():
        for h in range(HG_HEADS):
            sout_ref[0, h] = st_scr[h].T


def _hgrn(p1, lb_logits, norm_w, s0, batch, seq, row0):
    chunk = math.gcd(seq, HG_CHUNK)
    nchunk = seq // chunk
    assert row0 % chunk == 0
    rb0 = row0 // chunk

    def col(idx):
        return pl.BlockSpec((chunk, HG_WIDTH), lambda b, c: (rb0 + b * nchunk + c, idx))

    in_specs = [pl.BlockSpec(lb_logits.shape, lambda b, c: (0, 0)),
                pl.BlockSpec((1, HG_DIM), lambda b, c: (0, 0)),
                col(0), col(1), col(2), col(3)]
    args = [lb_logits, norm_w.reshape(1, HG_DIM), p1, p1, p1, p1]
    if s0 is not None:
        in_specs.append(pl.BlockSpec((1, HG_HEADS, HG_DIM, HG_DIM), lambda b, c: (b, 0, 0, 0)))
        args.append(s0)
    return pl.pallas_call(
        functools.partial(_hgrn_kernel, chunk=chunk, has_s0=s0 is not None),
        grid=(batch, nchunk),
        in_specs=in_specs,
        out_specs=[pl.BlockSpec((1, chunk, HG_WIDTH), lambda b, c: (b, c, 0)),
                   pl.BlockSpec((1, HG_HEADS, HG_DIM, HG_DIM), lambda b, c: (b, 0, 0, 0))],
        out_shape=[jax.ShapeDtypeStruct((batch, seq, HG_WIDTH), BF16),
                   jax.ShapeDtypeStruct((batch, HG_HEADS, HG_DIM, HG_DIM), F32)],
        scratch_shapes=[pltpu.VMEM((HG_HEADS, HG_DIM, HG_DIM), F32),
                        pltpu.VMEM((chunk, HG_WIDTH), F32)],
        compiler_params=_cparams(("parallel", "arbitrary")),
        name="hgrn_seq%d" % seq,
    )(*args)


QB = 128


def _attn_prompt_kernel(slopes_ref, q0_ref, q1_ref, q2_ref, k_ref, v_ref, o_ref,
                        og_scr, lse_scr, bias_scr, *, seq):
    h = pl.program_id(1)
    il = lax.broadcasted_iota(I32, (QB, 2 * QB), 0)
    jl = lax.broadcasted_iota(I32, (QB, 2 * QB), 1)
    delta = il + QB - jl
    band = jnp.where(delta >= 0, delta, ATT_TAPS + 1) <= ATT_TAPS
    prev_half = jl < QB
    q_refs = (q0_ref, q1_ref, q2_ref)

    for g, (window, dil) in enumerate(DILATION_PAIRS):
        assert window // dil == ATT_TAPS
        slope = slopes_ref[g, h]
        bias_scr[g] = jnp.where(band, (-slope * dil) * delta.astype(F32), NEG)
        nbr = (seq // dil) // QB
        q_ref = q_refs[g]

        def rows(start, dil=dil):
            return pl.ds(start, QB) if dil == 1 else pl.ds(start, QB, stride=dil)

        def body(ib, carry, g=g, dil=dil, nbr=nbr, q_ref=q_ref, rows=rows):
            res = ib // nbr
            jb = ib - res * nbr
            qs = res + dil * (jb * QB)
            ps = res + dil * jnp.maximum(jb * QB - QB, 0)
            qv = q_ref[rows(qs), :].astype(BF16)
            kb = jnp.concatenate([k_ref[rows(ps), :], k_ref[rows(qs), :]], axis=0).astype(BF16)
            vb = jnp.concatenate([v_ref[rows(ps), :], v_ref[rows(qs), :]], axis=0).astype(BF16)
            s = lax.dot_general(qv, kb, _NT, preferred_element_type=F32) * ATT_SCALE + bias_scr[g]
            s = jnp.where(jnp.where(prev_half, jb, 1) == 0, NEG, s)
            mx = jnp.max(s, axis=-1, keepdims=True)
            p = jnp.exp(s - mx)
            l = jnp.sum(p, axis=-1, keepdims=True)
            o = jnp.dot(p.astype(BF16), vb, preferred_element_type=F32) / l
            og_scr[g, rows(qs), :] = o
            lse_scr[g, rows(qs), :] = jnp.broadcast_to(mx + jnp.log(l), (QB, LANES))
            return carry

        lax.fori_loop(0, seq // QB, body, 0)

    def merge(tb, carry):
        r = pl.ds(pl.multiple_of(tb * QB, QB), QB)
        l0, l1, l2 = lse_scr[0, r, :], lse_scr[1, r, :], lse_scr[2, r, :]
        mx = jnp.maximum(jnp.maximum(l0, l1), l2)
        w0, w1, w2 = jnp.exp(l0 - mx), jnp.exp(l1 - mx), jnp.exp(l2 - mx)
        o = (w0 * og_scr[0, r, :] + w1 * og_scr[1, r, :] + w2 * og_scr[2, r, :]) / (w0 + w1 + w2)
        o_ref[r, :] = o.astype(o_ref.dtype)
        return carry

    lax.fori_loop(0, seq // QB, merge, 0)


def _attn_prompt(p2, slopes, batch, seq):
    assert seq % (QB * max(d for _, d in DILATION_PAIRS)) == 0

    def col(fn):
        return pl.BlockSpec((seq, ATT_HEAD_DIM), lambda b, h: (b, fn(h)))

    return pl.pallas_call(
        functools.partial(_attn_prompt_kernel, seq=seq),
        grid=(batch, ATT_KV_HEADS),
        in_specs=[pl.BlockSpec(memory_space=pltpu.SMEM),
                  col(lambda h: h), col(lambda h: ATT_KV_HEADS + h), col(lambda h: 2 * ATT_KV_HEADS + h),
                  col(lambda h: ATT_Q_HEADS + h), col(lambda h: ATT_Q_HEADS + ATT_KV_HEADS + h)],
        out_specs=pl.BlockSpec((seq, ATT_HEAD_DIM), lambda b, h: (b, h)),
        out_shape=jax.ShapeDtypeStruct((batch * seq, ATT_KV_WIDTH), BF16),
        scratch_shapes=[pltpu.VMEM((N_GROUPS, seq, ATT_HEAD_DIM), F32),
                        pltpu.VMEM((N_GROUPS, seq, LANES), F32),
                        pltpu.VMEM((N_GROUPS, QB, 2 * QB), F32)],
        compiler_params=_cparams(("parallel", "parallel")),
        name="attn_prompt",
    )(slopes, p2, p2, p2, p2, p2)


def _attn_sample_kernel(slopes_ref, qa_ref, qb_ref, qc_ref, kn_ref, vn_ref, kc_ref, vc_ref, o_ref,
                        *, ctx, nq):
    rows = N_GROUPS * nq
    gi = lax.broadcasted_iota(I32, (rows, 1), 0) // nq
    dil = jnp.where(gi == 0, DILATION_PAIRS[0][1], jnp.where(gi == 1, DILATION_PAIRS[1][1], DILATION_PAIRS[2][1]))
    win = jnp.where(gi == 0, DILATION_PAIRS[0][0], jnp.where(gi == 1, DILATION_PAIRS[1][0], DILATION_PAIRS[2][0]))

    def dist_valid(shape, key0, nkeys):
        r = lax.broadcasted_iota(I32, shape, 0)
        t = r - (r // nq) * nq
        col = lax.broadcasted_iota(I32, shape, 1)
        dist = ctx + t - (col + key0)
        bad = jnp.where(dist >= 0, dist & (dil - 1), 1)
        bad = jnp.where(dist <= win, bad, 1)
        bad = jnp.where(col < nkeys, bad, 1)
        return dist.astype(F32), bad == 0

    dist_c, ok_c = dist_valid((rows, ctx), 0, ctx)
    dist_n, ok_n = dist_valid((rows, LANES), ctx, nq)
    zpad = jnp.zeros((LANES - nq, ATT_HEAD_DIM), F32)

    for h in range(ATT_KV_HEADS):
        hs = slice(h * ATT_HEAD_DIM, (h + 1) * ATT_HEAD_DIM)
        slope = jnp.where(gi == 0, slopes_ref[0, h], jnp.where(gi == 1, slopes_ref[1, h], slopes_ref[2, h]))
        qh = jnp.concatenate([qa_ref[:, hs], qb_ref[:, hs], qc_ref[:, hs]], axis=0).astype(BF16)
        kc = kc_ref[0, pl.ds(h, ctx, stride=ATT_KV_HEADS), :].astype(BF16)
        vc = vc_ref[0, pl.ds(h, ctx, stride=ATT_KV_HEADS), :].astype(BF16)
        kn = jnp.concatenate([kn_ref[:, hs], zpad], axis=0).astype(BF16)
        vn = jnp.concatenate([vn_ref[:, hs], zpad], axis=0).astype(BF16)
        sc = lax.dot_general(qh, kc, _NT, preferred_element_type=F32) * ATT_SCALE - slope * dist_c
        sn = lax.dot_general(qh, kn, _NT, preferred_element_type=F32) * ATT_SCALE - slope * dist_n
        sc = jnp.where(ok_c, sc, NEG)
        sn = jnp.where(ok_n, sn, NEG)
        mx = jnp.maximum(jnp.max(sc, axis=-1, keepdims=True), jnp.max(sn, axis=-1, keepdims=True))
        pc = jnp.exp(sc - mx)
        pn = jnp.exp(sn - mx)
        l = jnp.sum(pc, axis=-1, keepdims=True) + jnp.sum(pn, axis=-1, keepdims=True)
        o = (jnp.dot(pc.astype(BF16), vc, preferred_element_type=F32)
             + jnp.dot(pn.astype(BF16), vn, preferred_element_type=F32)) / l
        lse = mx + jnp.log(l)
        l0, l1, l2 = lse[0:nq], lse[nq:2 * nq], lse[2 * nq:3 * nq]
        m3 = jnp.maximum(jnp.maximum(l0, l1), l2)
        w0, w1, w2 = jnp.exp(l0 - m3), jnp.exp(l1 - m3), jnp.exp(l2 - m3)
        out = (w0 * o[0:nq] + w1 * o[nq:2 * nq] + w2 * o[2 * nq:3 * nq]) / (w0 + w1 + w2)
        o_ref[0, :, hs] = out.astype(o_ref.dtype)


def _attn_sample(p2, cache_k, cache_v, slopes, batch, nq, row0):
    ctx = cache_k.shape[1] // ATT_KV_HEADS
    assert row0 % nq == 0 and nq == SUBLANES and ctx >= max(w for w, _ in DILATION_PAIRS)
    rb0 = row0 // nq

    def col(idx):
        return pl.BlockSpec((nq, ATT_KV_WIDTH), lambda b: (rb0 + b, idx))

    cache_spec = pl.BlockSpec((1, ctx * ATT_KV_HEADS, ATT_HEAD_DIM), lambda b: (b, 0, 0))
    return pl.pallas_call(
        functools.partial(_attn_sample_kernel, ctx=ctx, nq=nq),
        grid=(batch,),
        in_specs=[pl.BlockSpec(memory_space=pltpu.SMEM),
                  col(0), col(1), col(2), col(3), col(4), cache_spec, cache_spec],
        out_specs=pl.BlockSpec((1, nq, ATT_KV_WIDTH), lambda b: (b, 0, 0)),
        out_shape=jax.ShapeDtypeStruct((batch, nq, ATT_KV_WIDTH), BF16),
        compiler_params=_cparams(("parallel",)),
        name="attn_sample",
    )(slopes, p2, p2, p2, p2, p2, cache_k, cache_v)


def _layer_norm(y, g, b):
    mu = jnp.mean(y, axis=-1, keepdims=True)
    yc = y - mu
    var = jnp.mean(yc * yc, axis=-1, keepdims=True)
    return yc * lax.rsqrt(var + LN_EPS) * g + b


def _merge_kernel(x_ref, oa_ref, ob_ref, ga_ref, gb_ref, wa_ref, wb_ref, wo_ref, g1_ref, b1_ref,
                  wr0_ref, wr1_ref, br_ref,
                  h_ref, tope_ref, gate_ref, lrank_ref, cnt_ref):
    tm = x_ref.shape[0]
    a = jnp.dot(oa_ref[...], wa_ref[...], preferred_element_type=F32)
    b = jnp.dot(ob_ref[...], wb_ref[...], preferred_element_type=F32)
    merged = _sigmoid(ga_ref[...]) * a + _sigmoid(gb_ref[...]) * b
    z = jnp.dot(merged.astype(BF16), wo_ref[...], preferred_element_type=F32)
    hh = _layer_norm(DEEPNORM_ALPHA * x_ref[...] + z, g1_ref[...], b1_ref[...])
    h_ref[...] = hh

    h0 = hh.astype(BF16)
    h1 = (hh - h0.astype(F32)).astype(BF16)
    logits = (jnp.dot(h0, wr0_ref[...], preferred_element_type=F32)
              + jnp.dot(h1, wr0_ref[...], preferred_element_type=F32)
              + jnp.dot(h0, wr1_ref[...], preferred_element_type=F32)) + br_ref[...]

    lane = lax.broadcasted_iota(I32, (tm, N_EXPERTS), 1)
    work = logits
    vals, idxs = [], []
    onehot = jnp.zeros((tm, N_EXPERTS), F32)
    for _ in range(TOP_K):
        mx = jnp.max(work, axis=-1, keepdims=True)
        idx = jnp.min(jnp.where(work == mx, lane, N_EXPERTS), axis=-1, keepdims=True)
        hit = lane == idx
        vals.append(mx)
        idxs.append(idx)
        onehot = jnp.where(hit, 1.0, onehot)
        work = jnp.where(hit, -jnp.inf, work)
    ex = [jnp.exp(v - vals[0]) for v in vals]
    den = ex[0] + ex[1] + ex[2] + ex[3]

    r_tt = lax.broadcasted_iota(I32, (tm, tm), 0)
    c_tt = lax.broadcasted_iota(I32, (tm, tm), 1)
    before = jnp.where(r_tt > c_tt, 1.0, 0.0).astype(BF16)
    prefix = jnp.dot(before, onehot.astype(BF16), preferred_element_type=F32)
    for j in range(TOP_K):
        tope_ref[:, j:j + 1] = idxs[j]
        gate_ref[:, j:j + 1] = ex[j] / den
        lrank_ref[:, j:j + 1] = jnp.sum(jnp.where(lane == idxs[j], prefix, 0.0), axis=-1,
                                        keepdims=True).astype(I32)
    cnt_ref[0] = jnp.sum(onehot, axis=0, keepdims=True).astype(I32)


def _merge(xa, oa, ob, p3, wa, wb, wo, g1, b1, wr0, wr1, br):
    t = xa.shape[0]
    tm = TOKEN_TILE
    nt = t // tm
    assert t % tm == 0

    def full(a):
        return pl.BlockSpec(a.shape, lambda i: (0,) * a.ndim)

    def rowblk(width, idx=0):
        return pl.BlockSpec((tm, width), lambda i: (i, idx))

    return pl.pallas_call(
        _merge_kernel,
        grid=(nt,),
        in_specs=[rowblk(D_MODEL), rowblk(HG_WIDTH), rowblk(ATT_KV_WIDTH), rowblk(D_MODEL, 0),
                  rowblk(D_MODEL, 1), full(wa), full(wb), full(wo), full(g1), full(b1),
                  full(wr0), full(wr1), full(br)],
        out_specs=[rowblk(D_MODEL), rowblk(TOP_K), rowblk(TOP_K), rowblk(TOP_K),
                   pl.BlockSpec((1, 1, N_EXPERTS), lambda i: (i, 0, 0))],
        out_shape=[jax.ShapeDtypeStruct((t, D_MODEL), F32),
                   jax.ShapeDtypeStruct((t, TOP_K), I32),
                   jax.ShapeDtypeStruct((t, TOP_K), F32),
                   jax.ShapeDtypeStruct((t, TOP_K), I32),
                   jax.ShapeDtypeStruct((nt, 1, N_EXPERTS), I32)],
        compiler_params=_cparams(("parallel",)),
        name="merge_ln1_router",
    )(xa, oa, ob, p3, p3, wa, wb, wo, g1, b1, wr0, wr1, br)


def _gather_kernel(tok_ref, h_ref, o_ref, buf_ref):
    m = o_ref.shape[0]

    def body(q, carry):
        for u in range(SUBLANES):
            r = q * SUBLANES + u
            buf_ref[pl.ds(r, 1), :] = h_ref[pl.ds(tok_ref[0, 0, r], 1), :]
        return carry

    lax.fori_loop(0, m // SUBLANES, body, 0)
    o_ref[...] = buf_ref[...].astype(o_ref.dtype)


def _moe_gather(h, tok_pad):
    t, d = h.shape
    p = tok_pad.shape[0]
    m = MOE_BLOCK
    half = d // 2
    return pl.pallas_call(
        _gather_kernel,
        grid=(2, p // m),
        in_specs=[pl.BlockSpec((1, 1, m), lambda c, i: (i, 0, 0), memory_space=pltpu.SMEM),
                  pl.BlockSpec((t, half), lambda c, i: (0, c), pipeline_mode=pl.Buffered(1))],
        out_specs=pl.BlockSpec((m, half), lambda c, i: (i, c)),
        out_shape=jax.ShapeDtypeStruct((p, d), BF16),
        scratch_shapes=[pltpu.VMEM((m, half), F32)],
        compiler_params=_cparams(("arbitrary", "arbitrary")),
        name="moe_gather",
    )(tok_pad.reshape(p // m, 1, m), h)


def _moe_up_kernel(blk0_ref, nblk_ref, w_ref, b_ref, xs_ref, act_ref, wbf_scr, x_scr, o_scr,
                   sem_in, sem_out):
    e = pl.program_id(0)
    f = pl.program_id(1)
    n = nblk_ref[e]
    b0 = blk0_ref[e]
    m = x_scr.shape[1]
    ft = w_ref.shape[2]
    half = ft // 2

    def rows(b):
        return pl.ds(pl.multiple_of((b0 + b) * m, m), m)

    def x_copy(b, s):
        return pltpu.make_async_copy(xs_ref.at[rows(b)], x_scr.at[s], sem_in.at[s])

    def o_copy(b, s):
        cols = pl.ds(pl.multiple_of(f * half, LANES), half)
        return pltpu.make_async_copy(o_scr.at[s], act_ref.at[rows(b), cols], sem_out.at[s])

    @pl.when(n > 0)
    def _():
        x_copy(0, 0).start()
        wbf_scr[...] = w_ref[0].astype(BF16)
        even = (lax.broadcasted_iota(I32, (m, LANES), 1) & 1) == 0

        def body(b, carry):
            s = b & 1
            x_copy(b, s).wait()

            @pl.when(b + 1 < n)
            def _():
                x_copy(b + 1, 1 - s).start()

            @pl.when(b >= 2)
            def _():
                o_copy(b - 2, s).wait()

            x = x_scr[s]
            for g in range(ft // (2 * LANES)):
                cols = slice(2 * g * LANES, (2 * g + 2) * LANES)
                h = jnp.dot(x, wbf_scr[:, cols], preferred_element_type=F32) + b_ref[0, :, cols]
                ha, hb = h[:, :LANES], h[:, LANES:]
                glu = jnp.minimum(jnp.where(even, ha, pltpu.roll(hb, 1, 1)), SWIGLU_LIMIT)
                lin = jnp.where(even, pltpu.roll(ha, LANES - 1, 1), hb)
                lin = jnp.clip(lin, -SWIGLU_LIMIT, SWIGLU_LIMIT) + 1.0
                act = glu * _sigmoid(SWIGLU_ALPHA * glu) * lin
                o_scr[s, :, g * LANES:(g + 1) * LANES] = act.astype(o_scr.dtype)
            o_copy(b, s).start()
            return carry

        lax.fori_loop(0, n, body, 0)

        @pl.when(n >= 2)
        def _():
            o_copy(n - 2, n & 1).wait()
        o_copy(n - 1, (n - 1) & 1).wait()

    @pl.when(e == pl.num_programs(0) - 1)
    def _():
        o_scr[0] = jnp.zeros(o_scr.shape[1:], o_scr.dtype)

        def zbody(b, carry):
            o_copy(b, 0).start()
            o_copy(b, 0).wait()
            return carry

        lax.fori_loop(n, act_ref.shape[0] // m - b0, zbody, 0)


def _moe_up(xs, w_gate_up, b_gate_up, blk0, nblk):
    p, d = xs.shape
    m = MOE_BLOCK
    ft = MOE_FT
    ne = w_gate_up.shape[0]
    nf = w_gate_up.shape[2] // ft
    grid_spec = pltpu.PrefetchScalarGridSpec(
        num_scalar_prefetch=2,
        grid=(ne, nf),
        in_specs=[pl.BlockSpec((1, d, ft), lambda e, f, b0, nb: (e, 0, f)),
                  pl.BlockSpec((1, 1, ft), lambda e, f, b0, nb: (e, 0, f)),
                  pl.BlockSpec(memory_space=pl.ANY)],
        out_specs=pl.BlockSpec(memory_space=pl.ANY),
        scratch_shapes=[pltpu.VMEM((d, ft), BF16),
                        pltpu.VMEM((2, m, d), BF16),
                        pltpu.VMEM((2, m, ft // 2), BF16),
                        pltpu.SemaphoreType.DMA((2,)),
                        pltpu.SemaphoreType.DMA((2,))],
    )
    return pl.pallas_call(
        _moe_up_kernel,
        grid_spec=grid_spec,
        out_shape=jax.ShapeDtypeStruct((p, w_gate_up.shape[2] // 2), BF16),
        compiler_params=_cparams(("arbitrary", "arbitrary")),
        name="moe_up",
    )(blk0, nblk, w_gate_up, b_gate_up, xs)


def _hidden_perm():
    r = lax.broadcasted_iota(I32, (LANES, LANES), 0)
    c = lax.broadcasted_iota(I32, (LANES, LANES), 1)
    src = (r >> 1) + (r & 1) * (LANES // 2)
    return jnp.where(c == src, 1.0, 0.0).astype(BF16)


def _moe_down_kernel(be_ref, na_ref, nreal_ref, dest_ref, a_ref, wd_ref, bd_ref, yt_ref,
                     wbf_scr, y_scr, sem):
    i = pl.program_id(0)
    nb = pl.num_programs(0)
    m = a_ref.shape[0]
    slot = i % 2

    def row_copy(s, r, d):
        return pltpu.make_async_copy(y_scr.at[s, pl.ds(r, 1)], yt_ref.at[pl.ds(d, 1)], sem.at[s])

    def drain(s, count):
        @pl.when(count == m)
        def _():
            pltpu.make_async_copy(y_scr.at[s], yt_ref.at[pl.ds(0, m)], sem.at[s]).wait()

        @pl.when(count != m)
        def _():
            def wbody(_, carry):
                row_copy(s, 0, 0).wait()
                return carry
            lax.fori_loop(0, count, wbody, 0)

    @pl.when(i >= 2)
    def _():
        drain(slot, nreal_ref[i - 2])

    changed = jnp.where(i == 0, 1, be_ref[i] - be_ref[jnp.maximum(i - 1, 0)])

    @pl.when(changed != 0)
    def _():
        perm = _hidden_perm()
        for g in range(wd_ref.shape[1] // LANES):
            rows = slice(g * LANES, (g + 1) * LANES)
            wbf_scr[rows, :] = jnp.dot(perm, wd_ref[0, rows, :].astype(BF16),
                                       preferred_element_type=F32).astype(BF16)

    @pl.when(i < na_ref[0])
    def _():
        y_scr[slot] = jnp.dot(a_ref[...], wbf_scr[...], preferred_element_type=F32) + bd_ref[0]

        n = nreal_ref[i]
        unroll = 8

        def sbody8(q, carry):
            for u in range(unroll):
                r = q * unroll + u
                row_copy(slot, r, dest_ref[0, 0, r]).start()
            return carry

        def sbody1(r, carry):
            row_copy(slot, r, dest_ref[0, 0, r]).start()
            return carry

        lax.fori_loop(0, n // unroll, sbody8, 0)
        lax.fori_loop((n // unroll) * unroll, n, sbody1, 0)

    @pl.when(i == nb - 1)
    def _():
        @pl.when(i >= 1)
        def _():
            drain(1 - slot, nreal_ref[i - 1])
        drain(slot, nreal_ref[i])


def _moe_down(act, w_down, b_down, blk_expert, n_active, nreal, dest, n_rows):
    p, f = act.shape
    m = MOE_BLOCK
    d = w_down.shape[2]
    grid_spec = pltpu.PrefetchScalarGridSpec(
        num_scalar_prefetch=3,
        grid=(p // m,),
        in_specs=[pl.BlockSpec((1, 1, m), lambda i, be, na, nr: (i, 0, 0), memory_space=pltpu.SMEM),
                  pl.BlockSpec((m, f), lambda i, be, na, nr: (i, 0)),
                  pl.BlockSpec((1, f, d), lambda i, be, na, nr: (be[i], 0, 0)),
                  pl.BlockSpec((1, 1, d), lambda i, be, na, nr: (be[i], 0, 0))],
        out_specs=pl.BlockSpec(memory_space=pl.ANY),
        scratch_shapes=[pltpu.VMEM((f, d), BF16),
                        pltpu.VMEM((2, m, d), F32),
                        pltpu.SemaphoreType.DMA((2,))],
    )
    return pl.pallas_call(
        _moe_down_kernel,
        grid_spec=grid_spec,
        out_shape=jax.ShapeDtypeStruct((n_rows, d), F32),
        compiler_params=_cparams(("arbitrary",)),
        name="moe_down",
    )(blk_expert, n_active, nreal, dest.reshape(p // m, 1, m), act, w_down, b_down)


def _combine_kernel(h_ref, y0_ref, y1_ref, y2_ref, y3_ref, gate_ref, g2_ref, b2_ref, o_ref):
    ffn = gate_ref[:, 0:1] * y0_ref[...]
    for j, y_ref in enumerate((y1_ref, y2_ref, y3_ref), start=1):
        ffn = ffn + gate_ref[:, j:j + 1] * y_ref[...]
    o_ref[...] = _layer_norm(DEEPNORM_ALPHA * h_ref[...] + ffn, g2_ref[...], b2_ref[...])


def _combine(h, yt, gate, g2, b2):
    t, d = h.shape
    tm = TOKEN_TILE
    nt = t // tm

    def choice(j):
        return pl.BlockSpec((tm, d), lambda i: (j * nt + i, 0))

    return pl.pallas_call(
        _combine_kernel,
        grid=(nt,),
        in_specs=[pl.BlockSpec((tm, d), lambda i: (i, 0)),
                  choice(0), choice(1), choice(2), choice(3),
                  pl.BlockSpec((tm, TOP_K), lambda i: (i, 0)),
                  pl.BlockSpec((1, d), lambda i: (0, 0)),
                  pl.BlockSpec((1, d), lambda i: (0, 0))],
        out_specs=pl.BlockSpec((tm, d), lambda i: (i, 0)),
        out_shape=jax.ShapeDtypeStruct((t, d), F32),
        compiler_params=_cparams(("parallel",)),
        name="combine_ln2",
    )(h, yt, yt, yt, yt, gate, g2, b2)


def _route(top_e, lrank, tilecnt):
    t = top_e.shape[0]
    m = MOE_BLOCK
    n_assign = t * TOP_K
    n_blocks = n_assign // m + N_EXPERTS
    p = n_blocks * m
    tc = tilecnt.reshape(-1, N_EXPERTS)
    base = jnp.cumsum(tc, axis=0) - tc
    counts = jnp.sum(tc, axis=0)
    padded = (counts + m - 1) // m * m
    pend = jnp.cumsum(padded)
    pstart = pend - padded
    base_tok = jnp.repeat(base, TOKEN_TILE, axis=0)
    rank = lrank + jnp.take_along_axis(base_tok, top_e, axis=1)
    slot = (pstart[top_e] + rank).reshape(-1)
    a_idx = jnp.arange(n_assign, dtype=I32)
    dest = jnp.full((p,), -1, I32).at[slot].set((a_idx % TOP_K) * t + a_idx // TOP_K)
    tok_pad = jnp.where(dest >= 0, dest % t, 0)
    blk_start = jnp.arange(n_blocks, dtype=I32) * m
    blk_expert = jnp.minimum(jnp.sum((pend[None, :] <= blk_start[:, None]).astype(I32), axis=1),
                             N_EXPERTS - 1)
    n_active = (pend[-1:] // m).astype(I32)
    real_end = (pstart + counts)[blk_expert]
    nreal = jnp.where(blk_start < pend[-1], jnp.clip(real_end - blk_start, 0, m), 0).astype(I32)
    return tok_pad, dest, blk_expert, n_active, nreal, (pstart // m).astype(I32), (padded // m).astype(I32)


def kernel(x_prompt, x_sample, cache_attn_k, cache_attn_v, state_hgrn, w_in, hgrn_lb_logits,
           hgrn_norm_w, w_branch_a, w_branch_b, w_out, ln1_g, ln1_b, w_router, b_router,
           w_gate_up, b_gate_up, w_down, b_down, ln2_g, ln2_b):
    assert w_in.shape[0] == DEPTH == 1
    bp, lp, d = x_prompt.shape
    bs, ls, _ = x_sample.shape
    tp, ts = bp * lp, bs * ls
    t = tp + ts
    keep = min(max(w for w, _ in DILATION_PAIRS), lp)

    xa = jnp.concatenate([x_prompt.reshape(tp, d), x_sample.reshape(ts, d)], axis=0)
    xb = xa.astype(BF16)
    wi = w_in[0]
    c1, c2 = P1_WIDTH, P1_WIDTH + P2_WIDTH
    p1 = _matmul(xb, wi[:, :c1].astype(BF16), 768, 1024, F32, "inproj_hgrn")
    p2 = _matmul(xb, wi[:, c1:c2].astype(BF16), 768, 512, F32, "inproj_attn")
    p3 = _matmul(xb, wi[:, c2:].astype(BF16), 768, 1024, F32, "inproj_gate")

    oa_p, st_p = _hgrn(p1, hgrn_lb_logits, hgrn_norm_w[0], None, bp, lp, 0)
    oa_s, st_s = _hgrn(p1, hgrn_lb_logits, hgrn_norm_w[0],
                       state_hgrn.reshape(bs, HG_HEADS, HG_DIM, HG_DIM), bs, ls, tp)

    n = jnp.arange(1, ATT_Q_HEADS + 1, dtype=F32)
    slopes = jnp.power(2.0, -8.0 * n / ATT_Q_HEADS).reshape(N_GROUPS, ATT_KV_HEADS)
    ctx = cache_attn_k.shape[2]
    ob_p = _attn_prompt(p2, slopes, bp, lp)
    ob_s = _attn_sample(p2, cache_attn_k.reshape(bs, ctx * ATT_KV_HEADS, ATT_HEAD_DIM),
                        cache_attn_v.reshape(bs, ctx * ATT_KV_HEADS, ATT_HEAD_DIM), slopes, bs, ls, tp)

    oa = jnp.concatenate([oa_p.reshape(tp, HG_WIDTH), oa_s.reshape(ts, HG_WIDTH)], axis=0)
    ob = jnp.concatenate([ob_p, ob_s.reshape(ts, ATT_KV_WIDTH)], axis=0)
    wr = w_router[0]
    wr0 = wr.astype(BF16)
    wr1 = (wr - wr0.astype(F32)).astype(BF16)
    h, top_e, gate, lrank, tilecnt = _merge(
        xa, oa, ob, p3, w_branch_a[0].astype(BF16), w_branch_b[0].astype(BF16),
        w_out[0].astype(BF16), ln1_g, ln1_b, wr0, wr1, b_router)

    tok_pad, dest, blk_expert, n_active, nreal, blk0, nblk = _route(top_e, lrank, tilecnt)
    xs = _moe_gather(h, tok_pad)
    act = _moe_up(xs, w_gate_up.reshape(N_EXPERTS, d, 2 * D_FF),
                  b_gate_up.reshape(N_EXPERTS, 1, 2 * D_FF), blk0, nblk)
    yt = _moe_down(act, w_down.reshape(N_EXPERTS, D_FF, d), b_down.reshape(N_EXPERTS, 1, d),
                   blk_expert, n_active, nreal, dest, t * TOP_K)
    y = _combine(h, yt, gate, ln2_g, ln2_b)

    k_new = p2[:, ATT_Q_HEADS * ATT_HEAD_DIM:(ATT_Q_HEADS + ATT_KV_HEADS) * ATT_HEAD_DIM]
    v_new = p2[:, (ATT_Q_HEADS + ATT_KV_HEADS) * ATT_HEAD_DIM:]

    def kv_prompt(a):
        return a[:tp].reshape(bp, lp, ATT_KV_HEADS, ATT_HEAD_DIM)[:, lp - keep:][None]

    def kv_sample(a):
        return a[tp:].reshape(bs, ls, ATT_KV_HEADS, ATT_HEAD_DIM)[None]

    return (y[:tp].reshape(bp, lp, d), y[tp:].reshape(bs, ls, d),
            kv_prompt(k_new), kv_prompt(v_new), st_p[None],
            kv_sample(k_new), kv_sample(v_new), st_s[None])
```

```python
import functools
import math

import jax
import jax.numpy as jnp
from jax import lax
from jax.experimental import pallas as pl
from jax.experimental.pallas import tpu as pltpu

F32 = jnp.float32
BF16 = jnp.bfloat16
I32 = jnp.int32

D_MODEL = 2048
DEPTH = 1
HG_HEADS = 8
HG_DIM = 128
HG_WIDTH = HG_HEADS * HG_DIM
HG_CHUNK = 64
ATT_HEAD_DIM = 128
DILATION_PAIRS = ((128, 1), (512, 4), (2048, 16))
N_GROUPS = len(DILATION_PAIRS)
ATT_KV_HEADS = 4
ATT_Q_HEADS = N_GROUPS * ATT_KV_HEADS
ATT_TAPS = 128
ATT_KV_WIDTH = ATT_KV_HEADS * ATT_HEAD_DIM
ATT_SCALE = ATT_HEAD_DIM ** -0.5
N_EXPERTS = 32
TOP_K = 4
D_FF = D_MODEL
SWIGLU_ALPHA = 1.702
SWIGLU_LIMIT = 7.0
LN_EPS = 1e-5
RMS_EPS = 1e-5
DEEPNORM_ALPHA = (2 * DEPTH) ** 0.25

P1_WIDTH = 4 * HG_WIDTH
P2_WIDTH = (ATT_Q_HEADS + 2 * ATT_KV_HEADS) * ATT_HEAD_DIM
P3_WIDTH = 2 * D_MODEL

LANES = 128
SUBLANES = 8
VMEM_LIMIT = 60 * 1024 * 1024

TOKEN_TILE = 256
MOE_BLOCK = 512
MOE_FT = 1024
NEG = -1e30

_NT = (((1,), (1,)), ((), ()))
_TN = (((0,), (0,)), ((), ()))


def _cparams(sem):
    return pltpu.CompilerParams(dimension_semantics=sem, vmem_limit_bytes=VMEM_LIMIT)


def _sigmoid(x):
    return 1.0 / (1.0 + jnp.exp(-x))


def _mm_kernel(x_ref, w_ref, o_ref):
    o_ref[...] = jnp.dot(x_ref[...], w_ref[...], preferred_element_type=F32).astype(o_ref.dtype)


def _matmul(x, w, tm, tn, out_dtype, name):
    m, k = x.shape
    n = w.shape[1]
    assert m % tm == 0 and n % tn == 0
    return pl.pallas_call(
        _mm_kernel,
        grid=(n // tn, m // tm),
        in_specs=[pl.BlockSpec((tm, k), lambda j, i: (i, 0)),
                  pl.BlockSpec((k, tn), lambda j, i: (0, j))],
        out_specs=pl.BlockSpec((tm, tn), lambda j, i: (i, j)),
        out_shape=jax.ShapeDtypeStruct((m, n), out_dtype),
        compiler_params=_cparams(("parallel", "parallel")),
        name=name,
    )(x, w)


def _level_ref(g_scr, m, sl, chunk, sub):
    pieces = []
    for j in range(chunk // SUBLANES):
        if 2 * m >= SUBLANES:
            row = ((SUBLANES * j) // (2 * m)) * (2 * m) + m - 1
            pieces.append(jnp.broadcast_to(g_scr[row:row + 1, sl], (SUBLANES, LANES)))
        else:
            acc = None
            for u in range(SUBLANES // (2 * m)):
                row = SUBLANES * j + 2 * m * u + m - 1
                b = jnp.broadcast_to(g_scr[row:row + 1, sl], (SUBLANES, LANES))
                acc = b if acc is None else jnp.where(sub // (2 * m) == u, b, acc)
            pieces.append(acc)
    return pieces[0] if len(pieces) == 1 else jnp.concatenate(pieces, axis=0)


def _hgrn_kernel(lbl_ref, nw_ref, hq_ref, hf_ref, hi_ref, hg_ref, *rest, chunk, has_s0):
    if has_s0:
        s0_ref, o_ref, sout_ref, st_scr, g_scr = rest
    else:
        o_ref, sout_ref, st_scr, g_scr = rest
    c = pl.program_id(1)
    nc = pl.num_programs(1)

    @pl.when(c == 0)
    def _init():
        for h in range(HG_HEADS):
            if has_s0:
                st_scr[h] = s0_ref[0, h].T
            else:
                st_scr[h] = jnp.zeros((HG_DIM, HG_DIM), F32)

    lbl = lbl_ref[...]
    ex = jnp.exp(lbl - jnp.max(lbl, axis=0, keepdims=True))
    lb = ex[0:1] / jnp.sum(ex, axis=0, keepdims=True)

    f = lb + (1.0 - lb) * _sigmoid(hf_ref[...])
    logf = jnp.log(f)
    kall = 1.0 - f

    r_cc = lax.broadcasted_iota(I32, (chunk, chunk), 0)
    c_cc = lax.broadcasted_iota(I32, (chunk, chunk), 1)
    tri = jnp.where(r_cc >= c_cc, 1.0, 0.0).astype(BF16)
    p0 = logf.astype(BF16)
    r1 = logf - p0.astype(F32)
    p1 = r1.astype(BF16)
    p2 = (r1 - p1.astype(F32)).astype(BF16)
    g_all = (jnp.dot(tri, p0, preferred_element_type=F32)
             + jnp.dot(tri, p1, preferred_element_type=F32)
             + jnp.dot(tri, p2, preferred_element_type=F32))
    g_scr[...] = g_all

    sub = lax.broadcasted_iota(I32, (SUBLANES, LANES), 0)
    row_c = lax.broadcasted_iota(I32, (chunk, LANES), 0)
    nw = nw_ref[...]

    for h in range(HG_HEADS):
        sl = slice(h * HG_DIM, (h + 1) * HG_DIM)
        hq = hq_ref[:, sl]
        q = hq * _sigmoid(hq)
        k = kall[:, sl]
        g = g_all[:, sl]
        v = hi_ref[:, sl]
        vb = v.astype(BF16)

        p = lax.dot_general(q.astype(BF16), k.astype(BF16), _NT, preferred_element_type=F32)
        a = jnp.where(r_cc == c_cc, p, 0.0)
        m = 1
        while m < chunk:
            ref = _level_ref(g_scr, m, sl, chunk, sub)
            e = jnp.exp(-jnp.abs(g - ref))
            odd = ((row_c // m) & 1) == 1
            x = (jnp.where(odd, q, k) * e).astype(BF16)
            p = lax.dot_general(x, x, _NT, preferred_element_type=F32)
            pair = (r_cc // (2 * m)) == (c_cc // (2 * m))
            sel = jnp.where(pair, ((r_cc // m) & 1) - ((c_cc // m) & 1), 0) == 1
            a = jnp.where(sel, p, a)
            m *= 2

        st = st_scr[h]
        o = jnp.dot(a.astype(BF16), vb, preferred_element_type=F32)
        qe = (q * jnp.exp(g)).astype(BF16)
        o = o + lax.dot_general(qe, st.astype(BF16), _NT, preferred_element_type=F32)

        ms = jnp.mean(o * o, axis=-1, keepdims=True)
        hg = hg_ref[:, sl]
        out = o * lax.rsqrt(ms + RMS_EPS) * nw * (hg * _sigmoid(hg))
        o_ref[0, :, sl] = out.astype(o_ref.dtype)

        g_last = g[chunk - 1:chunk, :]
        kd = (k * jnp.exp(g_last - g)).astype(BF16)
        st_scr[h] = st * jnp.exp(g_last) + lax.dot_general(vb, kd, _TN, preferred_element_type=F32)

    @pl.when(c == nc - 1)
    def _fin():
        for h in range(HG_HEADS):
            sout_ref[0, h] = st_scr[h].T


def _hgrn(p1, lb_logits, norm_w, s0, batch, seq, row0):
    chunk = math.gcd(seq, HG_CHUNK)
    nchunk = seq // chunk
    assert row0 % chunk == 0
    rb0 = row0 // chunk

    def col(idx):
        return pl.BlockSpec((chunk, HG_WIDTH), lambda b, c: (rb0 + b * nchunk + c, idx))

    in_specs = [pl.BlockSpec(lb_logits.shape, lambda b, c: (0, 0)),
                pl.BlockSpec((1, HG_DIM), lambda b, c: (0, 0)),
                col(0), col(1), col(2), col(3)]
    args = [lb_logits, norm_w.reshape(1, HG_DIM), p1, p1, p1, p1]
    if s0 is not None:
        in_specs.append(pl.BlockSpec((1, HG_HEADS, HG_DIM, HG_DIM), lambda b, c: (b, 0, 0, 0)))
        args.append(s0)
    return pl.pallas_call(
        functools.partial(_hgrn_kernel, chunk=chunk, has_s0=s0 is not None),
        grid=(batch, nchunk),
        in_specs=in_specs,
        out_specs=[pl.BlockSpec((1, chunk, HG_WIDTH), lambda b, c: (b, c, 0)),
                   pl.BlockSpec((1, HG_HEADS, HG_DIM, HG_DIM), lambda b, c: (b, 0, 0, 0))],
        out_shape=[jax.ShapeDtypeStruct((batch, seq, HG_WIDTH), BF16),
                   jax.ShapeDtypeStruct((batch, HG_HEADS, HG_DIM, HG_DIM), F32)],
        scratch_shapes=[pltpu.VMEM((HG_HEADS, HG_DIM, HG_DIM), F32),
                        pltpu.VMEM((chunk, HG_WIDTH), F32)],
        compiler_params=_cparams(("parallel", "arbitrary")),
        name="hgrn_seq%d" % seq,
    )(*args)


QB = 128


def _attn_prompt_kernel(slopes_ref, q0_ref, q1_ref, q2_ref, k_ref, v_ref, o_ref,
                        og_scr, lse_scr, bias_scr, *, seq):
    h = pl.program_id(1)
    il = lax.broadcasted_iota(I32, (QB, 2 * QB), 0)
    jl = lax.broadcasted_iota(I32, (QB, 2 * QB), 1)
    delta = il + QB - jl
    band = jnp.where(delta >= 0, delta, ATT_TAPS + 1) <= ATT_TAPS
    prev_half = jl < QB
    q_refs = (q0_ref, q1_ref, q2_ref)

    for g, (window, dil) in enumerate(DILATION_PAIRS):
        assert window // dil == ATT_TAPS
        slope = slopes_ref[g, h]
        bias_scr[g] = jnp.where(band, (-slope * dil) * delta.astype(F32), NEG)
        nbr = (seq // dil) // QB
        q_ref = q_refs[g]

        def rows(start, dil=dil):
            return pl.ds(start, QB) if dil == 1 else pl.ds(start, QB, stride=dil)

        def body(ib, carry, g=g, dil=dil, nbr=nbr, q_ref=q_ref, rows=rows):
            res = ib // nbr
            jb = ib - res * nbr
            qs = res + dil * (jb * QB)
            ps = res + dil * jnp.maximum(jb * QB - QB, 0)
            qv = q_ref[rows(qs), :].astype(BF16)
            kb = jnp.concatenate([k_ref[rows(ps), :], k_ref[rows(qs), :]], axis=0).astype(BF16)
            vb = jnp.concatenate([v_ref[rows(ps), :], v_ref[rows(qs), :]], axis=0).astype(BF16)
            s = lax.dot_general(qv, kb, _NT, preferred_element_type=F32) * ATT_SCALE + bias_scr[g]
            s = jnp.where(jnp.where(prev_half, jb, 1) == 0, NEG, s)
            mx = jnp.max(s, axis=-1, keepdims=True)
            p = jnp.exp(s - mx)
            l = jnp.sum(p, axis=-1, keepdims=True)
            o = jnp.dot(p.astype(BF16), vb, preferred_element_type=F32) / l
            og_scr[g, rows(qs), :] = o
            lse_scr[g, rows(qs), :] = jnp.broadcast_to(mx + jnp.log(l), (QB, LANES))
            return carry

        lax.fori_loop(0, seq // QB, body, 0)

    def merge(tb, carry):
        r = pl.ds(pl.multiple_of(tb * QB, QB), QB)
        l0, l1, l2 = lse_scr[0, r, :], lse_scr[1, r, :], lse_scr[2, r, :]
        mx = jnp.maximum(jnp.maximum(l0, l1), l2)
        w0, w1, w2 = jnp.exp(l0 - mx), jnp.exp(l1 - mx), jnp.exp(l2 - mx)
        o = (w0 * og_scr[0, r, :] + w1 * og_scr[1, r, :] + w2 * og_scr[2, r, :]) / (w0 + w1 + w2)
        o_ref[r, :] = o.astype(o_ref.dtype)
        return carry

    lax.fori_loop(0, seq // QB, merge, 0)


def _attn_prompt(p2, slopes, batch, seq):
    assert seq % (QB * max(d for _, d in DILATION_PAIRS)) == 0

    def col(fn):
        return pl.BlockSpec((seq, ATT_HEAD_DIM), lambda b, h: (b, fn(h)))

    return pl.pallas_call(
        functools.partial(_attn_prompt_kernel, seq=seq),
        grid=(batch, ATT_KV_HEADS),
        in_specs=[pl.BlockSpec(memory_space=pltpu.SMEM),
                  col(lambda h: h), col(lambda h: ATT_KV_HEADS + h), col(lambda h: 2 * ATT_KV_HEADS + h),
                  col(lambda h: ATT_Q_HEADS + h), col(lambda h: ATT_Q_HEADS + ATT_KV_HEADS + h)],
        out_specs=pl.BlockSpec((seq, ATT_HEAD_DIM), lambda b, h: (b, h)),
        out_shape=jax.ShapeDtypeStruct((batch * seq, ATT_KV_WIDTH), BF16),
        scratch_shapes=[pltpu.VMEM((N_GROUPS, seq, ATT_HEAD_DIM), F32),
                        pltpu.VMEM((N_GROUPS, seq, LANES), F32),
                        pltpu.VMEM((N_GROUPS, QB, 2 * QB), F32)],
        compiler_params=_cparams(("parallel", "parallel")),
        name="attn_prompt",
    )(slopes, p2, p2, p2, p2, p2)


def _attn_sample_kernel(slopes_ref, qa_ref, qb_ref, qc_ref, kn_ref, vn_ref, kc_ref, vc_ref, o_ref,
                        *, ctx, nq):
    rows = N_GROUPS * nq
    gi = lax.broadcasted_iota(I32, (rows, 1), 0) // nq
    dil = jnp.where(gi == 0, DILATION_PAIRS[0][1], jnp.where(gi == 1, DILATION_PAIRS[1][1], DILATION_PAIRS[2][1]))
    win = jnp.where(gi == 0, DILATION_PAIRS[0][0], jnp.where(gi == 1, DILATION_PAIRS[1][0], DILATION_PAIRS[2][0]))

    def dist_valid(shape, key0, nkeys):
        r = lax.broadcasted_iota(I32, shape, 0)
        t = r - (r // nq) * nq
        col = lax.broadcasted_iota(I32, shape, 1)
        dist = ctx + t - (col + key0)
        bad = jnp.where(dist >= 0, dist & (dil - 1), 1)
        bad = jnp.where(dist <= win, bad, 1)
        bad = jnp.where(col < nkeys, bad, 1)
        return dist.astype(F32), bad == 0

    dist_c, ok_c = dist_valid((rows, ctx), 0, ctx)
    dist_n, ok_n = dist_valid((rows, LANES), ctx, nq)
    zpad = jnp.zeros((LANES - nq, ATT_HEAD_DIM), F32)

    for h in range(ATT_KV_HEADS):
        hs = slice(h * ATT_HEAD_DIM, (h + 1) * ATT_HEAD_DIM)
        slope = jnp.where(gi == 0, slopes_ref[0, h], jnp.where(gi == 1, slopes_ref[1, h], slopes_ref[2, h]))
        qh = jnp.concatenate([qa_ref[:, hs], qb_ref[:, hs], qc_ref[:, hs]], axis=0).astype(BF16)
        kc = kc_ref[0, pl.ds(h, ctx, stride=ATT_KV_HEADS), :].astype(BF16)
        vc = vc_ref[0, pl.ds(h, ctx, stride=ATT_KV_HEADS), :].astype(BF16)
        kn = jnp.concatenate([kn_ref[:, hs], zpad], axis=0).astype(BF16)
        vn = jnp.concatenate([vn_ref[:, hs], zpad], axis=0).astype(BF16)
        sc = lax.dot_general(qh, kc, _NT, preferred_element_type=F32) * ATT_SCALE - slope * dist_c
        sn = lax.dot_general(qh, kn, _NT, preferred_element_type=F32) * ATT_SCALE - slope * dist_n
        sc = jnp.where(ok_c, sc, NEG)
        sn = jnp.where(ok_n, sn, NEG)
        mx = jnp.maximum(jnp.max(sc, axis=-1, keepdims=True), jnp.max(sn, axis=-1, keepdims=True))
        pc = jnp.exp(sc - mx)
        pn = jnp.exp(sn - mx)
        l = jnp.sum(pc, axis=-1, keepdims=True) + jnp.sum(pn, axis=-1, keepdims=True)
        o = (jnp.dot(pc.astype(BF16), vc, preferred_element_type=F32)
             + jnp.dot(pn.astype(BF16), vn, preferred_element_type=F32)) / l
        lse = mx + jnp.log(l)
        l0, l1, l2 = lse[0:nq], lse[nq:2 * nq], lse[2 * nq:3 * nq]
        m3 = jnp.maximum(jnp.maximum(l0, l1), l2)
        w0, w1, w2 = jnp.exp(l0 - m3), jnp.exp(l1 - m3), jnp.exp(l2 - m3)
        out = (w0 * o[0:nq] + w1 * o[nq:2 * nq] + w2 * o[2 * nq:3 * nq]) / (w0 + w1 + w2)
        o_ref[0, :, hs] = out.astype(o_ref.dtype)


def _attn_sample(p2, cache_k, cache_v, slopes, batch, nq, row0):
    ctx = cache_k.shape[1] // ATT_KV_HEADS
    assert row0 % nq == 0 and nq == SUBLANES and ctx >= max(w for w, _ in DILATION_PAIRS)
    rb0 = row0 // nq

    def col(idx):
        return pl.BlockSpec((nq, ATT_KV_WIDTH), lambda b: (rb0 + b, idx))

    cache_spec = pl.BlockSpec((1, ctx * ATT_KV_HEADS, ATT_HEAD_DIM), lambda b: (b, 0, 0))
    return pl.pallas_call(
        functools.partial(_attn_sample_kernel, ctx=ctx, nq=nq),
        grid=(batch,),
        in_specs=[pl.BlockSpec(memory_space=pltpu.SMEM),
                  col(0), col(1), col(2), col(3), col(4), cache_spec, cache_spec],
        out_specs=pl.BlockSpec((1, nq, ATT_KV_WIDTH), lambda b: (b, 0, 0)),
        out_shape=jax.ShapeDtypeStruct((batch, nq, ATT_KV_WIDTH), BF16),
        compiler_params=_cparams(("parallel",)),
        name="attn_sample",
    )(slopes, p2, p2, p2, p2, p2, cache_k, cache_v)


def _layer_norm(y, g, b):
    mu = jnp.mean(y, axis=-1, keepdims=True)
    yc = y - mu
    var = jnp.mean(yc * yc, axis=-1, keepdims=True)
    return yc * lax.rsqrt(var + LN_EPS) * g + b


def _merge_kernel(x_ref, oa_ref, ob_ref, ga_ref, gb_ref, wa_ref, wb_ref, wo_ref, g1_ref, b1_ref,
                  wr0_ref, wr1_ref, br_ref,
                  h_ref, tope_ref, gate_ref, lrank_ref, cnt_ref):
    tm = x_ref.shape[0]
    a = jnp.dot(oa_ref[...], wa_ref[...], preferred_element_type=F32)
    b = jnp.dot(ob_ref[...], wb_ref[...], preferred_element_type=F32)
    merged = _sigmoid(ga_ref[...]) * a + _sigmoid(gb_ref[...]) * b
    z = jnp.dot(merged.astype(BF16), wo_ref[...], preferred_element_type=F32)
    hh = _layer_norm(DEEPNORM_ALPHA * x_ref[...] + z, g1_ref[...], b1_ref[...])
    h_ref[...] = hh

    h0 = hh.astype(BF16)
    h1 = (hh - h0.astype(F32)).astype(BF16)
    logits = (jnp.dot(h0, wr0_ref[...], preferred_element_type=F32)
              + jnp.dot(h1, wr0_ref[...], preferred_element_type=F32)
              + jnp.dot(h0, wr1_ref[...], preferred_element_type=F32)) + br_ref[...]

    lane = lax.broadcasted_iota(I32, (tm, N_EXPERTS), 1)
    work = logits
    vals, idxs = [], []
    onehot = jnp.zeros((tm, N_EXPERTS), F32)
    for _ in range(TOP_K):
        mx = jnp.max(work, axis=-1, keepdims=True)
        idx = jnp.min(jnp.where(work == mx, lane, N_EXPERTS), axis=-1, keepdims=True)
        hit = lane == idx
        vals.append(mx)
        idxs.append(idx)
        onehot = jnp.where(hit, 1.0, onehot)
        work = jnp.where(hit, -jnp.inf, work)
    ex = [jnp.exp(v - vals[0]) for v in vals]
    den = ex[0] + ex[1] + ex[2] + ex[3]

    r_tt = lax.broadcasted_iota(I32, (tm, tm), 0)
    c_tt = lax.broadcasted_iota(I32, (tm, tm), 1)
    before = jnp.where(r_tt > c_tt, 1.0, 0.0).astype(BF16)
    prefix = jnp.dot(before, onehot.astype(BF16), preferred_element_type=F32)
    for j in range(TOP_K):
        tope_ref[:, j:j + 1] = idxs[j]
        gate_ref[:, j:j + 1] = ex[j] / den
        lrank_ref[:, j:j + 1] = jnp.sum(jnp.where(lane == idxs[j], prefix, 0.0), axis=-1,
                                        keepdims=True).astype(I32)
    cnt_ref[0] = jnp.sum(onehot, axis=0, keepdims=True).astype(I32)


def _merge(xa, oa, ob, p3, wa, wb, wo, g1, b1, wr0, wr1, br):
    t = xa.shape[0]
    tm = TOKEN_TILE
    nt = t // tm
    assert t % tm == 0

    def full(a):
        return pl.BlockSpec(a.shape, lambda i: (0,) * a.ndim)

    def rowblk(width, idx=0):
        return pl.BlockSpec((tm, width), lambda i: (i, idx))

    return pl.pallas_call(
        _merge_kernel,
        grid=(nt,),
        in_specs=[rowblk(D_MODEL), rowblk(HG_WIDTH), rowblk(ATT_KV_WIDTH), rowblk(D_MODEL, 0),
                  rowblk(D_MODEL, 1), full(wa), full(wb), full(wo), full(g1), full(b1),
                  full(wr0), full(wr1), full(br)],
        out_specs=[rowblk(D_MODEL), rowblk(TOP_K), rowblk(TOP_K), rowblk(TOP_K),
                   pl.BlockSpec((1, 1, N_EXPERTS), lambda i: (i, 0, 0))],
        out_shape=[jax.ShapeDtypeStruct((t, D_MODEL), F32),
                   jax.ShapeDtypeStruct((t, TOP_K), I32),
                   jax.ShapeDtypeStruct((t, TOP_K), F32),
                   jax.ShapeDtypeStruct((t, TOP_K), I32),
                   jax.ShapeDtypeStruct((nt, 1, N_EXPERTS), I32)],
        compiler_params=_cparams(("parallel",)),
        name="merge_ln1_router",
    )(xa, oa, ob, p3, p3, wa, wb, wo, g1, b1, wr0, wr1, br)


def _gather_kernel(tok_ref, h_ref, o_ref, buf_ref):
    m = o_ref.shape[0]

    def body(q, carry):
        for u in range(SUBLANES):
            r = q * SUBLANES + u
            buf_ref[pl.ds(r, 1), :] = h_ref[pl.ds(tok_ref[0, 0, r], 1), :]
        return carry

    lax.fori_loop(0, m // SUBLANES, body, 0)
    o_ref[...] = buf_ref[...].astype(o_ref.dtype)


def _moe_gather(h, tok_pad):
    t, d = h.shape
    p = tok_pad.shape[0]
    m = MOE_BLOCK
    half = d // 2
    return pl.pallas_call(
        _gather_kernel,
        grid=(2, p // m),
        in_specs=[pl.BlockSpec((1, 1, m), lambda c, i: (i, 0, 0), memory_space=pltpu.SMEM),
                  pl.BlockSpec((t, half), lambda c, i: (0, c), pipeline_mode=pl.Buffered(1))],
        out_specs=pl.BlockSpec((m, half), lambda c, i: (i, c)),
        out_shape=jax.ShapeDtypeStruct((p, d), BF16),
        scratch_shapes=[pltpu.VMEM((m, half), F32)],
        compiler_params=_cparams(("arbitrary", "arbitrary")),
        name="moe_gather",
    )(tok_pad.reshape(p // m, 1, m), h)


def _moe_up_kernel(blk0_ref, nblk_ref, w_ref, b_ref, xs_ref, act_ref, wbf_scr, x_scr, o_scr,
                   sem_in, sem_out):
    e = pl.program_id(0)
    f = pl.program_id(1)
    n = nblk_ref[e]
    b0 = blk0_ref[e]
    m = x_scr.shape[1]
    ft = w_ref.shape[2]
    half = ft // 2

    def rows(b):
        return pl.ds(pl.multiple_of((b0 + b) * m, m), m)

    def x_copy(b, s):
        return pltpu.make_async_copy(xs_ref.at[rows(b)], x_scr.at[s], sem_in.at[s])

    def o_copy(b, s):
        cols = pl.ds(pl.multiple_of(f * half, LANES), half)
        return pltpu.make_async_copy(o_scr.at[s], act_ref.at[rows(b), cols], sem_out.at[s])

    @pl.when(n > 0)
    def _():
        x_copy(0, 0).start()
        wbf_scr[...] = w_ref[0].astype(BF16)
        even = (lax.broadcasted_iota(I32, (m, LANES), 1) & 1) == 0

        def body(b, carry):
            s = b & 1
            x_copy(b, s).wait()

            @pl.when(b + 1 < n)
            def _():
                x_copy(b + 1, 1 - s).start()

            @pl.when(b >= 2)
            def _():
                o_copy(b - 2, s).wait()

            x = x_scr[s]
            for g in range(ft // (2 * LANES)):
                cols = slice(2 * g * LANES, (2 * g + 2) * LANES)
                h = jnp.dot(x, wbf_scr[:, cols], preferred_element_type=F32) + b_ref[0, :, cols]
                ha, hb = h[:, :LANES], h[:, LANES:]
                glu = jnp.minimum(jnp.where(even, ha, pltpu.roll(hb, 1, 1)), SWIGLU_LIMIT)
                lin = jnp.where(even, pltpu.roll(ha, LANES - 1, 1), hb)
                lin = jnp.clip(lin, -SWIGLU_LIMIT, SWIGLU_LIMIT) + 1.0
                act = glu * _sigmoid(SWIGLU_ALPHA * glu) * lin
                o_scr[s, :, g * LANES:(g + 1) * LANES] = act.astype(o_scr.dtype)
            o_copy(b, s).start()
            return carry

        lax.fori_loop(0, n, body, 0)

        @pl.when(n >= 2)
        def _():
            o_copy(n - 2, n & 1).wait()
        o_copy(n - 1, (n - 1) & 1).wait()

    @pl.when(e == pl.num_programs(0) - 1)
    def _():
        o_scr[0] = jnp.zeros(o_scr.shape[1:], o_scr.dtype)

        nz = act_ref.shape[0] // m - b0
        lax.fori_loop(n, nz, lambda b, c: (o_copy(b, 0).start(), c)[1], 0)
        lax.fori_loop(n, nz, lambda b, c: (o_copy(b, 0).wait(), c)[1], 0)


def _moe_up(xs, w_gate_up, b_gate_up, blk0, nblk):
    p, d = xs.shape
    m = MOE_BLOCK
    ft = MOE_FT
    ne = w_gate_up.shape[0]
    nf = w_gate_up.shape[2] // ft
    grid_spec = pltpu.PrefetchScalarGridSpec(
        num_scalar_prefetch=2,
        grid=(ne, nf),
        in_specs=[pl.BlockSpec((1, d, ft), lambda e, f, b0, nb: (e, 0, f)),
                  pl.BlockSpec((1, 1, ft), lambda e, f, b0, nb: (e, 0, f)),
                  pl.BlockSpec(memory_space=pl.ANY)],
        out_specs=pl.BlockSpec(memory_space=pl.ANY),
        scratch_shapes=[pltpu.VMEM((d, ft), BF16),
                        pltpu.VMEM((2, m, d), BF16),
                        pltpu.VMEM((2, m, ft // 2), BF16),
                        pltpu.SemaphoreType.DMA((2,)),
                        pltpu.SemaphoreType.DMA((2,))],
    )
    return pl.pallas_call(
        _moe_up_kernel,
        grid_spec=grid_spec,
        out_shape=jax.ShapeDtypeStruct((p, w_gate_up.shape[2] // 2), BF16),
        compiler_params=_cparams(("arbitrary", "arbitrary")),
        name="moe_up",
    )(blk0, nblk, w_gate_up, b_gate_up, xs)


def _hidden_perm():
    r = lax.broadcasted_iota(I32, (LANES, LANES), 0)
    c = lax.broadcasted_iota(I32, (LANES, LANES), 1)
    src = (r >> 1) + (r & 1) * (LANES // 2)
    return jnp.where(c == src, 1.0, 0.0).astype(BF16)


def _moe_down_kernel(be_ref, na_ref, nreal_ref, dest_ref, a_ref, wd_ref, bd_ref, yt_ref,
                     wbf_scr, y_scr, sem):
    i = pl.program_id(0)
    nb = pl.num_programs(0)
    m = a_ref.shape[0]
    slot = i % 2

    def row_copy(s, r, d):
        return pltpu.make_async_copy(y_scr.at[s, pl.ds(r, 1)], yt_ref.at[pl.ds(d, 1)], sem.at[s])

    def drain(s, count):
        @pl.when(count == m)
        def _():
            pltpu.make_async_copy(y_scr.at[s], yt_ref.at[pl.ds(0, m)], sem.at[s]).wait()

        @pl.when(count != m)
        def _():
            def wbody(_, carry):
                row_copy(s, 0, 0).wait()
                return carry
            lax.fori_loop(0, count, wbody, 0)

    @pl.when(i >= 2)
    def _():
        drain(slot, nreal_ref[i - 2])

    changed = jnp.where(i == 0, 1, be_ref[i] - be_ref[jnp.maximum(i - 1, 0)])

    @pl.when(changed != 0)
    def _():
        perm = _hidden_perm()
        for g in range(wd_ref.shape[1] // LANES):
            rows = slice(g * LANES, (g + 1) * LANES)
            wbf_scr[rows, :] = jnp.dot(perm, wd_ref[0, rows, :].astype(BF16),
                                       preferred_element_type=F32).astype(BF16)

    @pl.when(i < na_ref[0])
    def _():
        y_scr[slot] = jnp.dot(a_ref[...], wbf_scr[...], preferred_element_type=F32) + bd_ref[0]

        n = nreal_ref[i]
        unroll = 8

        def sbody8(q, carry):
            for u in range(unroll):
                r = q * unroll + u
                row_copy(slot, r, dest_ref[0, 0, r]).start()
            return carry

        def sbody1(r, carry):
            row_copy(slot, r, dest_ref[0, 0, r]).start()
            return carry

        lax.fori_loop(0, n // unroll, sbody8, 0)
        lax.fori_loop((n // unroll) * unroll, n, sbody1, 0)

    @pl.when(i == nb - 1)
    def _():
        @pl.when(i >= 1)
        def _():
            drain(1 - slot, nreal_ref[i - 1])
        drain(slot, nreal_ref[i])


def _moe_down(act, w_down, b_down, blk_expert, n_active, nreal, dest, n_rows):
    p, f = act.shape
    m = MOE_BLOCK
    d = w_down.shape[2]
    grid_spec = pltpu.PrefetchScalarGridSpec(
        num_scalar_prefetch=3,
        grid=(p // m,),
        in_specs=[pl.BlockSpec((1, 1, m), lambda i, be, na, nr: (i, 0, 0), memory_space=pltpu.SMEM),
                  pl.BlockSpec((m, f), lambda i, be, na, nr: (i, 0)),
                  pl.BlockSpec((1, f, d), lambda i, be, na, nr: (be[i], 0, 0)),
                  pl.BlockSpec((1, 1, d), lambda i, be, na, nr: (be[i], 0, 0))],
        out_specs=pl.BlockSpec(memory_space=pl.ANY),
        scratch_shapes=[pltpu.VMEM((f, d), BF16),
                        pltpu.VMEM((2, m, d), F32),
                        pltpu.SemaphoreType.DMA((2,))],
    )
    return pl.pallas_call(
        _moe_down_kernel,
        grid_spec=grid_spec,
        out_shape=jax.ShapeDtypeStruct((n_rows, d), F32),
        compiler_params=_cparams(("arbitrary",)),
        name="moe_down",
    )(blk_expert, n_active, nreal, dest.reshape(p // m, 1, m), act, w_down, b_down)


def _combine_kernel(h_ref, y0_ref, y1_ref, y2_ref, y3_ref, gate_ref, g2_ref, b2_ref, o_ref):
    ffn = gate_ref[:, 0:1] * y0_ref[...]
    for j, y_ref in enumerate((y1_ref, y2_ref, y3_ref), start=1):
        ffn = ffn + gate_ref[:, j:j + 1] * y_ref[...]
    o_ref[...] = _layer_norm(DEEPNORM_ALPHA * h_ref[...] + ffn, g2_ref[...], b2_ref[...])


def _combine(h, yt, gate, g2, b2):
    t, d = h.shape
    tm = TOKEN_TILE
    nt = t // tm

    def choice(j):
        return pl.BlockSpec((tm, d), lambda i: (j * nt + i, 0))

    return pl.pallas_call(
        _combine_kernel,
        grid=(nt,),
        in_specs=[pl.BlockSpec((tm, d), lambda i: (i, 0)),
                  choice(0), choice(1), choice(2), choice(3),
                  pl.BlockSpec((tm, TOP_K), lambda i: (i, 0)),
                  pl.BlockSpec((1, d), lambda i: (0, 0)),
                  pl.BlockSpec((1, d), lambda i: (0, 0))],
        out_specs=pl.BlockSpec((tm, d), lambda i: (i, 0)),
        out_shape=jax.ShapeDtypeStruct((t, d), F32),
        compiler_params=_cparams(("parallel",)),
        name="combine_ln2",
    )(h, yt, yt, yt, yt, gate, g2, b2)


def _route(top_e, lrank, tilecnt):
    t = top_e.shape[0]
    m = MOE_BLOCK
    n_assign = t * TOP_K
    n_blocks = n_assign // m + N_EXPERTS
    p = n_blocks * m
    tc = tilecnt.reshape(-1, N_EXPERTS)
    base = jnp.cumsum(tc, axis=0) - tc
    counts = jnp.sum(tc, axis=0)
    padded = (counts + m - 1) // m * m
    pend = jnp.cumsum(padded)
    pstart = pend - padded
    base_tok = jnp.repeat(base, TOKEN_TILE, axis=0)
    rank = lrank + jnp.take_along_axis(base_tok, top_e, axis=1)
    slot = (pstart[top_e] + rank).reshape(-1)
    a_idx = jnp.arange(n_assign, dtype=I32)
    dest = jnp.full((p,), -1, I32).at[slot].set((a_idx % TOP_K) * t + a_idx // TOP_K)
    tok_pad = jnp.where(dest >= 0, dest % t, 0)
    blk_start = jnp.arange(n_blocks, dtype=I32) * m
    blk_expert = jnp.minimum(jnp.sum((pend[None, :] <= blk_start[:, None]).astype(I32), axis=1),
                             N_EXPERTS - 1)
    n_active = (pend[-1:] // m).astype(I32)
    real_end = (pstart + counts)[blk_expert]
    nreal = jnp.where(blk_start < pend[-1], jnp.clip(real_end - blk_start, 0, m), 0).astype(I32)
    return tok_pad, dest, blk_expert, n_active, nreal, (pstart // m).astype(I32), (padded // m).astype(I32)


def kernel(x_prompt, x_sample, cache_attn_k, cache_attn_v, state_hgrn, w_in, hgrn_lb_logits,
           hgrn_norm_w, w_branch_a, w_branch_b, w_out, ln1_g, ln1_b, w_router, b_router,
           w_gate_up, b_gate_up, w_down, b_down, ln2_g, ln2_b):
    assert w_in.shape[0] == DEPTH == 1
    bp, lp, d = x_prompt.shape
    bs, ls, _ = x_sample.shape
    tp, ts = bp * lp, bs * ls
    t = tp + ts
    keep = min(max(w for w, _ in DILATION_PAIRS), lp)

    xa = jnp.concatenate([x_prompt.reshape(tp, d), x_sample.reshape(ts, d)], axis=0)
    xb = xa.astype(BF16)
    wi = w_in[0]
    c1, c2 = P1_WIDTH, P1_WIDTH + P2_WIDTH
    p1 = _matmul(xb, wi[:, :c1].astype(BF16), 768, 1024, F32, "inproj_hgrn")
    p2 = _matmul(xb, wi[:, c1:c2].astype(BF16), 768, 512, F32, "inproj_attn")
    p3 = _matmul(xb, wi[:, c2:].astype(BF16), 768, 1024, F32, "inproj_gate")

    oa_p, st_p = _hgrn(p1, hgrn_lb_logits, hgrn_norm_w[0], None, bp, lp, 0)
    oa_s, st_s = _hgrn(p1, hgrn_lb_logits, hgrn_norm_w[0],
                       state_hgrn.reshape(bs, HG_HEADS, HG_DIM, HG_DIM), bs, ls, tp)

    n = jnp.arange(1, ATT_Q_HEADS + 1, dtype=F32)
    slopes = jnp.power(2.0, -8.0 * n / ATT_Q_HEADS).reshape(N_GROUPS, ATT_KV_HEADS)
    ctx = cache_attn_k.shape[2]
    ob_p = _attn_prompt(p2, slopes, bp, lp)
    ob_s = _attn_sample(p2, cache_attn_k.reshape(bs, ctx * ATT_KV_HEADS, ATT_HEAD_DIM),
                        cache_attn_v.reshape(bs, ctx * ATT_KV_HEADS, ATT_HEAD_DIM), slopes, bs, ls, tp)

    oa = jnp.concatenate([oa_p.reshape(tp, HG_WIDTH), oa_s.reshape(ts, HG_WIDTH)], axis=0)
    ob = jnp.concatenate([ob_p, ob_s.reshape(ts, ATT_KV_WIDTH)], axis=0)
    wr = w_router[0]
    wr0 = wr.astype(BF16)
    wr1 = (wr - wr0.astype(F32)).astype(BF16)
    h, top_e, gate, lrank, tilecnt = _merge(
        xa, oa, ob, p3, w_branch_a[0].astype(BF16), w_branch_b[0].astype(BF16),
        w_out[0].astype(BF16), ln1_g, ln1_b, wr0, wr1, b_router)

    tok_pad, dest, blk_expert, n_active, nreal, blk0, nblk = _route(top_e, lrank, tilecnt)
    xs = _moe_gather(h, tok_pad)
    act = _moe_up(xs, w_gate_up.reshape(N_EXPERTS, d, 2 * D_FF),
                  b_gate_up.reshape(N_EXPERTS, 1, 2 * D_FF), blk0, nblk)
    yt = _moe_down(act, w_down.reshape(N_EXPERTS, D_FF, d), b_down.reshape(N_EXPERTS, 1, d),
                   blk_expert, n_active, nreal, dest, t * TOP_K)
    y = _combine(h, yt, gate, ln2_g, ln2_b)

    k_new = p2[:, ATT_Q_HEADS * ATT_HEAD_DIM:(ATT_Q_HEADS + ATT_KV_HEADS) * ATT_HEAD_DIM]
    v_new = p2[:, (ATT_Q_HEADS + ATT_KV_HEADS) * ATT_HEAD_DIM:]

    def kv_prompt(a):
        return a[:tp].reshape(bp, lp, ATT_KV_HEADS, ATT_HEAD_DIM)[:, lp - keep:][None]

    def kv_sample(a):
        return a[tp:].reshape(bs, ls, ATT_KV_HEADS, ATT_HEAD_DIM)[None]

    return (y[:tp].reshape(bp, lp, d), y[tp:].reshape(bs, ls, d),
            kv_prompt(k_new), kv_prompt(v_new), st_p[None],
            kv_sample(k_new), kv_sample(v_new), st_s[None])
```

```python
import functools
import math

import jax
import jax.numpy as jnp
from jax import lax
from jax.experimental import pallas as pl
from jax.experimental.pallas import tpu as pltpu

F32 = jnp.float32
BF16 = jnp.bfloat16
I32 = jnp.int32

D_MODEL = 2048
DEPTH = 1
HG_HEADS = 8
HG_DIM = 128
HG_WIDTH = HG_HEADS * HG_DIM
HG_CHUNK = 64
ATT_HEAD_DIM = 128
DILATION_PAIRS = ((128, 1), (512, 4), (2048, 16))
N_GROUPS = len(DILATION_PAIRS)
ATT_KV_HEADS = 4
ATT_Q_HEADS = N_GROUPS * ATT_KV_HEADS
ATT_TAPS = 128
ATT_KV_WIDTH = ATT_KV_HEADS * ATT_HEAD_DIM
ATT_SCALE = ATT_HEAD_DIM ** -0.5
N_EXPERTS = 32
TOP_K = 4
D_FF = D_MODEL
SWIGLU_ALPHA = 1.702
SWIGLU_LIMIT = 7.0
LN_EPS = 1e-5
RMS_EPS = 1e-5
DEEPNORM_ALPHA = (2 * DEPTH) ** 0.25

P1_WIDTH = 4 * HG_WIDTH
P2_WIDTH = (ATT_Q_HEADS + 2 * ATT_KV_HEADS) * ATT_HEAD_DIM
P3_WIDTH = 2 * D_MODEL

LANES = 128
SUBLANES = 8
VMEM_LIMIT = 60 * 1024 * 1024

TOKEN_TILE = 256
MOE_BLOCK = 256
MOE_FT = 2048
NEG = -1e30

_NT = (((1,), (1,)), ((), ()))
_TN = (((0,), (0,)), ((), ()))


def _cparams(sem):
    return pltpu.CompilerParams(dimension_semantics=sem, vmem_limit_bytes=VMEM_LIMIT)


def _sigmoid(x):
    return 1.0 / (1.0 + jnp.exp(-x))


def _mm_kernel(x_ref, w_ref, o_ref):
    o_ref[...] = jnp.dot(x_ref[...], w_ref[...], preferred_element_type=F32).astype(o_ref.dtype)


def _matmul(x, w, tm, tn, out_dtype, name):
    m, k = x.shape
    n = w.shape[1]
    assert m % tm == 0 and n % tn == 0
    return pl.pallas_call(
        _mm_kernel,
        grid=(n // tn, m // tm),
        in_specs=[pl.BlockSpec((tm, k), lambda j, i: (i, 0)),
                  pl.BlockSpec((k, tn), lambda j, i: (0, j))],
        out_specs=pl.BlockSpec((tm, tn), lambda j, i: (i, j)),
        out_shape=jax.ShapeDtypeStruct((m, n), out_dtype),
        compiler_params=_cparams(("parallel", "parallel")),
        name=name,
    )(x, w)


def _level_ref(g_scr, m, sl, chunk, sub):
    pieces = []
    for j in range(chunk // SUBLANES):
        if 2 * m >= SUBLANES:
            row = ((SUBLANES * j) // (2 * m)) * (2 * m) + m - 1
            pieces.append(jnp.broadcast_to(g_scr[row:row + 1, sl], (SUBLANES, LANES)))
        else:
            acc = None
            for u in range(SUBLANES // (2 * m)):
                row = SUBLANES * j + 2 * m * u + m - 1
                b = jnp.broadcast_to(g_scr[row:row + 1, sl], (SUBLANES, LANES))
                acc = b if acc is None else jnp.where(sub // (2 * m) == u, b, acc)
            pieces.append(acc)
    return pieces[0] if len(pieces) == 1 else jnp.concatenate(pieces, axis=0)


def _hgrn_kernel(lbl_ref, nw_ref, hq_ref, hf_ref, hi_ref, hg_ref, *rest, chunk, has_s0):
    if has_s0:
        s0_ref, o_ref, sout_ref, st_scr, g_scr = rest
    else:
        o_ref, sout_ref, st_scr, g_scr = rest
    c = pl.program_id(1)
    nc = pl.num_programs(1)

    @pl.when(c == 0)
    def _init():
        for h in range(HG_HEADS):
            if has_s0:
                st_scr[h] = s0_ref[0, h].T
            else:
                st_scr[h] = jnp.zeros((HG_DIM, HG_DIM), F32)

    lbl = lbl_ref[...]
    ex = jnp.exp(lbl - jnp.max(lbl, axis=0, keepdims=True))
    lb = ex[0:1] / jnp.sum(ex, axis=0, keepdims=True)

    f = lb + (1.0 - lb) * _sigmoid(hf_ref[...])
    logf = jnp.log(f)
    kall = 1.0 - f

    r_cc = lax.broadcasted_iota(I32, (chunk, chunk), 0)
    c_cc = lax.broadcasted_iota(I32, (chunk, chunk), 1)
    tri = jnp.where(r_cc >= c_cc, 1.0, 0.0).astype(BF16)
    p0 = logf.astype(BF16)
    r1 = logf - p0.astype(F32)
    p1 = r1.astype(BF16)
    p2 = (r1 - p1.astype(F32)).astype(BF16)
    g_all = (jnp.dot(tri, p0, preferred_element_type=F32)
             + jnp.dot(tri, p1, preferred_element_type=F32)
             + jnp.dot(tri, p2, preferred_element_type=F32))
    g_scr[...] = g_all

    sub = lax.broadcasted_iota(I32, (SUBLANES, LANES), 0)
    row_c = lax.broadcasted_iota(I32, (chunk, LANES), 0)
    nw = nw_ref[...]

    for h in range(HG_HEADS):
        sl = slice(h * HG_DIM, (h + 1) * HG_DIM)
        hq = hq_ref[:, sl]
        q = hq * _sigmoid(hq)
        k = kall[:, sl]
        g = g_all[:, sl]
        v = hi_ref[:, sl]
        vb = v.astype(BF16)

        p = lax.dot_general(q.astype(BF16), k.astype(BF16), _NT, preferred_element_type=F32)
        a = jnp.where(r_cc == c_cc, p, 0.0)
        m = 1
        while m < chunk:
            ref = _level_ref(g_scr, m, sl, chunk, sub)
            e = jnp.exp(-jnp.abs(g - ref))
            odd = ((row_c // m) & 1) == 1
            x = (jnp.where(odd, q, k) * e).astype(BF16)
            p = lax.dot_general(x, x, _NT, preferred_element_type=F32)
            pair = (r_cc // (2 * m)) == (c_cc // (2 * m))
            sel = jnp.where(pair, ((r_cc // m) & 1) - ((c_cc // m) & 1), 0) == 1
            a = jnp.where(sel, p, a)
            m *= 2

        st = st_scr[h]
        o = jnp.dot(a.astype(BF16), vb, preferred_element_type=F32)
        qe = (q * jnp.exp(g)).astype(BF16)
        o = o + lax.dot_general(qe, st.astype(BF16), _NT, preferred_element_type=F32)

        ms = jnp.mean(o * o, axis=-1, keepdims=True)
        hg = hg_ref[:, sl]
        out = o * lax.rsqrt(ms + RMS_EPS) * nw * (hg * _sigmoid(hg))
        o_ref[0, :, sl] = out.astype(o_ref.dtype)

        g_last = g[chunk - 1:chunk, :]
        kd = (k * jnp.exp(g_last - g)).astype(BF16)
        st_scr[h] = st * jnp.exp(g_last) + lax.dot_general(vb, kd, _TN, preferred_element_type=F32)

    @pl.when(c == nc - 1)
    def _fin():
        for h in range(HG_HEADS):
            sout_ref[0, h] = st_scr[h].T


def _hgrn(p1, lb_logits, norm_w, s0, batch, seq, row0):
    chunk = math.gcd(seq, HG_CHUNK)
    nchunk = seq // chunk
    assert row0 % chunk == 0
    rb0 = row0 // chunk

    def col(idx):
        return pl.BlockSpec((chunk, HG_WIDTH), lambda b, c: (rb0 + b * nchunk + c, idx))

    in_specs = [pl.BlockSpec(lb_logits.shape, lambda b, c: (0, 0)),
                pl.BlockSpec((1, HG_DIM), lambda b, c: (0, 0)),
                col(0), col(1), col(2), col(3)]
    args = [lb_logits, norm_w.reshape(1, HG_DIM), p1, p1, p1, p1]
    if s0 is not None:
        in_specs.append(pl.BlockSpec((1, HG_HEADS, HG_DIM, HG_DIM), lambda b, c: (b, 0, 0, 0)))
        args.append(s0)
    return pl.pallas_call(
        functools.partial(_hgrn_kernel, chunk=chunk, has_s0=s0 is not None),
        grid=(batch, nchunk),
        in_specs=in_specs,
        out_specs=[pl.BlockSpec((1, chunk, HG_WIDTH), lambda b, c: (b, c, 0)),
                   pl.BlockSpec((1, HG_HEADS, HG_DIM, HG_DIM), lambda b, c: (b, 0, 0, 0))],
        out_shape=[jax.ShapeDtypeStruct((batch, seq, HG_WIDTH), BF16),
                   jax.ShapeDtypeStruct((batch, HG_HEADS, HG_DIM, HG_DIM), F32)],
        scratch_shapes=[pltpu.VMEM((HG_HEADS, HG_DIM, HG_DIM), F32),
                        pltpu.VMEM((chunk, HG_WIDTH), F32)],
        compiler_params=_cparams(("parallel", "arbitrary")),
        name="hgrn_seq%d" % seq,
    )(*args)


QB = 128


def _attn_prompt_kernel(slopes_ref, q0_ref, q1_ref, q2_ref, k_ref, v_ref, o_ref,
                        og_scr, lse_scr, bias_scr, *, seq):
    h = pl.program_id(1)
    il = lax.broadcasted_iota(I32, (QB, 2 * QB), 0)
    jl = lax.broadcasted_iota(I32, (QB, 2 * QB), 1)
    delta = il + QB - jl
    band = jnp.where(delta >= 0, delta, ATT_TAPS + 1) <= ATT_TAPS
    prev_half = jl < QB
    q_refs = (q0_ref, q1_ref, q2_ref)

    for g, (window, dil) in enumerate(DILATION_PAIRS):
        assert window // dil == ATT_TAPS
        slope = slopes_ref[g, h]
        bias_scr[g] = jnp.where(band, (-slope * dil) * delta.astype(F32), NEG)
        nbr = (seq // dil) // QB
        q_ref = q_refs[g]

        def rows(start, dil=dil):
            return pl.ds(start, QB) if dil == 1 else pl.ds(start, QB, stride=dil)

        def body(ib, carry, g=g, dil=dil, nbr=nbr, q_ref=q_ref, rows=rows):
            res = ib // nbr
            jb = ib - res * nbr
            qs = res + dil * (jb * QB)
            ps = res + dil * jnp.maximum(jb * QB - QB, 0)
            qv = q_ref[rows(qs), :].astype(BF16)
            kb = jnp.concatenate([k_ref[rows(ps), :], k_ref[rows(qs), :]], axis=0).astype(BF16)
            vb = jnp.concatenate([v_ref[rows(ps), :], v_ref[rows(qs), :]], axis=0).astype(BF16)
            s = lax.dot_general(qv, kb, _NT, preferred_element_type=F32) * ATT_SCALE + bias_scr[g]
            s = jnp.where(jnp.where(prev_half, jb, 1) == 0, NEG, s)
            mx = jnp.max(s, axis=-1, keepdims=True)
            p = jnp.exp(s - mx)
            l = jnp.sum(p, axis=-1, keepdims=True)
            o = jnp.dot(p.astype(BF16), vb, preferred_element_type=F32) / l
            og_scr[g, rows(qs), :] = o
            lse_scr[g, rows(qs), :] = jnp.broadcast_to(mx + jnp.log(l), (QB, LANES))
            return carry

        lax.fori_loop(0, seq // QB, body, 0)

    def merge(tb, carry):
        r = pl.ds(pl.multiple_of(tb * QB, QB), QB)
        l0, l1, l2 = lse_scr[0, r, :], lse_scr[1, r, :], lse_scr[2, r, :]
        mx = jnp.maximum(jnp.maximum(l0, l1), l2)
        w0, w1, w2 = jnp.exp(l0 - mx), jnp.exp(l1 - mx), jnp.exp(l2 - mx)
        o = (w0 * og_scr[0, r, :] + w1 * og_scr[1, r, :] + w2 * og_scr[2, r, :]) / (w0 + w1 + w2)
        o_ref[r, :] = o.astype(o_ref.dtype)
        return carry

    lax.fori_loop(0, seq // QB, merge, 0)


def _attn_prompt(p2, slopes, batch, seq):
    assert seq % (QB * max(d for _, d in DILATION_PAIRS)) == 0

    def col(fn):
        return pl.BlockSpec((seq, ATT_HEAD_DIM), lambda b, h: (b, fn(h)))

    return pl.pallas_call(
        functools.partial(_attn_prompt_kernel, seq=seq),
        grid=(batch, ATT_KV_HEADS),
        in_specs=[pl.BlockSpec(memory_space=pltpu.SMEM),
                  col(lambda h: h), col(lambda h: ATT_KV_HEADS + h), col(lambda h: 2 * ATT_KV_HEADS + h),
                  col(lambda h: ATT_Q_HEADS + h), col(lambda h: ATT_Q_HEADS + ATT_KV_HEADS + h)],
        out_specs=pl.BlockSpec((seq, ATT_HEAD_DIM), lambda b, h: (b, h)),
        out_shape=jax.ShapeDtypeStruct((batch * seq, ATT_KV_WIDTH), BF16),
        scratch_shapes=[pltpu.VMEM((N_GROUPS, seq, ATT_HEAD_DIM), F32),
                        pltpu.VMEM((N_GROUPS, seq, LANES), F32),
                        pltpu.VMEM((N_GROUPS, QB, 2 * QB), F32)],
        compiler_params=_cparams(("parallel", "parallel")),
        name="attn_prompt",
    )(slopes, p2, p2, p2, p2, p2)


def _attn_sample_kernel(slopes_ref, qa_ref, qb_ref, qc_ref, kn_ref, vn_ref, kc_ref, vc_ref, o_ref,
                        *, ctx, nq):
    rows = N_GROUPS * nq
    gi = lax.broadcasted_iota(I32, (rows, 1), 0) // nq
    dil = jnp.where(gi == 0, DILATION_PAIRS[0][1], jnp.where(gi == 1, DILATION_PAIRS[1][1], DILATION_PAIRS[2][1]))
    win = jnp.where(gi == 0, DILATION_PAIRS[0][0], jnp.where(gi == 1, DILATION_PAIRS[1][0], DILATION_PAIRS[2][0]))

    def dist_valid(shape, key0, nkeys):
        r = lax.broadcasted_iota(I32, shape, 0)
        t = r - (r // nq) * nq
        col = lax.broadcasted_iota(I32, shape, 1)
        dist = ctx + t - (col + key0)
        bad = jnp.where(dist >= 0, dist & (dil - 1), 1)
        bad = jnp.where(dist <= win, bad, 1)
        bad = jnp.where(col < nkeys, bad, 1)
        return dist.astype(F32), bad == 0

    dist_c, ok_c = dist_valid((rows, ctx), 0, ctx)
    dist_n, ok_n = dist_valid((rows, LANES), ctx, nq)
    zpad = jnp.zeros((LANES - nq, ATT_HEAD_DIM), F32)

    for h in range(ATT_KV_HEADS):
        hs = slice(h * ATT_HEAD_DIM, (h + 1) * ATT_HEAD_DIM)
        slope = jnp.where(gi == 0, slopes_ref[0, h], jnp.where(gi == 1, slopes_ref[1, h], slopes_ref[2, h]))
        qh = jnp.concatenate([qa_ref[:, hs], qb_ref[:, hs], qc_ref[:, hs]], axis=0).astype(BF16)
        kc = kc_ref[0, pl.ds(h, ctx, stride=ATT_KV_HEADS), :].astype(BF16)
        vc = vc_ref[0, pl.ds(h, ctx, stride=ATT_KV_HEADS), :].astype(BF16)
        kn = jnp.concatenate([kn_ref[:, hs], zpad], axis=0).astype(BF16)
        vn = jnp.concatenate([vn_ref[:, hs], zpad], axis=0).astype(BF16)
        sc = lax.dot_general(qh, kc, _NT, preferred_element_type=F32) * ATT_SCALE - slope * dist_c
        sn = lax.dot_general(qh, kn, _NT, preferred_element_type=F32) * ATT_SCALE - slope * dist_n
        sc = jnp.where(ok_c, sc, NEG)
        sn = jnp.where(ok_n, sn, NEG)
        mx = jnp.maximum(jnp.max(sc, axis=-1, keepdims=True), jnp.max(sn, axis=-1, keepdims=True))
        pc = jnp.exp(sc - mx)
        pn = jnp.exp(sn - mx)
        l = jnp.sum(pc, axis=-1, keepdims=True) + jnp.sum(pn, axis=-1, keepdims=True)
        o = (jnp.dot(pc.astype(BF16), vc, preferred_element_type=F32)
             + jnp.dot(pn.astype(BF16), vn, preferred_element_type=F32)) / l
        lse = mx + jnp.log(l)
        l0, l1, l2 = lse[0:nq], lse[nq:2 * nq], lse[2 * nq:3 * nq]
        m3 = jnp.maximum(jnp.maximum(l0, l1), l2)
        w0, w1, w2 = jnp.exp(l0 - m3), jnp.exp(l1 - m3), jnp.exp(l2 - m3)
        out = (w0 * o[0:nq] + w1 * o[nq:2 * nq] + w2 * o[2 * nq:3 * nq]) / (w0 + w1 + w2)
        o_ref[0, :, hs] = out.astype(o_ref.dtype)


def _attn_sample(p2, cache_k, cache_v, slopes, batch, nq, row0):
    ctx = cache_k.shape[1] // ATT_KV_HEADS
    assert row0 % nq == 0 and nq == SUBLANES and ctx >= max(w for w, _ in DILATION_PAIRS)
    rb0 = row0 // nq

    def col(idx):
        return pl.BlockSpec((nq, ATT_KV_WIDTH), lambda b: (rb0 + b, idx))

    cache_spec = pl.BlockSpec((1, ctx * ATT_KV_HEADS, ATT_HEAD_DIM), lambda b: (b, 0, 0))
    return pl.pallas_call(
        functools.partial(_attn_sample_kernel, ctx=ctx, nq=nq),
        grid=(batch,),
        in_specs=[pl.BlockSpec(memory_space=pltpu.SMEM),
                  col(0), col(1), col(2), col(3), col(4), cache_spec, cache_spec],
        out_specs=pl.BlockSpec((1, nq, ATT_KV_WIDTH), lambda b: (b, 0, 0)),
        out_shape=jax.ShapeDtypeStruct((batch, nq, ATT_KV_WIDTH), BF16),
        compiler_params=_cparams(("parallel",)),
        name="attn_sample",
    )(slopes, p2, p2, p2, p2, p2, cache_k, cache_v)


def _layer_norm(y, g, b):
    mu = jnp.mean(y, axis=-1, keepdims=True)
    yc = y - mu
    var = jnp.mean(yc * yc, axis=-1, keepdims=True)
    return yc * lax.rsqrt(var + LN_EPS) * g + b


def _merge_kernel(x_ref, oa_ref, ob_ref, ga_ref, gb_ref, wa_ref, wb_ref, wo_ref, g1_ref, b1_ref,
                  wr0_ref, wr1_ref, br_ref,
                  h_ref, tope_ref, gate_ref, lrank_ref, cnt_ref):
    tm = x_ref.shape[0]
    a = jnp.dot(oa_ref[...], wa_ref[...], preferred_element_type=F32)
    b = jnp.dot(ob_ref[...], wb_ref[...], preferred_element_type=F32)
    merged = _sigmoid(ga_ref[...]) * a + _sigmoid(gb_ref[...]) * b
    z = jnp.dot(merged.astype(BF16), wo_ref[...], preferred_element_type=F32)
    hh = _layer_norm(DEEPNORM_ALPHA * x_ref[...] + z, g1_ref[...], b1_ref[...])
    h_ref[...] = hh

    h0 = hh.astype(BF16)
    h1 = (hh - h0.astype(F32)).astype(BF16)
    logits = (jnp.dot(h0, wr0_ref[...], preferred_element_type=F32)
              + jnp.dot(h1, wr0_ref[...], preferred_element_type=F32)
              + jnp.dot(h0, wr1_ref[...], preferred_element_type=F32)) + br_ref[...]

    lane = lax.broadcasted_iota(I32, (tm, N_EXPERTS), 1)
    work = logits
    vals, idxs = [], []
    onehot = jnp.zeros((tm, N_EXPERTS), F32)
    for _ in range(TOP_K):
        mx = jnp.max(work, axis=-1, keepdims=True)
        idx = jnp.min(jnp.where(work == mx, lane, N_EXPERTS), axis=-1, keepdims=True)
        hit = lane == idx
        vals.append(mx)
        idxs.append(idx)
        onehot = jnp.where(hit, 1.0, onehot)
        work = jnp.where(hit, -jnp.inf, work)
    ex = [jnp.exp(v - vals[0]) for v in vals]
    den = ex[0] + ex[1] + ex[2] + ex[3]

    r_tt = lax.broadcasted_iota(I32, (tm, tm), 0)
    c_tt = lax.broadcasted_iota(I32, (tm, tm), 1)
    before = jnp.where(r_tt > c_tt, 1.0, 0.0).astype(BF16)
    prefix = jnp.dot(before, onehot.astype(BF16), preferred_element_type=F32)
    for j in range(TOP_K):
        tope_ref[:, j:j + 1] = idxs[j]
        gate_ref[:, j:j + 1] = ex[j] / den
        lrank_ref[:, j:j + 1] = jnp.sum(jnp.where(lane == idxs[j], prefix, 0.0), axis=-1,
                                        keepdims=True).astype(I32)
    cnt_ref[0] = jnp.sum(onehot, axis=0, keepdims=True).astype(I32)


def _merge(xa, oa, ob, p3, wa, wb, wo, g1, b1, wr0, wr1, br):
    t = xa.shape[0]
    tm = TOKEN_TILE
    nt = t // tm
    assert t % tm == 0

    def full(a):
        return pl.BlockSpec(a.shape, lambda i: (0,) * a.ndim)

    def rowblk(width, idx=0):
        return pl.BlockSpec((tm, width), lambda i: (i, idx))

    return pl.pallas_call(
        _merge_kernel,
        grid=(nt,),
        in_specs=[rowblk(D_MODEL), rowblk(HG_WIDTH), rowblk(ATT_KV_WIDTH), rowblk(D_MODEL, 0),
                  rowblk(D_MODEL, 1), full(wa), full(wb), full(wo), full(g1), full(b1),
                  full(wr0), full(wr1), full(br)],
        out_specs=[rowblk(D_MODEL), rowblk(TOP_K), rowblk(TOP_K), rowblk(TOP_K),
                   pl.BlockSpec((1, 1, N_EXPERTS), lambda i: (i, 0, 0))],
        out_shape=[jax.ShapeDtypeStruct((t, D_MODEL), F32),
                   jax.ShapeDtypeStruct((t, TOP_K), I32),
                   jax.ShapeDtypeStruct((t, TOP_K), F32),
                   jax.ShapeDtypeStruct((t, TOP_K), I32),
                   jax.ShapeDtypeStruct((nt, 1, N_EXPERTS), I32)],
        compiler_params=_cparams(("parallel",)),
        name="merge_ln1_router",
    )(xa, oa, ob, p3, p3, wa, wb, wo, g1, b1, wr0, wr1, br)


def _gather_kernel(tok_ref, h_ref, o_ref, buf_ref):
    m = o_ref.shape[0]

    def body(q, carry):
        for u in range(SUBLANES):
            r = q * SUBLANES + u
            buf_ref[pl.ds(r, 1), :] = h_ref[pl.ds(tok_ref[0, 0, r], 1), :]
        return carry

    lax.fori_loop(0, m // SUBLANES, body, 0)
    o_ref[...] = buf_ref[...].astype(o_ref.dtype)


def _moe_gather(h, tok_pad):
    t, d = h.shape
    p = tok_pad.shape[0]
    m = MOE_BLOCK
    half = d // 2
    return pl.pallas_call(
        _gather_kernel,
        grid=(2, p // m),
        in_specs=[pl.BlockSpec((1, 1, m), lambda c, i: (i, 0, 0), memory_space=pltpu.SMEM),
                  pl.BlockSpec((t, half), lambda c, i: (0, c), pipeline_mode=pl.Buffered(1))],
        out_specs=pl.BlockSpec((m, half), lambda c, i: (i, c)),
        out_shape=jax.ShapeDtypeStruct((p, d), BF16),
        scratch_shapes=[pltpu.VMEM((m, half), F32)],
        compiler_params=_cparams(("arbitrary", "arbitrary")),
        name="moe_gather",
    )(tok_pad.reshape(p // m, 1, m), h)


def _moe_up_kernel(blk0_ref, nblk_ref, w_ref, b_ref, xs_ref, act_ref, wbf_scr, x_scr, o_scr,
                   sem_in, sem_out):
    e = pl.program_id(0)
    f = pl.program_id(1)
    n = nblk_ref[e]
    b0 = blk0_ref[e]
    m = x_scr.shape[1]
    ft = w_ref.shape[2]
    half = ft // 2

    def rows(b):
        return pl.ds(pl.multiple_of((b0 + b) * m, m), m)

    def x_copy(b, s):
        return pltpu.make_async_copy(xs_ref.at[rows(b)], x_scr.at[s], sem_in.at[s])

    def o_copy(b, s):
        return pltpu.make_async_copy(o_scr.at[s], act_ref.at[f, rows(b)], sem_out.at[s])

    @pl.when(n > 0)
    def _():
        x_copy(0, 0).start()
        wbf_scr[...] = w_ref[0].astype(BF16)
        even = (lax.broadcasted_iota(I32, (m, LANES), 1) & 1) == 0

        def body(b, carry):
            s = b & 1
            x_copy(b, s).wait()

            @pl.when(b + 1 < n)
            def _():
                x_copy(b + 1, 1 - s).start()

            @pl.when(b >= 2)
            def _():
                o_copy(b - 2, s).wait()

            x = x_scr[s]
            for g in range(ft // (2 * LANES)):
                cols = slice(2 * g * LANES, (2 * g + 2) * LANES)
                h = jnp.dot(x, wbf_scr[:, cols], preferred_element_type=F32) + b_ref[0, :, cols]
                ha, hb = h[:, :LANES], h[:, LANES:]
                glu = jnp.minimum(jnp.where(even, ha, pltpu.roll(hb, 1, 1)), SWIGLU_LIMIT)
                lin = jnp.where(even, pltpu.roll(ha, LANES - 1, 1), hb)
                lin = jnp.clip(lin, -SWIGLU_LIMIT, SWIGLU_LIMIT) + 1.0
                act = glu * _sigmoid(SWIGLU_ALPHA * glu) * lin
                o_scr[s, :, g * LANES:(g + 1) * LANES] = act.astype(o_scr.dtype)
            o_copy(b, s).start()
            return carry

        lax.fori_loop(0, n, body, 0)

        @pl.when(n >= 2)
        def _():
            o_copy(n - 2, n & 1).wait()
        o_copy(n - 1, (n - 1) & 1).wait()

    @pl.when(e == pl.num_programs(0) - 1)
    def _():
        o_scr[0] = jnp.zeros(o_scr.shape[1:], o_scr.dtype)

        nz = act_ref.shape[1] // m - b0
        lax.fori_loop(n, nz, lambda b, c: (o_copy(b, 0).start(), c)[1], 0)
        lax.fori_loop(n, nz, lambda b, c: (o_copy(b, 0).wait(), c)[1], 0)


def _moe_up(xs, w_gate_up, b_gate_up, blk0, nblk):
    p, d = xs.shape
    m = MOE_BLOCK
    ft = MOE_FT
    ne = w_gate_up.shape[0]
    nf = w_gate_up.shape[2] // ft
    grid_spec = pltpu.PrefetchScalarGridSpec(
        num_scalar_prefetch=2,
        grid=(ne, nf),
        in_specs=[pl.BlockSpec((1, d, ft), lambda e, f, b0, nb: (e, 0, f)),
                  pl.BlockSpec((1, 1, ft), lambda e, f, b0, nb: (e, 0, f)),
                  pl.BlockSpec(memory_space=pl.ANY)],
        out_specs=pl.BlockSpec(memory_space=pl.ANY),
        scratch_shapes=[pltpu.VMEM((d, ft), BF16),
                        pltpu.VMEM((2, m, d), BF16),
                        pltpu.VMEM((2, m, ft // 2), BF16),
                        pltpu.SemaphoreType.DMA((2,)),
                        pltpu.SemaphoreType.DMA((2,))],
    )
    return pl.pallas_call(
        _moe_up_kernel,
        grid_spec=grid_spec,
        out_shape=jax.ShapeDtypeStruct((nf, p, ft // 2), BF16),
        compiler_params=_cparams(("arbitrary", "arbitrary")),
        name="moe_up",
    )(blk0, nblk, w_gate_up, b_gate_up, xs)


def _hidden_perm():
    r = lax.broadcasted_iota(I32, (LANES, LANES), 0)
    c = lax.broadcasted_iota(I32, (LANES, LANES), 1)
    src = (r >> 1) + (r & 1) * (LANES // 2)
    return jnp.where(c == src, 1.0, 0.0).astype(BF16)


def _moe_down_kernel(be_ref, na_ref, nreal_ref, dest_ref, a_ref, wd_ref, bd_ref, yt_ref,
                     wbf_scr, y_scr, sem):
    i = pl.program_id(0)
    nb = pl.num_programs(0)
    nf, m, fh = a_ref.shape
    slot = i % 2

    def row_copy(s, r, d):
        return pltpu.make_async_copy(y_scr.at[s, pl.ds(r, 1)], yt_ref.at[pl.ds(d, 1)], sem.at[s])

    def drain(s, count):
        @pl.when(count == m)
        def _():
            pltpu.make_async_copy(y_scr.at[s], yt_ref.at[pl.ds(0, m)], sem.at[s]).wait()

        @pl.when(count != m)
        def _():
            def wbody(_, carry):
                row_copy(s, 0, 0).wait()
                return carry
            lax.fori_loop(0, count, wbody, 0)

    @pl.when(i >= 2)
    def _():
        drain(slot, nreal_ref[i - 2])

    changed = jnp.where(i == 0, 1, be_ref[i] - be_ref[jnp.maximum(i - 1, 0)])

    @pl.when(changed != 0)
    def _():
        perm = _hidden_perm()
        for g in range(wd_ref.shape[1] // LANES):
            rows = slice(g * LANES, (g + 1) * LANES)
            wbf_scr[rows, :] = jnp.dot(perm, wd_ref[0, rows, :].astype(BF16),
                                       preferred_element_type=F32).astype(BF16)

    @pl.when(i < na_ref[0])
    def _():
        y = bd_ref[0]
        for j in range(nf):
            y = y + jnp.dot(a_ref[j], wbf_scr[j * fh:(j + 1) * fh, :], preferred_element_type=F32)
        y_scr[slot] = y

        n = nreal_ref[i]
        unroll = 8

        def sbody8(q, carry):
            for u in range(unroll):
                r = q * unroll + u
                row_copy(slot, r, dest_ref[0, 0, r]).start()
            return carry

        def sbody1(r, carry):
            row_copy(slot, r, dest_ref[0, 0, r]).start()
            return carry

        lax.fori_loop(0, n // unroll, sbody8, 0)
        lax.fori_loop((n // unroll) * unroll, n, sbody1, 0)

    @pl.when(i == nb - 1)
    def _():
        @pl.when(i >= 1)
        def _():
            drain(1 - slot, nreal_ref[i - 1])
        drain(slot, nreal_ref[i])


def _moe_down(act, w_down, b_down, blk_expert, n_active, nreal, dest, n_rows):
    nf, p, fh = act.shape
    f = nf * fh
    m = MOE_BLOCK
    d = w_down.shape[2]
    grid_spec = pltpu.PrefetchScalarGridSpec(
        num_scalar_prefetch=3,
        grid=(p // m,),
        in_specs=[pl.BlockSpec((1, 1, m), lambda i, be, na, nr: (i, 0, 0), memory_space=pltpu.SMEM),
                  pl.BlockSpec((nf, m, fh), lambda i, be, na, nr: (0, i, 0)),
                  pl.BlockSpec((1, f, d), lambda i, be, na, nr: (be[i], 0, 0)),
                  pl.BlockSpec((1, 1, d), lambda i, be, na, nr: (be[i], 0, 0))],
        out_specs=pl.BlockSpec(memory_space=pl.ANY),
        scratch_shapes=[pltpu.VMEM((f, d), BF16),
                        pltpu.VMEM((2, m, d), F32),
                        pltpu.SemaphoreType.DMA((2,))],
    )
    return pl.pallas_call(
        _moe_down_kernel,
        grid_spec=grid_spec,
        out_shape=jax.ShapeDtypeStruct((n_rows, d), F32),
        compiler_params=_cparams(("arbitrary",)),
        name="moe_down",
    )(blk_expert, n_active, nreal, dest.reshape(p // m, 1, m), act, w_down, b_down)


def _combine_kernel(h_ref, y0_ref, y1_ref, y2_ref, y3_ref, gate_ref, g2_ref, b2_ref, o_ref):
    ffn = gate_ref[:, 0:1] * y0_ref[...]
    for j, y_ref in enumerate((y1_ref, y2_ref, y3_ref), start=1):
        ffn = ffn + gate_ref[:, j:j + 1] * y_ref[...]
    o_ref[...] = _layer_norm(DEEPNORM_ALPHA * h_ref[...] + ffn, g2_ref[...], b2_ref[...])


def _combine(h, yt, gate, g2, b2):
    t, d = h.shape
    tm = TOKEN_TILE
    nt = t // tm

    def choice(j):
        return pl.BlockSpec((tm, d), lambda i: (j * nt + i, 0))

    return pl.pallas_call(
        _combine_kernel,
        grid=(nt,),
        in_specs=[pl.BlockSpec((tm, d), lambda i: (i, 0)),
                  choice(0), choice(1), choice(2), choice(3),
                  pl.BlockSpec((tm, TOP_K), lambda i: (i, 0)),
                  pl.BlockSpec((1, d), lambda i: (0, 0)),
                  pl.BlockSpec((1, d), lambda i: (0, 0))],
        out_specs=pl.BlockSpec((tm, d), lambda i: (i, 0)),
        out_shape=jax.ShapeDtypeStruct((t, d), F32),
        compiler_params=_cparams(("parallel",)),
        name="combine_ln2",
    )(h, yt, yt, yt, yt, gate, g2, b2)


def _route(top_e, lrank, tilecnt):
    t = top_e.shape[0]
    m = MOE_BLOCK
    n_assign = t * TOP_K
    n_blocks = n_assign // m + N_EXPERTS
    p = n_blocks * m
    tc = tilecnt.reshape(-1, N_EXPERTS)
    base = jnp.cumsum(tc, axis=0) - tc
    counts = jnp.sum(tc, axis=0)
    padded = (counts + m - 1) // m * m
    pend = jnp.cumsum(padded)
    pstart = pend - padded
    base_tok = jnp.repeat(base, TOKEN_TILE, axis=0)
    rank = lrank + jnp.take_along_axis(base_tok, top_e, axis=1)
    slot = (pstart[top_e] + rank).reshape(-1)
    a_idx = jnp.arange(n_assign, dtype=I32)
    dest = jnp.full((p,), -1, I32).at[slot].set((a_idx % TOP_K) * t + a_idx // TOP_K)
    tok_pad = jnp.where(dest >= 0, dest % t, 0)
    blk_start = jnp.arange(n_blocks, dtype=I32) * m
    blk_expert = jnp.minimum(jnp.sum((pend[None, :] <= blk_start[:, None]).astype(I32), axis=1),
                             N_EXPERTS - 1)
    n_active = (pend[-1:] // m).astype(I32)
    real_end = (pstart + counts)[blk_expert]
    nreal = jnp.where(blk_start < pend[-1], jnp.clip(real_end - blk_start, 0, m), 0).astype(I32)
    return tok_pad, dest, blk_expert, n_active, nreal, (pstart // m).astype(I32), (padded // m).astype(I32)


def kernel(x_prompt, x_sample, cache_attn_k, cache_attn_v, state_hgrn, w_in, hgrn_lb_logits,
           hgrn_norm_w, w_branch_a, w_branch_b, w_out, ln1_g, ln1_b, w_router, b_router,
           w_gate_up, b_gate_up, w_down, b_down, ln2_g, ln2_b):
    assert w_in.shape[0] == DEPTH == 1
    bp, lp, d = x_prompt.shape
    bs, ls, _ = x_sample.shape
    tp, ts = bp * lp, bs * ls
    t = tp + ts
    keep = min(max(w for w, _ in DILATION_PAIRS), lp)

    xa = jnp.concatenate([x_prompt.reshape(tp, d), x_sample.reshape(ts, d)], axis=0)
    xb = xa.astype(BF16)
    wi = w_in[0]
    c1, c2 = P1_WIDTH, P1_WIDTH + P2_WIDTH
    p1 = _matmul(xb, wi[:, :c1].astype(BF16), 768, 1024, F32, "inproj_hgrn")
    p2 = _matmul(xb, wi[:, c1:c2].astype(BF16), 768, 512, F32, "inproj_attn")
    p3 = _matmul(xb, wi[:, c2:].astype(BF16), 768, 1024, F32, "inproj_gate")

    oa_p, st_p = _hgrn(p1, hgrn_lb_logits, hgrn_norm_w[0], None, bp, lp, 0)
    oa_s, st_s = _hgrn(p1, hgrn_lb_logits, hgrn_norm_w[0],
                       state_hgrn.reshape(bs, HG_HEADS, HG_DIM, HG_DIM), bs, ls, tp)

    n = jnp.arange(1, ATT_Q_HEADS + 1, dtype=F32)
    slopes = jnp.power(2.0, -8.0 * n / ATT_Q_HEADS).reshape(N_GROUPS, ATT_KV_HEADS)
    ctx = cache_attn_k.shape[2]
    ob_p = _attn_prompt(p2, slopes, bp, lp)
    ob_s = _attn_sample(p2, cache_attn_k.reshape(bs, ctx * ATT_KV_HEADS, ATT_HEAD_DIM),
                        cache_attn_v.reshape(bs, ctx * ATT_KV_HEADS, ATT_HEAD_DIM), slopes, bs, ls, tp)

    oa = jnp.concatenate([oa_p.reshape(tp, HG_WIDTH), oa_s.reshape(ts, HG_WIDTH)], axis=0)
    ob = jnp.concatenate([ob_p, ob_s.reshape(ts, ATT_KV_WIDTH)], axis=0)
    wr = w_router[0]
    wr0 = wr.astype(BF16)
    wr1 = (wr - wr0.astype(F32)).astype(BF16)
    h, top_e, gate, lrank, tilecnt = _merge(
        xa, oa, ob, p3, w_branch_a[0].astype(BF16), w_branch_b[0].astype(BF16),
        w_out[0].astype(BF16), ln1_g, ln1_b, wr0, wr1, b_router)

    tok_pad, dest, blk_expert, n_active, nreal, blk0, nblk = _route(top_e, lrank, tilecnt)
    xs = _moe_gather(h, tok_pad)
    act = _moe_up(xs, w_gate_up.reshape(N_EXPERTS, d, 2 * D_FF),
                  b_gate_up.reshape(N_EXPERTS, 1, 2 * D_FF), blk0, nblk)
    yt = _moe_down(act, w_down.reshape(N_EXPERTS, D_FF, d), b_down.reshape(N_EXPERTS, 1, d),
                   blk_expert, n_active, nreal, dest, t * TOP_K)
    y = _combine(h, yt, gate, ln2_g, ln2_b)

    k_new = p2[:, ATT_Q_HEADS * ATT_HEAD_DIM:(ATT_Q_HEADS + ATT_KV_HEADS) * ATT_HEAD_DIM]
    v_new = p2[:, (ATT_Q_HEADS + ATT_KV_HEADS) * ATT_HEAD_DIM:]

    def kv_prompt(a):
        return a[:tp].reshape(bp, lp, ATT_KV_HEADS, ATT_HEAD_DIM)[:, lp - keep:][None]

    def kv_sample(a):
        return a[tp:].reshape(bs, ls, ATT_KV_HEADS, ATT_HEAD_DIM)[None]

    return (y[:tp].reshape(bp, lp, d), y[tp:].reshape(bs, ls, d),
            kv_prompt(k_new), kv_prompt(v_new), st_p[None],
            kv_sample(k_new), kv_sample(v_new), st_s[None])
```

```python
import functools
import math

import jax
import jax.numpy as jnp
from jax import lax
from jax.experimental import pallas as pl
from jax.experimental.pallas import tpu as pltpu

F32 = jnp.float32
BF16 = jnp.bfloat16
I32 = jnp.int32

D_MODEL = 2048
DEPTH = 1
HG_HEADS = 8
HG_DIM = 128
HG_WIDTH = HG_HEADS * HG_DIM
HG_CHUNK = 64
ATT_HEAD_DIM = 128
DILATION_PAIRS = ((128, 1), (512, 4), (2048, 16))
N_GROUPS = len(DILATION_PAIRS)
ATT_KV_HEADS = 4
ATT_Q_HEADS = N_GROUPS * ATT_KV_HEADS
ATT_TAPS = 128
ATT_KV_WIDTH = ATT_KV_HEADS * ATT_HEAD_DIM
ATT_SCALE = ATT_HEAD_DIM ** -0.5
N_EXPERTS = 32
TOP_K = 4
D_FF = D_MODEL
SWIGLU_ALPHA = 1.702
SWIGLU_LIMIT = 7.0
LN_EPS = 1e-5
RMS_EPS = 1e-5
DEEPNORM_ALPHA = (2 * DEPTH) ** 0.25

P1_WIDTH = 4 * HG_WIDTH
P2_WIDTH = (ATT_Q_HEADS + 2 * ATT_KV_HEADS) * ATT_HEAD_DIM
P3_WIDTH = 2 * D_MODEL

LANES = 128
SUBLANES = 8
VMEM_LIMIT = 60 * 1024 * 1024

TOKEN_TILE = 256
MOE_BLOCK = 256
MOE_FT = 2048
W_STREAMS = 4
NEG = -1e30

_NT = (((1,), (1,)), ((), ()))
_TN = (((0,), (0,)), ((), ()))


def _cparams(sem):
    return pltpu.CompilerParams(dimension_semantics=sem, vmem_limit_bytes=VMEM_LIMIT)


def _sigmoid(x):
    return 1.0 / (1.0 + jnp.exp(-x))


def _mm_kernel(x_ref, w_ref, o_ref):
    o_ref[...] = jnp.dot(x_ref[...], w_ref[...], preferred_element_type=F32).astype(o_ref.dtype)


def _matmul(x, w, tm, tn, out_dtype, name):
    m, k = x.shape
    n = w.shape[1]
    assert m % tm == 0 and n % tn == 0
    return pl.pallas_call(
        _mm_kernel,
        grid=(n // tn, m // tm),
        in_specs=[pl.BlockSpec((tm, k), lambda j, i: (i, 0)),
                  pl.BlockSpec((k, tn), lambda j, i: (0, j))],
        out_specs=pl.BlockSpec((tm, tn), lambda j, i: (i, j)),
        out_shape=jax.ShapeDtypeStruct((m, n), out_dtype),
        compiler_params=_cparams(("parallel", "parallel")),
        name=name,
    )(x, w)


def _level_ref(g_scr, m, sl, chunk, sub):
    pieces = []
    for j in range(chunk // SUBLANES):
        if 2 * m >= SUBLANES:
            row = ((SUBLANES * j) // (2 * m)) * (2 * m) + m - 1
            pieces.append(jnp.broadcast_to(g_scr[row:row + 1, sl], (SUBLANES, LANES)))
        else:
            acc = None
            for u in range(SUBLANES // (2 * m)):
                row = SUBLANES * j + 2 * m * u + m - 1
                b = jnp.broadcast_to(g_scr[row:row + 1, sl], (SUBLANES, LANES))
                acc = b if acc is None else jnp.where(sub // (2 * m) == u, b, acc)
            pieces.append(acc)
    return pieces[0] if len(pieces) == 1 else jnp.concatenate(pieces, axis=0)


def _hgrn_kernel(lbl_ref, nw_ref, hq_ref, hf_ref, hi_ref, hg_ref, *rest, chunk, has_s0):
    if has_s0:
        s0_ref, o_ref, sout_ref, st_scr, g_scr = rest
    else:
        o_ref, sout_ref, st_scr, g_scr = rest
    c = pl.program_id(1)
    nc = pl.num_programs(1)

    @pl.when(c == 0)
    def _init():
        for h in range(HG_HEADS):
            if has_s0:
                st_scr[h] = s0_ref[0, h].T
            else:
                st_scr[h] = jnp.zeros((HG_DIM, HG_DIM), F32)

    lbl = lbl_ref[...]
    ex = jnp.exp(lbl - jnp.max(lbl, axis=0, keepdims=True))
    lb = ex[0:1] / jnp.sum(ex, axis=0, keepdims=True)

    f = lb + (1.0 - lb) * _sigmoid(hf_ref[...])
    logf = jnp.log(f)
    kall = 1.0 - f

    r_cc = lax.broadcasted_iota(I32, (chunk, chunk), 0)
    c_cc = lax.broadcasted_iota(I32, (chunk, chunk), 1)
    tri = jnp.where(r_cc >= c_cc, 1.0, 0.0).astype(BF16)
    p0 = logf.astype(BF16)
    r1 = logf - p0.astype(F32)
    p1 = r1.astype(BF16)
    p2 = (r1 - p1.astype(F32)).astype(BF16)
    g_all = (jnp.dot(tri, p0, preferred_element_type=F32)
             + jnp.dot(tri, p1, preferred_element_type=F32)
             + jnp.dot(tri, p2, preferred_element_type=F32))
    g_scr[...] = g_all

    sub = lax.broadcasted_iota(I32, (SUBLANES, LANES), 0)
    row_c = lax.broadcasted_iota(I32, (chunk, LANES), 0)
    nw = nw_ref[...]

    for h in range(HG_HEADS):
        sl = slice(h * HG_DIM, (h + 1) * HG_DIM)
        hq = hq_ref[:, sl]
        q = hq * _sigmoid(hq)
        k = kall[:, sl]
        g = g_all[:, sl]
        v = hi_ref[:, sl]
        vb = v.astype(BF16)

        p = lax.dot_general(q.astype(BF16), k.astype(BF16), _NT, preferred_element_type=F32)
        a = jnp.where(r_cc == c_cc, p, 0.0)
        m = 1
        while m < chunk:
            ref = _level_ref(g_scr, m, sl, chunk, sub)
            e = jnp.exp(-jnp.abs(g - ref))
            odd = ((row_c // m) & 1) == 1
            x = (jnp.where(odd, q, k) * e).astype(BF16)
            p = lax.dot_general(x, x, _NT, preferred_element_type=F32)
            pair = (r_cc // (2 * m)) == (c_cc // (2 * m))
            sel = jnp.where(pair, ((r_cc // m) & 1) - ((c_cc // m) & 1), 0) == 1
            a = jnp.where(sel, p, a)
            m *= 2

        st = st_scr[h]
        o = jnp.dot(a.astype(BF16), vb, preferred_element_type=F32)
        qe = (q * jnp.exp(g)).astype(BF16)
        o = o + lax.dot_general(qe, st.astype(BF16), _NT, preferred_element_type=F32)

        ms = jnp.mean(o * o, axis=-1, keepdims=True)
        hg = hg_ref[:, sl]
        out = o * lax.rsqrt(ms + RMS_EPS) * nw * (hg * _sigmoid(hg))
        o_ref[0, :, sl] = out.astype(o_ref.dtype)

        g_last = g[chunk - 1:chunk, :]
        kd = (k * jnp.exp(g_last - g)).astype(BF16)
        st_scr[h] = st * jnp.exp(g_last) + lax.dot_general(vb, kd, _TN, preferred_element_type=F32)

    @pl.when(c == nc - 1)
    def _fin():
        for h in range(HG_HEADS):
            sout_ref[0, h] = st_scr[h].T


def _hgrn(p1, lb_logits, norm_w, s0, batch, seq, row0):
    chunk = math.gcd(seq, HG_CHUNK)
    nchunk = seq // chunk
    assert row0 % chunk == 0
    rb0 = row0 // chunk

    def col(idx):
        return pl.BlockSpec((chunk, HG_WIDTH), lambda b, c: (rb0 + b * nchunk + c, idx))

    in_specs = [pl.BlockSpec(lb_logits.shape, lambda b, c: (0, 0)),
                pl.BlockSpec((1, HG_DIM), lambda b, c: (0, 0)),
                col(0), col(1), col(2), col(3)]
    args = [lb_logits, norm_w.reshape(1, HG_DIM), p1, p1, p1, p1]
    if s0 is not None:
        in_specs.append(pl.BlockSpec((1, HG_HEADS, HG_DIM, HG_DIM), lambda b, c: (b, 0, 0, 0)))
        args.append(s0)
    return pl.pallas_call(
        functools.partial(_hgrn_kernel, chunk=chunk, has_s0=s0 is not None),
        grid=(batch, nchunk),
        in_specs=in_specs,
        out_specs=[pl.BlockSpec((1, chunk, HG_WIDTH), lambda b, c: (b, c, 0)),
                   pl.BlockSpec((1, HG_HEADS, HG_DIM, HG_DIM), lambda b, c: (b, 0, 0, 0))],
        out_shape=[jax.ShapeDtypeStruct((batch, seq, HG_WIDTH), BF16),
                   jax.ShapeDtypeStruct((batch, HG_HEADS, HG_DIM, HG_DIM), F32)],
        scratch_shapes=[pltpu.VMEM((HG_HEADS, HG_DIM, HG_DIM), F32),
                        pltpu.VMEM((chunk, HG_WIDTH), F32)],
        compiler_params=_cparams(("parallel", "arbitrary")),
        name="hgrn_seq%d" % seq,
    )(*args)


QB = 128


def _attn_prompt_kernel(slopes_ref, q0_ref, q1_ref, q2_ref, k_ref, v_ref, o_ref,
                        og_scr, lse_scr, bias_scr, *, seq):
    h = pl.program_id(1)
    il = lax.broadcasted_iota(I32, (QB, 2 * QB), 0)
    jl = lax.broadcasted_iota(I32, (QB, 2 * QB), 1)
    delta = il + QB - jl
    band = jnp.where(delta >= 0, delta, ATT_TAPS + 1) <= ATT_TAPS
    prev_half = jl < QB
    q_refs = (q0_ref, q1_ref, q2_ref)

    for g, (window, dil) in enumerate(DILATION_PAIRS):
        assert window // dil == ATT_TAPS
        slope = slopes_ref[g, h]
        bias_scr[g] = jnp.where(band, (-slope * dil) * delta.astype(F32), NEG)
        nbr = (seq // dil) // QB
        q_ref = q_refs[g]

        def rows(start, dil=dil):
            return pl.ds(start, QB) if dil == 1 else pl.ds(start, QB, stride=dil)

        def body(ib, carry, g=g, dil=dil, nbr=nbr, q_ref=q_ref, rows=rows):
            res = ib // nbr
            jb = ib - res * nbr
            qs = res + dil * (jb * QB)
            ps = res + dil * jnp.maximum(jb * QB - QB, 0)
            qv = q_ref[rows(qs), :].astype(BF16)
            kb = jnp.concatenate([k_ref[rows(ps), :], k_ref[rows(qs), :]], axis=0).astype(BF16)
            vb = jnp.concatenate([v_ref[rows(ps), :], v_ref[rows(qs), :]], axis=0).astype(BF16)
            s = lax.dot_general(qv, kb, _NT, preferred_element_type=F32) * ATT_SCALE + bias_scr[g]
            s = jnp.where(jnp.where(prev_half, jb, 1) == 0, NEG, s)
            mx = jnp.max(s, axis=-1, keepdims=True)
            p = jnp.exp(s - mx)
            l = jnp.sum(p, axis=-1, keepdims=True)
            o = jnp.dot(p.astype(BF16), vb, preferred_element_type=F32) / l
            og_scr[g, rows(qs), :] = o
            lse_scr[g, rows(qs), :] = jnp.broadcast_to(mx + jnp.log(l), (QB, LANES))
            return carry

        lax.fori_loop(0, seq // QB, body, 0)

    def merge(tb, carry):
        r = pl.ds(pl.multiple_of(tb * QB, QB), QB)
        l0, l1, l2 = lse_scr[0, r, :], lse_scr[1, r, :], lse_scr[2, r, :]
        mx = jnp.maximum(jnp.maximum(l0, l1), l2)
        w0, w1, w2 = jnp.exp(l0 - mx), jnp.exp(l1 - mx), jnp.exp(l2 - mx)
        o = (w0 * og_scr[0, r, :] + w1 * og_scr[1, r, :] + w2 * og_scr[2, r, :]) / (w0 + w1 + w2)
        o_ref[r, :] = o.astype(o_ref.dtype)
        return carry

    lax.fori_loop(0, seq // QB, merge, 0)


def _attn_prompt(p2, slopes, batch, seq):
    assert seq % (QB * max(d for _, d in DILATION_PAIRS)) == 0

    def col(fn):
        return pl.BlockSpec((seq, ATT_HEAD_DIM), lambda b, h: (b, fn(h)))

    return pl.pallas_call(
        functools.partial(_attn_prompt_kernel, seq=seq),
        grid=(batch, ATT_KV_HEADS),
        in_specs=[pl.BlockSpec(memory_space=pltpu.SMEM),
                  col(lambda h: h), col(lambda h: ATT_KV_HEADS + h), col(lambda h: 2 * ATT_KV_HEADS + h),
                  col(lambda h: ATT_Q_HEADS + h), col(lambda h: ATT_Q_HEADS + ATT_KV_HEADS + h)],
        out_specs=pl.BlockSpec((seq, ATT_HEAD_DIM), lambda b, h: (b, h)),
        out_shape=jax.ShapeDtypeStruct((batch * seq, ATT_KV_WIDTH), BF16),
        scratch_shapes=[pltpu.VMEM((N_GROUPS, seq, ATT_HEAD_DIM), F32),
                        pltpu.VMEM((N_GROUPS, seq, LANES), F32),
                        pltpu.VMEM((N_GROUPS, QB, 2 * QB), F32)],
        compiler_params=_cparams(("parallel", "parallel")),
        name="attn_prompt",
    )(slopes, p2, p2, p2, p2, p2)


def _attn_sample_kernel(slopes_ref, qa_ref, qb_ref, qc_ref, kn_ref, vn_ref, kc_ref, vc_ref, o_ref,
                        *, ctx, nq):
    rows = N_GROUPS * nq
    gi = lax.broadcasted_iota(I32, (rows, 1), 0) // nq
    dil = jnp.where(gi == 0, DILATION_PAIRS[0][1], jnp.where(gi == 1, DILATION_PAIRS[1][1], DILATION_PAIRS[2][1]))
    win = jnp.where(gi == 0, DILATION_PAIRS[0][0], jnp.where(gi == 1, DILATION_PAIRS[1][0], DILATION_PAIRS[2][0]))

    def dist_valid(shape, key0, nkeys):
        r = lax.broadcasted_iota(I32, shape, 0)
        t = r - (r // nq) * nq
        col = lax.broadcasted_iota(I32, shape, 1)
        dist = ctx + t - (col + key0)
        bad = jnp.where(dist >= 0, dist & (dil - 1), 1)
        bad = jnp.where(dist <= win, bad, 1)
        bad = jnp.where(col < nkeys, bad, 1)
        return dist.astype(F32), bad == 0

    dist_c, ok_c = dist_valid((rows, ctx), 0, ctx)
    dist_n, ok_n = dist_valid((rows, LANES), ctx, nq)
    zpad = jnp.zeros((LANES - nq, ATT_HEAD_DIM), F32)

    for h in range(ATT_KV_HEADS):
        hs = slice(h * ATT_HEAD_DIM, (h + 1) * ATT_HEAD_DIM)
        slope = jnp.where(gi == 0, slopes_ref[0, h], jnp.where(gi == 1, slopes_ref[1, h], slopes_ref[2, h]))
        qh = jnp.concatenate([qa_ref[:, hs], qb_ref[:, hs], qc_ref[:, hs]], axis=0).astype(BF16)
        kc = kc_ref[0, pl.ds(h, ctx, stride=ATT_KV_HEADS), :].astype(BF16)
        vc = vc_ref[0, pl.ds(h, ctx, stride=ATT_KV_HEADS), :].astype(BF16)
        kn = jnp.concatenate([kn_ref[:, hs], zpad], axis=0).astype(BF16)
        vn = jnp.concatenate([vn_ref[:, hs], zpad], axis=0).astype(BF16)
        sc = lax.dot_general(qh, kc, _NT, preferred_element_type=F32) * ATT_SCALE - slope * dist_c
        sn = lax.dot_general(qh, kn, _NT, preferred_element_type=F32) * ATT_SCALE - slope * dist_n
        sc = jnp.where(ok_c, sc, NEG)
        sn = jnp.where(ok_n, sn, NEG)
        mx = jnp.maximum(jnp.max(sc, axis=-1, keepdims=True), jnp.max(sn, axis=-1, keepdims=True))
        pc = jnp.exp(sc - mx)
        pn = jnp.exp(sn - mx)
        l = jnp.sum(pc, axis=-1, keepdims=True) + jnp.sum(pn, axis=-1, keepdims=True)
        o = (jnp.dot(pc.astype(BF16), vc, preferred_element_type=F32)
             + jnp.dot(pn.astype(BF16), vn, preferred_element_type=F32)) / l
        lse = mx + jnp.log(l)
        l0, l1, l2 = lse[0:nq], lse[nq:2 * nq], lse[2 * nq:3 * nq]
        m3 = jnp.maximum(jnp.maximum(l0, l1), l2)
        w0, w1, w2 = jnp.exp(l0 - m3), jnp.exp(l1 - m3), jnp.exp(l2 - m3)
        out = (w0 * o[0:nq] + w1 * o[nq:2 * nq] + w2 * o[2 * nq:3 * nq]) / (w0 + w1 + w2)
        o_ref[0, :, hs] = out.astype(o_ref.dtype)


def _attn_sample(p2, cache_k, cache_v, slopes, batch, nq, row0):
    ctx = cache_k.shape[1] // ATT_KV_HEADS
    assert row0 % nq == 0 and nq == SUBLANES and ctx >= max(w for w, _ in DILATION_PAIRS)
    rb0 = row0 // nq

    def col(idx):
        return pl.BlockSpec((nq, ATT_KV_WIDTH), lambda b: (rb0 + b, idx))

    cache_spec = pl.BlockSpec((1, ctx * ATT_KV_HEADS, ATT_HEAD_DIM), lambda b: (b, 0, 0))
    return pl.pallas_call(
        functools.partial(_attn_sample_kernel, ctx=ctx, nq=nq),
        grid=(batch,),
        in_specs=[pl.BlockSpec(memory_space=pltpu.SMEM),
                  col(0), col(1), col(2), col(3), col(4), cache_spec, cache_spec],
        out_specs=pl.BlockSpec((1, nq, ATT_KV_WIDTH), lambda b: (b, 0, 0)),
        out_shape=jax.ShapeDtypeStruct((batch, nq, ATT_KV_WIDTH), BF16),
        compiler_params=_cparams(("parallel",)),
        name="attn_sample",
    )(slopes, p2, p2, p2, p2, p2, cache_k, cache_v)


def _layer_norm(y, g, b):
    mu = jnp.mean(y, axis=-1, keepdims=True)
    yc = y - mu
    var = jnp.mean(yc * yc, axis=-1, keepdims=True)
    return yc * lax.rsqrt(var + LN_EPS) * g + b


def _merge_kernel(x_ref, oa_ref, ob_ref, ga_ref, gb_ref, wa_ref, wb_ref, wo_ref, g1_ref, b1_ref,
                  wr0_ref, wr1_ref, br_ref,
                  h_ref, tope_ref, gate_ref, lrank_ref, cnt_ref):
    tm = x_ref.shape[0]
    a = jnp.dot(oa_ref[...], wa_ref[...], preferred_element_type=F32)
    b = jnp.dot(ob_ref[...], wb_ref[...], preferred_element_type=F32)
    merged = _sigmoid(ga_ref[...]) * a + _sigmoid(gb_ref[...]) * b
    z = jnp.dot(merged.astype(BF16), wo_ref[...], preferred_element_type=F32)
    hh = _layer_norm(DEEPNORM_ALPHA * x_ref[...] + z, g1_ref[...], b1_ref[...])
    h_ref[...] = hh

    h0 = hh.astype(BF16)
    h1 = (hh - h0.astype(F32)).astype(BF16)
    logits = (jnp.dot(h0, wr0_ref[...], preferred_element_type=F32)
              + jnp.dot(h1, wr0_ref[...], preferred_element_type=F32)
              + jnp.dot(h0, wr1_ref[...], preferred_element_type=F32)) + br_ref[...]

    lane = lax.broadcasted_iota(I32, (tm, N_EXPERTS), 1)
    work = logits
    vals, idxs = [], []
    onehot = jnp.zeros((tm, N_EXPERTS), F32)
    for _ in range(TOP_K):
        mx = jnp.max(work, axis=-1, keepdims=True)
        idx = jnp.min(jnp.where(work == mx, lane, N_EXPERTS), axis=-1, keepdims=True)
        hit = lane == idx
        vals.append(mx)
        idxs.append(idx)
        onehot = jnp.where(hit, 1.0, onehot)
        work = jnp.where(hit, -jnp.inf, work)
    ex = [jnp.exp(v - vals[0]) for v in vals]
    den = ex[0] + ex[1] + ex[2] + ex[3]

    r_tt = lax.broadcasted_iota(I32, (tm, tm), 0)
    c_tt = lax.broadcasted_iota(I32, (tm, tm), 1)
    before = jnp.where(r_tt > c_tt, 1.0, 0.0).astype(BF16)
    prefix = jnp.dot(before, onehot.astype(BF16), preferred_element_type=F32)
    for j in range(TOP_K):
        tope_ref[:, j:j + 1] = idxs[j]
        gate_ref[:, j:j + 1] = ex[j] / den
        lrank_ref[:, j:j + 1] = jnp.sum(jnp.where(lane == idxs[j], prefix, 0.0), axis=-1,
                                        keepdims=True).astype(I32)
    cnt_ref[0] = jnp.sum(onehot, axis=0, keepdims=True).astype(I32)


def _merge(xa, oa, ob, p3, wa, wb, wo, g1, b1, wr0, wr1, br):
    t = xa.shape[0]
    tm = TOKEN_TILE
    nt = t // tm
    assert t % tm == 0

    def full(a):
        return pl.BlockSpec(a.shape, lambda i: (0,) * a.ndim)

    def rowblk(width, idx=0):
        return pl.BlockSpec((tm, width), lambda i: (i, idx))

    return pl.pallas_call(
        _merge_kernel,
        grid=(nt,),
        in_specs=[rowblk(D_MODEL), rowblk(HG_WIDTH), rowblk(ATT_KV_WIDTH), rowblk(D_MODEL, 0),
                  rowblk(D_MODEL, 1), full(wa), full(wb), full(wo), full(g1), full(b1),
                  full(wr0), full(wr1), full(br)],
        out_specs=[rowblk(D_MODEL), rowblk(TOP_K), rowblk(TOP_K), rowblk(TOP_K),
                   pl.BlockSpec((1, 1, N_EXPERTS), lambda i: (i, 0, 0))],
        out_shape=[jax.ShapeDtypeStruct((t, D_MODEL), F32),
                   jax.ShapeDtypeStruct((t, TOP_K), I32),
                   jax.ShapeDtypeStruct((t, TOP_K), F32),
                   jax.ShapeDtypeStruct((t, TOP_K), I32),
                   jax.ShapeDtypeStruct((nt, 1, N_EXPERTS), I32)],
        compiler_params=_cparams(("parallel",)),
        name="merge_ln1_router",
    )(xa, oa, ob, p3, p3, wa, wb, wo, g1, b1, wr0, wr1, br)


def _gather_kernel(tok_ref, h_ref, o_ref, buf_ref):
    m = o_ref.shape[0]

    def body(q, carry):
        for u in range(SUBLANES):
            r = q * SUBLANES + u
            buf_ref[pl.ds(r, 1), :] = h_ref[pl.ds(tok_ref[0, 0, r], 1), :]
        return carry

    lax.fori_loop(0, m // SUBLANES, body, 0)
    o_ref[...] = buf_ref[...].astype(o_ref.dtype)


def _moe_gather(h, tok_pad):
    t, d = h.shape
    p = tok_pad.shape[0]
    m = MOE_BLOCK
    half = d // 2
    return pl.pallas_call(
        _gather_kernel,
        grid=(2, p // m),
        in_specs=[pl.BlockSpec((1, 1, m), lambda c, i: (i, 0, 0), memory_space=pltpu.SMEM),
                  pl.BlockSpec((t, half), lambda c, i: (0, c), pipeline_mode=pl.Buffered(1))],
        out_specs=pl.BlockSpec((m, half), lambda c, i: (i, c)),
        out_shape=jax.ShapeDtypeStruct((p, d), BF16),
        scratch_shapes=[pltpu.VMEM((m, half), F32)],
        compiler_params=_cparams(("arbitrary", "arbitrary")),
        name="moe_gather",
    )(tok_pad.reshape(p // m, 1, m), h)


def _moe_up_kernel(blk0_ref, nblk_ref, w0_ref, w1_ref, w2_ref, w3_ref, b_ref, xs_ref, act_ref,
                   wbf_scr, x_scr, o_scr, sem_in, sem_out):
    e = pl.program_id(0)
    f = pl.program_id(1)
    n = nblk_ref[e]
    b0 = blk0_ref[e]
    m = x_scr.shape[1]
    w_refs = (w0_ref, w1_ref, w2_ref, w3_ref)
    kq, ft = w0_ref.shape[1:]
    half = ft // 2

    def rows(b):
        return pl.ds(pl.multiple_of((b0 + b) * m, m), m)

    def x_copy(b, s):
        return pltpu.make_async_copy(xs_ref.at[rows(b)], x_scr.at[s], sem_in.at[s])

    def o_copy(b, s):
        return pltpu.make_async_copy(o_scr.at[s], act_ref.at[f, rows(b)], sem_out.at[s])

    @pl.when(n > 0)
    def _():
        x_copy(0, 0).start()
        for q, w_ref in enumerate(w_refs):
            wbf_scr[q * kq:(q + 1) * kq, :] = w_ref[0].astype(BF16)
        even = (lax.broadcasted_iota(I32, (m, LANES), 1) & 1) == 0

        def body(b, carry):
            s = b & 1
            x_copy(b, s).wait()

            @pl.when(b + 1 < n)
            def _():
                x_copy(b + 1, 1 - s).start()

            @pl.when(b >= 2)
            def _():
                o_copy(b - 2, s).wait()

            x = x_scr[s]
            for g in range(ft // (2 * LANES)):
                cols = slice(2 * g * LANES, (2 * g + 2) * LANES)
                h = jnp.dot(x, wbf_scr[:, cols], preferred_element_type=F32) + b_ref[0, :, cols]
                ha, hb = h[:, :LANES], h[:, LANES:]
                glu = jnp.minimum(jnp.where(even, ha, pltpu.roll(hb, 1, 1)), SWIGLU_LIMIT)
                lin = jnp.where(even, pltpu.roll(ha, LANES - 1, 1), hb)
                lin = jnp.clip(lin, -SWIGLU_LIMIT, SWIGLU_LIMIT) + 1.0
                act = glu * _sigmoid(SWIGLU_ALPHA * glu) * lin
                o_scr[s, :, g * LANES:(g + 1) * LANES] = act.astype(o_scr.dtype)
            o_copy(b, s).start()
            return carry

        lax.fori_loop(0, n, body, 0)

        @pl.when(n >= 2)
        def _():
            o_copy(n - 2, n & 1).wait()
        o_copy(n - 1, (n - 1) & 1).wait()

    @pl.when(e == pl.num_programs(0) - 1)
    def _():
        o_scr[0] = jnp.zeros(o_scr.shape[1:], o_scr.dtype)

        nz = act_ref.shape[1] // m - b0
        lax.fori_loop(n, nz, lambda b, c: (o_copy(b, 0).start(), c)[1], 0)
        lax.fori_loop(n, nz, lambda b, c: (o_copy(b, 0).wait(), c)[1], 0)


def _moe_up(xs, w_gate_up, b_gate_up, blk0, nblk):
    p, d = xs.shape
    m = MOE_BLOCK
    ft = MOE_FT
    ne = w_gate_up.shape[0]
    nf = w_gate_up.shape[2] // ft

    def wslice(q, e, f, b0, nb):
        return (e, q, f)

    grid_spec = pltpu.PrefetchScalarGridSpec(
        num_scalar_prefetch=2,
        grid=(ne, nf),
        in_specs=[pl.BlockSpec((1, d // W_STREAMS, ft), functools.partial(wslice, q))
                  for q in range(W_STREAMS)] + [
                  pl.BlockSpec((1, 1, ft), lambda e, f, b0, nb: (e, 0, f)),
                  pl.BlockSpec(memory_space=pl.ANY)],
        out_specs=pl.BlockSpec(memory_space=pl.ANY),
        scratch_shapes=[pltpu.VMEM((d, ft), BF16),
                        pltpu.VMEM((2, m, d), BF16),
                        pltpu.VMEM((2, m, ft // 2), BF16),
                        pltpu.SemaphoreType.DMA((2,)),
                        pltpu.SemaphoreType.DMA((2,))],
    )
    return pl.pallas_call(
        _moe_up_kernel,
        grid_spec=grid_spec,
        out_shape=jax.ShapeDtypeStruct((nf, p, ft // 2), BF16),
        compiler_params=_cparams(("arbitrary", "arbitrary")),
        name="moe_up",
    )(blk0, nblk, *([w_gate_up] * W_STREAMS), b_gate_up, xs)


def _hidden_perm():
    r = lax.broadcasted_iota(I32, (LANES, LANES), 0)
    c = lax.broadcasted_iota(I32, (LANES, LANES), 1)
    src = (r >> 1) + (r & 1) * (LANES // 2)
    return jnp.where(c == src, 1.0, 0.0).astype(BF16)


def _moe_down_kernel(be_ref, na_ref, nreal_ref, dest_ref, a_ref, wd0_ref, wd1_ref, wd2_ref, wd3_ref,
                     bd_ref, yt_ref, wbf_scr, y_scr, sem):
    i = pl.program_id(0)
    nb = pl.num_programs(0)
    nf, m, fh = a_ref.shape
    slot = i % 2

    def row_copy(s, r, d):
        return pltpu.make_async_copy(y_scr.at[s, pl.ds(r, 1)], yt_ref.at[pl.ds(d, 1)], sem.at[s])

    def drain(s, count):
        @pl.when(count == m)
        def _():
            pltpu.make_async_copy(y_scr.at[s], yt_ref.at[pl.ds(0, m)], sem.at[s]).wait()

        @pl.when(count != m)
        def _():
            def wbody(_, carry):
                row_copy(s, 0, 0).wait()
                return carry
            lax.fori_loop(0, count, wbody, 0)

    @pl.when(i >= 2)
    def _():
        drain(slot, nreal_ref[i - 2])

    changed = jnp.where(i == 0, 1, be_ref[i] - be_ref[jnp.maximum(i - 1, 0)])

    @pl.when(changed != 0)
    def _():
        perm = _hidden_perm()
        for q, wd_ref in enumerate((wd0_ref, wd1_ref, wd2_ref, wd3_ref)):
            kq = wd_ref.shape[1]
            for g in range(kq // LANES):
                rows = slice(g * LANES, (g + 1) * LANES)
                wbf_scr[q * kq + g * LANES:q * kq + (g + 1) * LANES, :] = jnp.dot(
                    perm, wd_ref[0, rows, :].astype(BF16), preferred_element_type=F32).astype(BF16)

    @pl.when(i < na_ref[0])
    def _():
        y = bd_ref[0]
        for j in range(nf):
            y = y + jnp.dot(a_ref[j], wbf_scr[j * fh:(j + 1) * fh, :], preferred_element_type=F32)
        y_scr[slot] = y

        n = nreal_ref[i]
        unroll = 8

        def sbody8(q, carry):
            for u in range(unroll):
                r = q * unroll + u
                row_copy(slot, r, dest_ref[0, 0, r]).start()
            return carry

        def sbody1(r, carry):
            row_copy(slot, r, dest_ref[0, 0, r]).start()
            return carry

        lax.fori_loop(0, n // unroll, sbody8, 0)
        lax.fori_loop((n // unroll) * unroll, n, sbody1, 0)

    @pl.when(i == nb - 1)
    def _():
        @pl.when(i >= 1)
        def _():
            drain(1 - slot, nreal_ref[i - 1])
        drain(slot, nreal_ref[i])


def _moe_down(act, w_down, b_down, blk_expert, n_active, nreal, dest, n_rows):
    nf, p, fh = act.shape
    f = nf * fh
    m = MOE_BLOCK
    d = w_down.shape[2]

    def wslice(q, i, be, na, nr):
        return (be[i], q, 0)

    grid_spec = pltpu.PrefetchScalarGridSpec(
        num_scalar_prefetch=3,
        grid=(p // m,),
        in_specs=[pl.BlockSpec((1, 1, m), lambda i, be, na, nr: (i, 0, 0), memory_space=pltpu.SMEM),
                  pl.BlockSpec((nf, m, fh), lambda i, be, na, nr: (0, i, 0)),
                  *[pl.BlockSpec((1, f // W_STREAMS, d), functools.partial(wslice, q))
                    for q in range(W_STREAMS)],
                  pl.BlockSpec((1, 1, d), lambda i, be, na, nr: (be[i], 0, 0))],
        out_specs=pl.BlockSpec(memory_space=pl.ANY),
        scratch_shapes=[pltpu.VMEM((f, d), BF16),
                        pltpu.VMEM((2, m, d), F32),
                        pltpu.SemaphoreType.DMA((2,))],
    )
    return pl.pallas_call(
        _moe_down_kernel,
        grid_spec=grid_spec,
        out_shape=jax.ShapeDtypeStruct((n_rows, d), F32),
        compiler_params=_cparams(("arbitrary",)),
        name="moe_down",
    )(blk_expert, n_active, nreal, dest.reshape(p // m, 1, m), act, *([w_down] * W_STREAMS), b_down)


def _combine_kernel(h_ref, y0_ref, y1_ref, y2_ref, y3_ref, gate_ref, g2_ref, b2_ref, o_ref):
    ffn = gate_ref[:, 0:1] * y0_ref[...]
    for j, y_ref in enumerate((y1_ref, y2_ref, y3_ref), start=1):
        ffn = ffn + gate_ref[:, j:j + 1] * y_ref[...]
    o_ref[...] = _layer_norm(DEEPNORM_ALPHA * h_ref[...] + ffn, g2_ref[...], b2_ref[...])


def _combine(h, yt, gate, g2, b2):
    t, d = h.shape
    tm = TOKEN_TILE
    nt = t // tm

    def choice(j):
        return pl.BlockSpec((tm, d), lambda i: (j * nt + i, 0))

    return pl.pallas_call(
        _combine_kernel,
        grid=(nt,),
        in_specs=[pl.BlockSpec((tm, d), lambda i: (i, 0)),
                  choice(0), choice(1), choice(2), choice(3),
                  pl.BlockSpec((tm, TOP_K), lambda i: (i, 0)),
                  pl.BlockSpec((1, d), lambda i: (0, 0)),
                  pl.BlockSpec((1, d), lambda i: (0, 0))],
        out_specs=pl.BlockSpec((tm, d), lambda i: (i, 0)),
        out_shape=jax.ShapeDtypeStruct((t, d), F32),
        compiler_params=_cparams(("parallel",)),
        name="combine_ln2",
    )(h, yt, yt, yt, yt, gate, g2, b2)


def _route(top_e, lrank, tilecnt):
    t = top_e.shape[0]
    m = MOE_BLOCK
    n_assign = t * TOP_K
    n_blocks = n_assign // m + N_EXPERTS
    p = n_blocks * m
    tc = tilecnt.reshape(-1, N_EXPERTS)
    base = jnp.cumsum(tc, axis=0) - tc
    counts = jnp.sum(tc, axis=0)
    padded = (counts + m - 1) // m * m
    pend = jnp.cumsum(padded)
    pstart = pend - padded
    base_tok = jnp.repeat(base, TOKEN_TILE, axis=0)
    rank = lrank + jnp.take_along_axis(base_tok, top_e, axis=1)
    slot = (pstart[top_e] + rank).reshape(-1)
    a_idx = jnp.arange(n_assign, dtype=I32)
    dest = jnp.full((p,), -1, I32).at[slot].set((a_idx % TOP_K) * t + a_idx // TOP_K)
    tok_pad = jnp.where(dest >= 0, dest % t, 0)
    blk_start = jnp.arange(n_blocks, dtype=I32) * m
    blk_expert = jnp.minimum(jnp.sum((pend[None, :] <= blk_start[:, None]).astype(I32), axis=1),
                             N_EXPERTS - 1)
    n_active = (pend[-1:] // m).astype(I32)
    real_end = (pstart + counts)[blk_expert]
    nreal = jnp.where(blk_start < pend[-1], jnp.clip(real_end - blk_start, 0, m), 0).astype(I32)
    return tok_pad, dest, blk_expert, n_active, nreal, (pstart // m).astype(I32), (padded // m).astype(I32)


def kernel(x_prompt, x_sample, cache_attn_k, cache_attn_v, state_hgrn, w_in, hgrn_lb_logits,
           hgrn_norm_w, w_branch_a, w_branch_b, w_out, ln1_g, ln1_b, w_router, b_router,
           w_gate_up, b_gate_up, w_down, b_down, ln2_g, ln2_b):
    assert w_in.shape[0] == DEPTH == 1
    bp, lp, d = x_prompt.shape
    bs, ls, _ = x_sample.shape
    tp, ts = bp * lp, bs * ls
    t = tp + ts
    keep = min(max(w for w, _ in DILATION_PAIRS), lp)

    xa = jnp.concatenate([x_prompt.reshape(tp, d), x_sample.reshape(ts, d)], axis=0)
    xb = xa.astype(BF16)
    wi = w_in[0]
    c1, c2 = P1_WIDTH, P1_WIDTH + P2_WIDTH
    p1 = _matmul(xb, wi[:, :c1].astype(BF16), 768, 1024, F32, "inproj_hgrn")
    p2 = _matmul(xb, wi[:, c1:c2].astype(BF16), 768, 512, F32, "inproj_attn")
    p3 = _matmul(xb, wi[:, c2:].astype(BF16), 768, 1024, F32, "inproj_gate")

    oa_p, st_p = _hgrn(p1, hgrn_lb_logits, hgrn_norm_w[0], None, bp, lp, 0)
    oa_s, st_s = _hgrn(p1, hgrn_lb_logits, hgrn_norm_w[0],
                       state_hgrn.reshape(bs, HG_HEADS, HG_DIM, HG_DIM), bs, ls, tp)

    n = jnp.arange(1, ATT_Q_HEADS + 1, dtype=F32)
    slopes = jnp.power(2.0, -8.0 * n / ATT_Q_HEADS).reshape(N_GROUPS, ATT_KV_HEADS)
    ctx = cache_attn_k.shape[2]
    ob_p = _attn_prompt(p2, slopes, bp, lp)
    ob_s = _attn_sample(p2, cache_attn_k.reshape(bs, ctx * ATT_KV_HEADS, ATT_HEAD_DIM),
                        cache_attn_v.reshape(bs, ctx * ATT_KV_HEADS, ATT_HEAD_DIM), slopes, bs, ls, tp)

    oa = jnp.concatenate([oa_p.reshape(tp, HG_WIDTH), oa_s.reshape(ts, HG_WIDTH)], axis=0)
    ob = jnp.concatenate([ob_p, ob_s.reshape(ts, ATT_KV_WIDTH)], axis=0)
    wr = w_router[0]
    wr0 = wr.astype(BF16)
    wr1 = (wr - wr0.astype(F32)).astype(BF16)
    h, top_e, gate, lrank, tilecnt = _merge(
        xa, oa, ob, p3, w_branch_a[0].astype(BF16), w_branch_b[0].astype(BF16),
        w_out[0].astype(BF16), ln1_g, ln1_b, wr0, wr1, b_router)

    tok_pad, dest, blk_expert, n_active, nreal, blk0, nblk = _route(top_e, lrank, tilecnt)
    xs = _moe_gather(h, tok_pad)
    act = _moe_up(xs, w_gate_up.reshape(N_EXPERTS, d, 2 * D_FF),
                  b_gate_up.reshape(N_EXPERTS, 1, 2 * D_FF), blk0, nblk)
    yt = _moe_down(act, w_down.reshape(N_EXPERTS, D_FF, d), b_down.reshape(N_EXPERTS, 1, d),
                   blk_expert, n_active, nreal, dest, t * TOP_K)
    y = _combine(h, yt, gate, ln2_g, ln2_b)

    k_new = p2[:, ATT_Q_HEADS * ATT_HEAD_DIM:(ATT_Q_HEADS + ATT_KV_HEADS) * ATT_HEAD_DIM]
    v_new = p2[:, (ATT_Q_HEADS + ATT_KV_HEADS) * ATT_HEAD_DIM:]

    def kv_prompt(a):
        return a[:tp].reshape(bp, lp, ATT_KV_HEADS, ATT_HEAD_DIM)[:, lp - keep:][None]

    def kv_sample(a):
        return a[tp:].reshape(bs, ls, ATT_KV_HEADS, ATT_HEAD_DIM)[None]

    return (y[:tp].reshape(bp, lp, d), y[tp:].reshape(bs, ls, d),
            kv_prompt(k_new), kv_prompt(v_new), st_p[None],
            kv_sample(k_new), kv_sample(v_new), st_s[None])
```

```python
import functools
import math

import jax
import jax.numpy as jnp
from jax import lax
from jax.experimental import pallas as pl
from jax.experimental.pallas import tpu as pltpu

F32 = jnp.float32
BF16 = jnp.bfloat16
I32 = jnp.int32

D_MODEL = 2048
DEPTH = 1
HG_HEADS = 8
HG_DIM = 128
HG_WIDTH = HG_HEADS * HG_DIM
HG_CHUNK = 64
ATT_HEAD_DIM = 128
DILATION_PAIRS = ((128, 1), (512, 4), (2048, 16))
N_GROUPS = len(DILATION_PAIRS)
ATT_KV_HEADS = 4
ATT_Q_HEADS = N_GROUPS * ATT_KV_HEADS
ATT_TAPS = 128
ATT_KV_WIDTH = ATT_KV_HEADS * ATT_HEAD_DIM
ATT_SCALE = ATT_HEAD_DIM ** -0.5
N_EXPERTS = 32
TOP_K = 4
D_FF = D_MODEL
SWIGLU_ALPHA = 1.702
SWIGLU_LIMIT = 7.0
LN_EPS = 1e-5
RMS_EPS = 1e-5
DEEPNORM_ALPHA = (2 * DEPTH) ** 0.25

P1_WIDTH = 4 * HG_WIDTH
P2_WIDTH = (ATT_Q_HEADS + 2 * ATT_KV_HEADS) * ATT_HEAD_DIM
P3_WIDTH = 2 * D_MODEL

LANES = 128
SUBLANES = 8
VMEM_LIMIT = 60 * 1024 * 1024

TOKEN_TILE = 256
MOE_BLOCK = 256
MOE_FT = 2048
W_STREAMS = 4
NEG = -1e30

_NT = (((1,), (1,)), ((), ()))
_TN = (((0,), (0,)), ((), ()))


def _cparams(sem):
    return pltpu.CompilerParams(dimension_semantics=sem, vmem_limit_bytes=VMEM_LIMIT)


def _sigmoid(x):
    return 1.0 / (1.0 + jnp.exp(-x))


def _mm_kernel(x_ref, w_ref, o_ref):
    o_ref[...] = jnp.dot(x_ref[...], w_ref[...], preferred_element_type=F32).astype(o_ref.dtype)


def _matmul(x, w, tm, tn, out_dtype, name):
    m, k = x.shape
    n = w.shape[1]
    assert m % tm == 0 and n % tn == 0
    return pl.pallas_call(
        _mm_kernel,
        grid=(n // tn, m // tm),
        in_specs=[pl.BlockSpec((tm, k), lambda j, i: (i, 0)),
                  pl.BlockSpec((k, tn), lambda j, i: (0, j))],
        out_specs=pl.BlockSpec((tm, tn), lambda j, i: (i, j)),
        out_shape=jax.ShapeDtypeStruct((m, n), out_dtype),
        compiler_params=_cparams(("parallel", "parallel")),
        name=name,
    )(x, w)


def _level_ref(g_scr, m, sl, chunk, sub):
    pieces = []
    for j in range(chunk // SUBLANES):
        if 2 * m >= SUBLANES:
            row = ((SUBLANES * j) // (2 * m)) * (2 * m) + m - 1
            pieces.append(jnp.broadcast_to(g_scr[row:row + 1, sl], (SUBLANES, LANES)))
        else:
            acc = None
            for u in range(SUBLANES // (2 * m)):
                row = SUBLANES * j + 2 * m * u + m - 1
                b = jnp.broadcast_to(g_scr[row:row + 1, sl], (SUBLANES, LANES))
                acc = b if acc is None else jnp.where(sub // (2 * m) == u, b, acc)
            pieces.append(acc)
    return pieces[0] if len(pieces) == 1 else jnp.concatenate(pieces, axis=0)


def _hgrn_kernel(lbl_ref, nw_ref, hq_ref, hf_ref, hi_ref, hg_ref, *rest, chunk, has_s0):
    if has_s0:
        s0_ref, o_ref, sout_ref, st_scr, g_scr = rest
    else:
        o_ref, sout_ref, st_scr, g_scr = rest
    c = pl.program_id(1)
    nc = pl.num_programs(1)

    @pl.when(c == 0)
    def _init():
        for h in range(HG_HEADS):
            if has_s0:
                st_scr[h] = s0_ref[0, h].T
            else:
                st_scr[h] = jnp.zeros((HG_DIM, HG_DIM), F32)

    lbl = lbl_ref[...]
    ex = jnp.exp(lbl - jnp.max(lbl, axis=0, keepdims=True))
    lb = ex[0:1] / jnp.sum(ex, axis=0, keepdims=True)

    f = lb + (1.0 - lb) * _sigmoid(hf_ref[...])
    logf = jnp.log(f)
    kall = 1.0 - f

    r_cc = lax.broadcasted_iota(I32, (chunk, chunk), 0)
    c_cc = lax.broadcasted_iota(I32, (chunk, chunk), 1)
    tri = jnp.where(r_cc >= c_cc, 1.0, 0.0).astype(BF16)
    p0 = logf.astype(BF16)
    r1 = logf - p0.astype(F32)
    p1 = r1.astype(BF16)
    p2 = (r1 - p1.astype(F32)).astype(BF16)
    g_all = (jnp.dot(tri, p0, preferred_element_type=F32)
             + jnp.dot(tri, p1, preferred_element_type=F32)
             + jnp.dot(tri, p2, preferred_element_type=F32))
    g_scr[...] = g_all

    sub = lax.broadcasted_iota(I32, (SUBLANES, LANES), 0)
    row_c = lax.broadcasted_iota(I32, (chunk, LANES), 0)
    nw = nw_ref[...]

    for h in range(HG_HEADS):
        sl = slice(h * HG_DIM, (h + 1) * HG_DIM)
        hq = hq_ref[:, sl]
        q = hq * _sigmoid(hq)
        k = kall[:, sl]
        g = g_all[:, sl]
        v = hi_ref[:, sl]
        vb = v.astype(BF16)

        p = lax.dot_general(q.astype(BF16), k.astype(BF16), _NT, preferred_element_type=F32)
        a = jnp.where(r_cc == c_cc, p, 0.0)
        m = 1
        while m < chunk:
            ref = _level_ref(g_scr, m, sl, chunk, sub)
            e = jnp.exp(-jnp.abs(g - ref))
            odd = ((row_c // m) & 1) == 1
            x = (jnp.where(odd, q, k) * e).astype(BF16)
            p = lax.dot_general(x, x, _NT, preferred_element_type=F32)
            pair = (r_cc // (2 * m)) == (c_cc // (2 * m))
            sel = jnp.where(pair, ((r_cc // m) & 1) - ((c_cc // m) & 1), 0) == 1
            a = jnp.where(sel, p, a)
            m *= 2

        st = st_scr[h]
        o = jnp.dot(a.astype(BF16), vb, preferred_element_type=F32)
        qe = (q * jnp.exp(g)).astype(BF16)
        o = o + lax.dot_general(qe, st.astype(BF16), _NT, preferred_element_type=F32)

        ms = jnp.mean(o * o, axis=-1, keepdims=True)
        hg = hg_ref[:, sl]
        out = o * lax.rsqrt(ms + RMS_EPS) * nw * (hg * _sigmoid(hg))
        o_ref[0, :, sl] = out.astype(o_ref.dtype)

        g_last = g[chunk - 1:chunk, :]
        kd = (k * jnp.exp(g_last - g)).astype(BF16)
        st_scr[h] = st * jnp.exp(g_last) + lax.dot_general(vb, kd, _TN, preferred_element_type=F32)

    @pl.when(c == nc - 1)
    def _fin():
        for h in range(HG_HEADS):
            sout_ref[0, h] = st_scr[h].T


def _hgrn(p1, lb_logits, norm_w, s0, batch, seq, row0):
    chunk = math.gcd(seq, HG_CHUNK)
    nchunk = seq // chunk
    assert row0 % chunk == 0
    rb0 = row0 // chunk

    def col(idx):
        return pl.BlockSpec((chunk, HG_WIDTH), lambda b, c: (rb0 + b * nchunk + c, idx))

    in_specs = [pl.BlockSpec(lb_logits.shape, lambda b, c: (0, 0)),
                pl.BlockSpec((1, HG_DIM), lambda b, c: (0, 0)),
                col(0), col(1), col(2), col(3)]
    args = [lb_logits, norm_w.reshape(1, HG_DIM), p1, p1, p1, p1]
    if s0 is not None:
        in_specs.append(pl.BlockSpec((1, HG_HEADS, HG_DIM, HG_DIM), lambda b, c: (b, 0, 0, 0)))
        args.append(s0)
    return pl.pallas_call(
        functools.partial(_hgrn_kernel, chunk=chunk, has_s0=s0 is not None),
        grid=(batch, nchunk),
        in_specs=in_specs,
        out_specs=[pl.BlockSpec((1, chunk, HG_WIDTH), lambda b, c: (b, c, 0)),
                   pl.BlockSpec((1, HG_HEADS, HG_DIM, HG_DIM), lambda b, c: (b, 0, 0, 0))],
        out_shape=[jax.ShapeDtypeStruct((batch, seq, HG_WIDTH), BF16),
                   jax.ShapeDtypeStruct((batch, HG_HEADS, HG_DIM, HG_DIM), F32)],
        scratch_shapes=[pltpu.VMEM((HG_HEADS, HG_DIM, HG_DIM), F32),
                        pltpu.VMEM((chunk, HG_WIDTH), F32)],
        compiler_params=_cparams(("parallel", "arbitrary")),
        name="hgrn_seq%d" % seq,
    )(*args)


QB = 128


def _attn_prompt_kernel(slopes_ref, q0_ref, q1_ref, q2_ref, k_ref, v_ref, o_ref,
                        og_scr, lse_scr, bias_scr, *, seq):
    h = pl.program_id(1)
    il = lax.broadcasted_iota(I32, (QB, 2 * QB), 0)
    jl = lax.broadcasted_iota(I32, (QB, 2 * QB), 1)
    delta = il + QB - jl
    band = jnp.where(delta >= 0, delta, ATT_TAPS + 1) <= ATT_TAPS
    prev_half = jl < QB
    q_refs = (q0_ref, q1_ref, q2_ref)

    for g, (window, dil) in enumerate(DILATION_PAIRS):
        assert window // dil == ATT_TAPS
        slope = slopes_ref[g, h]
        bias_scr[g] = jnp.where(band, (-slope * dil) * delta.astype(F32), NEG)
        nbr = (seq // dil) // QB
        q_ref = q_refs[g]

        def rows(start, dil=dil):
            return pl.ds(start, QB) if dil == 1 else pl.ds(start, QB, stride=dil)

        def body(ib, carry, g=g, dil=dil, nbr=nbr, q_ref=q_ref, rows=rows):
            res = ib // nbr
            jb = ib - res * nbr
            qs = res + dil * (jb * QB)
            ps = res + dil * jnp.maximum(jb * QB - QB, 0)
            qv = q_ref[rows(qs), :].astype(BF16)
            kb = jnp.concatenate([k_ref[rows(ps), :], k_ref[rows(qs), :]], axis=0).astype(BF16)
            vb = jnp.concatenate([v_ref[rows(ps), :], v_ref[rows(qs), :]], axis=0).astype(BF16)
            s = lax.dot_general(qv, kb, _NT, preferred_element_type=F32) * ATT_SCALE + bias_scr[g]
            s = jnp.where(jnp.where(prev_half, jb, 1) == 0, NEG, s)
            mx = jnp.max(s, axis=-1, keepdims=True)
            p = jnp.exp(s - mx)
            l = jnp.sum(p, axis=-1, keepdims=True)
            o = jnp.dot(p.astype(BF16), vb, preferred_element_type=F32) / l
            og_scr[g, rows(qs), :] = o
            lse_scr[g, rows(qs), :] = jnp.broadcast_to(mx + jnp.log(l), (QB, LANES))
            return carry

        lax.fori_loop(0, seq // QB, body, 0)

    def merge(tb, carry):
        r = pl.ds(pl.multiple_of(tb * QB, QB), QB)
        l0, l1, l2 = lse_scr[0, r, :], lse_scr[1, r, :], lse_scr[2, r, :]
        mx = jnp.maximum(jnp.maximum(l0, l1), l2)
        w0, w1, w2 = jnp.exp(l0 - mx), jnp.exp(l1 - mx), jnp.exp(l2 - mx)
        o = (w0 * og_scr[0, r, :] + w1 * og_scr[1, r, :] + w2 * og_scr[2, r, :]) / (w0 + w1 + w2)
        o_ref[r, :] = o.astype(o_ref.dtype)
        return carry

    lax.fori_loop(0, seq // QB, merge, 0)


def _attn_prompt(p2, slopes, batch, seq):
    assert seq % (QB * max(d for _, d in DILATION_PAIRS)) == 0

    def col(fn):
        return pl.BlockSpec((seq, ATT_HEAD_DIM), lambda b, h: (b, fn(h)))

    return pl.pallas_call(
        functools.partial(_attn_prompt_kernel, seq=seq),
        grid=(batch, ATT_KV_HEADS),
        in_specs=[pl.BlockSpec(memory_space=pltpu.SMEM),
                  col(lambda h: h), col(lambda h: ATT_KV_HEADS + h), col(lambda h: 2 * ATT_KV_HEADS + h),
                  col(lambda h: ATT_Q_HEADS + h), col(lambda h: ATT_Q_HEADS + ATT_KV_HEADS + h)],
        out_specs=pl.BlockSpec((seq, ATT_HEAD_DIM), lambda b, h: (b, h)),
        out_shape=jax.ShapeDtypeStruct((batch * seq, ATT_KV_WIDTH), BF16),
        scratch_shapes=[pltpu.VMEM((N_GROUPS, seq, ATT_HEAD_DIM), F32),
                        pltpu.VMEM((N_GROUPS, seq, LANES), F32),
                        pltpu.VMEM((N_GROUPS, QB, 2 * QB), F32)],
        compiler_params=_cparams(("parallel", "parallel")),
        name="attn_prompt",
    )(slopes, p2, p2, p2, p2, p2)


def _attn_sample_kernel(slopes_ref, qa_ref, qb_ref, qc_ref, kn_ref, vn_ref, kc_ref, vc_ref, o_ref,
                        *, ctx, nq):
    rows = N_GROUPS * nq
    gi = lax.broadcasted_iota(I32, (rows, 1), 0) // nq
    dil = jnp.where(gi == 0, DILATION_PAIRS[0][1], jnp.where(gi == 1, DILATION_PAIRS[1][1], DILATION_PAIRS[2][1]))
    win = jnp.where(gi == 0, DILATION_PAIRS[0][0], jnp.where(gi == 1, DILATION_PAIRS[1][0], DILATION_PAIRS[2][0]))

    def dist_valid(shape, key0, nkeys):
        r = lax.broadcasted_iota(I32, shape, 0)
        t = r - (r // nq) * nq
        col = lax.broadcasted_iota(I32, shape, 1)
        dist = ctx + t - (col + key0)
        bad = jnp.where(dist >= 0, dist & (dil - 1), 1)
        bad = jnp.where(dist <= win, bad, 1)
        bad = jnp.where(col < nkeys, bad, 1)
        return dist.astype(F32), bad == 0

    dist_c, ok_c = dist_valid((rows, ctx), 0, ctx)
    dist_n, ok_n = dist_valid((rows, LANES), ctx, nq)
    zpad = jnp.zeros((LANES - nq, ATT_HEAD_DIM), F32)

    for h in range(ATT_KV_HEADS):
        hs = slice(h * ATT_HEAD_DIM, (h + 1) * ATT_HEAD_DIM)
        slope = jnp.where(gi == 0, slopes_ref[0, h], jnp.where(gi == 1, slopes_ref[1, h], slopes_ref[2, h]))
        qh = jnp.concatenate([qa_ref[:, hs], qb_ref[:, hs], qc_ref[:, hs]], axis=0).astype(BF16)
        kc = kc_ref[0, pl.ds(h, ctx, stride=ATT_KV_HEADS), :].astype(BF16)
        vc = vc_ref[0, pl.ds(h, ctx, stride=ATT_KV_HEADS), :].astype(BF16)
        kn = jnp.concatenate([kn_ref[:, hs], zpad], axis=0).astype(BF16)
        vn = jnp.concatenate([vn_ref[:, hs], zpad], axis=0).astype(BF16)
        sc = lax.dot_general(qh, kc, _NT, preferred_element_type=F32) * ATT_SCALE - slope * dist_c
        sn = lax.dot_general(qh, kn, _NT, preferred_element_type=F32) * ATT_SCALE - slope * dist_n
        sc = jnp.where(ok_c, sc, NEG)
        sn = jnp.where(ok_n, sn, NEG)
        mx = jnp.maximum(jnp.max(sc, axis=-1, keepdims=True), jnp.max(sn, axis=-1, keepdims=True))
        pc = jnp.exp(sc - mx)
        pn = jnp.exp(sn - mx)
        l = jnp.sum(pc, axis=-1, keepdims=True) + jnp.sum(pn, axis=-1, keepdims=True)
        o = (jnp.dot(pc.astype(BF16), vc, preferred_element_type=F32)
             + jnp.dot(pn.astype(BF16), vn, preferred_element_type=F32)) / l
        lse = mx + jnp.log(l)
        l0, l1, l2 = lse[0:nq], lse[nq:2 * nq], lse[2 * nq:3 * nq]
        m3 = jnp.maximum(jnp.maximum(l0, l1), l2)
        w0, w1, w2 = jnp.exp(l0 - m3), jnp.exp(l1 - m3), jnp.exp(l2 - m3)
        out = (w0 * o[0:nq] + w1 * o[nq:2 * nq] + w2 * o[2 * nq:3 * nq]) / (w0 + w1 + w2)
        o_ref[0, :, hs] = out.astype(o_ref.dtype)


def _attn_sample(p2, cache_k, cache_v, slopes, batch, nq, row0):
    ctx = cache_k.shape[1] // ATT_KV_HEADS
    assert row0 % nq == 0 and nq == SUBLANES and ctx >= max(w for w, _ in DILATION_PAIRS)
    rb0 = row0 // nq

    def col(idx):
        return pl.BlockSpec((nq, ATT_KV_WIDTH), lambda b: (rb0 + b, idx))

    cache_spec = pl.BlockSpec((1, ctx * ATT_KV_HEADS, ATT_HEAD_DIM), lambda b: (b, 0, 0))
    return pl.pallas_call(
        functools.partial(_attn_sample_kernel, ctx=ctx, nq=nq),
        grid=(batch,),
        in_specs=[pl.BlockSpec(memory_space=pltpu.SMEM),
                  col(0), col(1), col(2), col(3), col(4), cache_spec, cache_spec],
        out_specs=pl.BlockSpec((1, nq, ATT_KV_WIDTH), lambda b: (b, 0, 0)),
        out_shape=jax.ShapeDtypeStruct((batch, nq, ATT_KV_WIDTH), BF16),
        compiler_params=_cparams(("parallel",)),
        name="attn_sample",
    )(slopes, p2, p2, p2, p2, p2, cache_k, cache_v)


def _layer_norm(y, g, b):
    mu = jnp.mean(y, axis=-1, keepdims=True)
    yc = y - mu
    var = jnp.mean(yc * yc, axis=-1, keepdims=True)
    return yc * lax.rsqrt(var + LN_EPS) * g + b


def _merge_kernel(xp_ref, xs_ref, oap_ref, oas_ref, obp_ref, obs_ref, ga_ref, gb_ref, wa_ref, wb_ref,
                  wo_ref, g1_ref, b1_ref, wr0_ref, wr1_ref, br_ref,
                  h_ref, tope_ref, gate_ref, lrank_ref, cnt_ref, *, prompt_tiles):
    tm = xp_ref.shape[0]
    is_prompt = pl.program_id(0) < prompt_tiles
    x = jnp.where(is_prompt, xp_ref[...], xs_ref[...])
    oa = jnp.where(is_prompt, oap_ref[...], oas_ref[...])
    ob = jnp.where(is_prompt, obp_ref[...], obs_ref[...])
    a = jnp.dot(oa, wa_ref[...], preferred_element_type=F32)
    b = jnp.dot(ob, wb_ref[...], preferred_element_type=F32)
    merged = _sigmoid(ga_ref[...]) * a + _sigmoid(gb_ref[...]) * b
    z = jnp.dot(merged.astype(BF16), wo_ref[...], preferred_element_type=F32)
    hh = _layer_norm(DEEPNORM_ALPHA * x + z, g1_ref[...], b1_ref[...])
    h_ref[...] = hh

    h0 = hh.astype(BF16)
    h1 = (hh - h0.astype(F32)).astype(BF16)
    logits = (jnp.dot(h0, wr0_ref[...], preferred_element_type=F32)
              + jnp.dot(h1, wr0_ref[...], preferred_element_type=F32)
              + jnp.dot(h0, wr1_ref[...], preferred_element_type=F32)) + br_ref[...]

    lane = lax.broadcasted_iota(I32, (tm, N_EXPERTS), 1)
    work = logits
    vals, idxs = [], []
    onehot = jnp.zeros((tm, N_EXPERTS), F32)
    for _ in range(TOP_K):
        mx = jnp.max(work, axis=-1, keepdims=True)
        idx = jnp.min(jnp.where(work == mx, lane, N_EXPERTS), axis=-1, keepdims=True)
        hit = lane == idx
        vals.append(mx)
        idxs.append(idx)
        onehot = jnp.where(hit, 1.0, onehot)
        work = jnp.where(hit, -jnp.inf, work)
    ex = [jnp.exp(v - vals[0]) for v in vals]
    den = ex[0] + ex[1] + ex[2] + ex[3]

    r_tt = lax.broadcasted_iota(I32, (tm, tm), 0)
    c_tt = lax.broadcasted_iota(I32, (tm, tm), 1)
    before = jnp.where(r_tt > c_tt, 1.0, 0.0).astype(BF16)
    prefix = jnp.dot(before, onehot.astype(BF16), preferred_element_type=F32)
    for j in range(TOP_K):
        tope_ref[:, j:j + 1] = idxs[j]
        gate_ref[:, j:j + 1] = ex[j] / den
        lrank_ref[:, j:j + 1] = jnp.sum(jnp.where(lane == idxs[j], prefix, 0.0), axis=-1,
                                        keepdims=True).astype(I32)
    cnt_ref[0] = jnp.sum(onehot, axis=0, keepdims=True).astype(I32)


def _merge(xp, xs, oap, oas, obp, obs, p3, wa, wb, wo, g1, b1, wr0, wr1, br):
    tm = TOKEN_TILE
    assert xp.shape[0] % tm == 0 and xs.shape[0] % tm == 0
    ntp = xp.shape[0] // tm
    nt = ntp + xs.shape[0] // tm
    t = nt * tm

    def prow(width):
        return pl.BlockSpec((tm, width), lambda i: (jnp.minimum(i, ntp - 1), 0))

    def srow(width):
        return pl.BlockSpec((tm, width), lambda i: (jnp.maximum(i - ntp, 0), 0))

    def full(a):
        return pl.BlockSpec(a.shape, lambda i: (0,) * a.ndim)

    def rowblk(width, idx=0):
        return pl.BlockSpec((tm, width), lambda i: (i, idx))

    return pl.pallas_call(
        functools.partial(_merge_kernel, prompt_tiles=ntp),
        grid=(nt,),
        in_specs=[prow(D_MODEL), srow(D_MODEL), prow(HG_WIDTH), srow(HG_WIDTH),
                  prow(ATT_KV_WIDTH), srow(ATT_KV_WIDTH), rowblk(D_MODEL, 0),
                  rowblk(D_MODEL, 1), full(wa), full(wb), full(wo), full(g1), full(b1),
                  full(wr0), full(wr1), full(br)],
        out_specs=[rowblk(D_MODEL), rowblk(TOP_K), rowblk(TOP_K), rowblk(TOP_K),
                   pl.BlockSpec((1, 1, N_EXPERTS), lambda i: (i, 0, 0))],
        out_shape=[jax.ShapeDtypeStruct((t, D_MODEL), F32),
                   jax.ShapeDtypeStruct((t, TOP_K), I32),
                   jax.ShapeDtypeStruct((t, TOP_K), F32),
                   jax.ShapeDtypeStruct((t, TOP_K), I32),
                   jax.ShapeDtypeStruct((nt, 1, N_EXPERTS), I32)],
        compiler_params=_cparams(("parallel",)),
        name="merge_ln1_router",
    )(xp, xs, oap, oas, obp, obs, p3, p3, wa, wb, wo, g1, b1, wr0, wr1, br)


def _gather_kernel(tok_ref, h_ref, o_ref, buf_ref):
    m = o_ref.shape[0]

    def body(q, carry):
        for u in range(SUBLANES):
            r = q * SUBLANES + u
            buf_ref[pl.ds(r, 1), :] = h_ref[pl.ds(tok_ref[0, 0, r], 1), :]
        return carry

    lax.fori_loop(0, m // SUBLANES, body, 0)
    o_ref[...] = buf_ref[...].astype(o_ref.dtype)


def _moe_gather(h, tok_pad):
    t, d = h.shape
    p = tok_pad.shape[0]
    m = MOE_BLOCK
    half = d // 2
    return pl.pallas_call(
        _gather_kernel,
        grid=(2, p // m),
        in_specs=[pl.BlockSpec((1, 1, m), lambda c, i: (i, 0, 0), memory_space=pltpu.SMEM),
                  pl.BlockSpec((t, half), lambda c, i: (0, c), pipeline_mode=pl.Buffered(1))],
        out_specs=pl.BlockSpec((m, half), lambda c, i: (i, c)),
        out_shape=jax.ShapeDtypeStruct((p, d), BF16),
        scratch_shapes=[pltpu.VMEM((m, half), F32)],
        compiler_params=_cparams(("arbitrary", "arbitrary")),
        name="moe_gather",
    )(tok_pad.reshape(p // m, 1, m), h)


def _moe_up_kernel(blk0_ref, nblk_ref, w0_ref, w1_ref, w2_ref, w3_ref, b_ref, xs_ref, act_ref,
                   wbf_scr, x_scr, h_scr, o_scr, sem_in, sem_out):
    e = pl.program_id(0)
    f = pl.program_id(1)
    n = nblk_ref[e]
    b0 = blk0_ref[e]
    m = x_scr.shape[1]
    w_refs = (w0_ref, w1_ref, w2_ref, w3_ref)
    kq, ft = w0_ref.shape[1:]
    half = ft // 2

    def rows(b):
        return pl.ds(pl.multiple_of((b0 + b) * m, m), m)

    def x_copy(b, s):
        return pltpu.make_async_copy(xs_ref.at[rows(b)], x_scr.at[s], sem_in.at[s])

    def o_copy(b, s):
        return pltpu.make_async_copy(o_scr.at[s], act_ref.at[f, rows(b)], sem_out.at[s])

    @pl.when(n > 0)
    def _():
        x_copy(0, 0).start()
        for q, w_ref in enumerate(w_refs):
            wbf_scr[q * kq:(q + 1) * kq, :] = w_ref[0].astype(BF16)
        even = (lax.broadcasted_iota(I32, (m, LANES), 1) & 1) == 0
        groups = ft // (2 * LANES)

        def fetch(b):
            x_copy(b, b & 1).wait()

            @pl.when(b + 1 < n)
            def _():
                x_copy(b + 1, 1 - (b & 1)).start()

        def product(b):
            x = x_scr[b & 1]
            for g in range(groups):
                cols = slice(2 * g * LANES, (2 * g + 2) * LANES)
                h_scr[b & 1, :, cols] = (jnp.dot(x, wbf_scr[:, cols], preferred_element_type=F32)
                                         + b_ref[0, :, cols])

        def activation(b):
            s = b & 1
            for g in range(groups):
                ha = h_scr[s, :, 2 * g * LANES:(2 * g + 1) * LANES]
                hb = h_scr[s, :, (2 * g + 1) * LANES:(2 * g + 2) * LANES]
                glu = jnp.minimum(jnp.where(even, ha, pltpu.roll(hb, 1, 1)), SWIGLU_LIMIT)
                lin = jnp.where(even, pltpu.roll(ha, LANES - 1, 1), hb)
                lin = jnp.clip(lin, -SWIGLU_LIMIT, SWIGLU_LIMIT) + 1.0
                act = glu * _sigmoid(SWIGLU_ALPHA * glu) * lin
                o_scr[s, :, g * LANES:(g + 1) * LANES] = act.astype(o_scr.dtype)
            o_copy(b, s).start()

        fetch(0)
        product(0)

        def body(b, carry):
            fetch(b)

            @pl.when(b >= 3)
            def _():
                o_copy(b - 3, (b - 1) & 1).wait()

            product(b)
            activation(b - 1)
            return carry

        lax.fori_loop(1, n, body, 0)

        @pl.when(n >= 3)
        def _():
            o_copy(n - 3, (n - 1) & 1).wait()
        activation(n - 1)

        @pl.when(n >= 2)
        def _():
            o_copy(n - 2, n & 1).wait()
        o_copy(n - 1, (n - 1) & 1).wait()

    @pl.when(e == pl.num_programs(0) - 1)
    def _():
        o_scr[0] = jnp.zeros(o_scr.shape[1:], o_scr.dtype)

        nz = act_ref.shape[1] // m - b0
        lax.fori_loop(n, nz, lambda b, c: (o_copy(b, 0).start(), c)[1], 0)
        lax.fori_loop(n, nz, lambda b, c: (o_copy(b, 0).wait(), c)[1], 0)


def _moe_up(xs, w_gate_up, b_gate_up, blk0, nblk):
    p, d = xs.shape
    m = MOE_BLOCK
    ft = MOE_FT
    ne = w_gate_up.shape[0]
    nf = w_gate_up.shape[2] // ft

    def wslice(q, e, f, b0, nb):
        return (e, q, f)

    grid_spec = pltpu.PrefetchScalarGridSpec(
        num_scalar_prefetch=2,
        grid=(ne, nf),
        in_specs=[pl.BlockSpec((1, d // W_STREAMS, ft), functools.partial(wslice, q))
                  for q in range(W_STREAMS)] + [
                  pl.BlockSpec((1, 1, ft), lambda e, f, b0, nb: (e, 0, f)),
                  pl.BlockSpec(memory_space=pl.ANY)],
        out_specs=pl.BlockSpec(memory_space=pl.ANY),
        scratch_shapes=[pltpu.VMEM((d, ft), BF16),
                        pltpu.VMEM((2, m, d), BF16),
                        pltpu.VMEM((2, m, ft), F32),
                        pltpu.VMEM((2, m, ft // 2), BF16),
                        pltpu.SemaphoreType.DMA((2,)),
                        pltpu.SemaphoreType.DMA((2,))],
    )
    return pl.pallas_call(
        _moe_up_kernel,
        grid_spec=grid_spec,
        out_shape=jax.ShapeDtypeStruct((nf, p, ft // 2), BF16),
        compiler_params=_cparams(("arbitrary", "arbitrary")),
        name="moe_up",
    )(blk0, nblk, *([w_gate_up] * W_STREAMS), b_gate_up, xs)


def _hidden_perm():
    r = lax.broadcasted_iota(I32, (LANES, LANES), 0)
    c = lax.broadcasted_iota(I32, (LANES, LANES), 1)
    src = (r >> 1) + (r & 1) * (LANES // 2)
    return jnp.where(c == src, 1.0, 0.0).astype(BF16)


def _moe_down_kernel(be_ref, na_ref, nreal_ref, dest_ref, a_ref, wd0_ref, wd1_ref, wd2_ref, wd3_ref,
                     bd_ref, yt_ref, wbf_scr, y_scr, sem):
    i = pl.program_id(0)
    nb = pl.num_programs(0)
    nf, m, fh = a_ref.shape
    slot = i % 2

    def row_copy(s, r, d):
        return pltpu.make_async_copy(y_scr.at[s, pl.ds(r, 1)], yt_ref.at[pl.ds(d, 1)], sem.at[s])

    def drain(s, count):
        @pl.when(count == m)
        def _():
            pltpu.make_async_copy(y_scr.at[s], yt_ref.at[pl.ds(0, m)], sem.at[s]).wait()

        @pl.when(count != m)
        def _():
            def wbody(_, carry):
                row_copy(s, 0, 0).wait()
                return carry
            lax.fori_loop(0, count, wbody, 0)

    @pl.when(i >= 2)
    def _():
        drain(slot, nreal_ref[i - 2])

    changed = jnp.where(i == 0, 1, be_ref[i] - be_ref[jnp.maximum(i - 1, 0)])

    @pl.when(changed != 0)
    def _():
        perm = _hidden_perm()
        for q, wd_ref in enumerate((wd0_ref, wd1_ref, wd2_ref, wd3_ref)):
            kq = wd_ref.shape[1]
            for g in range(kq // LANES):
                rows = slice(g * LANES, (g + 1) * LANES)
                wbf_scr[q * kq + g * LANES:q * kq + (g + 1) * LANES, :] = jnp.dot(
                    perm, wd_ref[0, rows, :].astype(BF16), preferred_element_type=F32).astype(BF16)

    @pl.when(i < na_ref[0])
    def _():
        y = bd_ref[0]
        for j in range(nf):
            y = y + jnp.dot(a_ref[j], wbf_scr[j * fh:(j + 1) * fh, :], preferred_element_type=F32)
        y_scr[slot] = y

        n = nreal_ref[i]
        unroll = 8

        def sbody8(q, carry):
            for u in range(unroll):
                r = q * unroll + u
                row_copy(slot, r, dest_ref[0, 0, r]).start()
            return carry

        def sbody1(r, carry):
            row_copy(slot, r, dest_ref[0, 0, r]).start()
            return carry

        lax.fori_loop(0, n // unroll, sbody8, 0)
        lax.fori_loop((n // unroll) * unroll, n, sbody1, 0)

    @pl.when(i == nb - 1)
    def _():
        @pl.when(i >= 1)
        def _():
            drain(1 - slot, nreal_ref[i - 1])
        drain(slot, nreal_ref[i])


def _moe_down(act, w_down, b_down, blk_expert, n_active, nreal, dest, n_rows):
    nf, p, fh = act.shape
    f = nf * fh
    m = MOE_BLOCK
    d = w_down.shape[2]

    def wslice(q, i, be, na, nr):
        return (be[i], q, 0)

    grid_spec = pltpu.PrefetchScalarGridSpec(
        num_scalar_prefetch=3,
        grid=(p // m,),
        in_specs=[pl.BlockSpec((1, 1, m), lambda i, be, na, nr: (i, 0, 0), memory_space=pltpu.SMEM),
                  pl.BlockSpec((nf, m, fh), lambda i, be, na, nr: (0, i, 0)),
                  *[pl.BlockSpec((1, f // W_STREAMS, d), functools.partial(wslice, q))
                    for q in range(W_STREAMS)],
                  pl.BlockSpec((1, 1, d), lambda i, be, na, nr: (be[i], 0, 0))],
        out_specs=pl.BlockSpec(memory_space=pl.ANY),
        scratch_shapes=[pltpu.VMEM((f, d), BF16),
                        pltpu.VMEM((2, m, d), F32),
                        pltpu.SemaphoreType.DMA((2,))],
    )
    return pl.pallas_call(
        _moe_down_kernel,
        grid_spec=grid_spec,
        out_shape=jax.ShapeDtypeStruct((n_rows, d), F32),
        compiler_params=_cparams(("arbitrary",)),
        name="moe_down",
    )(blk_expert, n_active, nreal, dest.reshape(p // m, 1, m), act, *([w_down] * W_STREAMS), b_down)


def _combine_kernel(h_ref, y0_ref, y1_ref, y2_ref, y3_ref, gate_ref, g2_ref, b2_ref, o_ref):
    ffn = gate_ref[:, 0:1] * y0_ref[...]
    for j, y_ref in enumerate((y1_ref, y2_ref, y3_ref), start=1):
        ffn = ffn + gate_ref[:, j:j + 1] * y_ref[...]
    o_ref[...] = _layer_norm(DEEPNORM_ALPHA * h_ref[...] + ffn, g2_ref[...], b2_ref[...])


def _combine(h, yt, gate, g2, b2):
    t, d = h.shape
    tm = TOKEN_TILE
    nt = t // tm

    def choice(j):
        return pl.BlockSpec((tm, d), lambda i: (j * nt + i, 0))

    return pl.pallas_call(
        _combine_kernel,
        grid=(nt,),
        in_specs=[pl.BlockSpec((tm, d), lambda i: (i, 0)),
                  choice(0), choice(1), choice(2), choice(3),
                  pl.BlockSpec((tm, TOP_K), lambda i: (i, 0)),
                  pl.BlockSpec((1, d), lambda i: (0, 0)),
                  pl.BlockSpec((1, d), lambda i: (0, 0))],
        out_specs=pl.BlockSpec((tm, d), lambda i: (i, 0)),
        out_shape=jax.ShapeDtypeStruct((t, d), F32),
        compiler_params=_cparams(("parallel",)),
        name="combine_ln2",
    )(h, yt, yt, yt, yt, gate, g2, b2)


def _route(top_e, lrank, tilecnt):
    t = top_e.shape[0]
    m = MOE_BLOCK
    n_assign = t * TOP_K
    n_blocks = n_assign // m + N_EXPERTS
    p = n_blocks * m
    tc = tilecnt.reshape(-1, N_EXPERTS)
    base = jnp.cumsum(tc, axis=0) - tc
    counts = jnp.sum(tc, axis=0)
    padded = (counts + m - 1) // m * m
    pend = jnp.cumsum(padded)
    pstart = pend - padded
    base_tok = jnp.repeat(base, TOKEN_TILE, axis=0)
    rank = lrank + jnp.take_along_axis(base_tok, top_e, axis=1)
    slot = (pstart[top_e] + rank).reshape(-1)
    a_idx = jnp.arange(n_assign, dtype=I32)
    dest = jnp.full((p,), -1, I32).at[slot].set((a_idx % TOP_K) * t + a_idx // TOP_K,
                                               unique_indices=True)
    tok_pad = jnp.where(dest >= 0, dest % t, 0)
    blk_start = jnp.arange(n_blocks, dtype=I32) * m
    blk_expert = jnp.minimum(jnp.sum((pend[None, :] <= blk_start[:, None]).astype(I32), axis=1),
                             N_EXPERTS - 1)
    n_active = (pend[-1:] // m).astype(I32)
    real_end = (pstart + counts)[blk_expert]
    nreal = jnp.where(blk_start < pend[-1], jnp.clip(real_end - blk_start, 0, m), 0).astype(I32)
    return tok_pad, dest, blk_expert, n_active, nreal, (pstart // m).astype(I32), (padded // m).astype(I32)


def kernel(x_prompt, x_sample, cache_attn_k, cache_attn_v, state_hgrn, w_in, hgrn_lb_logits,
           hgrn_norm_w, w_branch_a, w_branch_b, w_out, ln1_g, ln1_b, w_router, b_router,
           w_gate_up, b_gate_up, w_down, b_down, ln2_g, ln2_b):
    assert w_in.shape[0] == DEPTH == 1
    bp, lp, d = x_prompt.shape
    bs, ls, _ = x_sample.shape
    tp, ts = bp * lp, bs * ls
    t = tp + ts
    keep = min(max(w for w, _ in DILATION_PAIRS), lp)

    xp2, xs2 = x_prompt.reshape(tp, d), x_sample.reshape(ts, d)
    xb = jnp.concatenate([xp2.astype(BF16), xs2.astype(BF16)], axis=0)
    wi = w_in[0]
    c1, c2 = P1_WIDTH, P1_WIDTH + P2_WIDTH
    p1 = _matmul(xb, wi[:, :c1].astype(BF16), 768, 1024, F32, "inproj_hgrn")
    p2 = _matmul(xb, wi[:, c1:c2].astype(BF16), 768, 512, F32, "inproj_attn")
    p3 = _matmul(xb, wi[:, c2:].astype(BF16), 768, 1024, F32, "inproj_gate")

    oa_p, st_p = _hgrn(p1, hgrn_lb_logits, hgrn_norm_w[0], None, bp, lp, 0)
    oa_s, st_s = _hgrn(p1, hgrn_lb_logits, hgrn_norm_w[0],
                       state_hgrn.reshape(bs, HG_HEADS, HG_DIM, HG_DIM), bs, ls, tp)

    n = jnp.arange(1, ATT_Q_HEADS + 1, dtype=F32)
    slopes = jnp.power(2.0, -8.0 * n / ATT_Q_HEADS).reshape(N_GROUPS, ATT_KV_HEADS)
    ctx = cache_attn_k.shape[2]
    ob_p = _attn_prompt(p2, slopes, bp, lp)
    ob_s = _attn_sample(p2, cache_attn_k.reshape(bs, ctx * ATT_KV_HEADS, ATT_HEAD_DIM),
                        cache_attn_v.reshape(bs, ctx * ATT_KV_HEADS, ATT_HEAD_DIM), slopes, bs, ls, tp)

    wr = w_router[0]
    wr0 = wr.astype(BF16)
    wr1 = (wr - wr0.astype(F32)).astype(BF16)
    h, top_e, gate, lrank, tilecnt = _merge(
        xp2, xs2, oa_p.reshape(tp, HG_WIDTH), oa_s.reshape(ts, HG_WIDTH), ob_p,
        ob_s.reshape(ts, ATT_KV_WIDTH), p3, w_branch_a[0].astype(BF16), w_branch_b[0].astype(BF16),
        w_out[0].astype(BF16), ln1_g, ln1_b, wr0, wr1, b_router)

    tok_pad, dest, blk_expert, n_active, nreal, blk0, nblk = _route(top_e, lrank, tilecnt)
    xs = _moe_gather(h, tok_pad)
    act = _moe_up(xs, w_gate_up.reshape(N_EXPERTS, d, 2 * D_FF),
                  b_gate_up.reshape(N_EXPERTS, 1, 2 * D_FF), blk0, nblk)
    yt = _moe_down(act, w_down.reshape(N_EXPERTS, D_FF, d), b_down.reshape(N_EXPERTS, 1, d),
                   blk_expert, n_active, nreal, dest, t * TOP_K)
    y = _combine(h, yt, gate, ln2_g, ln2_b)

    k_new = p2[:, ATT_Q_HEADS * ATT_HEAD_DIM:(ATT_Q_HEADS + ATT_KV_HEADS) * ATT_HEAD_DIM]
    v_new = p2[:, (ATT_Q_HEADS + ATT_KV_HEADS) * ATT_HEAD_DIM:]

    def kv_prompt(a):
        return a[:tp].reshape(bp, lp, ATT_KV_HEADS, ATT_HEAD_DIM)[:, lp - keep:][None]

    def kv_sample(a):
        return a[tp:].reshape(bs, ls, ATT_KV_HEADS, ATT_HEAD_DIM)[None]

    return (y[:tp].reshape(bp, lp, d), y[tp:].reshape(bs, ls, d),
            kv_prompt(k_new), kv_prompt(v_new), st_p[None],
            kv_sample(k_new), kv_sample(v_new), st_s[None])
```

```python
import functools
import math

import jax
import jax.numpy as jnp
from jax import lax
from jax.experimental import pallas as pl
from jax.experimental.pallas import tpu as pltpu

F32 = jnp.float32
BF16 = jnp.bfloat16
I32 = jnp.int32

D_MODEL = 2048
DEPTH = 1
HG_HEADS = 8
HG_DIM = 128
HG_WIDTH = HG_HEADS * HG_DIM
HG_CHUNK = 64
ATT_HEAD_DIM = 128
DILATION_PAIRS = ((128, 1), (512, 4), (2048, 16))
N_GROUPS = len(DILATION_PAIRS)
ATT_KV_HEADS = 4
ATT_Q_HEADS = N_GROUPS * ATT_KV_HEADS
ATT_TAPS = 128
ATT_KV_WIDTH = ATT_KV_HEADS * ATT_HEAD_DIM
ATT_SCALE = ATT_HEAD_DIM ** -0.5
N_EXPERTS = 32
TOP_K = 4
D_FF = D_MODEL
SWIGLU_ALPHA = 1.702
SWIGLU_LIMIT = 7.0
LN_EPS = 1e-5
RMS_EPS = 1e-5
DEEPNORM_ALPHA = (2 * DEPTH) ** 0.25

P1_WIDTH = 4 * HG_WIDTH
P2_WIDTH = (ATT_Q_HEADS + 2 * ATT_KV_HEADS) * ATT_HEAD_DIM
P3_WIDTH = 2 * D_MODEL

LANES = 128
SUBLANES = 8
VMEM_LIMIT = 60 * 1024 * 1024

TOKEN_TILE = 256
MOE_BLOCK = 256
MOE_FT = 2048
W_STREAMS = 4
NEG = -1e30

_NT = (((1,), (1,)), ((), ()))
_TN = (((0,), (0,)), ((), ()))


def _cparams(sem):
    return pltpu.CompilerParams(dimension_semantics=sem, vmem_limit_bytes=VMEM_LIMIT)


def _sigmoid(x):
    return 1.0 / (1.0 + jnp.exp(-x))


def _mm_kernel(x_ref, w_ref, o_ref):
    o_ref[...] = jnp.dot(x_ref[...], w_ref[...], preferred_element_type=F32).astype(o_ref.dtype)


def _matmul(x, w, tm, tn, out_dtype, name):
    m, k = x.shape
    n = w.shape[1]
    assert m % tm == 0 and n % tn == 0
    return pl.pallas_call(
        _mm_kernel,
        grid=(n // tn, m // tm),
        in_specs=[pl.BlockSpec((tm, k), lambda j, i: (i, 0)),
                  pl.BlockSpec((k, tn), lambda j, i: (0, j))],
        out_specs=pl.BlockSpec((tm, tn), lambda j, i: (i, j)),
        out_shape=jax.ShapeDtypeStruct((m, n), out_dtype),
        compiler_params=_cparams(("parallel", "parallel")),
        name=name,
    )(x, w)


def _level_ref(g_scr, m, sl, chunk, sub):
    pieces = []
    for j in range(chunk // SUBLANES):
        if 2 * m >= SUBLANES:
            row = ((SUBLANES * j) // (2 * m)) * (2 * m) + m - 1
            pieces.append(jnp.broadcast_to(g_scr[row:row + 1, sl], (SUBLANES, LANES)))
        else:
            acc = None
            for u in range(SUBLANES // (2 * m)):
                row = SUBLANES * j + 2 * m * u + m - 1
                b = jnp.broadcast_to(g_scr[row:row + 1, sl], (SUBLANES, LANES))
                acc = b if acc is None else jnp.where(sub // (2 * m) == u, b, acc)
            pieces.append(acc)
    return pieces[0] if len(pieces) == 1 else jnp.concatenate(pieces, axis=0)


def _hgrn_kernel(lbl_ref, nw_ref, hq_ref, hf_ref, hi_ref, hg_ref, *rest, chunk, has_s0):
    if has_s0:
        s0_ref, o_ref, sout_ref, st_scr, g_scr = rest
    else:
        o_ref, sout_ref, st_scr, g_scr = rest
    c = pl.program_id(1)
    nc = pl.num_programs(1)

    @pl.when(c == 0)
    def _init():
        for h in range(HG_HEADS):
            if has_s0:
                st_scr[h] = s0_ref[0, h].T
            else:
                st_scr[h] = jnp.zeros((HG_DIM, HG_DIM), F32)

    lbl = lbl_ref[...]
    ex = jnp.exp(lbl - jnp.max(lbl, axis=0, keepdims=True))
    lb = ex[0:1] / jnp.sum(ex, axis=0, keepdims=True)

    f = lb + (1.0 - lb) * _sigmoid(hf_ref[...])
    logf = jnp.log(f)
    kall = 1.0 - f

    r_cc = lax.broadcasted_iota(I32, (chunk, chunk), 0)
    c_cc = lax.broadcasted_iota(I32, (chunk, chunk), 1)
    tri = jnp.where(r_cc >= c_cc, 1.0, 0.0).astype(BF16)
    p0 = logf.astype(BF16)
    r1 = logf - p0.astype(F32)
    p1 = r1.astype(BF16)
    p2 = (r1 - p1.astype(F32)).astype(BF16)
    g_all = (jnp.dot(tri, p0, preferred_element_type=F32)
             + jnp.dot(tri, p1, preferred_element_type=F32)
             + jnp.dot(tri, p2, preferred_element_type=F32))
    g_scr[...] = g_all

    sub = lax.broadcasted_iota(I32, (SUBLANES, LANES), 0)
    row_c = lax.broadcasted_iota(I32, (chunk, LANES), 0)
    nw = nw_ref[...]

    for h in range(HG_HEADS):
        sl = slice(h * HG_DIM, (h + 1) * HG_DIM)
        hq = hq_ref[:, sl]
        q = hq * _sigmoid(hq)
        k = kall[:, sl]
        g = g_all[:, sl]
        v = hi_ref[:, sl]
        vb = v.astype(BF16)

        p = lax.dot_general(q.astype(BF16), k.astype(BF16), _NT, preferred_element_type=F32)
        a = jnp.where(r_cc == c_cc, p, 0.0)
        m = 1
        while m < chunk:
            ref = _level_ref(g_scr, m, sl, chunk, sub)
            e = jnp.exp(-jnp.abs(g - ref))
            odd = ((row_c // m) & 1) == 1
            x = (jnp.where(odd, q, k) * e).astype(BF16)
            p = lax.dot_general(x, x, _NT, preferred_element_type=F32)
            pair = (r_cc // (2 * m)) == (c_cc // (2 * m))
            sel = jnp.where(pair, ((r_cc // m) & 1) - ((c_cc // m) & 1), 0) == 1
            a = jnp.where(sel, p, a)
            m *= 2

        st = st_scr[h]
        o = jnp.dot(a.astype(BF16), vb, preferred_element_type=F32)
        qe = (q * jnp.exp(g)).astype(BF16)
        o = o + lax.dot_general(qe, st.astype(BF16), _NT, preferred_element_type=F32)

        ms = jnp.mean(o * o, axis=-1, keepdims=True)
        hg = hg_ref[:, sl]
        out = o * lax.rsqrt(ms + RMS_EPS) * nw * (hg * _sigmoid(hg))
        o_ref[0, :, sl] = out.astype(o_ref.dtype)

        g_last = g[chunk - 1:chunk, :]
        kd = (k * jnp.exp(g_last - g)).astype(BF16)
        st_scr[h] = st * jnp.exp(g_last) + lax.dot_general(vb, kd, _TN, preferred_element_type=F32)

    @pl.when(c == nc - 1)
    def _fin():
        for h in range(HG_HEADS):
            sout_ref[0, h] = st_scr[h].T


def _hgrn(p1, lb_logits, norm_w, s0, batch, seq, row0):
    chunk = math.gcd(seq, HG_CHUNK)
    nchunk = seq // chunk
    assert row0 % chunk == 0
    rb0 = row0 // chunk

    def col(idx):
        return pl.BlockSpec((chunk, HG_WIDTH), lambda b, c: (rb0 + b * nchunk + c, idx))

    in_specs = [pl.BlockSpec(lb_logits.shape, lambda b, c: (0, 0)),
                pl.BlockSpec((1, HG_DIM), lambda b, c: (0, 0)),
                col(0), col(1), col(2), col(3)]
    args = [lb_logits, norm_w.reshape(1, HG_DIM), p1, p1, p1, p1]
    if s0 is not None:
        in_specs.append(pl.BlockSpec((1, HG_HEADS, HG_DIM, HG_DIM), lambda b, c: (b, 0, 0, 0)))
        args.append(s0)
    return pl.pallas_call(
        functools.partial(_hgrn_kernel, chunk=chunk, has_s0=s0 is not None),
        grid=(batch, nchunk),
        in_specs=in_specs,
        out_specs=[pl.BlockSpec((1, chunk, HG_WIDTH), lambda b, c: (b, c, 0)),
                   pl.BlockSpec((1, HG_HEADS, HG_DIM, HG_DIM), lambda b, c: (b, 0, 0, 0))],
        out_shape=[jax.ShapeDtypeStruct((batch, seq, HG_WIDTH), BF16),
                   jax.ShapeDtypeStruct((batch, HG_HEADS, HG_DIM, HG_DIM), F32)],
        scratch_shapes=[pltpu.VMEM((HG_HEADS, HG_DIM, HG_DIM), F32),
                        pltpu.VMEM((chunk, HG_WIDTH), F32)],
        compiler_params=_cparams(("parallel", "arbitrary")),
        name="hgrn_seq%d" % seq,
    )(*args)


QB = 128
ATTN_UNROLL = 4


def _attn_prompt_kernel(slopes_ref, q0_ref, q1_ref, q2_ref, k_ref, v_ref, o_ref,
                        og_scr, lse_scr, bias_scr, *, seq):
    h = pl.program_id(1)
    il = lax.broadcasted_iota(I32, (QB, 2 * QB), 0)
    jl = lax.broadcasted_iota(I32, (QB, 2 * QB), 1)
    delta = il + QB - jl
    band = jnp.where(delta >= 0, delta, ATT_TAPS + 1) <= ATT_TAPS
    prev_half = jl < QB
    q_refs = (q0_ref, q1_ref, q2_ref)

    for g, (window, dil) in enumerate(DILATION_PAIRS):
        assert window // dil == ATT_TAPS
        slope = slopes_ref[g, h]
        bias_scr[g] = jnp.where(band, (-slope * dil) * delta.astype(F32), NEG)
        nbr = (seq // dil) // QB
        q_ref = q_refs[g]

        def rows(start, dil=dil):
            return pl.ds(start, QB) if dil == 1 else pl.ds(start, QB, stride=dil)

        def body(ib, carry, g=g, dil=dil, nbr=nbr, q_ref=q_ref, rows=rows):
            res = ib // nbr
            jb = ib - res * nbr
            qs = res + dil * (jb * QB)
            ps = res + dil * jnp.maximum(jb * QB - QB, 0)
            qv = q_ref[rows(qs), :].astype(BF16)
            kb = jnp.concatenate([k_ref[rows(ps), :], k_ref[rows(qs), :]], axis=0).astype(BF16)
            vb = jnp.concatenate([v_ref[rows(ps), :], v_ref[rows(qs), :]], axis=0).astype(BF16)
            s = lax.dot_general(qv, kb, _NT, preferred_element_type=F32) * ATT_SCALE + bias_scr[g]
            s = jnp.where(jnp.where(prev_half, jb, 1) == 0, NEG, s)
            mx = jnp.max(s, axis=-1, keepdims=True)
            p = jnp.exp(s - mx)
            l = jnp.sum(p, axis=-1, keepdims=True)
            o = jnp.dot(p.astype(BF16), vb, preferred_element_type=F32) / l
            og_scr[g, rows(qs), :] = o
            lse_scr[g, rows(qs), :] = jnp.broadcast_to(mx + jnp.log(l), (QB, LANES))
            return carry

        lax.fori_loop(0, seq // QB, body, 0, unroll=ATTN_UNROLL)

    def merge(tb, carry):
        r = pl.ds(pl.multiple_of(tb * QB, QB), QB)
        l0, l1, l2 = lse_scr[0, r, :], lse_scr[1, r, :], lse_scr[2, r, :]
        mx = jnp.maximum(jnp.maximum(l0, l1), l2)
        w0, w1, w2 = jnp.exp(l0 - mx), jnp.exp(l1 - mx), jnp.exp(l2 - mx)
        o = (w0 * og_scr[0, r, :] + w1 * og_scr[1, r, :] + w2 * og_scr[2, r, :]) / (w0 + w1 + w2)
        o_ref[r, :] = o.astype(o_ref.dtype)
        return carry

    lax.fori_loop(0, seq // QB, merge, 0, unroll=ATTN_UNROLL)


def _attn_prompt(p2, slopes, batch, seq):
    assert seq % (QB * max(d for _, d in DILATION_PAIRS)) == 0

    def col(fn):
        return pl.BlockSpec((seq, ATT_HEAD_DIM), lambda b, h: (b, fn(h)))

    return pl.pallas_call(
        functools.partial(_attn_prompt_kernel, seq=seq),
        grid=(batch, ATT_KV_HEADS),
        in_specs=[pl.BlockSpec(memory_space=pltpu.SMEM),
                  col(lambda h: h), col(lambda h: ATT_KV_HEADS + h), col(lambda h: 2 * ATT_KV_HEADS + h),
                  col(lambda h: ATT_Q_HEADS + h), col(lambda h: ATT_Q_HEADS + ATT_KV_HEADS + h)],
        out_specs=pl.BlockSpec((seq, ATT_HEAD_DIM), lambda b, h: (b, h)),
        out_shape=jax.ShapeDtypeStruct((batch * seq, ATT_KV_WIDTH), BF16),
        scratch_shapes=[pltpu.VMEM((N_GROUPS, seq, ATT_HEAD_DIM), F32),
                        pltpu.VMEM((N_GROUPS, seq, LANES), F32),
                        pltpu.VMEM((N_GROUPS, QB, 2 * QB), F32)],
        compiler_params=_cparams(("parallel", "parallel")),
        name="attn_prompt",
    )(slopes, p2, p2, p2, p2, p2)


def _attn_sample_kernel(slopes_ref, qa_ref, qb_ref, qc_ref, kn_ref, vn_ref, kc_ref, vc_ref, o_ref,
                        *, ctx, nq):
    rows = N_GROUPS * nq
    gi = lax.broadcasted_iota(I32, (rows, 1), 0) // nq
    dil = jnp.where(gi == 0, DILATION_PAIRS[0][1], jnp.where(gi == 1, DILATION_PAIRS[1][1], DILATION_PAIRS[2][1]))
    win = jnp.where(gi == 0, DILATION_PAIRS[0][0], jnp.where(gi == 1, DILATION_PAIRS[1][0], DILATION_PAIRS[2][0]))

    def dist_valid(shape, key0, nkeys):
        r = lax.broadcasted_iota(I32, shape, 0)
        t = r - (r // nq) * nq
        col = lax.broadcasted_iota(I32, shape, 1)
        dist = ctx + t - (col + key0)
        bad = jnp.where(dist >= 0, dist & (dil - 1), 1)
        bad = jnp.where(dist <= win, bad, 1)
        bad = jnp.where(col < nkeys, bad, 1)
        return dist.astype(F32), bad == 0

    dist_c, ok_c = dist_valid((rows, ctx), 0, ctx)
    dist_n, ok_n = dist_valid((rows, LANES), ctx, nq)
    zpad = jnp.zeros((LANES - nq, ATT_HEAD_DIM), F32)

    for h in range(ATT_KV_HEADS):
        hs = slice(h * ATT_HEAD_DIM, (h + 1) * ATT_HEAD_DIM)
        slope = jnp.where(gi == 0, slopes_ref[0, h], jnp.where(gi == 1, slopes_ref[1, h], slopes_ref[2, h]))
        qh = jnp.concatenate([qa_ref[:, hs], qb_ref[:, hs], qc_ref[:, hs]], axis=0).astype(BF16)
        kc = kc_ref[0, pl.ds(h, ctx, stride=ATT_KV_HEADS), :].astype(BF16)
        vc = vc_ref[0, pl.ds(h, ctx, stride=ATT_KV_HEADS), :].astype(BF16)
        kn = jnp.concatenate([kn_ref[:, hs], zpad], axis=0).astype(BF16)
        vn = jnp.concatenate([vn_ref[:, hs], zpad], axis=0).astype(BF16)
        sc = lax.dot_general(qh, kc, _NT, preferred_element_type=F32) * ATT_SCALE - slope * dist_c
        sn = lax.dot_general(qh, kn, _NT, preferred_element_type=F32) * ATT_SCALE - slope * dist_n
        sc = jnp.where(ok_c, sc, NEG)
        sn = jnp.where(ok_n, sn, NEG)
        mx = jnp.maximum(jnp.max(sc, axis=-1, keepdims=True), jnp.max(sn, axis=-1, keepdims=True))
        pc = jnp.exp(sc - mx)
        pn = jnp.exp(sn - mx)
        l = jnp.sum(pc, axis=-1, keepdims=True) + jnp.sum(pn, axis=-1, keepdims=True)
        o = (jnp.dot(pc.astype(BF16), vc, preferred_element_type=F32)
             + jnp.dot(pn.astype(BF16), vn, preferred_element_type=F32)) / l
        lse = mx + jnp.log(l)
        l0, l1, l2 = lse[0:nq], lse[nq:2 * nq], lse[2 * nq:3 * nq]
        m3 = jnp.maximum(jnp.maximum(l0, l1), l2)
        w0, w1, w2 = jnp.exp(l0 - m3), jnp.exp(l1 - m3), jnp.exp(l2 - m3)
        out = (w0 * o[0:nq] + w1 * o[nq:2 * nq] + w2 * o[2 * nq:3 * nq]) / (w0 + w1 + w2)
        o_ref[0, :, hs] = out.astype(o_ref.dtype)


def _attn_sample(p2, cache_k, cache_v, slopes, batch, nq, row0):
    ctx = cache_k.shape[1] // ATT_KV_HEADS
    assert row0 % nq == 0 and nq == SUBLANES and ctx >= max(w for w, _ in DILATION_PAIRS)
    rb0 = row0 // nq

    def col(idx):
        return pl.BlockSpec((nq, ATT_KV_WIDTH), lambda b: (rb0 + b, idx))

    cache_spec = pl.BlockSpec((1, ctx * ATT_KV_HEADS, ATT_HEAD_DIM), lambda b: (b, 0, 0))
    return pl.pallas_call(
        functools.partial(_attn_sample_kernel, ctx=ctx, nq=nq),
        grid=(batch,),
        in_specs=[pl.BlockSpec(memory_space=pltpu.SMEM),
                  col(0), col(1), col(2), col(3), col(4), cache_spec, cache_spec],
        out_specs=pl.BlockSpec((1, nq, ATT_KV_WIDTH), lambda b: (b, 0, 0)),
        out_shape=jax.ShapeDtypeStruct((batch, nq, ATT_KV_WIDTH), BF16),
        compiler_params=_cparams(("parallel",)),
        name="attn_sample",
    )(slopes, p2, p2, p2, p2, p2, cache_k, cache_v)


def _layer_norm(y, g, b):
    mu = jnp.mean(y, axis=-1, keepdims=True)
    yc = y - mu
    var = jnp.mean(yc * yc, axis=-1, keepdims=True)
    return yc * lax.rsqrt(var + LN_EPS) * g + b


def _merge_kernel(xp_ref, xs_ref, oap_ref, oas_ref, obp_ref, obs_ref, ga_ref, gb_ref, wa_ref, wb_ref,
                  wo_ref, g1_ref, b1_ref, wr0_ref, wr1_ref, br_ref,
                  h_ref, tope_ref, gate_ref, lrank_ref, cnt_ref, *, prompt_tiles):
    tm = xp_ref.shape[0]
    is_prompt = pl.program_id(0) < prompt_tiles
    x = jnp.where(is_prompt, xp_ref[...], xs_ref[...])
    oa = jnp.where(is_prompt, oap_ref[...], oas_ref[...])
    ob = jnp.where(is_prompt, obp_ref[...], obs_ref[...])
    a = jnp.dot(oa, wa_ref[...], preferred_element_type=F32)
    b = jnp.dot(ob, wb_ref[...], preferred_element_type=F32)
    merged = _sigmoid(ga_ref[...]) * a + _sigmoid(gb_ref[...]) * b
    z = jnp.dot(merged.astype(BF16), wo_ref[...], preferred_element_type=F32)
    hh = _layer_norm(DEEPNORM_ALPHA * x + z, g1_ref[...], b1_ref[...])
    h_ref[...] = hh

    h0 = hh.astype(BF16)
    h1 = (hh - h0.astype(F32)).astype(BF16)
    logits = (jnp.dot(h0, wr0_ref[...], preferred_element_type=F32)
              + jnp.dot(h1, wr0_ref[...], preferred_element_type=F32)
              + jnp.dot(h0, wr1_ref[...], preferred_element_type=F32)) + br_ref[...]

    lane = lax.broadcasted_iota(I32, (tm, N_EXPERTS), 1)
    work = logits
    vals, idxs = [], []
    onehot = jnp.zeros((tm, N_EXPERTS), F32)
    for _ in range(TOP_K):
        mx = jnp.max(work, axis=-1, keepdims=True)
        idx = jnp.min(jnp.where(work == mx, lane, N_EXPERTS), axis=-1, keepdims=True)
        hit = lane == idx
        vals.append(mx)
        idxs.append(idx)
        onehot = jnp.where(hit, 1.0, onehot)
        work = jnp.where(hit, -jnp.inf, work)
    ex = [jnp.exp(v - vals[0]) for v in vals]
    den = ex[0] + ex[1] + ex[2] + ex[3]

    r_tt = lax.broadcasted_iota(I32, (tm, tm), 0)
    c_tt = lax.broadcasted_iota(I32, (tm, tm), 1)
    before = jnp.where(r_tt > c_tt, 1.0, 0.0).astype(BF16)
    prefix = jnp.dot(before, onehot.astype(BF16), preferred_element_type=F32)
    for j in range(TOP_K):
        tope_ref[:, j:j + 1] = idxs[j]
        gate_ref[:, j:j + 1] = ex[j] / den
        lrank_ref[:, j:j + 1] = jnp.sum(jnp.where(lane == idxs[j], prefix, 0.0), axis=-1,
                                        keepdims=True).astype(I32)
    cnt_ref[0] = jnp.sum(onehot, axis=0, keepdims=True).astype(I32)


def _merge(xp, xs, oap, oas, obp, obs, p3, wa, wb, wo, g1, b1, wr0, wr1, br):
    tm = TOKEN_TILE
    assert xp.shape[0] % tm == 0 and xs.shape[0] % tm == 0
    ntp = xp.shape[0] // tm
    nt = ntp + xs.shape[0] // tm
    t = nt * tm

    def prow(width):
        return pl.BlockSpec((tm, width), lambda i: (jnp.minimum(i, ntp - 1), 0))

    def srow(width):
        return pl.BlockSpec((tm, width), lambda i: (jnp.maximum(i - ntp, 0), 0))

    def full(a):
        return pl.BlockSpec(a.shape, lambda i: (0,) * a.ndim)

    def rowblk(width, idx=0):
        return pl.BlockSpec((tm, width), lambda i: (i, idx))

    return pl.pallas_call(
        functools.partial(_merge_kernel, prompt_tiles=ntp),
        grid=(nt,),
        in_specs=[prow(D_MODEL), srow(D_MODEL), prow(HG_WIDTH), srow(HG_WIDTH),
                  prow(ATT_KV_WIDTH), srow(ATT_KV_WIDTH), rowblk(D_MODEL, 0),
                  rowblk(D_MODEL, 1), full(wa), full(wb), full(wo), full(g1), full(b1),
                  full(wr0), full(wr1), full(br)],
        out_specs=[rowblk(D_MODEL), rowblk(TOP_K), rowblk(TOP_K), rowblk(TOP_K),
                   pl.BlockSpec((1, 1, N_EXPERTS), lambda i: (i, 0, 0))],
        out_shape=[jax.ShapeDtypeStruct((t, D_MODEL), F32),
                   jax.ShapeDtypeStruct((t, TOP_K), I32),
                   jax.ShapeDtypeStruct((t, TOP_K), F32),
                   jax.ShapeDtypeStruct((t, TOP_K), I32),
                   jax.ShapeDtypeStruct((nt, 1, N_EXPERTS), I32)],
        compiler_params=_cparams(("parallel",)),
        name="merge_ln1_router",
    )(xp, xs, oap, oas, obp, obs, p3, p3, wa, wb, wo, g1, b1, wr0, wr1, br)


def _gather_kernel(tok_ref, h_ref, o_ref, buf_ref):
    m = o_ref.shape[0]

    def body(q, carry):
        for u in range(SUBLANES):
            r = q * SUBLANES + u
            buf_ref[pl.ds(r, 1), :] = h_ref[pl.ds(tok_ref[0, 0, r], 1), :]
        return carry

    lax.fori_loop(0, m // SUBLANES, body, 0)
    o_ref[...] = buf_ref[...].astype(o_ref.dtype)


def _moe_gather(h, tok_pad):
    t, d = h.shape
    p = tok_pad.shape[0]
    m = MOE_BLOCK
    half = d // 2
    return pl.pallas_call(
        _gather_kernel,
        grid=(2, p // m),
        in_specs=[pl.BlockSpec((1, 1, m), lambda c, i: (i, 0, 0), memory_space=pltpu.SMEM),
                  pl.BlockSpec((t, half), lambda c, i: (0, c), pipeline_mode=pl.Buffered(1))],
        out_specs=pl.BlockSpec((m, half), lambda c, i: (i, c)),
        out_shape=jax.ShapeDtypeStruct((p, d), BF16),
        scratch_shapes=[pltpu.VMEM((m, half), F32)],
        compiler_params=_cparams(("arbitrary", "arbitrary")),
        name="moe_gather",
    )(tok_pad.reshape(p // m, 1, m), h)


def _moe_up_kernel(blk0_ref, nblk_ref, w0_ref, w1_ref, w2_ref, w3_ref, b_ref, xs_ref, act_ref,
                   wbf_scr, x_scr, o_scr, sem_in, sem_out):
    e = pl.program_id(0)
    f = pl.program_id(1)
    n = nblk_ref[e]
    b0 = blk0_ref[e]
    m = x_scr.shape[1]
    w_refs = (w0_ref, w1_ref, w2_ref, w3_ref)
    kq, ft = w0_ref.shape[1:]
    half = ft // 2

    def rows(b):
        return pl.ds(pl.multiple_of((b0 + b) * m, m), m)

    def x_copy(b, s):
        return pltpu.make_async_copy(xs_ref.at[rows(b)], x_scr.at[s], sem_in.at[s])

    def o_copy(b, s):
        return pltpu.make_async_copy(o_scr.at[s], act_ref.at[f, rows(b)], sem_out.at[s])

    @pl.when(n > 0)
    def _():
        x_copy(0, 0).start()
        for q, w_ref in enumerate(w_refs):
            wbf_scr[q * kq:(q + 1) * kq, :] = w_ref[0].astype(BF16)
        even = (lax.broadcasted_iota(I32, (m, LANES), 1) & 1) == 0

        def body(b, carry):
            s = b & 1
            x_copy(b, s).wait()

            @pl.when(b + 1 < n)
            def _():
                x_copy(b + 1, 1 - s).start()

            @pl.when(b >= 2)
            def _():
                o_copy(b - 2, s).wait()

            x = x_scr[s]
            for g in range(ft // (2 * LANES)):
                cols = slice(2 * g * LANES, (2 * g + 2) * LANES)
                h = jnp.dot(x, wbf_scr[:, cols], preferred_element_type=F32) + b_ref[0, :, cols]
                ha, hb = h[:, :LANES], h[:, LANES:]
                glu = jnp.minimum(jnp.where(even, ha, pltpu.roll(hb, 1, 1)), SWIGLU_LIMIT)
                lin = jnp.where(even, pltpu.roll(ha, LANES - 1, 1), hb)
                lin = jnp.clip(lin, -SWIGLU_LIMIT, SWIGLU_LIMIT) + 1.0
                act = glu * _sigmoid(SWIGLU_ALPHA * glu) * lin
                o_scr[s, :, g * LANES:(g + 1) * LANES] = act.astype(o_scr.dtype)
            o_copy(b, s).start()
            return carry

        lax.fori_loop(0, n, body, 0)

        @pl.when(n >= 2)
        def _():
            o_copy(n - 2, n & 1).wait()
        o_copy(n - 1, (n - 1) & 1).wait()

    @pl.when(e == pl.num_programs(0) - 1)
    def _():
        o_scr[0] = jnp.zeros(o_scr.shape[1:], o_scr.dtype)

        nz = act_ref.shape[1] // m - b0
        lax.fori_loop(n, nz, lambda b, c: (o_copy(b, 0).start(), c)[1], 0)
        lax.fori_loop(n, nz, lambda b, c: (o_copy(b, 0).wait(), c)[1], 0)


def _moe_up(xs, w_gate_up, b_gate_up, blk0, nblk):
    p, d = xs.shape
    m = MOE_BLOCK
    ft = MOE_FT
    ne = w_gate_up.shape[0]
    nf = w_gate_up.shape[2] // ft

    def wslice(q, e, f, b0, nb):
        return (e, q, f)

    grid_spec = pltpu.PrefetchScalarGridSpec(
        num_scalar_prefetch=2,
        grid=(ne, nf),
        in_specs=[pl.BlockSpec((1, d // W_STREAMS, ft), functools.partial(wslice, q))
                  for q in range(W_STREAMS)] + [
                  pl.BlockSpec((1, 1, ft), lambda e, f, b0, nb: (e, 0, f)),
                  pl.BlockSpec(memory_space=pl.ANY)],
        out_specs=pl.BlockSpec(memory_space=pl.ANY),
        scratch_shapes=[pltpu.VMEM((d, ft), BF16),
                        pltpu.VMEM((2, m, d), BF16),
                        pltpu.VMEM((2, m, ft // 2), BF16),
                        pltpu.SemaphoreType.DMA((2,)),
                        pltpu.SemaphoreType.DMA((2,))],
    )
    return pl.pallas_call(
        _moe_up_kernel,
        grid_spec=grid_spec,
        out_shape=jax.ShapeDtypeStruct((nf, p, ft // 2), BF16),
        compiler_params=_cparams(("arbitrary", "arbitrary")),
        name="moe_up",
    )(blk0, nblk, *([w_gate_up] * W_STREAMS), b_gate_up, xs)


def _hidden_perm():
    r = lax.broadcasted_iota(I32, (LANES, LANES), 0)
    c = lax.broadcasted_iota(I32, (LANES, LANES), 1)
    src = (r >> 1) + (r & 1) * (LANES // 2)
    return jnp.where(c == src, 1.0, 0.0).astype(BF16)


def _moe_down_kernel(be_ref, na_ref, nreal_ref, dest_ref, a_ref, wd0_ref, wd1_ref, wd2_ref, wd3_ref,
                     bd_ref, yt_ref, wbf_scr, y_scr, sem):
    i = pl.program_id(0)
    nb = pl.num_programs(0)
    nf, m, fh = a_ref.shape
    slot = i % 2

    def row_copy(s, r, d):
        return pltpu.make_async_copy(y_scr.at[s, pl.ds(r, 1)], yt_ref.at[pl.ds(d, 1)], sem.at[s])

    def drain(s, count):
        @pl.when(count == m)
        def _():
            pltpu.make_async_copy(y_scr.at[s], yt_ref.at[pl.ds(0, m)], sem.at[s]).wait()

        @pl.when(count != m)
        def _():
            def wbody(_, carry):
                row_copy(s, 0, 0).wait()
                return carry
            lax.fori_loop(0, count, wbody, 0)

    @pl.when(i >= 2)
    def _():
        drain(slot, nreal_ref[i - 2])

    changed = jnp.where(i == 0, 1, be_ref[i] - be_ref[jnp.maximum(i - 1, 0)])

    @pl.when(changed != 0)
    def _():
        perm = _hidden_perm()
        for q, wd_ref in enumerate((wd0_ref, wd1_ref, wd2_ref, wd3_ref)):
            kq = wd_ref.shape[1]
            for g in range(kq // LANES):
                rows = slice(g * LANES, (g + 1) * LANES)
                wbf_scr[q * kq + g * LANES:q * kq + (g + 1) * LANES, :] = jnp.dot(
                    perm, wd_ref[0, rows, :].astype(BF16), preferred_element_type=F32).astype(BF16)

    @pl.when(i < na_ref[0])
    def _():
        y = bd_ref[0]
        for j in range(nf):
            y = y + jnp.dot(a_ref[j], wbf_scr[j * fh:(j + 1) * fh, :], preferred_element_type=F32)
        y_scr[slot] = y

        n = nreal_ref[i]
        unroll = 8

        def sbody8(q, carry):
            for u in range(unroll):
                r = q * unroll + u
                row_copy(slot, r, dest_ref[0, 0, r]).start()
            return carry

        def sbody1(r, carry):
            row_copy(slot, r, dest_ref[0, 0, r]).start()
            return carry

        lax.fori_loop(0, n // unroll, sbody8, 0)
        lax.fori_loop((n // unroll) * unroll, n, sbody1, 0)

    @pl.when(i == nb - 1)
    def _():
        @pl.when(i >= 1)
        def _():
            drain(1 - slot, nreal_ref[i - 1])
        drain(slot, nreal_ref[i])


def _moe_down(act, w_down, b_down, blk_expert, n_active, nreal, dest, n_rows):
    nf, p, fh = act.shape
    f = nf * fh
    m = MOE_BLOCK
    d = w_down.shape[2]

    def wslice(q, i, be, na, nr):
        return (be[i], q, 0)

    grid_spec = pltpu.PrefetchScalarGridSpec(
        num_scalar_prefetch=3,
        grid=(p // m,),
        in_specs=[pl.BlockSpec((1, 1, m), lambda i, be, na, nr: (i, 0, 0), memory_space=pltpu.SMEM),
                  pl.BlockSpec((nf, m, fh), lambda i, be, na, nr: (0, i, 0)),
                  *[pl.BlockSpec((1, f // W_STREAMS, d), functools.partial(wslice, q))
                    for q in range(W_STREAMS)],
                  pl.BlockSpec((1, 1, d), lambda i, be, na, nr: (be[i], 0, 0))],
        out_specs=pl.BlockSpec(memory_space=pl.ANY),
        scratch_shapes=[pltpu.VMEM((f, d), BF16),
                        pltpu.VMEM((2, m, d), F32),
                        pltpu.SemaphoreType.DMA((2,))],
    )
    return pl.pallas_call(
        _moe_down_kernel,
        grid_spec=grid_spec,
        out_shape=jax.ShapeDtypeStruct((n_rows, d), F32),
        compiler_params=_cparams(("arbitrary",)),
        name="moe_down",
    )(blk_expert, n_active, nreal, dest.reshape(p // m, 1, m), act, *([w_down] * W_STREAMS), b_down)


def _combine_kernel(h_ref, y0_ref, y1_ref, y2_ref, y3_ref, gate_ref, g2_ref, b2_ref, o_ref):
    ffn = gate_ref[:, 0:1] * y0_ref[...]
    for j, y_ref in enumerate((y1_ref, y2_ref, y3_ref), start=1):
        ffn = ffn + gate_ref[:, j:j + 1] * y_ref[...]
    o_ref[...] = _layer_norm(DEEPNORM_ALPHA * h_ref[...] + ffn, g2_ref[...], b2_ref[...])


def _combine(h, yt, gate, g2, b2):
    t, d = h.shape
    tm = TOKEN_TILE
    nt = t // tm

    def choice(j):
        return pl.BlockSpec((tm, d), lambda i: (j * nt + i, 0))

    return pl.pallas_call(
        _combine_kernel,
        grid=(nt,),
        in_specs=[pl.BlockSpec((tm, d), lambda i: (i, 0)),
                  choice(0), choice(1), choice(2), choice(3),
                  pl.BlockSpec((tm, TOP_K), lambda i: (i, 0)),
                  pl.BlockSpec((1, d), lambda i: (0, 0)),
                  pl.BlockSpec((1, d), lambda i: (0, 0))],
        out_specs=pl.BlockSpec((tm, d), lambda i: (i, 0)),
        out_shape=jax.ShapeDtypeStruct((t, d), F32),
        compiler_params=_cparams(("parallel",)),
        name="combine_ln2",
    )(h, yt, yt, yt, yt, gate, g2, b2)


def _route(top_e, lrank, tilecnt):
    t = top_e.shape[0]
    m = MOE_BLOCK
    n_assign = t * TOP_K
    n_blocks = n_assign // m + N_EXPERTS
    p = n_blocks * m
    tc = tilecnt.reshape(-1, N_EXPERTS)
    base = jnp.cumsum(tc, axis=0) - tc
    counts = jnp.sum(tc, axis=0)
    padded = (counts + m - 1) // m * m
    pend = jnp.cumsum(padded)
    pstart = pend - padded
    base_tok = jnp.repeat(base, TOKEN_TILE, axis=0)
    rank = lrank + jnp.take_along_axis(base_tok, top_e, axis=1)
    slot = (pstart[top_e] + rank).reshape(-1)
    a_idx = jnp.arange(n_assign, dtype=I32)
    dest = jnp.full((p,), -1, I32).at[slot].set((a_idx % TOP_K) * t + a_idx // TOP_K,
                                               unique_indices=True)
    tok_pad = jnp.where(dest >= 0, dest % t, 0)
    blk_start = jnp.arange(n_blocks, dtype=I32) * m
    blk_expert = jnp.minimum(jnp.sum((pend[None, :] <= blk_start[:, None]).astype(I32), axis=1),
                             N_EXPERTS - 1)
    n_active = (pend[-1:] // m).astype(I32)
    real_end = (pstart + counts)[blk_expert]
    nreal = jnp.where(blk_start < pend[-1], jnp.clip(real_end - blk_start, 0, m), 0).astype(I32)
    return tok_pad, dest, blk_expert, n_active, nreal, (pstart // m).astype(I32), (padded // m).astype(I32)


def kernel(x_prompt, x_sample, cache_attn_k, cache_attn_v, state_hgrn, w_in, hgrn_lb_logits,
           hgrn_norm_w, w_branch_a, w_branch_b, w_out, ln1_g, ln1_b, w_router, b_router,
           w_gate_up, b_gate_up, w_down, b_down, ln2_g, ln2_b):
    assert w_in.shape[0] == DEPTH == 1
    bp, lp, d = x_prompt.shape
    bs, ls, _ = x_sample.shape
    tp, ts = bp * lp, bs * ls
    t = tp + ts
    keep = min(max(w for w, _ in DILATION_PAIRS), lp)

    xp2, xs2 = x_prompt.reshape(tp, d), x_sample.reshape(ts, d)
    xb = jnp.concatenate([xp2.astype(BF16), xs2.astype(BF16)], axis=0)
    wi = w_in[0]
    c1, c2 = P1_WIDTH, P1_WIDTH + P2_WIDTH
    p1 = _matmul(xb, wi[:, :c1].astype(BF16), 768, 1024, F32, "inproj_hgrn")
    p2 = _matmul(xb, wi[:, c1:c2].astype(BF16), 768, 512, F32, "inproj_attn")
    p3 = _matmul(xb, wi[:, c2:].astype(BF16), 768, 1024, F32, "inproj_gate")

    oa_p, st_p = _hgrn(p1, hgrn_lb_logits, hgrn_norm_w[0], None, bp, lp, 0)
    oa_s, st_s = _hgrn(p1, hgrn_lb_logits, hgrn_norm_w[0],
                       state_hgrn.reshape(bs, HG_HEADS, HG_DIM, HG_DIM), bs, ls, tp)

    n = jnp.arange(1, ATT_Q_HEADS + 1, dtype=F32)
    slopes = jnp.power(2.0, -8.0 * n / ATT_Q_HEADS).reshape(N_GROUPS, ATT_KV_HEADS)
    ctx = cache_attn_k.shape[2]
    ob_p = _attn_prompt(p2, slopes, bp, lp)
    ob_s = _attn_sample(p2, cache_attn_k.reshape(bs, ctx * ATT_KV_HEADS, ATT_HEAD_DIM),
                        cache_attn_v.reshape(bs, ctx * ATT_KV_HEADS, ATT_HEAD_DIM), slopes, bs, ls, tp)

    wr = w_router[0]
    wr0 = wr.astype(BF16)
    wr1 = (wr - wr0.astype(F32)).astype(BF16)
    h, top_e, gate, lrank, tilecnt = _merge(
        xp2, xs2, oa_p.reshape(tp, HG_WIDTH), oa_s.reshape(ts, HG_WIDTH), ob_p,
        ob_s.reshape(ts, ATT_KV_WIDTH), p3, w_branch_a[0].astype(BF16), w_branch_b[0].astype(BF16),
        w_out[0].astype(BF16), ln1_g, ln1_b, wr0, wr1, b_router)

    tok_pad, dest, blk_expert, n_active, nreal, blk0, nblk = _route(top_e, lrank, tilecnt)
    xs = _moe_gather(h, tok_pad)
    act = _moe_up(xs, w_gate_up.reshape(N_EXPERTS, d, 2 * D_FF),
                  b_gate_up.reshape(N_EXPERTS, 1, 2 * D_FF), blk0, nblk)
    yt = _moe_down(act, w_down.reshape(N_EXPERTS, D_FF, d), b_down.reshape(N_EXPERTS, 1, d),
                   blk_expert, n_active, nreal, dest, t * TOP_K)
    y = _combine(h, yt, gate, ln2_g, ln2_b)

    k_new = p2[:, ATT_Q_HEADS * ATT_HEAD_DIM:(ATT_Q_HEADS + ATT_KV_HEADS) * ATT_HEAD_DIM]
    v_new = p2[:, (ATT_Q_HEADS + ATT_KV_HEADS) * ATT_HEAD_DIM:]

    def kv_prompt(a):
        return a[:tp].reshape(bp, lp, ATT_KV_HEADS, ATT_HEAD_DIM)[:, lp - keep:][None]

    def kv_sample(a):
        return a[tp:].reshape(bs, ls, ATT_KV_HEADS, ATT_HEAD_DIM)[None]

    return (y[:tp].reshape(bp, lp, d), y[tp:].reshape(bs, ls, d),
            kv_prompt(k_new), kv_prompt(v_new), st_p[None],
            kv_sample(k_new), kv_sample(v_new), st_s[None])
```

```python
import functools
import math

import jax
import jax.numpy as jnp
from jax import lax
from jax.experimental import pallas as pl
from jax.experimental.pallas import tpu as pltpu

F32 = jnp.float32
BF16 = jnp.bfloat16
I32 = jnp.int32

D_MODEL = 2048
DEPTH = 1
HG_HEADS = 8
HG_DIM = 128
HG_WIDTH = HG_HEADS * HG_DIM
HG_CHUNK = 64
ATT_HEAD_DIM = 128
DILATION_PAIRS = ((128, 1), (512, 4), (2048, 16))
N_GROUPS = len(DILATION_PAIRS)
ATT_KV_HEADS = 4
ATT_Q_HEADS = N_GROUPS * ATT_KV_HEADS
ATT_TAPS = 128
ATT_KV_WIDTH = ATT_KV_HEADS * ATT_HEAD_DIM
ATT_SCALE = ATT_HEAD_DIM ** -0.5
N_EXPERTS = 32
TOP_K = 4
D_FF = D_MODEL
SWIGLU_ALPHA = 1.702
SWIGLU_LIMIT = 7.0
LN_EPS = 1e-5
RMS_EPS = 1e-5
DEEPNORM_ALPHA = (2 * DEPTH) ** 0.25

P1_WIDTH = 4 * HG_WIDTH
P2_WIDTH = (ATT_Q_HEADS + 2 * ATT_KV_HEADS) * ATT_HEAD_DIM
P3_WIDTH = 2 * D_MODEL

LANES = 128
SUBLANES = 8
VMEM_LIMIT = 60 * 1024 * 1024

TOKEN_TILE = 256
MOE_BLOCK = 256
MOE_FT = 2048
W_STREAMS = 4
NEG = -1e30

_NT = (((1,), (1,)), ((), ()))
_TN = (((0,), (0,)), ((), ()))


def _cparams(sem):
    return pltpu.CompilerParams(dimension_semantics=sem, vmem_limit_bytes=VMEM_LIMIT)


def _sigmoid(x):
    return 1.0 / (1.0 + jnp.exp(-x))


def _mm_kernel(x_ref, w_ref, o_ref):
    o_ref[...] = jnp.dot(x_ref[...], w_ref[...], preferred_element_type=F32).astype(o_ref.dtype)


def _matmul(x, w, tm, tn, out_dtype, name):
    m, k = x.shape
    n = w.shape[1]
    assert m % tm == 0 and n % tn == 0
    return pl.pallas_call(
        _mm_kernel,
        grid=(n // tn, m // tm),
        in_specs=[pl.BlockSpec((tm, k), lambda j, i: (i, 0)),
                  pl.BlockSpec((k, tn), lambda j, i: (0, j))],
        out_specs=pl.BlockSpec((tm, tn), lambda j, i: (i, j)),
        out_shape=jax.ShapeDtypeStruct((m, n), out_dtype),
        compiler_params=_cparams(("parallel", "parallel")),
        name=name,
    )(x, w)


def _level_ref(g_scr, m, sl, chunk, sub):
    pieces = []
    for j in range(chunk // SUBLANES):
        if 2 * m >= SUBLANES:
            row = ((SUBLANES * j) // (2 * m)) * (2 * m) + m - 1
            pieces.append(jnp.broadcast_to(g_scr[row:row + 1, sl], (SUBLANES, LANES)))
        else:
            acc = None
            for u in range(SUBLANES // (2 * m)):
                row = SUBLANES * j + 2 * m * u + m - 1
                b = jnp.broadcast_to(g_scr[row:row + 1, sl], (SUBLANES, LANES))
                acc = b if acc is None else jnp.where(sub // (2 * m) == u, b, acc)
            pieces.append(acc)
    return pieces[0] if len(pieces) == 1 else jnp.concatenate(pieces, axis=0)


def _hgrn_kernel(lbl_ref, nw_ref, hq_ref, hf_ref, hi_ref, hg_ref, *rest, chunk, has_s0):
    if has_s0:
        s0_ref, o_ref, sout_ref, st_scr, g_scr = rest
    else:
        o_ref, sout_ref, st_scr, g_scr = rest
    c = pl.program_id(1)
    nc = pl.num_programs(1)

    @pl.when(c == 0)
    def _init():
        for h in range(HG_HEADS):
            if has_s0:
                st_scr[h] = s0_ref[0, h].T
            else:
                st_scr[h] = jnp.zeros((HG_DIM, HG_DIM), F32)

    lbl = lbl_ref[...]
    ex = jnp.exp(lbl - jnp.max(lbl, axis=0, keepdims=True))
    lb = ex[0:1] / jnp.sum(ex, axis=0, keepdims=True)

    r_cc = lax.broadcasted_iota(I32, (chunk, chunk), 0)
    c_cc = lax.broadcasted_iota(I32, (chunk, chunk), 1)
    tri = jnp.where(r_cc >= c_cc, 1.0, 0.0).astype(BF16)
    sub = lax.broadcasted_iota(I32, (SUBLANES, LANES), 0)
    row_c = lax.broadcasted_iota(I32, (chunk, LANES), 0)
    nw = nw_ref[...]

    for h in range(HG_HEADS):
        sl = slice(h * HG_DIM, (h + 1) * HG_DIM)
        f = lb[:, sl] + (1.0 - lb[:, sl]) * _sigmoid(hf_ref[:, sl])
        logf = jnp.log(f)
        k = 1.0 - f
        p0 = logf.astype(BF16)
        r1 = logf - p0.astype(F32)
        p1 = r1.astype(BF16)
        p2 = (r1 - p1.astype(F32)).astype(BF16)
        g = (jnp.dot(tri, p0, preferred_element_type=F32)
             + jnp.dot(tri, p1, preferred_element_type=F32)
             + jnp.dot(tri, p2, preferred_element_type=F32))
        g_scr[:, sl] = g
        hq = hq_ref[:, sl]
        q = hq * _sigmoid(hq)
        v = hi_ref[:, sl]
        vb = v.astype(BF16)

        p = lax.dot_general(q.astype(BF16), k.astype(BF16), _NT, preferred_element_type=F32)
        a = jnp.where(r_cc == c_cc, p, 0.0)
        m = 1
        while m < chunk:
            ref = _level_ref(g_scr, m, sl, chunk, sub)
            e = jnp.exp(-jnp.abs(g - ref))
            odd = ((row_c // m) & 1) == 1
            x = (jnp.where(odd, q, k) * e).astype(BF16)
            p = lax.dot_general(x, x, _NT, preferred_element_type=F32)
            pair = (r_cc // (2 * m)) == (c_cc // (2 * m))
            sel = jnp.where(pair, ((r_cc // m) & 1) - ((c_cc // m) & 1), 0) == 1
            a = jnp.where(sel, p, a)
            m *= 2

        st = st_scr[h]
        o = jnp.dot(a.astype(BF16), vb, preferred_element_type=F32)
        qe = (q * jnp.exp(g)).astype(BF16)
        o = o + lax.dot_general(qe, st.astype(BF16), _NT, preferred_element_type=F32)

        ms = jnp.mean(o * o, axis=-1, keepdims=True)
        hg = hg_ref[:, sl]
        out = o * lax.rsqrt(ms + RMS_EPS) * nw * (hg * _sigmoid(hg))
        o_ref[0, :, sl] = out.astype(o_ref.dtype)

        g_last = g[chunk - 1:chunk, :]
        kd = (k * jnp.exp(g_last - g)).astype(BF16)
        st_scr[h] = st * jnp.exp(g_last) + lax.dot_general(vb, kd, _TN, preferred_element_type=F32)

    @pl.when(c == nc - 1)
    def _fin():
        for h in range(HG_HEADS):
            sout_ref[0, h] = st_scr[h].T


def _hgrn(p1, lb_logits, norm_w, s0, batch, seq, row0):
    chunk = math.gcd(seq, HG_CHUNK)
    nchunk = seq // chunk
    assert row0 % chunk == 0
    rb0 = row0 // chunk

    def col(idx):
        return pl.BlockSpec((chunk, HG_WIDTH), lambda b, c: (rb0 + b * nchunk + c, idx))

    in_specs = [pl.BlockSpec(lb_logits.shape, lambda b, c: (0, 0)),
                pl.BlockSpec((1, HG_DIM), lambda b, c: (0, 0)),
                col(0), col(1), col(2), col(3)]
    args = [lb_logits, norm_w.reshape(1, HG_DIM), p1, p1, p1, p1]
    if s0 is not None:
        in_specs.append(pl.BlockSpec((1, HG_HEADS, HG_DIM, HG_DIM), lambda b, c: (b, 0, 0, 0)))
        args.append(s0)
    return pl.pallas_call(
        functools.partial(_hgrn_kernel, chunk=chunk, has_s0=s0 is not None),
        grid=(batch, nchunk),
        in_specs=in_specs,
        out_specs=[pl.BlockSpec((1, chunk, HG_WIDTH), lambda b, c: (b, c, 0)),
                   pl.BlockSpec((1, HG_HEADS, HG_DIM, HG_DIM), lambda b, c: (b, 0, 0, 0))],
        out_shape=[jax.ShapeDtypeStruct((batch, seq, HG_WIDTH), BF16),
                   jax.ShapeDtypeStruct((batch, HG_HEADS, HG_DIM, HG_DIM), F32)],
        scratch_shapes=[pltpu.VMEM((HG_HEADS, HG_DIM, HG_DIM), F32),
                        pltpu.VMEM((chunk, HG_WIDTH), F32)],
        compiler_params=_cparams(("parallel", "arbitrary")),
        name="hgrn_seq%d" % seq,
    )(*args)


QB = 128
ATTN_UNROLL = 8


def _attn_prompt_kernel(slopes_ref, q0_ref, q1_ref, q2_ref, k_ref, v_ref, o_ref,
                        og_scr, lse_scr, bias_scr, *, seq):
    h = pl.program_id(1)
    il = lax.broadcasted_iota(I32, (QB, 2 * QB), 0)
    jl = lax.broadcasted_iota(I32, (QB, 2 * QB), 1)
    delta = il + QB - jl
    band = jnp.where(delta >= 0, delta, ATT_TAPS + 1) <= ATT_TAPS
    prev_half = jl < QB
    q_refs = (q0_ref, q1_ref, q2_ref)

    for g, (window, dil) in enumerate(DILATION_PAIRS):
        assert window // dil == ATT_TAPS
        slope = slopes_ref[g, h]
        bias_scr[g] = jnp.where(band, (-slope * dil) * delta.astype(F32), NEG)
        nbr = (seq // dil) // QB
        q_ref = q_refs[g]

        def rows(start, dil=dil):
            return pl.ds(start, QB) if dil == 1 else pl.ds(start, QB, stride=dil)

        def body(ib, carry, g=g, dil=dil, nbr=nbr, q_ref=q_ref, rows=rows):
            res = ib // nbr
            jb = ib - res * nbr
            qs = res + dil * (jb * QB)
            ps = res + dil * jnp.maximum(jb * QB - QB, 0)
            qv = q_ref[rows(qs), :].astype(BF16)
            kb = jnp.concatenate([k_ref[rows(ps), :], k_ref[rows(qs), :]], axis=0).astype(BF16)
            vb = jnp.concatenate([v_ref[rows(ps), :], v_ref[rows(qs), :]], axis=0).astype(BF16)
            s = lax.dot_general(qv, kb, _NT, preferred_element_type=F32) * ATT_SCALE + bias_scr[g]
            s = jnp.where(jnp.where(prev_half, jb, 1) == 0, NEG, s)
            mx = jnp.max(s, axis=-1, keepdims=True)
            p = jnp.exp(s - mx)
            l = jnp.sum(p, axis=-1, keepdims=True)
            o = jnp.dot(p.astype(BF16), vb, preferred_element_type=F32) / l
            og_scr[g, rows(qs), :] = o
            lse_scr[g, rows(qs), :] = jnp.broadcast_to(mx + jnp.log(l), (QB, LANES))
            return carry

        lax.fori_loop(0, seq // QB, body, 0, unroll=ATTN_UNROLL)

    def merge(tb, carry):
        r = pl.ds(pl.multiple_of(tb * QB, QB), QB)
        l0, l1, l2 = lse_scr[0, r, :], lse_scr[1, r, :], lse_scr[2, r, :]
        mx = jnp.maximum(jnp.maximum(l0, l1), l2)
        w0, w1, w2 = jnp.exp(l0 - mx), jnp.exp(l1 - mx), jnp.exp(l2 - mx)
        o = (w0 * og_scr[0, r, :] + w1 * og_scr[1, r, :] + w2 * og_scr[2, r, :]) / (w0 + w1 + w2)
        o_ref[r, :] = o.astype(o_ref.dtype)
        return carry

    lax.fori_loop(0, seq // QB, merge, 0, unroll=ATTN_UNROLL)


def _attn_prompt(p2, slopes, batch, seq):
    assert seq % (QB * max(d for _, d in DILATION_PAIRS)) == 0

    def col(fn):
        return pl.BlockSpec((seq, ATT_HEAD_DIM), lambda b, h: (b, fn(h)))

    return pl.pallas_call(
        functools.partial(_attn_prompt_kernel, seq=seq),
        grid=(batch, ATT_KV_HEADS),
        in_specs=[pl.BlockSpec(memory_space=pltpu.SMEM),
                  col(lambda h: h), col(lambda h: ATT_KV_HEADS + h), col(lambda h: 2 * ATT_KV_HEADS + h),
                  col(lambda h: ATT_Q_HEADS + h), col(lambda h: ATT_Q_HEADS + ATT_KV_HEADS + h)],
        out_specs=pl.BlockSpec((seq, ATT_HEAD_DIM), lambda b, h: (b, h)),
        out_shape=jax.ShapeDtypeStruct((batch * seq, ATT_KV_WIDTH), BF16),
        scratch_shapes=[pltpu.VMEM((N_GROUPS, seq, ATT_HEAD_DIM), F32),
                        pltpu.VMEM((N_GROUPS, seq, LANES), F32),
                        pltpu.VMEM((N_GROUPS, QB, 2 * QB), F32)],
        compiler_params=_cparams(("parallel", "parallel")),
        name="attn_prompt",
    )(slopes, p2, p2, p2, p2, p2)


def _attn_sample_kernel(slopes_ref, qa_ref, qb_ref, qc_ref, kn_ref, vn_ref, kc_ref, vc_ref, o_ref,
                        *, ctx, nq):
    rows = N_GROUPS * nq
    gi = lax.broadcasted_iota(I32, (rows, 1), 0) // nq
    dil = jnp.where(gi == 0, DILATION_PAIRS[0][1], jnp.where(gi == 1, DILATION_PAIRS[1][1], DILATION_PAIRS[2][1]))
    win = jnp.where(gi == 0, DILATION_PAIRS[0][0], jnp.where(gi == 1, DILATION_PAIRS[1][0], DILATION_PAIRS[2][0]))

    def dist_valid(shape, key0, nkeys):
        r = lax.broadcasted_iota(I32, shape, 0)
        t = r - (r // nq) * nq
        col = lax.broadcasted_iota(I32, shape, 1)
        dist = ctx + t - (col + key0)
        bad = jnp.where(dist >= 0, dist & (dil - 1), 1)
        bad = jnp.where(dist <= win, bad, 1)
        bad = jnp.where(col < nkeys, bad, 1)
        return dist.astype(F32), bad == 0

    dist_c, ok_c = dist_valid((rows, ctx), 0, ctx)
    dist_n, ok_n = dist_valid((rows, LANES), ctx, nq)
    zpad = jnp.zeros((LANES - nq, ATT_HEAD_DIM), F32)

    for h in range(ATT_KV_HEADS):
        hs = slice(h * ATT_HEAD_DIM, (h + 1) * ATT_HEAD_DIM)
        slope = jnp.where(gi == 0, slopes_ref[0, h], jnp.where(gi == 1, slopes_ref[1, h], slopes_ref[2, h]))
        qh = jnp.concatenate([qa_ref[:, hs], qb_ref[:, hs], qc_ref[:, hs]], axis=0).astype(BF16)
        kc = kc_ref[0, pl.ds(h, ctx, stride=ATT_KV_HEADS), :].astype(BF16)
        vc = vc_ref[0, pl.ds(h, ctx, stride=ATT_KV_HEADS), :].astype(BF16)
        kn = jnp.concatenate([kn_ref[:, hs], zpad], axis=0).astype(BF16)
        vn = jnp.concatenate([vn_ref[:, hs], zpad], axis=0).astype(BF16)
        sc = lax.dot_general(qh, kc, _NT, preferred_element_type=F32) * ATT_SCALE - slope * dist_c
        sn = lax.dot_general(qh, kn, _NT, preferred_element_type=F32) * ATT_SCALE - slope * dist_n
        sc = jnp.where(ok_c, sc, NEG)
        sn = jnp.where(ok_n, sn, NEG)
        mx = jnp.maximum(jnp.max(sc, axis=-1, keepdims=True), jnp.max(sn, axis=-1, keepdims=True))
        pc = jnp.exp(sc - mx)
        pn = jnp.exp(sn - mx)
        l = jnp.sum(pc, axis=-1, keepdims=True) + jnp.sum(pn, axis=-1, keepdims=True)
        o = (jnp.dot(pc.astype(BF16), vc, preferred_element_type=F32)
             + jnp.dot(pn.astype(BF16), vn, preferred_element_type=F32)) / l
        lse = mx + jnp.log(l)
        l0, l1, l2 = lse[0:nq], lse[nq:2 * nq], lse[2 * nq:3 * nq]
        m3 = jnp.maximum(jnp.maximum(l0, l1), l2)
        w0, w1, w2 = jnp.exp(l0 - m3), jnp.exp(l1 - m3), jnp.exp(l2 - m3)
        out = (w0 * o[0:nq] + w1 * o[nq:2 * nq] + w2 * o[2 * nq:3 * nq]) / (w0 + w1 + w2)
        o_ref[0, :, hs] = out.astype(o_ref.dtype)


def _attn_sample(p2, cache_k, cache_v, slopes, batch, nq, row0):
    ctx = cache_k.shape[1] // ATT_KV_HEADS
    assert row0 % nq == 0 and nq == SUBLANES and ctx >= max(w for w, _ in DILATION_PAIRS)
    rb0 = row0 // nq

    def col(idx):
        return pl.BlockSpec((nq, ATT_KV_WIDTH), lambda b: (rb0 + b, idx))

    cache_spec = pl.BlockSpec((1, ctx * ATT_KV_HEADS, ATT_HEAD_DIM), lambda b: (b, 0, 0))
    return pl.pallas_call(
        functools.partial(_attn_sample_kernel, ctx=ctx, nq=nq),
        grid=(batch,),
        in_specs=[pl.BlockSpec(memory_space=pltpu.SMEM),
                  col(0), col(1), col(2), col(3), col(4), cache_spec, cache_spec],
        out_specs=pl.BlockSpec((1, nq, ATT_KV_WIDTH), lambda b: (b, 0, 0)),
        out_shape=jax.ShapeDtypeStruct((batch, nq, ATT_KV_WIDTH), BF16),
        compiler_params=_cparams(("parallel",)),
        name="attn_sample",
    )(slopes, p2, p2, p2, p2, p2, cache_k, cache_v)


def _layer_norm(y, g, b):
    mu = jnp.mean(y, axis=-1, keepdims=True)
    yc = y - mu
    var = jnp.mean(yc * yc, axis=-1, keepdims=True)
    return yc * lax.rsqrt(var + LN_EPS) * g + b


def _merge_kernel(xp_ref, xs_ref, oap_ref, oas_ref, obp_ref, obs_ref, ga_ref, gb_ref, wa_ref, wb_ref,
                  wo_ref, g1_ref, b1_ref, wr0_ref, wr1_ref, br_ref,
                  h_ref, tope_ref, gate_ref, lrank_ref, cnt_ref, *, prompt_tiles):
    tm = xp_ref.shape[0]
    is_prompt = pl.program_id(0) < prompt_tiles
    x = jnp.where(is_prompt, xp_ref[...], xs_ref[...])
    oa = jnp.where(is_prompt, oap_ref[...], oas_ref[...])
    ob = jnp.where(is_prompt, obp_ref[...], obs_ref[...])
    a = jnp.dot(oa, wa_ref[...], preferred_element_type=F32)
    b = jnp.dot(ob, wb_ref[...], preferred_element_type=F32)
    merged = _sigmoid(ga_ref[...]) * a + _sigmoid(gb_ref[...]) * b
    z = jnp.dot(merged.astype(BF16), wo_ref[...], preferred_element_type=F32)
    hh = _layer_norm(DEEPNORM_ALPHA * x + z, g1_ref[...], b1_ref[...])
    h_ref[...] = hh

    h0 = hh.astype(BF16)
    h1 = (hh - h0.astype(F32)).astype(BF16)
    logits = (jnp.dot(h0, wr0_ref[...], preferred_element_type=F32)
              + jnp.dot(h1, wr0_ref[...], preferred_element_type=F32)
              + jnp.dot(h0, wr1_ref[...], preferred_element_type=F32)) + br_ref[...]

    lane = lax.broadcasted_iota(I32, (tm, N_EXPERTS), 1)
    work = logits
    vals, idxs = [], []
    onehot = jnp.zeros((tm, N_EXPERTS), F32)
    for _ in range(TOP_K):
        mx = jnp.max(work, axis=-1, keepdims=True)
        idx = jnp.min(jnp.where(work == mx, lane, N_EXPERTS), axis=-1, keepdims=True)
        hit = lane == idx
        vals.append(mx)
        idxs.append(idx)
        onehot = jnp.where(hit, 1.0, onehot)
        work = jnp.where(hit, -jnp.inf, work)
    ex = [jnp.exp(v - vals[0]) for v in vals]
    den = ex[0] + ex[1] + ex[2] + ex[3]

    r_tt = lax.broadcasted_iota(I32, (tm, tm), 0)
    c_tt = lax.broadcasted_iota(I32, (tm, tm), 1)
    before = jnp.where(r_tt > c_tt, 1.0, 0.0).astype(BF16)
    prefix = jnp.dot(before, onehot.astype(BF16), preferred_element_type=F32)
    for j in range(TOP_K):
        tope_ref[:, j:j + 1] = idxs[j]
        gate_ref[:, j:j + 1] = ex[j] / den
        lrank_ref[:, j:j + 1] = jnp.sum(jnp.where(lane == idxs[j], prefix, 0.0), axis=-1,
                                        keepdims=True).astype(I32)
    cnt_ref[0] = jnp.sum(onehot, axis=0, keepdims=True).astype(I32)


def _merge(xp, xs, oap, oas, obp, obs, p3, wa, wb, wo, g1, b1, wr0, wr1, br):
    tm = TOKEN_TILE
    assert xp.shape[0] % tm == 0 and xs.shape[0] % tm == 0
    ntp = xp.shape[0] // tm
    nt = ntp + xs.shape[0] // tm
    t = nt * tm

    def prow(width):
        return pl.BlockSpec((tm, width), lambda i: (jnp.minimum(i, ntp - 1), 0))

    def srow(width):
        return pl.BlockSpec((tm, width), lambda i: (jnp.maximum(i - ntp, 0), 0))

    def full(a):
        return pl.BlockSpec(a.shape, lambda i: (0,) * a.ndim)

    def rowblk(width, idx=0):
        return pl.BlockSpec((tm, width), lambda i: (i, idx))

    return pl.pallas_call(
        functools.partial(_merge_kernel, prompt_tiles=ntp),
        grid=(nt,),
        in_specs=[prow(D_MODEL), srow(D_MODEL), prow(HG_WIDTH), srow(HG_WIDTH),
                  prow(ATT_KV_WIDTH), srow(ATT_KV_WIDTH), rowblk(D_MODEL, 0),
                  rowblk(D_MODEL, 1), full(wa), full(wb), full(wo), full(g1), full(b1),
                  full(wr0), full(wr1), full(br)],
        out_specs=[rowblk(D_MODEL), rowblk(TOP_K), rowblk(TOP_K), rowblk(TOP_K),
                   pl.BlockSpec((1, 1, N_EXPERTS), lambda i: (i, 0, 0))],
        out_shape=[jax.ShapeDtypeStruct((t, D_MODEL), F32),
                   jax.ShapeDtypeStruct((t, TOP_K), I32),
                   jax.ShapeDtypeStruct((t, TOP_K), F32),
                   jax.ShapeDtypeStruct((t, TOP_K), I32),
                   jax.ShapeDtypeStruct((nt, 1, N_EXPERTS), I32)],
        compiler_params=_cparams(("parallel",)),
        name="merge_ln1_router",
    )(xp, xs, oap, oas, obp, obs, p3, p3, wa, wb, wo, g1, b1, wr0, wr1, br)


def _gather_kernel(tok_ref, h_ref, o_ref, buf_ref):
    m = o_ref.shape[0]

    def body(q, carry):
        for u in range(SUBLANES):
            r = q * SUBLANES + u
            buf_ref[pl.ds(r, 1), :] = h_ref[pl.ds(tok_ref[0, 0, r], 1), :]
        return carry

    lax.fori_loop(0, m // SUBLANES, body, 0)
    o_ref[...] = buf_ref[...].astype(o_ref.dtype)


def _moe_gather(h, tok_pad):
    t, d = h.shape
    p = tok_pad.shape[0]
    m = MOE_BLOCK
    half = d // 2
    return pl.pallas_call(
        _gather_kernel,
        grid=(2, p // m),
        in_specs=[pl.BlockSpec((1, 1, m), lambda c, i: (i, 0, 0), memory_space=pltpu.SMEM),
                  pl.BlockSpec((t, half), lambda c, i: (0, c), pipeline_mode=pl.Buffered(1))],
        out_specs=pl.BlockSpec((m, half), lambda c, i: (i, c)),
        out_shape=jax.ShapeDtypeStruct((p, d), BF16),
        scratch_shapes=[pltpu.VMEM((m, half), F32)],
        compiler_params=_cparams(("arbitrary", "arbitrary")),
        name="moe_gather",
    )(tok_pad.reshape(p // m, 1, m), h)


def _moe_up_kernel(blk0_ref, nblk_ref, w0_ref, w1_ref, w2_ref, w3_ref, b_ref, xs_ref, act_ref,
                   wbf_scr, x_scr, o_scr, sem_in, sem_out):
    e = pl.program_id(0)
    f = pl.program_id(1)
    n = nblk_ref[e]
    b0 = blk0_ref[e]
    m = x_scr.shape[1]
    w_refs = (w0_ref, w1_ref, w2_ref, w3_ref)
    kq, ft = w0_ref.shape[1:]
    half = ft // 2

    def rows(b):
        return pl.ds(pl.multiple_of((b0 + b) * m, m), m)

    def x_copy(b, s):
        return pltpu.make_async_copy(xs_ref.at[rows(b)], x_scr.at[s], sem_in.at[s])

    def o_copy(b, s):
        return pltpu.make_async_copy(o_scr.at[s], act_ref.at[f, rows(b)], sem_out.at[s])

    @pl.when(n > 0)
    def _():
        x_copy(0, 0).start()
        for q, w_ref in enumerate(w_refs):
            wbf_scr[q * kq:(q + 1) * kq, :] = w_ref[0].astype(BF16)
        even = (lax.broadcasted_iota(I32, (m, LANES), 1) & 1) == 0

        def body(b, carry):
            s = b & 1
            x_copy(b, s).wait()

            @pl.when(b + 1 < n)
            def _():
                x_copy(b + 1, 1 - s).start()

            @pl.when(b >= 2)
            def _():
                o_copy(b - 2, s).wait()

            x = x_scr[s]
            for g in range(ft // (2 * LANES)):
                cols = slice(2 * g * LANES, (2 * g + 2) * LANES)
                h = jnp.dot(x, wbf_scr[:, cols], preferred_element_type=F32) + b_ref[0, :, cols]
                ha, hb = h[:, :LANES], h[:, LANES:]
                glu = jnp.minimum(jnp.where(even, ha, pltpu.roll(hb, 1, 1)), SWIGLU_LIMIT)
                lin = jnp.where(even, pltpu.roll(ha, LANES - 1, 1), hb)
                lin = jnp.clip(lin, -SWIGLU_LIMIT, SWIGLU_LIMIT) + 1.0
                act = glu * _sigmoid(SWIGLU_ALPHA * glu) * lin
                o_scr[s, :, g * LANES:(g + 1) * LANES] = act.astype(o_scr.dtype)
            o_copy(b, s).start()
            return carry

        lax.fori_loop(0, n, body, 0)

        @pl.when(n >= 2)
        def _():
            o_copy(n - 2, n & 1).wait()
        o_copy(n - 1, (n - 1) & 1).wait()

    @pl.when(e == pl.num_programs(0) - 1)
    def _():
        o_scr[0] = jnp.zeros(o_scr.shape[1:], o_scr.dtype)

        nz = act_ref.shape[1] // m - b0
        lax.fori_loop(n, nz, lambda b, c: (o_copy(b, 0).start(), c)[1], 0)
        lax.fori_loop(n, nz, lambda b, c: (o_copy(b, 0).wait(), c)[1], 0)


def _moe_up(xs, w_gate_up, b_gate_up, blk0, nblk):
    p, d = xs.shape
    m = MOE_BLOCK
    ft = MOE_FT
    ne = w_gate_up.shape[0]
    nf = w_gate_up.shape[2] // ft

    def wslice(q, e, f, b0, nb):
        return (e, q, f)

    grid_spec = pltpu.PrefetchScalarGridSpec(
        num_scalar_prefetch=2,
        grid=(ne, nf),
        in_specs=[pl.BlockSpec((1, d // W_STREAMS, ft), functools.partial(wslice, q))
                  for q in range(W_STREAMS)] + [
                  pl.BlockSpec((1, 1, ft), lambda e, f, b0, nb: (e, 0, f)),
                  pl.BlockSpec(memory_space=pl.ANY)],
        out_specs=pl.BlockSpec(memory_space=pl.ANY),
        scratch_shapes=[pltpu.VMEM((d, ft), BF16),
                        pltpu.VMEM((2, m, d), BF16),
                        pltpu.VMEM((2, m, ft // 2), BF16),
                        pltpu.SemaphoreType.DMA((2,)),
                        pltpu.SemaphoreType.DMA((2,))],
    )
    return pl.pallas_call(
        _moe_up_kernel,
        grid_spec=grid_spec,
        out_shape=jax.ShapeDtypeStruct((nf, p, ft // 2), BF16),
        compiler_params=_cparams(("arbitrary", "arbitrary")),
        name="moe_up",
    )(blk0, nblk, *([w_gate_up] * W_STREAMS), b_gate_up, xs)


def _hidden_perm():
    r = lax.broadcasted_iota(I32, (LANES, LANES), 0)
    c = lax.broadcasted_iota(I32, (LANES, LANES), 1)
    src = (r >> 1) + (r & 1) * (LANES // 2)
    return jnp.where(c == src, 1.0, 0.0).astype(BF16)


def _moe_down_kernel(be_ref, na_ref, nreal_ref, dest_ref, a_ref, wd0_ref, wd1_ref, wd2_ref, wd3_ref,
                     bd_ref, yt_ref, wbf_scr, y_scr, sem):
    i = pl.program_id(0)
    nb = pl.num_programs(0)
    nf, m, fh = a_ref.shape
    slot = i % 2

    def row_copy(s, r, d):
        return pltpu.make_async_copy(y_scr.at[s, pl.ds(r, 1)], yt_ref.at[pl.ds(d, 1)], sem.at[s])

    def drain(s, count):
        @pl.when(count == m)
        def _():
            pltpu.make_async_copy(y_scr.at[s], yt_ref.at[pl.ds(0, m)], sem.at[s]).wait()

        @pl.when(count != m)
        def _():
            def wbody(_, carry):
                row_copy(s, 0, 0).wait()
                return carry
            lax.fori_loop(0, count, wbody, 0)

    @pl.when(i >= 2)
    def _():
        drain(slot, nreal_ref[i - 2])

    changed = jnp.where(i == 0, 1, be_ref[i] - be_ref[jnp.maximum(i - 1, 0)])

    @pl.when(changed != 0)
    def _():
        perm = _hidden_perm()
        for q, wd_ref in enumerate((wd0_ref, wd1_ref, wd2_ref, wd3_ref)):
            kq = wd_ref.shape[1]
            for g in range(kq // LANES):
                rows = slice(g * LANES, (g + 1) * LANES)
                wbf_scr[q * kq + g * LANES:q * kq + (g + 1) * LANES, :] = jnp.dot(
                    perm, wd_ref[0, rows, :].astype(BF16), preferred_element_type=F32).astype(BF16)

    @pl.when(i < na_ref[0])
    def _():
        y = bd_ref[0]
        for j in range(nf):
            y = y + jnp.dot(a_ref[j], wbf_scr[j * fh:(j + 1) * fh, :], preferred_element_type=F32)
        y_scr[slot] = y

        n = nreal_ref[i]
        unroll = 8

        def sbody8(q, carry):
            for u in range(unroll):
                r = q * unroll + u
                row_copy(slot, r, dest_ref[0, 0, r]).start()
            return carry

        def sbody1(r, carry):
            row_copy(slot, r, dest_ref[0, 0, r]).start()
            return carry

        lax.fori_loop(0, n // unroll, sbody8, 0)
        lax.fori_loop((n // unroll) * unroll, n, sbody1, 0)

    @pl.when(i == nb - 1)
    def _():
        @pl.when(i >= 1)
        def _():
            drain(1 - slot, nreal_ref[i - 1])
        drain(slot, nreal_ref[i])


def _moe_down(act, w_down, b_down, blk_expert, n_active, nreal, dest, n_rows):
    nf, p, fh = act.shape
    f = nf * fh
    m = MOE_BLOCK
    d = w_down.shape[2]

    def wslice(q, i, be, na, nr):
        return (be[i], q, 0)

    grid_spec = pltpu.PrefetchScalarGridSpec(
        num_scalar_prefetch=3,
        grid=(p // m,),
        in_specs=[pl.BlockSpec((1, 1, m), lambda i, be, na, nr: (i, 0, 0), memory_space=pltpu.SMEM),
                  pl.BlockSpec((nf, m, fh), lambda i, be, na, nr: (0, i, 0)),
                  *[pl.BlockSpec((1, f // W_STREAMS, d), functools.partial(wslice, q))
                    for q in range(W_STREAMS)],
                  pl.BlockSpec((1, 1, d), lambda i, be, na, nr: (be[i], 0, 0))],
        out_specs=pl.BlockSpec(memory_space=pl.ANY),
        scratch_shapes=[pltpu.VMEM((f, d), BF16),
                        pltpu.VMEM((2, m, d), F32),
                        pltpu.SemaphoreType.DMA((2,))],
    )
    return pl.pallas_call(
        _moe_down_kernel,
        grid_spec=grid_spec,
        out_shape=jax.ShapeDtypeStruct((n_rows, d), F32),
        compiler_params=_cparams(("arbitrary",)),
        name="moe_down",
    )(blk_expert, n_active, nreal, dest.reshape(p // m, 1, m), act, *([w_down] * W_STREAMS), b_down)


def _combine_kernel(h_ref, y0_ref, y1_ref, y2_ref, y3_ref, gate_ref, g2_ref, b2_ref, o_ref):
    ffn = gate_ref[:, 0:1] * y0_ref[...]
    for j, y_ref in enumerate((y1_ref, y2_ref, y3_ref), start=1):
        ffn = ffn + gate_ref[:, j:j + 1] * y_ref[...]
    o_ref[...] = _layer_norm(DEEPNORM_ALPHA * h_ref[...] + ffn, g2_ref[...], b2_ref[...])


def _combine(h, yt, gate, g2, b2):
    t, d = h.shape
    tm = TOKEN_TILE
    nt = t // tm

    def choice(j):
        return pl.BlockSpec((tm, d), lambda i: (j * nt + i, 0))

    return pl.pallas_call(
        _combine_kernel,
        grid=(nt,),
        in_specs=[pl.BlockSpec((tm, d), lambda i: (i, 0)),
                  choice(0), choice(1), choice(2), choice(3),
                  pl.BlockSpec((tm, TOP_K), lambda i: (i, 0)),
                  pl.BlockSpec((1, d), lambda i: (0, 0)),
                  pl.BlockSpec((1, d), lambda i: (0, 0))],
        out_specs=pl.BlockSpec((tm, d), lambda i: (i, 0)),
        out_shape=jax.ShapeDtypeStruct((t, d), F32),
        compiler_params=_cparams(("parallel",)),
        name="combine_ln2",
    )(h, yt, yt, yt, yt, gate, g2, b2)


def _route(top_e, lrank, tilecnt):
    t = top_e.shape[0]
    m = MOE_BLOCK
    n_assign = t * TOP_K
    n_blocks = n_assign // m + N_EXPERTS
    p = n_blocks * m
    tc = tilecnt.reshape(-1, N_EXPERTS)
    base = jnp.cumsum(tc, axis=0) - tc
    counts = jnp.sum(tc, axis=0)
    padded = (counts + m - 1) // m * m
    pend = jnp.cumsum(padded)
    pstart = pend - padded
    base_tok = jnp.repeat(base, TOKEN_TILE, axis=0)
    rank = lrank + jnp.take_along_axis(base_tok, top_e, axis=1)
    slot = (pstart[top_e] + rank).reshape(-1)
    a_idx = jnp.arange(n_assign, dtype=I32)
    dest = jnp.full((p,), -1, I32).at[slot].set((a_idx % TOP_K) * t + a_idx // TOP_K,
                                               unique_indices=True)
    tok_pad = jnp.where(dest >= 0, dest % t, 0)
    blk_start = jnp.arange(n_blocks, dtype=I32) * m
    blk_expert = jnp.minimum(jnp.sum((pend[None, :] <= blk_start[:, None]).astype(I32), axis=1),
                             N_EXPERTS - 1)
    n_active = (pend[-1:] // m).astype(I32)
    real_end = (pstart + counts)[blk_expert]
    nreal = jnp.where(blk_start < pend[-1], jnp.clip(real_end - blk_start, 0, m), 0).astype(I32)
    return tok_pad, dest, blk_expert, n_active, nreal, (pstart // m).astype(I32), (padded // m).astype(I32)


def kernel(x_prompt, x_sample, cache_attn_k, cache_attn_v, state_hgrn, w_in, hgrn_lb_logits,
           hgrn_norm_w, w_branch_a, w_branch_b, w_out, ln1_g, ln1_b, w_router, b_router,
           w_gate_up, b_gate_up, w_down, b_down, ln2_g, ln2_b):
    assert w_in.shape[0] == DEPTH == 1
    bp, lp, d = x_prompt.shape
    bs, ls, _ = x_sample.shape
    tp, ts = bp * lp, bs * ls
    t = tp + ts
    keep = min(max(w for w, _ in DILATION_PAIRS), lp)

    xp2, xs2 = x_prompt.reshape(tp, d), x_sample.reshape(ts, d)
    xb = jnp.concatenate([xp2.astype(BF16), xs2.astype(BF16)], axis=0)
    wi = w_in[0]
    c1, c2 = P1_WIDTH, P1_WIDTH + P2_WIDTH
    p1 = _matmul(xb, wi[:, :c1].astype(BF16), 768, 1024, F32, "inproj_hgrn")
    p2 = _matmul(xb, wi[:, c1:c2].astype(BF16), 768, 1280, F32, "inproj_attn")
    p3 = _matmul(xb, wi[:, c2:].astype(BF16), 768, 1024, F32, "inproj_gate")

    oa_p, st_p = _hgrn(p1, hgrn_lb_logits, hgrn_norm_w[0], None, bp, lp, 0)
    oa_s, st_s = _hgrn(p1, hgrn_lb_logits, hgrn_norm_w[0],
                       state_hgrn.reshape(bs, HG_HEADS, HG_DIM, HG_DIM), bs, ls, tp)

    n = jnp.arange(1, ATT_Q_HEADS + 1, dtype=F32)
    slopes = jnp.power(2.0, -8.0 * n / ATT_Q_HEADS).reshape(N_GROUPS, ATT_KV_HEADS)
    ctx = cache_attn_k.shape[2]
    ob_p = _attn_prompt(p2, slopes, bp, lp)
    ob_s = _attn_sample(p2, cache_attn_k.reshape(bs, ctx * ATT_KV_HEADS, ATT_HEAD_DIM),
                        cache_attn_v.reshape(bs, ctx * ATT_KV_HEADS, ATT_HEAD_DIM), slopes, bs, ls, tp)

    wr = w_router[0]
    wr0 = wr.astype(BF16)
    wr1 = (wr - wr0.astype(F32)).astype(BF16)
    h, top_e, gate, lrank, tilecnt = _merge(
        xp2, xs2, oa_p.reshape(tp, HG_WIDTH), oa_s.reshape(ts, HG_WIDTH), ob_p,
        ob_s.reshape(ts, ATT_KV_WIDTH), p3, w_branch_a[0].astype(BF16), w_branch_b[0].astype(BF16),
        w_out[0].astype(BF16), ln1_g, ln1_b, wr0, wr1, b_router)

    tok_pad, dest, blk_expert, n_active, nreal, blk0, nblk = _route(top_e, lrank, tilecnt)
    xs = _moe_gather(h, tok_pad)
    act = _moe_up(xs, w_gate_up.reshape(N_EXPERTS, d, 2 * D_FF),
                  b_gate_up.reshape(N_EXPERTS, 1, 2 * D_FF), blk0, nblk)
    yt = _moe_down(act, w_down.reshape(N_EXPERTS, D_FF, d), b_down.reshape(N_EXPERTS, 1, d),
                   blk_expert, n_active, nreal, dest, t * TOP_K)
    y = _combine(h, yt, gate, ln2_g, ln2_b)

    k_new = p2[:, ATT_Q_HEADS * ATT_HEAD_DIM:(ATT_Q_HEADS + ATT_KV_HEADS) * ATT_HEAD_DIM]
    v_new = p2[:, (ATT_Q_HEADS + ATT_KV_HEADS) * ATT_HEAD_DIM:]

    def kv_prompt(a):
        return a[:tp].reshape(bp, lp, ATT_KV_HEADS, ATT_HEAD_DIM)[:, lp - keep:][None]

    def kv_sample(a):
        return a[tp:].reshape(bs, ls, ATT_KV_HEADS, ATT_HEAD_DIM)[None]

    return (y[:tp].reshape(bp, lp, d), y[tp:].reshape(bs, ls, d),
            kv_prompt(k_new), kv_prompt(v_new), st_p[None],
            kv_sample(k_new), kv_sample(v_new), st_s[None])
```

```python
import functools
import math

import jax
import jax.numpy as jnp
from jax import lax
from jax.experimental import pallas as pl
from jax.experimental.pallas import tpu as pltpu

F32 = jnp.float32
BF16 = jnp.bfloat16
I32 = jnp.int32

D_MODEL = 2048
DEPTH = 1
HG_HEADS = 8
HG_DIM = 128
HG_WIDTH = HG_HEADS * HG_DIM
HG_CHUNK = 64
ATT_HEAD_DIM = 128
DILATION_PAIRS = ((128, 1), (512, 4), (2048, 16))
N_GROUPS = len(DILATION_PAIRS)
ATT_KV_HEADS = 4
ATT_Q_HEADS = N_GROUPS * ATT_KV_HEADS
ATT_TAPS = 128
ATT_KV_WIDTH = ATT_KV_HEADS * ATT_HEAD_DIM
ATT_SCALE = ATT_HEAD_DIM ** -0.5
N_EXPERTS = 32
TOP_K = 4
D_FF = D_MODEL
SWIGLU_ALPHA = 1.702
SWIGLU_LIMIT = 7.0
LN_EPS = 1e-5
RMS_EPS = 1e-5
DEEPNORM_ALPHA = (2 * DEPTH) ** 0.25

P1_WIDTH = 4 * HG_WIDTH
P2_WIDTH = (ATT_Q_HEADS + 2 * ATT_KV_HEADS) * ATT_HEAD_DIM
P3_WIDTH = 2 * D_MODEL

LANES = 128
SUBLANES = 8
VMEM_LIMIT = 60 * 1024 * 1024

TOKEN_TILE = 256
MOE_BLOCK = 256
MOE_FT = 2048
W_STREAMS = 4
NEG = -1e30

_NT = (((1,), (1,)), ((), ()))
_TN = (((0,), (0,)), ((), ()))


def _cparams(sem):
    return pltpu.CompilerParams(dimension_semantics=sem, vmem_limit_bytes=VMEM_LIMIT)


def _sigmoid(x):
    return 1.0 / (1.0 + jnp.exp(-x))


def _mm_kernel(x_ref, w_ref, o_ref):
    o_ref[...] = jnp.dot(x_ref[...], w_ref[...], preferred_element_type=F32).astype(o_ref.dtype)


def _matmul(x, w, tm, tn, out_dtype, name):
    m, k = x.shape
    n = w.shape[1]
    assert m % tm == 0 and n % tn == 0
    return pl.pallas_call(
        _mm_kernel,
        grid=(n // tn, m // tm),
        in_specs=[pl.BlockSpec((tm, k), lambda j, i: (i, 0)),
                  pl.BlockSpec((k, tn), lambda j, i: (0, j))],
        out_specs=pl.BlockSpec((tm, tn), lambda j, i: (i, j)),
        out_shape=jax.ShapeDtypeStruct((m, n), out_dtype),
        compiler_params=_cparams(("parallel", "parallel")),
        name=name,
    )(x, w)


def _level_ref(g_scr, m, sl, chunk, sub):
    pieces = []
    for j in range(chunk // SUBLANES):
        if 2 * m >= SUBLANES:
            row = ((SUBLANES * j) // (2 * m)) * (2 * m) + m - 1
            pieces.append(jnp.broadcast_to(g_scr[row:row + 1, sl], (SUBLANES, LANES)))
        else:
            acc = None
            for u in range(SUBLANES // (2 * m)):
                row = SUBLANES * j + 2 * m * u + m - 1
                b = jnp.broadcast_to(g_scr[row:row + 1, sl], (SUBLANES, LANES))
                acc = b if acc is None else jnp.where(sub // (2 * m) == u, b, acc)
            pieces.append(acc)
    return pieces[0] if len(pieces) == 1 else jnp.concatenate(pieces, axis=0)


def _hgrn_kernel(lbl_ref, nw_ref, hq_ref, hf_ref, hi_ref, hg_ref, *rest, chunk, has_s0):
    if has_s0:
        s0_ref, o_ref, sout_ref, st_scr, g_scr = rest
    else:
        o_ref, sout_ref, st_scr, g_scr = rest
    c = pl.program_id(1)
    nc = pl.num_programs(1)

    @pl.when(c == 0)
    def _init():
        for h in range(HG_HEADS):
            if has_s0:
                st_scr[h] = s0_ref[0, h].T
            else:
                st_scr[h] = jnp.zeros((HG_DIM, HG_DIM), F32)

    lbl = lbl_ref[...]
    ex = jnp.exp(lbl - jnp.max(lbl, axis=0, keepdims=True))
    lb = ex[0:1] / jnp.sum(ex, axis=0, keepdims=True)

    f = lb + (1.0 - lb) * _sigmoid(hf_ref[...])
    logf = jnp.log(f)
    kall = 1.0 - f

    r_cc = lax.broadcasted_iota(I32, (chunk, chunk), 0)
    c_cc = lax.broadcasted_iota(I32, (chunk, chunk), 1)
    tri = jnp.where(r_cc >= c_cc, 1.0, 0.0).astype(BF16)
    p0 = logf.astype(BF16)
    r1 = logf - p0.astype(F32)
    p1 = r1.astype(BF16)
    p2 = (r1 - p1.astype(F32)).astype(BF16)
    g_all = (jnp.dot(tri, p0, preferred_element_type=F32)
             + jnp.dot(tri, p1, preferred_element_type=F32)
             + jnp.dot(tri, p2, preferred_element_type=F32))
    g_scr[...] = g_all

    sub = lax.broadcasted_iota(I32, (SUBLANES, LANES), 0)
    row_c = lax.broadcasted_iota(I32, (chunk, LANES), 0)
    nw = nw_ref[...]

    for h in range(HG_HEADS):
        sl = slice(h * HG_DIM, (h + 1) * HG_DIM)
        hq = hq_ref[:, sl]
        q = hq * _sigmoid(hq)
        k = kall[:, sl]
        g = g_all[:, sl]
        v = hi_ref[:, sl]
        vb = v.astype(BF16)

        p = lax.dot_general(q.astype(BF16), k.astype(BF16), _NT, preferred_element_type=F32)
        a = jnp.where(r_cc == c_cc, p, 0.0)
        m = 1
        while m < chunk:
            ref = _level_ref(g_scr, m, sl, chunk, sub)
            e = jnp.exp(-jnp.abs(g - ref))
            odd = ((row_c // m) & 1) == 1
            x = (jnp.where(odd, q, k) * e).astype(BF16)
            p = lax.dot_general(x, x, _NT, preferred_element_type=F32)
            pair = (r_cc // (2 * m)) == (c_cc // (2 * m))
            sel = jnp.where(pair, ((r_cc // m) & 1) - ((c_cc // m) & 1), 0) == 1
            a = jnp.where(sel, p, a)
            m *= 2

        st = st_scr[h]
        o = jnp.dot(a.astype(BF16), vb, preferred_element_type=F32)
        qe = (q * jnp.exp(g)).astype(BF16)
        o = o + lax.dot_general(qe, st.astype(BF16), _NT, preferred_element_type=F32)

        ms = jnp.mean(o * o, axis=-1, keepdims=True)
        hg = hg_ref[:, sl]
        out = o * lax.rsqrt(ms + RMS_EPS) * nw * (hg * _sigmoid(hg))
        o_ref[0, :, sl] = out.astype(o_ref.dtype)

        g_last = g[chunk - 1:chunk, :]
        kd = (k * jnp.exp(g_last - g)).astype(BF16)
        st_scr[h] = st * jnp.exp(g_last) + lax.dot_general(vb, kd, _TN, preferred_element_type=F32)

    @pl.when(c == nc - 1)
    def _fin():
        for h in range(HG_HEADS):
            sout_ref[0, h] = st_scr[h].T


def _hgrn(p1, lb_logits, norm_w, s0, batch, seq, row0):
    chunk = math.gcd(seq, HG_CHUNK)
    nchunk = seq // chunk
    assert row0 % chunk == 0
    rb0 = row0 // chunk

    def col(idx):
        return pl.BlockSpec((chunk, HG_WIDTH), lambda b, c: (rb0 + b * nchunk + c, idx))

    in_specs = [pl.BlockSpec(lb_logits.shape, lambda b, c: (0, 0)),
                pl.BlockSpec((1, HG_DIM), lambda b, c: (0, 0)),
                col(0), col(1), col(2), col(3)]
    args = [lb_logits, norm_w.reshape(1, HG_DIM), p1, p1, p1, p1]
    if s0 is not None:
        in_specs.append(pl.BlockSpec((1, HG_HEADS, HG_DIM, HG_DIM), lambda b, c: (b, 0, 0, 0)))
        args.append(s0)
    return pl.pallas_call(
        functools.partial(_hgrn_kernel, chunk=chunk, has_s0=s0 is not None),
        grid=(batch, nchunk),
        in_specs=in_specs,
        out_specs=[pl.BlockSpec((1, chunk, HG_WIDTH), lambda b, c: (b, c, 0)),
                   pl.BlockSpec((1, HG_HEADS, HG_DIM, HG_DIM), lambda b, c: (b, 0, 0, 0))],
        out_shape=[jax.ShapeDtypeStruct((batch, seq, HG_WIDTH), BF16),
                   jax.ShapeDtypeStruct((batch, HG_HEADS, HG_DIM, HG_DIM), F32)],
        scratch_shapes=[pltpu.VMEM((HG_HEADS, HG_DIM, HG_DIM), F32),
                        pltpu.VMEM((chunk, HG_WIDTH), F32)],
        compiler_params=_cparams(("parallel", "arbitrary")),
        name="hgrn_seq%d" % seq,
    )(*args)


QB = 128
ATTN_UNROLL = 8


def _attn_prompt_kernel(slopes_ref, q0_ref, q1_ref, q2_ref, k_ref, v_ref, o_ref,
                        og_scr, lse_scr, bias_scr, *, seq):
    h = pl.program_id(1)
    il = lax.broadcasted_iota(I32, (QB, 2 * QB), 0)
    jl = lax.broadcasted_iota(I32, (QB, 2 * QB), 1)
    delta = il + QB - jl
    band = jnp.where(delta >= 0, delta, ATT_TAPS + 1) <= ATT_TAPS
    prev_half = jl < QB
    q_refs = (q0_ref, q1_ref, q2_ref)

    for g, (window, dil) in enumerate(DILATION_PAIRS):
        assert window // dil == ATT_TAPS
        slope = slopes_ref[g, h]
        bias_scr[g] = jnp.where(band, (-slope * dil) * delta.astype(F32), NEG)
        nbr = (seq // dil) // QB
        q_ref = q_refs[g]

        def rows(start, dil=dil):
            return pl.ds(start, QB) if dil == 1 else pl.ds(start, QB, stride=dil)

        def body(ib, carry, g=g, dil=dil, nbr=nbr, q_ref=q_ref, rows=rows):
            res = ib // nbr
            jb = ib - res * nbr
            qs = res + dil * (jb * QB)
            ps = res + dil * jnp.maximum(jb * QB - QB, 0)
            qv = q_ref[rows(qs), :].astype(BF16)
            kb = jnp.concatenate([k_ref[rows(ps), :], k_ref[rows(qs), :]], axis=0).astype(BF16)
            vb = jnp.concatenate([v_ref[rows(ps), :], v_ref[rows(qs), :]], axis=0).astype(BF16)
            s = lax.dot_general(qv, kb, _NT, preferred_element_type=F32) * ATT_SCALE + bias_scr[g]
            s = jnp.where(jnp.where(prev_half, jb, 1) == 0, NEG, s)
            mx = jnp.max(s, axis=-1, keepdims=True)
            p = jnp.exp(s - mx)
            l = jnp.sum(p, axis=-1, keepdims=True)
            o = jnp.dot(p.astype(BF16), vb, preferred_element_type=F32) / l
            og_scr[g, rows(qs), :] = o
            lse_scr[g, rows(qs), :] = jnp.broadcast_to(mx + jnp.log(l), (QB, LANES))
            return carry

        lax.fori_loop(0, seq // QB, body, 0, unroll=ATTN_UNROLL)

    def merge(tb, carry):
        r = pl.ds(pl.multiple_of(tb * QB, QB), QB)
        l0, l1, l2 = lse_scr[0, r, :], lse_scr[1, r, :], lse_scr[2, r, :]
        mx = jnp.maximum(jnp.maximum(l0, l1), l2)
        w0, w1, w2 = jnp.exp(l0 - mx), jnp.exp(l1 - mx), jnp.exp(l2 - mx)
        o = (w0 * og_scr[0, r, :] + w1 * og_scr[1, r, :] + w2 * og_scr[2, r, :]) / (w0 + w1 + w2)
        o_ref[r, :] = o.astype(o_ref.dtype)
        return carry

    lax.fori_loop(0, seq // QB, merge, 0, unroll=ATTN_UNROLL)


def _attn_prompt(p2, slopes, batch, seq):
    assert seq % (QB * max(d for _, d in DILATION_PAIRS)) == 0

    def col(fn):
        return pl.BlockSpec((seq, ATT_HEAD_DIM), lambda b, h: (b, fn(h)))

    return pl.pallas_call(
        functools.partial(_attn_prompt_kernel, seq=seq),
        grid=(batch, ATT_KV_HEADS),
        in_specs=[pl.BlockSpec(memory_space=pltpu.SMEM),
                  col(lambda h: h), col(lambda h: ATT_KV_HEADS + h), col(lambda h: 2 * ATT_KV_HEADS + h),
                  col(lambda h: ATT_Q_HEADS + h), col(lambda h: ATT_Q_HEADS + ATT_KV_HEADS + h)],
        out_specs=pl.BlockSpec((seq, ATT_HEAD_DIM), lambda b, h: (b, h)),
        out_shape=jax.ShapeDtypeStruct((batch * seq, ATT_KV_WIDTH), BF16),
        scratch_shapes=[pltpu.VMEM((N_GROUPS, seq, ATT_HEAD_DIM), F32),
                        pltpu.VMEM((N_GROUPS, seq, LANES), F32),
                        pltpu.VMEM((N_GROUPS, QB, 2 * QB), F32)],
        compiler_params=_cparams(("parallel", "parallel")),
        name="attn_prompt",
    )(slopes, p2, p2, p2, p2, p2)


def _attn_sample_kernel(slopes_ref, qa_ref, qb_ref, qc_ref, kn_ref, vn_ref, kc_ref, vc_ref, o_ref,
                        *, ctx, nq):
    rows = N_GROUPS * nq
    gi = lax.broadcasted_iota(I32, (rows, 1), 0) // nq
    dil = jnp.where(gi == 0, DILATION_PAIRS[0][1], jnp.where(gi == 1, DILATION_PAIRS[1][1], DILATION_PAIRS[2][1]))
    win = jnp.where(gi == 0, DILATION_PAIRS[0][0], jnp.where(gi == 1, DILATION_PAIRS[1][0], DILATION_PAIRS[2][0]))

    def dist_valid(shape, key0, nkeys):
        r = lax.broadcasted_iota(I32, shape, 0)
        t = r - (r // nq) * nq
        col = lax.broadcasted_iota(I32, shape, 1)
        dist = ctx + t - (col + key0)
        bad = jnp.where(dist >= 0, dist & (dil - 1), 1)
        bad = jnp.where(dist <= win, bad, 1)
        bad = jnp.where(col < nkeys, bad, 1)
        return dist.astype(F32), bad == 0

    dist_c, ok_c = dist_valid((rows, ctx), 0, ctx)
    dist_n, ok_n = dist_valid((rows, LANES), ctx, nq)
    zpad = jnp.zeros((LANES - nq, ATT_HEAD_DIM), F32)

    for h in range(ATT_KV_HEADS):
        hs = slice(h * ATT_HEAD_DIM, (h + 1) * ATT_HEAD_DIM)
        slope = jnp.where(gi == 0, slopes_ref[0, h], jnp.where(gi == 1, slopes_ref[1, h], slopes_ref[2, h]))
        qh = jnp.concatenate([qa_ref[:, hs], qb_ref[:, hs], qc_ref[:, hs]], axis=0).astype(BF16)
        kc = kc_ref[0, pl.ds(h, ctx, stride=ATT_KV_HEADS), :].astype(BF16)
        vc = vc_ref[0, pl.ds(h, ctx, stride=ATT_KV_HEADS), :].astype(BF16)
        kn = jnp.concatenate([kn_ref[:, hs], zpad], axis=0).astype(BF16)
        vn = jnp.concatenate([vn_ref[:, hs], zpad], axis=0).astype(BF16)
        sc = lax.dot_general(qh, kc, _NT, preferred_element_type=F32) * ATT_SCALE - slope * dist_c
        sn = lax.dot_general(qh, kn, _NT, preferred_element_type=F32) * ATT_SCALE - slope * dist_n
        sc = jnp.where(ok_c, sc, NEG)
        sn = jnp.where(ok_n, sn, NEG)
        mx = jnp.maximum(jnp.max(sc, axis=-1, keepdims=True), jnp.max(sn, axis=-1, keepdims=True))
        pc = jnp.exp(sc - mx)
        pn = jnp.exp(sn - mx)
        l = jnp.sum(pc, axis=-1, keepdims=True) + jnp.sum(pn, axis=-1, keepdims=True)
        o = (jnp.dot(pc.astype(BF16), vc, preferred_element_type=F32)
             + jnp.dot(pn.astype(BF16), vn, preferred_element_type=F32)) / l
        lse = mx + jnp.log(l)
        l0, l1, l2 = lse[0:nq], lse[nq:2 * nq], lse[2 * nq:3 * nq]
        m3 = jnp.maximum(jnp.maximum(l0, l1), l2)
        w0, w1, w2 = jnp.exp(l0 - m3), jnp.exp(l1 - m3), jnp.exp(l2 - m3)
        out = (w0 * o[0:nq] + w1 * o[nq:2 * nq] + w2 * o[2 * nq:3 * nq]) / (w0 + w1 + w2)
        o_ref[0, :, hs] = out.astype(o_ref.dtype)


def _attn_sample(p2, cache_k, cache_v, slopes, batch, nq, row0):
    ctx = cache_k.shape[1] // ATT_KV_HEADS
    assert row0 % nq == 0 and nq == SUBLANES and ctx >= max(w for w, _ in DILATION_PAIRS)
    rb0 = row0 // nq

    def col(idx):
        return pl.BlockSpec((nq, ATT_KV_WIDTH), lambda b: (rb0 + b, idx))

    cache_spec = pl.BlockSpec((1, ctx * ATT_KV_HEADS, ATT_HEAD_DIM), lambda b: (b, 0, 0))
    return pl.pallas_call(
        functools.partial(_attn_sample_kernel, ctx=ctx, nq=nq),
        grid=(batch,),
        in_specs=[pl.BlockSpec(memory_space=pltpu.SMEM),
                  col(0), col(1), col(2), col(3), col(4), cache_spec, cache_spec],
        out_specs=pl.BlockSpec((1, nq, ATT_KV_WIDTH), lambda b: (b, 0, 0)),
        out_shape=jax.ShapeDtypeStruct((batch, nq, ATT_KV_WIDTH), BF16),
        compiler_params=_cparams(("parallel",)),
        name="attn_sample",
    )(slopes, p2, p2, p2, p2, p2, cache_k, cache_v)


def _layer_norm(y, g, b):
    mu = jnp.mean(y, axis=-1, keepdims=True)
    yc = y - mu
    var = jnp.mean(yc * yc, axis=-1, keepdims=True)
    return yc * lax.rsqrt(var + LN_EPS) * g + b


def _merge_kernel(xp_ref, xs_ref, oap_ref, oas_ref, obp_ref, obs_ref, ga_ref, gb_ref, wa_ref, wb_ref,
                  wo_ref, g1_ref, b1_ref, wr0_ref, wr1_ref, br_ref,
                  h_ref, hp_ref, tope_ref, gate_ref, lrank_ref, cnt_ref, *, prompt_tiles):
    tm = xp_ref.shape[0]
    is_prompt = pl.program_id(0) < prompt_tiles
    x = jnp.where(is_prompt, xp_ref[...], xs_ref[...])
    oa = jnp.where(is_prompt, oap_ref[...], oas_ref[...])
    ob = jnp.where(is_prompt, obp_ref[...], obs_ref[...])
    a = jnp.dot(oa, wa_ref[...], preferred_element_type=F32)
    b = jnp.dot(ob, wb_ref[...], preferred_element_type=F32)
    merged = _sigmoid(ga_ref[...]) * a + _sigmoid(gb_ref[...]) * b
    z = jnp.dot(merged.astype(BF16), wo_ref[...], preferred_element_type=F32)
    hh = _layer_norm(DEEPNORM_ALPHA * x + z, g1_ref[...], b1_ref[...])
    h_ref[...] = hh
    hp_ref[...] = _pack_bf16_pairs(hh)

    h0 = hh.astype(BF16)
    h1 = (hh - h0.astype(F32)).astype(BF16)
    logits = (jnp.dot(h0, wr0_ref[...], preferred_element_type=F32)
              + jnp.dot(h1, wr0_ref[...], preferred_element_type=F32)
              + jnp.dot(h0, wr1_ref[...], preferred_element_type=F32)) + br_ref[...]

    lane = lax.broadcasted_iota(I32, (tm, N_EXPERTS), 1)
    work = logits
    vals, idxs = [], []
    onehot = jnp.zeros((tm, N_EXPERTS), F32)
    for _ in range(TOP_K):
        mx = jnp.max(work, axis=-1, keepdims=True)
        idx = jnp.min(jnp.where(work == mx, lane, N_EXPERTS), axis=-1, keepdims=True)
        hit = lane == idx
        vals.append(mx)
        idxs.append(idx)
        onehot = jnp.where(hit, 1.0, onehot)
        work = jnp.where(hit, -jnp.inf, work)
    ex = [jnp.exp(v - vals[0]) for v in vals]
    den = ex[0] + ex[1] + ex[2] + ex[3]

    r_tt = lax.broadcasted_iota(I32, (tm, tm), 0)
    c_tt = lax.broadcasted_iota(I32, (tm, tm), 1)
    before = jnp.where(r_tt > c_tt, 1.0, 0.0).astype(BF16)
    prefix = jnp.dot(before, onehot.astype(BF16), preferred_element_type=F32)
    for j in range(TOP_K):
        tope_ref[:, j:j + 1] = idxs[j]
        gate_ref[:, j:j + 1] = ex[j] / den
        lrank_ref[:, j:j + 1] = jnp.sum(jnp.where(lane == idxs[j], prefix, 0.0), axis=-1,
                                        keepdims=True).astype(I32)
    cnt_ref[0] = jnp.sum(onehot, axis=0, keepdims=True).astype(I32)


def _merge(xp, xs, oap, oas, obp, obs, p3, wa, wb, wo, g1, b1, wr0, wr1, br):
    tm = TOKEN_TILE
    assert xp.shape[0] % tm == 0 and xs.shape[0] % tm == 0
    ntp = xp.shape[0] // tm
    nt = ntp + xs.shape[0] // tm
    t = nt * tm

    def prow(width):
        return pl.BlockSpec((tm, width), lambda i: (jnp.minimum(i, ntp - 1), 0))

    def srow(width):
        return pl.BlockSpec((tm, width), lambda i: (jnp.maximum(i - ntp, 0), 0))

    def full(a):
        return pl.BlockSpec(a.shape, lambda i: (0,) * a.ndim)

    def rowblk(width, idx=0):
        return pl.BlockSpec((tm, width), lambda i: (i, idx))

    return pl.pallas_call(
        functools.partial(_merge_kernel, prompt_tiles=ntp),
        grid=(nt,),
        in_specs=[prow(D_MODEL), srow(D_MODEL), prow(HG_WIDTH), srow(HG_WIDTH),
                  prow(ATT_KV_WIDTH), srow(ATT_KV_WIDTH), rowblk(D_MODEL, 0),
                  rowblk(D_MODEL, 1), full(wa), full(wb), full(wo), full(g1), full(b1),
                  full(wr0), full(wr1), full(br)],
        out_specs=[rowblk(D_MODEL), rowblk(D_MODEL // 2), rowblk(TOP_K), rowblk(TOP_K), rowblk(TOP_K),
                   pl.BlockSpec((1, 1, N_EXPERTS), lambda i: (i, 0, 0))],
        out_shape=[jax.ShapeDtypeStruct((t, D_MODEL), F32),
                   jax.ShapeDtypeStruct((t, D_MODEL // 2), jnp.uint32),
                   jax.ShapeDtypeStruct((t, TOP_K), I32),
                   jax.ShapeDtypeStruct((t, TOP_K), F32),
                   jax.ShapeDtypeStruct((t, TOP_K), I32),
                   jax.ShapeDtypeStruct((nt, 1, N_EXPERTS), I32)],
        compiler_params=_cparams(("parallel",)),
        name="merge_ln1_router",
    )(xp, xs, oap, oas, obp, obs, p3, p3, wa, wb, wo, g1, b1, wr0, wr1, br)


def _pack_bf16_pairs(x):
    w = x.shape[1] // 2
    lo = lax.bitcast_convert_type(x[:, :w].astype(BF16).astype(F32), jnp.uint32)
    hi = lax.bitcast_convert_type(x[:, w:].astype(BF16).astype(F32), jnp.uint32)
    return (hi & jnp.uint32(0xFFFF0000)) | (lo >> 16)


def _unpack_bf16_pairs(u):
    lo = lax.bitcast_convert_type(u << 16, F32).astype(BF16)
    hi = lax.bitcast_convert_type(u & jnp.uint32(0xFFFF0000), F32).astype(BF16)
    return lo, hi


def _gather_kernel(tok_ref, h_ref, o_ref):
    m = o_ref.shape[0]

    def body(q, carry):
        for u in range(SUBLANES):
            r = q * SUBLANES + u
            o_ref[pl.ds(r, 1), :] = h_ref[pl.ds(tok_ref[0, 0, r], 1), :]
        return carry

    lax.fori_loop(0, m // SUBLANES, body, 0)


def _moe_gather(hp, tok_pad):
    t, w = hp.shape
    p = tok_pad.shape[0]
    m = MOE_BLOCK
    return pl.pallas_call(
        _gather_kernel,
        grid=(p // m,),
        in_specs=[pl.BlockSpec((1, 1, m), lambda i: (i, 0, 0), memory_space=pltpu.SMEM),
                  pl.BlockSpec((t, w), lambda i: (0, 0), pipeline_mode=pl.Buffered(1))],
        out_specs=pl.BlockSpec((m, w), lambda i: (i, 0)),
        out_shape=jax.ShapeDtypeStruct((p, w), hp.dtype),
        compiler_params=_cparams(("arbitrary",)),
        name="moe_gather",
    )(tok_pad.reshape(p // m, 1, m), hp)


def _moe_up_kernel(blk0_ref, nblk_ref, w0_ref, w1_ref, w2_ref, w3_ref, b_ref, xs_ref, act_ref,
                   wbf_scr, x_scr, o_scr, sem_in, sem_out):
    e = pl.program_id(0)
    f = pl.program_id(1)
    n = nblk_ref[e]
    b0 = blk0_ref[e]
    m = x_scr.shape[1]
    w_refs = (w0_ref, w1_ref, w2_ref, w3_ref)
    kq, ft = w0_ref.shape[1:]
    half = ft // 2

    def rows(b):
        return pl.ds(pl.multiple_of((b0 + b) * m, m), m)

    def x_copy(b, s):
        return pltpu.make_async_copy(xs_ref.at[rows(b)], x_scr.at[s], sem_in.at[s])

    def o_copy(b, s):
        return pltpu.make_async_copy(o_scr.at[s], act_ref.at[f, rows(b)], sem_out.at[s])

    @pl.when(n > 0)
    def _():
        x_copy(0, 0).start()
        for q, w_ref in enumerate(w_refs):
            wbf_scr[q * kq:(q + 1) * kq, :] = w_ref[0].astype(BF16)
        even = (lax.broadcasted_iota(I32, (m, LANES), 1) & 1) == 0

        def body(b, carry):
            s = b & 1
            x_copy(b, s).wait()

            @pl.when(b + 1 < n)
            def _():
                x_copy(b + 1, 1 - s).start()

            @pl.when(b >= 2)
            def _():
                o_copy(b - 2, s).wait()

            x_lo, x_hi = _unpack_bf16_pairs(x_scr[s])
            kh = x_lo.shape[1]
            for g in range(ft // (2 * LANES)):
                cols = slice(2 * g * LANES, (2 * g + 2) * LANES)
                h = (jnp.dot(x_lo, wbf_scr[:kh, cols], preferred_element_type=F32)
                     + jnp.dot(x_hi, wbf_scr[kh:, cols], preferred_element_type=F32)
                     + b_ref[0, :, cols])
                ha, hb = h[:, :LANES], h[:, LANES:]
                glu = jnp.minimum(jnp.where(even, ha, pltpu.roll(hb, 1, 1)), SWIGLU_LIMIT)
                lin = jnp.where(even, pltpu.roll(ha, LANES - 1, 1), hb)
                lin = jnp.clip(lin, -SWIGLU_LIMIT, SWIGLU_LIMIT) + 1.0
                act = glu * _sigmoid(SWIGLU_ALPHA * glu) * lin
                o_scr[s, :, g * LANES:(g + 1) * LANES] = act.astype(o_scr.dtype)
            o_copy(b, s).start()
            return carry

        lax.fori_loop(0, n, body, 0)

        @pl.when(n >= 2)
        def _():
            o_copy(n - 2, n & 1).wait()
        o_copy(n - 1, (n - 1) & 1).wait()

    @pl.when(e == pl.num_programs(0) - 1)
    def _():
        o_scr[0] = jnp.zeros(o_scr.shape[1:], o_scr.dtype)

        nz = act_ref.shape[1] // m - b0
        lax.fori_loop(n, nz, lambda b, c: (o_copy(b, 0).start(), c)[1], 0)
        lax.fori_loop(n, nz, lambda b, c: (o_copy(b, 0).wait(), c)[1], 0)


def _moe_up(xs, w_gate_up, b_gate_up, blk0, nblk):
    p, w = xs.shape
    d = 2 * w
    m = MOE_BLOCK
    ft = MOE_FT
    ne = w_gate_up.shape[0]
    nf = w_gate_up.shape[2] // ft

    def wslice(q, e, f, b0, nb):
        return (e, q, f)

    grid_spec = pltpu.PrefetchScalarGridSpec(
        num_scalar_prefetch=2,
        grid=(ne, nf),
        in_specs=[pl.BlockSpec((1, d // W_STREAMS, ft), functools.partial(wslice, q))
                  for q in range(W_STREAMS)] + [
                  pl.BlockSpec((1, 1, ft), lambda e, f, b0, nb: (e, 0, f)),
                  pl.BlockSpec(memory_space=pl.ANY)],
        out_specs=pl.BlockSpec(memory_space=pl.ANY),
        scratch_shapes=[pltpu.VMEM((d, ft), BF16),
                        pltpu.VMEM((2, m, w), xs.dtype),
                        pltpu.VMEM((2, m, ft // 2), BF16),
                        pltpu.SemaphoreType.DMA((2,)),
                        pltpu.SemaphoreType.DMA((2,))],
    )
    return pl.pallas_call(
        _moe_up_kernel,
        grid_spec=grid_spec,
        out_shape=jax.ShapeDtypeStruct((nf, p, ft // 2), BF16),
        compiler_params=_cparams(("arbitrary", "arbitrary")),
        name="moe_up",
    )(blk0, nblk, *([w_gate_up] * W_STREAMS), b_gate_up, xs)


def _hidden_perm():
    r = lax.broadcasted_iota(I32, (LANES, LANES), 0)
    c = lax.broadcasted_iota(I32, (LANES, LANES), 1)
    src = (r >> 1) + (r & 1) * (LANES // 2)
    return jnp.where(c == src, 1.0, 0.0).astype(BF16)


def _moe_down_kernel(be_ref, na_ref, nreal_ref, dest_ref, a_ref, wd0_ref, wd1_ref, wd2_ref, wd3_ref,
                     bd_ref, yt_ref, wbf_scr, y_scr, sem):
    i = pl.program_id(0)
    nb = pl.num_programs(0)
    nf, m, fh = a_ref.shape
    slot = i % 2

    def row_copy(s, r, d):
        return pltpu.make_async_copy(y_scr.at[s, pl.ds(r, 1)], yt_ref.at[pl.ds(d, 1)], sem.at[s])

    def drain(s, count):
        @pl.when(count == m)
        def _():
            pltpu.make_async_copy(y_scr.at[s], yt_ref.at[pl.ds(0, m)], sem.at[s]).wait()

        @pl.when(count != m)
        def _():
            def wbody(_, carry):
                row_copy(s, 0, 0).wait()
                return carry
            lax.fori_loop(0, count, wbody, 0)

    @pl.when(i >= 2)
    def _():
        drain(slot, nreal_ref[i - 2])

    changed = jnp.where(i == 0, 1, be_ref[i] - be_ref[jnp.maximum(i - 1, 0)])

    @pl.when(changed != 0)
    def _():
        perm = _hidden_perm()
        for q, wd_ref in enumerate((wd0_ref, wd1_ref, wd2_ref, wd3_ref)):
            kq = wd_ref.shape[1]
            for g in range(kq // LANES):
                rows = slice(g * LANES, (g + 1) * LANES)
                wbf_scr[q * kq + g * LANES:q * kq + (g + 1) * LANES, :] = jnp.dot(
                    perm, wd_ref[0, rows, :].astype(BF16), preferred_element_type=F32).astype(BF16)

    @pl.when(i < na_ref[0])
    def _():
        y = bd_ref[0]
        for j in range(nf):
            y = y + jnp.dot(a_ref[j], wbf_scr[j * fh:(j + 1) * fh, :], preferred_element_type=F32)
        y_scr[slot] = y

        n = nreal_ref[i]
        unroll = 8

        def sbody8(q, carry):
            for u in range(unroll):
                r = q * unroll + u
                row_copy(slot, r, dest_ref[0, 0, r]).start()
            return carry

        def sbody1(r, carry):
            row_copy(slot, r, dest_ref[0, 0, r]).start()
            return carry

        lax.fori_loop(0, n // unroll, sbody8, 0)
        lax.fori_loop((n // unroll) * unroll, n, sbody1, 0)

    @pl.when(i == nb - 1)
    def _():
        @pl.when(i >= 1)
        def _():
            drain(1 - slot, nreal_ref[i - 1])
        drain(slot, nreal_ref[i])


def _moe_down(act, w_down, b_down, blk_expert, n_active, nreal, dest, n_rows):
    nf, p, fh = act.shape
    f = nf * fh
    m = MOE_BLOCK
    d = w_down.shape[2]

    def wslice(q, i, be, na, nr):
        return (be[i], q, 0)

    grid_spec = pltpu.PrefetchScalarGridSpec(
        num_scalar_prefetch=3,
        grid=(p // m,),
        in_specs=[pl.BlockSpec((1, 1, m), lambda i, be, na, nr: (i, 0, 0), memory_space=pltpu.SMEM),
                  pl.BlockSpec((nf, m, fh), lambda i, be, na, nr: (0, i, 0)),
                  *[pl.BlockSpec((1, f // W_STREAMS, d), functools.partial(wslice, q))
                    for q in range(W_STREAMS)],
                  pl.BlockSpec((1, 1, d), lambda i, be, na, nr: (be[i], 0, 0))],
        out_specs=pl.BlockSpec(memory_space=pl.ANY),
        scratch_shapes=[pltpu.VMEM((f, d), BF16),
                        pltpu.VMEM((2, m, d), F32),
                        pltpu.SemaphoreType.DMA((2,))],
    )
    return pl.pallas_call(
        _moe_down_kernel,
        grid_spec=grid_spec,
        out_shape=jax.ShapeDtypeStruct((n_rows, d), F32),
        compiler_params=_cparams(("arbitrary",)),
        name="moe_down",
    )(blk_expert, n_active, nreal, dest.reshape(p // m, 1, m), act, *([w_down] * W_STREAMS), b_down)


def _combine_kernel(h_ref, y0_ref, y1_ref, y2_ref, y3_ref, gate_ref, g2_ref, b2_ref, o_ref):
    ffn = gate_ref[:, 0:1] * y0_ref[...]
    for j, y_ref in enumerate((y1_ref, y2_ref, y3_ref), start=1):
        ffn = ffn + gate_ref[:, j:j + 1] * y_ref[...]
    o_ref[...] = _layer_norm(DEEPNORM_ALPHA * h_ref[...] + ffn, g2_ref[...], b2_ref[...])


def _combine(h, yt, gate, g2, b2):
    t, d = h.shape
    tm = TOKEN_TILE
    nt = t // tm

    def choice(j):
        return pl.BlockSpec((tm, d), lambda i: (j * nt + i, 0))

    return pl.pallas_call(
        _combine_kernel,
        grid=(nt,),
        in_specs=[pl.BlockSpec((tm, d), lambda i: (i, 0)),
                  choice(0), choice(1), choice(2), choice(3),
                  pl.BlockSpec((tm, TOP_K), lambda i: (i, 0)),
                  pl.BlockSpec((1, d), lambda i: (0, 0)),
                  pl.BlockSpec((1, d), lambda i: (0, 0))],
        out_specs=pl.BlockSpec((tm, d), lambda i: (i, 0)),
        out_shape=jax.ShapeDtypeStruct((t, d), F32),
        compiler_params=_cparams(("parallel",)),
        name="combine_ln2",
    )(h, yt, yt, yt, yt, gate, g2, b2)


def _route(top_e, lrank, tilecnt):
    t = top_e.shape[0]
    m = MOE_BLOCK
    n_assign = t * TOP_K
    n_blocks = n_assign // m + N_EXPERTS
    p = n_blocks * m
    tc = tilecnt.reshape(-1, N_EXPERTS)
    base = jnp.cumsum(tc, axis=0) - tc
    counts = jnp.sum(tc, axis=0)
    padded = (counts + m - 1) // m * m
    pend = jnp.cumsum(padded)
    pstart = pend - padded
    base_tok = jnp.repeat(base, TOKEN_TILE, axis=0)
    rank = lrank + jnp.take_along_axis(base_tok, top_e, axis=1)
    slot = (pstart[top_e] + rank).reshape(-1)
    a_idx = jnp.arange(n_assign, dtype=I32)
    dest = jnp.full((p,), -1, I32).at[slot].set((a_idx % TOP_K) * t + a_idx // TOP_K,
                                               unique_indices=True)
    tok_pad = jnp.where(dest >= 0, dest % t, 0)
    blk_start = jnp.arange(n_blocks, dtype=I32) * m
    blk_expert = jnp.minimum(jnp.sum((pend[None, :] <= blk_start[:, None]).astype(I32), axis=1),
                             N_EXPERTS - 1)
    n_active = (pend[-1:] // m).astype(I32)
    real_end = (pstart + counts)[blk_expert]
    nreal = jnp.where(blk_start < pend[-1], jnp.clip(real_end - blk_start, 0, m), 0).astype(I32)
    return tok_pad, dest, blk_expert, n_active, nreal, (pstart // m).astype(I32), (padded // m).astype(I32)


def kernel(x_prompt, x_sample, cache_attn_k, cache_attn_v, state_hgrn, w_in, hgrn_lb_logits,
           hgrn_norm_w, w_branch_a, w_branch_b, w_out, ln1_g, ln1_b, w_router, b_router,
           w_gate_up, b_gate_up, w_down, b_down, ln2_g, ln2_b):
    assert w_in.shape[0] == DEPTH == 1
    bp, lp, d = x_prompt.shape
    bs, ls, _ = x_sample.shape
    tp, ts = bp * lp, bs * ls
    t = tp + ts
    keep = min(max(w for w, _ in DILATION_PAIRS), lp)

    xp2, xs2 = x_prompt.reshape(tp, d), x_sample.reshape(ts, d)
    xb = jnp.concatenate([xp2.astype(BF16), xs2.astype(BF16)], axis=0)
    wi = w_in[0]
    c1, c2 = P1_WIDTH, P1_WIDTH + P2_WIDTH
    p1 = _matmul(xb, wi[:, :c1].astype(BF16), 768, 1024, F32, "inproj_hgrn")
    p2 = _matmul(xb, wi[:, c1:c2].astype(BF16), 768, 1280, F32, "inproj_attn")
    p3 = _matmul(xb, wi[:, c2:].astype(BF16), 768, 1024, F32, "inproj_gate")

    oa_p, st_p = _hgrn(p1, hgrn_lb_logits, hgrn_norm_w[0], None, bp, lp, 0)
    oa_s, st_s = _hgrn(p1, hgrn_lb_logits, hgrn_norm_w[0],
                       state_hgrn.reshape(bs, HG_HEADS, HG_DIM, HG_DIM), bs, ls, tp)

    n = jnp.arange(1, ATT_Q_HEADS + 1, dtype=F32)
    slopes = jnp.power(2.0, -8.0 * n / ATT_Q_HEADS).reshape(N_GROUPS, ATT_KV_HEADS)
    ctx = cache_attn_k.shape[2]
    ob_p = _attn_prompt(p2, slopes, bp, lp)
    ob_s = _attn_sample(p2, cache_attn_k.reshape(bs, ctx * ATT_KV_HEADS, ATT_HEAD_DIM),
                        cache_attn_v.reshape(bs, ctx * ATT_KV_HEADS, ATT_HEAD_DIM), slopes, bs, ls, tp)

    wr = w_router[0]
    wr0 = wr.astype(BF16)
    wr1 = (wr - wr0.astype(F32)).astype(BF16)
    h, hp, top_e, gate, lrank, tilecnt = _merge(
        xp2, xs2, oa_p.reshape(tp, HG_WIDTH), oa_s.reshape(ts, HG_WIDTH), ob_p,
        ob_s.reshape(ts, ATT_KV_WIDTH), p3, w_branch_a[0].astype(BF16), w_branch_b[0].astype(BF16),
        w_out[0].astype(BF16), ln1_g, ln1_b, wr0, wr1, b_router)

    tok_pad, dest, blk_expert, n_active, nreal, blk0, nblk = _route(top_e, lrank, tilecnt)
    xs = _moe_gather(hp, tok_pad)
    act = _moe_up(xs, w_gate_up.reshape(N_EXPERTS, d, 2 * D_FF),
                  b_gate_up.reshape(N_EXPERTS, 1, 2 * D_FF), blk0, nblk)
    yt = _moe_down(act, w_down.reshape(N_EXPERTS, D_FF, d), b_down.reshape(N_EXPERTS, 1, d),
                   blk_expert, n_active, nreal, dest, t * TOP_K)
    y = _combine(h, yt, gate, ln2_g, ln2_b)

    k_new = p2[:, ATT_Q_HEADS * ATT_HEAD_DIM:(ATT_Q_HEADS + ATT_KV_HEADS) * ATT_HEAD_DIM]
    v_new = p2[:, (ATT_Q_HEADS + ATT_KV_HEADS) * ATT_HEAD_DIM:]

    def kv_prompt(a):
        return a[:tp].reshape(bp, lp, ATT_KV_HEADS, ATT_HEAD_DIM)[:, lp - keep:][None]

    def kv_sample(a):
        return a[tp:].reshape(bs, ls, ATT_KV_HEADS, ATT_HEAD_DIM)[None]

    return (y[:tp].reshape(bp, lp, d), y[tp:].reshape(bs, ls, d),
            kv_prompt(k_new), kv_prompt(v_new), st_p[None],
            kv_sample(k_new), kv_sample(v_new), st_s[None])
```

```python
import functools
import math

import jax
import jax.numpy as jnp
from jax import lax
from jax.experimental import pallas as pl
from jax.experimental.pallas import tpu as pltpu

F32 = jnp.float32
BF16 = jnp.bfloat16
I32 = jnp.int32

D_MODEL = 2048
DEPTH = 1
HG_HEADS = 8
HG_DIM = 128
HG_WIDTH = HG_HEADS * HG_DIM
HG_CHUNK = 64
ATT_HEAD_DIM = 128
DILATION_PAIRS = ((128, 1), (512, 4), (2048, 16))
N_GROUPS = len(DILATION_PAIRS)
ATT_KV_HEADS = 4
ATT_Q_HEADS = N_GROUPS * ATT_KV_HEADS
ATT_TAPS = 128
ATT_KV_WIDTH = ATT_KV_HEADS * ATT_HEAD_DIM
ATT_SCALE = ATT_HEAD_DIM ** -0.5
N_EXPERTS = 32
TOP_K = 4
D_FF = D_MODEL
SWIGLU_ALPHA = 1.702
SWIGLU_LIMIT = 7.0
LN_EPS = 1e-5
RMS_EPS = 1e-5
DEEPNORM_ALPHA = (2 * DEPTH) ** 0.25

P1_WIDTH = 4 * HG_WIDTH
P2_WIDTH = (ATT_Q_HEADS + 2 * ATT_KV_HEADS) * ATT_HEAD_DIM
P3_WIDTH = 2 * D_MODEL

LANES = 128
SUBLANES = 8
VMEM_LIMIT = 60 * 1024 * 1024

TOKEN_TILE = 256
MOE_BLOCK = 256
MOE_FT = 2048
W_STREAMS = 4
NEG = -1e30

_NT = (((1,), (1,)), ((), ()))
_TN = (((0,), (0,)), ((), ()))


def _cparams(sem):
    return pltpu.CompilerParams(dimension_semantics=sem, vmem_limit_bytes=VMEM_LIMIT)


def _sigmoid(x):
    return 1.0 / (1.0 + jnp.exp(-x))


def _mm_kernel(x_ref, w_ref, o_ref):
    o_ref[...] = jnp.dot(x_ref[...], w_ref[...], preferred_element_type=F32).astype(o_ref.dtype)


def _matmul(x, w, tm, tn, out_dtype, name):
    m, k = x.shape
    n = w.shape[1]
    assert m % tm == 0 and n % tn == 0
    return pl.pallas_call(
        _mm_kernel,
        grid=(n // tn, m // tm),
        in_specs=[pl.BlockSpec((tm, k), lambda j, i: (i, 0)),
                  pl.BlockSpec((k, tn), lambda j, i: (0, j))],
        out_specs=pl.BlockSpec((tm, tn), lambda j, i: (i, j)),
        out_shape=jax.ShapeDtypeStruct((m, n), out_dtype),
        compiler_params=_cparams(("parallel", "parallel")),
        name=name,
    )(x, w)


def _level_ref(g_scr, m, sl, chunk, sub):
    pieces = []
    for j in range(chunk // SUBLANES):
        if 2 * m >= SUBLANES:
            row = ((SUBLANES * j) // (2 * m)) * (2 * m) + m - 1
            pieces.append(jnp.broadcast_to(g_scr[row:row + 1, sl], (SUBLANES, LANES)))
        else:
            acc = None
            for u in range(SUBLANES // (2 * m)):
                row = SUBLANES * j + 2 * m * u + m - 1
                b = jnp.broadcast_to(g_scr[row:row + 1, sl], (SUBLANES, LANES))
                acc = b if acc is None else jnp.where(sub // (2 * m) == u, b, acc)
            pieces.append(acc)
    return pieces[0] if len(pieces) == 1 else jnp.concatenate(pieces, axis=0)


def _hgrn_kernel(lbl_ref, nw_ref, hq_ref, hf_ref, hi_ref, hg_ref, *rest, chunk, has_s0):
    if has_s0:
        s0_ref, o_ref, sout_ref, st_scr, g_scr = rest
    else:
        o_ref, sout_ref, st_scr, g_scr = rest
    c = pl.program_id(1)
    nc = pl.num_programs(1)

    @pl.when(c == 0)
    def _init():
        for h in range(HG_HEADS):
            if has_s0:
                st_scr[h] = s0_ref[0, h].T
            else:
                st_scr[h] = jnp.zeros((HG_DIM, HG_DIM), F32)

    lbl = lbl_ref[...]
    ex = jnp.exp(lbl - jnp.max(lbl, axis=0, keepdims=True))
    lb = ex[0:1] / jnp.sum(ex, axis=0, keepdims=True)

    f = lb + (1.0 - lb) * _sigmoid(hf_ref[...])
    logf = jnp.log(f)
    kall = 1.0 - f

    r_cc = lax.broadcasted_iota(I32, (chunk, chunk), 0)
    c_cc = lax.broadcasted_iota(I32, (chunk, chunk), 1)
    tri = jnp.where(r_cc >= c_cc, 1.0, 0.0).astype(BF16)
    p0 = logf.astype(BF16)
    r1 = logf - p0.astype(F32)
    p1 = r1.astype(BF16)
    p2 = (r1 - p1.astype(F32)).astype(BF16)
    g_all = (jnp.dot(tri, p0, preferred_element_type=F32)
             + jnp.dot(tri, p1, preferred_element_type=F32)
             + jnp.dot(tri, p2, preferred_element_type=F32))
    g_scr[...] = g_all

    sub = lax.broadcasted_iota(I32, (SUBLANES, LANES), 0)
    row_c = lax.broadcasted_iota(I32, (chunk, LANES), 0)
    nw = nw_ref[...]

    for h in range(HG_HEADS):
        sl = slice(h * HG_DIM, (h + 1) * HG_DIM)
        hq = hq_ref[:, sl]
        q = hq * _sigmoid(hq)
        k = kall[:, sl]
        g = g_all[:, sl]
        v = hi_ref[:, sl]
        vb = v.astype(BF16)

        p = lax.dot_general(q.astype(BF16), k.astype(BF16), _NT, preferred_element_type=F32)
        a = jnp.where(r_cc == c_cc, p, 0.0)
        m = 1
        while m < chunk:
            ref = _level_ref(g_scr, m, sl, chunk, sub)
            e = jnp.exp(-jnp.abs(g - ref))
            odd = ((row_c // m) & 1) == 1
            x = (jnp.where(odd, q, k) * e).astype(BF16)
            p = lax.dot_general(x, x, _NT, preferred_element_type=F32)
            pair = (r_cc // (2 * m)) == (c_cc // (2 * m))
            sel = jnp.where(pair, ((r_cc // m) & 1) - ((c_cc // m) & 1), 0) == 1
            a = jnp.where(sel, p, a)
            m *= 2

        st = st_scr[h]
        o = jnp.dot(a.astype(BF16), vb, preferred_element_type=F32)
        qe = (q * jnp.exp(g)).astype(BF16)
        o = o + lax.dot_general(qe, st.astype(BF16), _NT, preferred_element_type=F32)

        ms = jnp.mean(o * o, axis=-1, keepdims=True)
        hg = hg_ref[:, sl]
        out = o * lax.rsqrt(ms + RMS_EPS) * nw * (hg * _sigmoid(hg))
        o_ref[0, :, sl] = out.astype(o_ref.dtype)

        g_last = g[chunk - 1:chunk, :]
        kd = (k * jnp.exp(g_last - g)).astype(BF16)
        st_scr[h] = st * jnp.exp(g_last) + lax.dot_general(vb, kd, _TN, preferred_element_type=F32)

    @pl.when(c == nc - 1)
    def _fin():
        for h in range(HG_HEADS):
            sout_ref[0, h] = st_scr[h].T


def _hgrn(p1, lb_logits, norm_w, s0, batch, seq, row0):
    chunk = math.gcd(seq, HG_CHUNK)
    nchunk = seq // chunk
    assert row0 % chunk == 0
    rb0 = row0 // chunk

    def col(idx):
        return pl.BlockSpec((chunk, HG_WIDTH), lambda b, c: (rb0 + b * nchunk + c, idx))

    in_specs = [pl.BlockSpec(lb_logits.shape, lambda b, c: (0, 0)),
                pl.BlockSpec((1, HG_DIM), lambda b, c: (0, 0)),
                col(0), col(1), col(2), col(3)]
    args = [lb_logits, norm_w.reshape(1, HG_DIM), p1, p1, p1, p1]
    if s0 is not None:
        in_specs.append(pl.BlockSpec((1, HG_HEADS, HG_DIM, HG_DIM), lambda b, c: (b, 0, 0, 0)))
        args.append(s0)
    return pl.pallas_call(
        functools.partial(_hgrn_kernel, chunk=chunk, has_s0=s0 is not None),
        grid=(batch, nchunk),
        in_specs=in_specs,
        out_specs=[pl.BlockSpec((1, chunk, HG_WIDTH), lambda b, c: (b, c, 0)),
                   pl.BlockSpec((1, HG_HEADS, HG_DIM, HG_DIM), lambda b, c: (b, 0, 0, 0))],
        out_shape=[jax.ShapeDtypeStruct((batch, seq, HG_WIDTH), BF16),
                   jax.ShapeDtypeStruct((batch, HG_HEADS, HG_DIM, HG_DIM), F32)],
        scratch_shapes=[pltpu.VMEM((HG_HEADS, HG_DIM, HG_DIM), F32),
                        pltpu.VMEM((chunk, HG_WIDTH), F32)],
        compiler_params=_cparams(("parallel", "arbitrary")),
        name="hgrn_seq%d" % seq,
    )(*args)


QB = 128
ATTN_UNROLL = 8


def _attn_prompt_kernel(slopes_ref, q0_ref, q1_ref, q2_ref, k_ref, v_ref, o_ref,
                        og_scr, lse_scr, bias_scr, *, seq):
    h = pl.program_id(1)
    il = lax.broadcasted_iota(I32, (QB, 2 * QB), 0)
    jl = lax.broadcasted_iota(I32, (QB, 2 * QB), 1)
    delta = il + QB - jl
    band = jnp.where(delta >= 0, delta, ATT_TAPS + 1) <= ATT_TAPS
    prev_half = jl < QB
    q_refs = (q0_ref, q1_ref, q2_ref)

    for g, (window, dil) in enumerate(DILATION_PAIRS):
        assert window // dil == ATT_TAPS
        slope = slopes_ref[g, h]
        bias_scr[g] = jnp.where(band, (-slope * dil) * delta.astype(F32), NEG)
        nbr = (seq // dil) // QB
        q_ref = q_refs[g]

        def rows(start, dil=dil):
            return pl.ds(start, QB) if dil == 1 else pl.ds(start, QB, stride=dil)

        def body(ib, carry, g=g, dil=dil, nbr=nbr, q_ref=q_ref, rows=rows):
            res = ib // nbr
            jb = ib - res * nbr
            qs = res + dil * (jb * QB)
            ps = res + dil * jnp.maximum(jb * QB - QB, 0)
            qv = q_ref[rows(qs), :].astype(BF16)
            kb = jnp.concatenate([k_ref[rows(ps), :], k_ref[rows(qs), :]], axis=0).astype(BF16)
            vb = jnp.concatenate([v_ref[rows(ps), :], v_ref[rows(qs), :]], axis=0).astype(BF16)
            s = lax.dot_general(qv, kb, _NT, preferred_element_type=F32) * ATT_SCALE + bias_scr[g]
            s = jnp.where(jnp.where(prev_half, jb, 1) == 0, NEG, s)
            mx = jnp.max(s, axis=-1, keepdims=True)
            p = jnp.exp(s - mx)
            l = jnp.sum(p, axis=-1, keepdims=True)
            o = jnp.dot(p.astype(BF16), vb, preferred_element_type=F32) / l
            og_scr[g, rows(qs), :] = o
            lse_scr[g, rows(qs), :] = jnp.broadcast_to(mx + jnp.log(l), (QB, LANES))
            return carry

        lax.fori_loop(0, seq // QB, body, 0, unroll=ATTN_UNROLL)

    def merge(tb, carry):
        r = pl.ds(pl.multiple_of(tb * QB, QB), QB)
        l0, l1, l2 = lse_scr[0, r, :], lse_scr[1, r, :], lse_scr[2, r, :]
        mx = jnp.maximum(jnp.maximum(l0, l1), l2)
        w0, w1, w2 = jnp.exp(l0 - mx), jnp.exp(l1 - mx), jnp.exp(l2 - mx)
        o = (w0 * og_scr[0, r, :] + w1 * og_scr[1, r, :] + w2 * og_scr[2, r, :]) / (w0 + w1 + w2)
        o_ref[r, :] = o.astype(o_ref.dtype)
        return carry

    lax.fori_loop(0, seq // QB, merge, 0, unroll=ATTN_UNROLL)


def _attn_prompt(p2, slopes, batch, seq):
    assert seq % (QB * max(d for _, d in DILATION_PAIRS)) == 0

    def col(fn):
        return pl.BlockSpec((seq, ATT_HEAD_DIM), lambda b, h: (b, fn(h)))

    return pl.pallas_call(
        functools.partial(_attn_prompt_kernel, seq=seq),
        grid=(batch, ATT_KV_HEADS),
        in_specs=[pl.BlockSpec(memory_space=pltpu.SMEM),
                  col(lambda h: h), col(lambda h: ATT_KV_HEADS + h), col(lambda h: 2 * ATT_KV_HEADS + h),
                  col(lambda h: ATT_Q_HEADS + h), col(lambda h: ATT_Q_HEADS + ATT_KV_HEADS + h)],
        out_specs=pl.BlockSpec((seq, ATT_HEAD_DIM), lambda b, h: (b, h)),
        out_shape=jax.ShapeDtypeStruct((batch * seq, ATT_KV_WIDTH), BF16),
        scratch_shapes=[pltpu.VMEM((N_GROUPS, seq, ATT_HEAD_DIM), F32),
                        pltpu.VMEM((N_GROUPS, seq, LANES), F32),
                        pltpu.VMEM((N_GROUPS, QB, 2 * QB), F32)],
        compiler_params=_cparams(("parallel", "parallel")),
        name="attn_prompt",
    )(slopes, p2, p2, p2, p2, p2)


def _attn_sample_kernel(slopes_ref, qa_ref, qb_ref, qc_ref, kn_ref, vn_ref, kc_ref, vc_ref, o_ref,
                        *, ctx, nq):
    rows = N_GROUPS * nq
    gi = lax.broadcasted_iota(I32, (rows, 1), 0) // nq
    dil = jnp.where(gi == 0, DILATION_PAIRS[0][1], jnp.where(gi == 1, DILATION_PAIRS[1][1], DILATION_PAIRS[2][1]))
    win = jnp.where(gi == 0, DILATION_PAIRS[0][0], jnp.where(gi == 1, DILATION_PAIRS[1][0], DILATION_PAIRS[2][0]))

    def dist_valid(shape, key0, nkeys):
        r = lax.broadcasted_iota(I32, shape, 0)
        t = r - (r // nq) * nq
        col = lax.broadcasted_iota(I32, shape, 1)
        dist = ctx + t - (col + key0)
        bad = jnp.where(dist >= 0, dist & (dil - 1), 1)
        bad = jnp.where(dist <= win, bad, 1)
        bad = jnp.where(col < nkeys, bad, 1)
        return dist.astype(F32), bad == 0

    dist_c, ok_c = dist_valid((rows, ctx), 0, ctx)
    dist_n, ok_n = dist_valid((rows, LANES), ctx, nq)
    zpad = jnp.zeros((LANES - nq, ATT_HEAD_DIM), F32)

    for h in range(ATT_KV_HEADS):
        hs = slice(h * ATT_HEAD_DIM, (h + 1) * ATT_HEAD_DIM)
        slope = jnp.where(gi == 0, slopes_ref[0, h], jnp.where(gi == 1, slopes_ref[1, h], slopes_ref[2, h]))
        qh = jnp.concatenate([qa_ref[:, hs], qb_ref[:, hs], qc_ref[:, hs]], axis=0).astype(BF16)
        kc = kc_ref[0, pl.ds(h, ctx, stride=ATT_KV_HEADS), :].astype(BF16)
        vc = vc_ref[0, pl.ds(h, ctx, stride=ATT_KV_HEADS), :].astype(BF16)
        kn = jnp.concatenate([kn_ref[:, hs], zpad], axis=0).astype(BF16)
        vn = jnp.concatenate([vn_ref[:, hs], zpad], axis=0).astype(BF16)
        sc = lax.dot_general(qh, kc, _NT, preferred_element_type=F32) * ATT_SCALE - slope * dist_c
        sn = lax.dot_general(qh, kn, _NT, preferred_element_type=F32) * ATT_SCALE - slope * dist_n
        sc = jnp.where(ok_c, sc, NEG)
        sn = jnp.where(ok_n, sn, NEG)
        mx = jnp.maximum(jnp.max(sc, axis=-1, keepdims=True), jnp.max(sn, axis=-1, keepdims=True))
        pc = jnp.exp(sc - mx)
        pn = jnp.exp(sn - mx)
        l = jnp.sum(pc, axis=-1, keepdims=True) + jnp.sum(pn, axis=-1, keepdims=True)
        o = (jnp.dot(pc.astype(BF16), vc, preferred_element_type=F32)
             + jnp.dot(pn.astype(BF16), vn, preferred_element_type=F32)) / l
        lse = mx + jnp.log(l)
        l0, l1, l2 = lse[0:nq], lse[nq:2 * nq], lse[2 * nq:3 * nq]
        m3 = jnp.maximum(jnp.maximum(l0, l1), l2)
        w0, w1, w2 = jnp.exp(l0 - m3), jnp.exp(l1 - m3), jnp.exp(l2 - m3)
        out = (w0 * o[0:nq] + w1 * o[nq:2 * nq] + w2 * o[2 * nq:3 * nq]) / (w0 + w1 + w2)
        o_ref[0, :, hs] = out.astype(o_ref.dtype)


def _attn_sample(p2, cache_k, cache_v, slopes, batch, nq, row0):
    ctx = cache_k.shape[1] // ATT_KV_HEADS
    assert row0 % nq == 0 and nq == SUBLANES and ctx >= max(w for w, _ in DILATION_PAIRS)
    rb0 = row0 // nq

    def col(idx):
        return pl.BlockSpec((nq, ATT_KV_WIDTH), lambda b: (rb0 + b, idx))

    cache_spec = pl.BlockSpec((1, ctx * ATT_KV_HEADS, ATT_HEAD_DIM), lambda b: (b, 0, 0))
    return pl.pallas_call(
        functools.partial(_attn_sample_kernel, ctx=ctx, nq=nq),
        grid=(batch,),
        in_specs=[pl.BlockSpec(memory_space=pltpu.SMEM),
                  col(0), col(1), col(2), col(3), col(4), cache_spec, cache_spec],
        out_specs=pl.BlockSpec((1, nq, ATT_KV_WIDTH), lambda b: (b, 0, 0)),
        out_shape=jax.ShapeDtypeStruct((batch, nq, ATT_KV_WIDTH), BF16),
        compiler_params=_cparams(("parallel",)),
        name="attn_sample",
    )(slopes, p2, p2, p2, p2, p2, cache_k, cache_v)


def _layer_norm(y, g, b):
    mu = jnp.mean(y, axis=-1, keepdims=True)
    yc = y - mu
    var = jnp.mean(yc * yc, axis=-1, keepdims=True)
    return yc * lax.rsqrt(var + LN_EPS) * g + b


def _merge_kernel(xp_ref, xs_ref, oap_ref, oas_ref, obp_ref, obs_ref, ga_ref, gb_ref, wa_ref, wb_ref,
                  wo_ref, g1_ref, b1_ref, wr0_ref, wr1_ref, br_ref,
                  h_ref, hp_ref, tope_ref, gate_ref, lrank_ref, cnt_ref, *, prompt_tiles):
    tm = xp_ref.shape[0]
    is_prompt = pl.program_id(0) < prompt_tiles
    x = jnp.where(is_prompt, xp_ref[...], xs_ref[...])
    oa = jnp.where(is_prompt, oap_ref[...], oas_ref[...])
    ob = jnp.where(is_prompt, obp_ref[...], obs_ref[...])
    a = jnp.dot(oa, wa_ref[...], preferred_element_type=F32)
    b = jnp.dot(ob, wb_ref[...], preferred_element_type=F32)
    merged = _sigmoid(ga_ref[...]) * a + _sigmoid(gb_ref[...]) * b
    z = jnp.dot(merged.astype(BF16), wo_ref[...], preferred_element_type=F32)
    hh = _layer_norm(DEEPNORM_ALPHA * x + z, g1_ref[...], b1_ref[...])
    h_ref[...] = hh
    hp_ref[...] = _pack_bf16_pairs(hh)

    h0 = hh.astype(BF16)
    h1 = (hh - h0.astype(F32)).astype(BF16)
    logits = (jnp.dot(h0, wr0_ref[...], preferred_element_type=F32)
              + jnp.dot(h1, wr0_ref[...], preferred_element_type=F32)
              + jnp.dot(h0, wr1_ref[...], preferred_element_type=F32)) + br_ref[...]

    lane = lax.broadcasted_iota(I32, (tm, N_EXPERTS), 1)
    work = logits
    vals, idxs = [], []
    onehot = jnp.zeros((tm, N_EXPERTS), F32)
    for _ in range(TOP_K):
        mx = jnp.max(work, axis=-1, keepdims=True)
        idx = jnp.min(jnp.where(work == mx, lane, N_EXPERTS), axis=-1, keepdims=True)
        hit = lane == idx
        vals.append(mx)
        idxs.append(idx)
        onehot = jnp.where(hit, 1.0, onehot)
        work = jnp.where(hit, -jnp.inf, work)
    ex = [jnp.exp(v - vals[0]) for v in vals]
    den = ex[0] + ex[1] + ex[2] + ex[3]

    r_tt = lax.broadcasted_iota(I32, (tm, tm), 0)
    c_tt = lax.broadcasted_iota(I32, (tm, tm), 1)
    before = jnp.where(r_tt > c_tt, 1.0, 0.0).astype(BF16)
    prefix = jnp.dot(before, onehot.astype(BF16), preferred_element_type=F32)
    for j in range(TOP_K):
        tope_ref[:, j:j + 1] = idxs[j]
        gate_ref[:, j:j + 1] = ex[j] / den
        lrank_ref[:, j:j + 1] = jnp.sum(jnp.where(lane == idxs[j], prefix, 0.0), axis=-1,
                                        keepdims=True).astype(I32)
    cnt_ref[0] = jnp.sum(onehot, axis=0, keepdims=True).astype(I32)


def _merge(xp, xs, oap, oas, obp, obs, p3, wa, wb, wo, g1, b1, wr0, wr1, br):
    tm = TOKEN_TILE
    assert xp.shape[0] % tm == 0 and xs.shape[0] % tm == 0
    ntp = xp.shape[0] // tm
    nt = ntp + xs.shape[0] // tm
    t = nt * tm

    def prow(width):
        return pl.BlockSpec((tm, width), lambda i: (jnp.minimum(i, ntp - 1), 0))

    def srow(width):
        return pl.BlockSpec((tm, width), lambda i: (jnp.maximum(i - ntp, 0), 0))

    def full(a):
        return pl.BlockSpec(a.shape, lambda i: (0,) * a.ndim)

    def rowblk(width, idx=0):
        return pl.BlockSpec((tm, width), lambda i: (i, idx))

    return pl.pallas_call(
        functools.partial(_merge_kernel, prompt_tiles=ntp),
        grid=(nt,),
        in_specs=[prow(D_MODEL), srow(D_MODEL), prow(HG_WIDTH), srow(HG_WIDTH),
                  prow(ATT_KV_WIDTH), srow(ATT_KV_WIDTH), rowblk(D_MODEL, 0),
                  rowblk(D_MODEL, 1), full(wa), full(wb), full(wo), full(g1), full(b1),
                  full(wr0), full(wr1), full(br)],
        out_specs=[rowblk(D_MODEL), rowblk(D_MODEL // 2), rowblk(TOP_K), rowblk(TOP_K), rowblk(TOP_K),
                   pl.BlockSpec((1, 1, N_EXPERTS), lambda i: (i, 0, 0))],
        out_shape=[jax.ShapeDtypeStruct((t, D_MODEL), F32),
                   jax.ShapeDtypeStruct((t, D_MODEL // 2), jnp.uint32),
                   jax.ShapeDtypeStruct((t, TOP_K), I32),
                   jax.ShapeDtypeStruct((t, TOP_K), F32),
                   jax.ShapeDtypeStruct((t, TOP_K), I32),
                   jax.ShapeDtypeStruct((nt, 1, N_EXPERTS), I32)],
        compiler_params=_cparams(("parallel",)),
        name="merge_ln1_router",
    )(xp, xs, oap, oas, obp, obs, p3, p3, wa, wb, wo, g1, b1, wr0, wr1, br)


def _pack_bf16_pairs(x):
    w = x.shape[1] // 2
    lo = lax.bitcast_convert_type(x[:, :w].astype(BF16).astype(F32), jnp.uint32)
    hi = lax.bitcast_convert_type(x[:, w:].astype(BF16).astype(F32), jnp.uint32)
    return (hi & jnp.uint32(0xFFFF0000)) | (lo >> 16)


def _unpack_bf16_pairs(u):
    lo = lax.bitcast_convert_type(u << 16, F32).astype(BF16)
    hi = lax.bitcast_convert_type(u & jnp.uint32(0xFFFF0000), F32).astype(BF16)
    return lo, hi


def _gather_kernel(tok_ref, h_ref, o_ref):
    m = o_ref.shape[0]

    def body(q, carry):
        for u in range(SUBLANES):
            r = q * SUBLANES + u
            o_ref[pl.ds(r, 1), :] = h_ref[pl.ds(tok_ref[0, 0, r], 1), :]
        return carry

    lax.fori_loop(0, m // SUBLANES, body, 0)


def _moe_gather(hp, tok_pad):
    t, w = hp.shape
    p = tok_pad.shape[0]
    m = MOE_BLOCK
    return pl.pallas_call(
        _gather_kernel,
        grid=(p // m,),
        in_specs=[pl.BlockSpec((1, 1, m), lambda i: (i, 0, 0), memory_space=pltpu.SMEM),
                  pl.BlockSpec((t, w), lambda i: (0, 0), pipeline_mode=pl.Buffered(1))],
        out_specs=pl.BlockSpec((m, w), lambda i: (i, 0)),
        out_shape=jax.ShapeDtypeStruct((p, w), hp.dtype),
        compiler_params=_cparams(("arbitrary",)),
        name="moe_gather",
    )(tok_pad.reshape(p // m, 1, m), hp)


def _moe_up_kernel(blk0_ref, nblk_ref, w0_ref, w1_ref, w2_ref, w3_ref, b_ref, xs_ref, act_ref,
                   wbf_scr, x_scr, o_scr, sem_in, sem_out):
    e = pl.program_id(0)
    f = pl.program_id(1)
    n = nblk_ref[e]
    b0 = blk0_ref[e]
    m = x_scr.shape[1]
    w_refs = (w0_ref, w1_ref, w2_ref, w3_ref)
    kq, ft = w0_ref.shape[1:]
    half = ft // 2

    def rows(b):
        return pl.ds(pl.multiple_of((b0 + b) * m, m), m)

    def x_copy(b, s):
        return pltpu.make_async_copy(xs_ref.at[rows(b)], x_scr.at[s], sem_in.at[s])

    def o_copy(b, s):
        return pltpu.make_async_copy(o_scr.at[s], act_ref.at[f, rows(b)], sem_out.at[s])

    @pl.when(n > 0)
    def _():
        x_copy(0, 0).start(priority=1)
        for q, w_ref in enumerate(w_refs):
            wbf_scr[q * kq:(q + 1) * kq, :] = w_ref[0].astype(BF16)
        even = (lax.broadcasted_iota(I32, (m, LANES), 1) & 1) == 0

        def body(b, carry):
            s = b & 1
            x_copy(b, s).wait()

            @pl.when(b + 1 < n)
            def _():
                x_copy(b + 1, 1 - s).start(priority=1)

            @pl.when(b >= 2)
            def _():
                o_copy(b - 2, s).wait()

            x_lo, x_hi = _unpack_bf16_pairs(x_scr[s])
            kh = x_lo.shape[1]
            for g in range(ft // (2 * LANES)):
                cols = slice(2 * g * LANES, (2 * g + 2) * LANES)
                h = (jnp.dot(x_lo, wbf_scr[:kh, cols], preferred_element_type=F32)
                     + jnp.dot(x_hi, wbf_scr[kh:, cols], preferred_element_type=F32)
                     + b_ref[0, :, cols])
                ha, hb = h[:, :LANES], h[:, LANES:]
                glu = jnp.minimum(jnp.where(even, ha, pltpu.roll(hb, 1, 1)), SWIGLU_LIMIT)
                lin = jnp.where(even, pltpu.roll(ha, LANES - 1, 1), hb)
                lin = jnp.clip(lin, -SWIGLU_LIMIT, SWIGLU_LIMIT) + 1.0
                act = glu * _sigmoid(SWIGLU_ALPHA * glu) * lin
                o_scr[s, :, g * LANES:(g + 1) * LANES] = act.astype(o_scr.dtype)
            o_copy(b, s).start(priority=1)
            return carry

        lax.fori_loop(0, n, body, 0)

        @pl.when(n >= 2)
        def _():
            o_copy(n - 2, n & 1).wait()
        o_copy(n - 1, (n - 1) & 1).wait()

    @pl.when(e == pl.num_programs(0) - 1)
    def _():
        o_scr[0] = jnp.zeros(o_scr.shape[1:], o_scr.dtype)

        nz = act_ref.shape[1] // m - b0
        lax.fori_loop(n, nz, lambda b, c: (o_copy(b, 0).start(), c)[1], 0)
        lax.fori_loop(n, nz, lambda b, c: (o_copy(b, 0).wait(), c)[1], 0)


def _moe_up(xs, w_gate_up, b_gate_up, blk0, nblk):
    p, w = xs.shape
    d = 2 * w
    m = MOE_BLOCK
    ft = MOE_FT
    ne = w_gate_up.shape[0]
    nf = w_gate_up.shape[2] // ft

    def wslice(q, e, f, b0, nb):
        return (e, q, f)

    grid_spec = pltpu.PrefetchScalarGridSpec(
        num_scalar_prefetch=2,
        grid=(ne, nf),
        in_specs=[pl.BlockSpec((1, d // W_STREAMS, ft), functools.partial(wslice, q))
                  for q in range(W_STREAMS)] + [
                  pl.BlockSpec((1, 1, ft), lambda e, f, b0, nb: (e, 0, f)),
                  pl.BlockSpec(memory_space=pl.ANY)],
        out_specs=pl.BlockSpec(memory_space=pl.ANY),
        scratch_shapes=[pltpu.VMEM((d, ft), BF16),
                        pltpu.VMEM((2, m, w), xs.dtype),
                        pltpu.VMEM((2, m, ft // 2), BF16),
                        pltpu.SemaphoreType.DMA((2,)),
                        pltpu.SemaphoreType.DMA((2,))],
    )
    return pl.pallas_call(
        _moe_up_kernel,
        grid_spec=grid_spec,
        out_shape=jax.ShapeDtypeStruct((nf, p, ft // 2), BF16),
        compiler_params=_cparams(("arbitrary", "arbitrary")),
        name="moe_up",
    )(blk0, nblk, *([w_gate_up] * W_STREAMS), b_gate_up, xs)


def _hidden_perm():
    r = lax.broadcasted_iota(I32, (LANES, LANES), 0)
    c = lax.broadcasted_iota(I32, (LANES, LANES), 1)
    src = (r >> 1) + (r & 1) * (LANES // 2)
    return jnp.where(c == src, 1.0, 0.0).astype(BF16)


def _moe_down_kernel(be_ref, na_ref, nreal_ref, dest_ref, a_ref, wd0_ref, wd1_ref, wd2_ref, wd3_ref,
                     bd_ref, yt_ref, wbf_scr, y_scr, sem):
    i = pl.program_id(0)
    nb = pl.num_programs(0)
    nf, m, fh = a_ref.shape
    slot = i % 2

    def row_copy(s, r, d):
        return pltpu.make_async_copy(y_scr.at[s, pl.ds(r, 1)], yt_ref.at[pl.ds(d, 1)], sem.at[s])

    def drain(s, count):
        @pl.when(count == m)
        def _():
            pltpu.make_async_copy(y_scr.at[s], yt_ref.at[pl.ds(0, m)], sem.at[s]).wait()

        @pl.when(count != m)
        def _():
            def wbody(_, carry):
                row_copy(s, 0, 0).wait()
                return carry
            lax.fori_loop(0, count, wbody, 0)

    @pl.when(i >= 2)
    def _():
        drain(slot, nreal_ref[i - 2])

    changed = jnp.where(i == 0, 1, be_ref[i] - be_ref[jnp.maximum(i - 1, 0)])

    @pl.when(changed != 0)
    def _():
        perm = _hidden_perm()
        for q, wd_ref in enumerate((wd0_ref, wd1_ref, wd2_ref, wd3_ref)):
            kq = wd_ref.shape[1]
            for g in range(kq // LANES):
                rows = slice(g * LANES, (g + 1) * LANES)
                wbf_scr[q * kq + g * LANES:q * kq + (g + 1) * LANES, :] = jnp.dot(
                    perm, wd_ref[0, rows, :].astype(BF16), preferred_element_type=F32).astype(BF16)

    @pl.when(i < na_ref[0])
    def _():
        y = bd_ref[0]
        for j in range(nf):
            y = y + jnp.dot(a_ref[j], wbf_scr[j * fh:(j + 1) * fh, :], preferred_element_type=F32)
        y_scr[slot] = y

        n = nreal_ref[i]
        unroll = 8

        def sbody8(q, carry):
            for u in range(unroll):
                r = q * unroll + u
                row_copy(slot, r, dest_ref[0, 0, r]).start()
            return carry

        def sbody1(r, carry):
            row_copy(slot, r, dest_ref[0, 0, r]).start()
            return carry

        lax.fori_loop(0, n // unroll, sbody8, 0)
        lax.fori_loop((n // unroll) * unroll, n, sbody1, 0)

    @pl.when(i == nb - 1)
    def _():
        @pl.when(i >= 1)
        def _():
            drain(1 - slot, nreal_ref[i - 1])
        drain(slot, nreal_ref[i])


def _moe_down(act, w_down, b_down, blk_expert, n_active, nreal, dest, n_rows):
    nf, p, fh = act.shape
    f = nf * fh
    m = MOE_BLOCK
    d = w_down.shape[2]

    def wslice(q, i, be, na, nr):
        return (be[i], q, 0)

    grid_spec = pltpu.PrefetchScalarGridSpec(
        num_scalar_prefetch=3,
        grid=(p // m,),
        in_specs=[pl.BlockSpec((1, 1, m), lambda i, be, na, nr: (i, 0, 0), memory_space=pltpu.SMEM),
                  pl.BlockSpec((nf, m, fh), lambda i, be, na, nr: (0, i, 0)),
                  *[pl.BlockSpec((1, f // W_STREAMS, d), functools.partial(wslice, q))
                    for q in range(W_STREAMS)],
                  pl.BlockSpec((1, 1, d), lambda i, be, na, nr: (be[i], 0, 0))],
        out_specs=pl.BlockSpec(memory_space=pl.ANY),
        scratch_shapes=[pltpu.VMEM((f, d), BF16),
                        pltpu.VMEM((2, m, d), F32),
                        pltpu.SemaphoreType.DMA((2,))],
    )
    return pl.pallas_call(
        _moe_down_kernel,
        grid_spec=grid_spec,
        out_shape=jax.ShapeDtypeStruct((n_rows, d), F32),
        compiler_params=_cparams(("arbitrary",)),
        name="moe_down",
    )(blk_expert, n_active, nreal, dest.reshape(p // m, 1, m), act, *([w_down] * W_STREAMS), b_down)


def _combine_kernel(h_ref, y0_ref, y1_ref, y2_ref, y3_ref, gate_ref, g2_ref, b2_ref, o_ref):
    ffn = gate_ref[:, 0:1] * y0_ref[...]
    for j, y_ref in enumerate((y1_ref, y2_ref, y3_ref), start=1):
        ffn = ffn + gate_ref[:, j:j + 1] * y_ref[...]
    o_ref[...] = _layer_norm(DEEPNORM_ALPHA * h_ref[...] + ffn, g2_ref[...], b2_ref[...])


def _combine(h, yt, gate, g2, b2):
    t, d = h.shape
    tm = TOKEN_TILE
    nt = t // tm

    def choice(j):
        return pl.BlockSpec((tm, d), lambda i: (j * nt + i, 0))

    return pl.pallas_call(
        _combine_kernel,
        grid=(nt,),
        in_specs=[pl.BlockSpec((tm, d), lambda i: (i, 0)),
                  choice(0), choice(1), choice(2), choice(3),
                  pl.BlockSpec((tm, TOP_K), lambda i: (i, 0)),
                  pl.BlockSpec((1, d), lambda i: (0, 0)),
                  pl.BlockSpec((1, d), lambda i: (0, 0))],
        out_specs=pl.BlockSpec((tm, d), lambda i: (i, 0)),
        out_shape=jax.ShapeDtypeStruct((t, d), F32),
        compiler_params=_cparams(("parallel",)),
        name="combine_ln2",
    )(h, yt, yt, yt, yt, gate, g2, b2)


def _route(top_e, lrank, tilecnt):
    t = top_e.shape[0]
    m = MOE_BLOCK
    n_assign = t * TOP_K
    n_blocks = n_assign // m + N_EXPERTS
    p = n_blocks * m
    tc = tilecnt.reshape(-1, N_EXPERTS)
    base = jnp.cumsum(tc, axis=0) - tc
    counts = jnp.sum(tc, axis=0)
    padded = (counts + m - 1) // m * m
    pend = jnp.cumsum(padded)
    pstart = pend - padded
    base_tok = jnp.repeat(base, TOKEN_TILE, axis=0)
    rank = lrank + jnp.take_along_axis(base_tok, top_e, axis=1)
    slot = (pstart[top_e] + rank).reshape(-1)
    a_idx = jnp.arange(n_assign, dtype=I32)
    dest = jnp.full((p,), -1, I32).at[slot].set((a_idx % TOP_K) * t + a_idx // TOP_K,
                                               unique_indices=True)
    tok_pad = jnp.where(dest >= 0, dest % t, 0)
    blk_start = jnp.arange(n_blocks, dtype=I32) * m
    blk_expert = jnp.minimum(jnp.sum((pend[None, :] <= blk_start[:, None]).astype(I32), axis=1),
                             N_EXPERTS - 1)
    n_active = (pend[-1:] // m).astype(I32)
    real_end = (pstart + counts)[blk_expert]
    nreal = jnp.where(blk_start < pend[-1], jnp.clip(real_end - blk_start, 0, m), 0).astype(I32)
    return tok_pad, dest, blk_expert, n_active, nreal, (pstart // m).astype(I32), (padded // m).astype(I32)


def kernel(x_prompt, x_sample, cache_attn_k, cache_attn_v, state_hgrn, w_in, hgrn_lb_logits,
           hgrn_norm_w, w_branch_a, w_branch_b, w_out, ln1_g, ln1_b, w_router, b_router,
           w_gate_up, b_gate_up, w_down, b_down, ln2_g, ln2_b):
    assert w_in.shape[0] == DEPTH == 1
    bp, lp, d = x_prompt.shape
    bs, ls, _ = x_sample.shape
    tp, ts = bp * lp, bs * ls
    t = tp + ts
    keep = min(max(w for w, _ in DILATION_PAIRS), lp)

    xp2, xs2 = x_prompt.reshape(tp, d), x_sample.reshape(ts, d)
    xb = jnp.concatenate([xp2.astype(BF16), xs2.astype(BF16)], axis=0)
    wi = w_in[0]
    c1, c2 = P1_WIDTH, P1_WIDTH + P2_WIDTH
    p1 = _matmul(xb, wi[:, :c1].astype(BF16), 768, 1024, F32, "inproj_hgrn")
    p2 = _matmul(xb, wi[:, c1:c2].astype(BF16), 768, 1280, F32, "inproj_attn")
    p3 = _matmul(xb, wi[:, c2:].astype(BF16), 768, 1024, F32, "inproj_gate")

    oa_p, st_p = _hgrn(p1, hgrn_lb_logits, hgrn_norm_w[0], None, bp, lp, 0)
    oa_s, st_s = _hgrn(p1, hgrn_lb_logits, hgrn_norm_w[0],
                       state_hgrn.reshape(bs, HG_HEADS, HG_DIM, HG_DIM), bs, ls, tp)

    n = jnp.arange(1, ATT_Q_HEADS + 1, dtype=F32)
    slopes = jnp.power(2.0, -8.0 * n / ATT_Q_HEADS).reshape(N_GROUPS, ATT_KV_HEADS)
    ctx = cache_attn_k.shape[2]
    ob_p = _attn_prompt(p2, slopes, bp, lp)
    ob_s = _attn_sample(p2, cache_attn_k.reshape(bs, ctx * ATT_KV_HEADS, ATT_HEAD_DIM),
                        cache_attn_v.reshape(bs, ctx * ATT_KV_HEADS, ATT_HEAD_DIM), slopes, bs, ls, tp)

    wr = w_router[0]
    wr0 = wr.astype(BF16)
    wr1 = (wr - wr0.astype(F32)).astype(BF16)
    h, hp, top_e, gate, lrank, tilecnt = _merge(
        xp2, xs2, oa_p.reshape(tp, HG_WIDTH), oa_s.reshape(ts, HG_WIDTH), ob_p,
        ob_s.reshape(ts, ATT_KV_WIDTH), p3, w_branch_a[0].astype(BF16), w_branch_b[0].astype(BF16),
        w_out[0].astype(BF16), ln1_g, ln1_b, wr0, wr1, b_router)

    tok_pad, dest, blk_expert, n_active, nreal, blk0, nblk = _route(top_e, lrank, tilecnt)
    xs = _moe_gather(hp, tok_pad)
    act = _moe_up(xs, w_gate_up.reshape(N_EXPERTS, d, 2 * D_FF),
                  b_gate_up.reshape(N_EXPERTS, 1, 2 * D_FF), blk0, nblk)
    yt = _moe_down(act, w_down.reshape(N_EXPERTS, D_FF, d), b_down.reshape(N_EXPERTS, 1, d),
                   blk_expert, n_active, nreal, dest, t * TOP_K)
    y = _combine(h, yt, gate, ln2_g, ln2_b)

    k_new = p2[:, ATT_Q_HEADS * ATT_HEAD_DIM:(ATT_Q_HEADS + ATT_KV_HEADS) * ATT_HEAD_DIM]
    v_new = p2[:, (ATT_Q_HEADS + ATT_KV_HEADS) * ATT_HEAD_DIM:]

    def kv_prompt(a):
        return a[:tp].reshape(bp, lp, ATT_KV_HEADS, ATT_HEAD_DIM)[:, lp - keep:][None]

    def kv_sample(a):
        return a[tp:].reshape(bs, ls, ATT_KV_HEADS, ATT_HEAD_DIM)[None]

    return (y[:tp].reshape(bp, lp, d), y[tp:].reshape(bs, ls, d),
            kv_prompt(k_new), kv_prompt(v_new), st_p[None],
            kv_sample(k_new), kv_sample(v_new), st_s[None])
```

```python
import functools
import math

import jax
import jax.numpy as jnp
from jax import lax
from jax.experimental import pallas as pl
from jax.experimental.pallas import tpu as pltpu

F32 = jnp.float32
BF16 = jnp.bfloat16
I32 = jnp.int32

D_MODEL = 2048
DEPTH = 1
HG_HEADS = 8
HG_DIM = 128
HG_WIDTH = HG_HEADS * HG_DIM
HG_CHUNK = 64
ATT_HEAD_DIM = 128
DILATION_PAIRS = ((128, 1), (512, 4), (2048, 16))
N_GROUPS = len(DILATION_PAIRS)
ATT_KV_HEADS = 4
ATT_Q_HEADS = N_GROUPS * ATT_KV_HEADS
ATT_TAPS = 128
ATT_KV_WIDTH = ATT_KV_HEADS * ATT_HEAD_DIM
ATT_SCALE = ATT_HEAD_DIM ** -0.5
N_EXPERTS = 32
TOP_K = 4
D_FF = D_MODEL
SWIGLU_ALPHA = 1.702
SWIGLU_LIMIT = 7.0
LN_EPS = 1e-5
RMS_EPS = 1e-5
DEEPNORM_ALPHA = (2 * DEPTH) ** 0.25

P1_WIDTH = 4 * HG_WIDTH
P2_WIDTH = (ATT_Q_HEADS + 2 * ATT_KV_HEADS) * ATT_HEAD_DIM
P3_WIDTH = 2 * D_MODEL

LANES = 128
SUBLANES = 8
VMEM_LIMIT = 60 * 1024 * 1024

TOKEN_TILE = 256
MOE_BLOCK = 256
MOE_FT = 2048
W_STREAMS = 4
NEG = -1e30

_NT = (((1,), (1,)), ((), ()))
_TN = (((0,), (0,)), ((), ()))


def _cparams(sem):
    return pltpu.CompilerParams(dimension_semantics=sem, vmem_limit_bytes=VMEM_LIMIT)


def _sigmoid(x):
    return 1.0 / (1.0 + jnp.exp(-x))


def _mm_kernel(x_ref, w_ref, o_ref):
    o_ref[...] = jnp.dot(x_ref[...], w_ref[...], preferred_element_type=F32).astype(o_ref.dtype)


def _matmul(x, w, tm, tn, out_dtype, name):
    m, k = x.shape
    n = w.shape[1]
    assert m % tm == 0 and n % tn == 0
    return pl.pallas_call(
        _mm_kernel,
        grid=(n // tn, m // tm),
        in_specs=[pl.BlockSpec((tm, k), lambda j, i: (i, 0)),
                  pl.BlockSpec((k, tn), lambda j, i: (0, j))],
        out_specs=pl.BlockSpec((tm, tn), lambda j, i: (i, j)),
        out_shape=jax.ShapeDtypeStruct((m, n), out_dtype),
        compiler_params=_cparams(("parallel", "parallel")),
        name=name,
    )(x, w)


def _level_ref(g_scr, m, sl, chunk, sub):
    pieces = []
    for j in range(chunk // SUBLANES):
        if 2 * m >= SUBLANES:
            row = ((SUBLANES * j) // (2 * m)) * (2 * m) + m - 1
            pieces.append(jnp.broadcast_to(g_scr[row:row + 1, sl], (SUBLANES, LANES)))
        else:
            acc = None
            for u in range(SUBLANES // (2 * m)):
                row = SUBLANES * j + 2 * m * u + m - 1
                b = jnp.broadcast_to(g_scr[row:row + 1, sl], (SUBLANES, LANES))
                acc = b if acc is None else jnp.where(sub // (2 * m) == u, b, acc)
            pieces.append(acc)
    return pieces[0] if len(pieces) == 1 else jnp.concatenate(pieces, axis=0)


def _hgrn_kernel(lbl_ref, nw_ref, hq_ref, hf_ref, hi_ref, hg_ref, *rest, chunk, has_s0):
    if has_s0:
        s0_ref, o_ref, sout_ref, st_scr, g_scr = rest
    else:
        o_ref, sout_ref, st_scr, g_scr = rest
    c = pl.program_id(1)
    nc = pl.num_programs(1)

    @pl.when(c == 0)
    def _init():
        for h in range(HG_HEADS):
            if has_s0:
                st_scr[h] = s0_ref[0, h].T
            else:
                st_scr[h] = jnp.zeros((HG_DIM, HG_DIM), F32)

    lbl = lbl_ref[...]
    ex = jnp.exp(lbl - jnp.max(lbl, axis=0, keepdims=True))
    lb = ex[0:1] / jnp.sum(ex, axis=0, keepdims=True)

    f = lb + (1.0 - lb) * _sigmoid(hf_ref[...])
    logf = jnp.log(f)
    kall = 1.0 - f

    r_cc = lax.broadcasted_iota(I32, (chunk, chunk), 0)
    c_cc = lax.broadcasted_iota(I32, (chunk, chunk), 1)
    tri = jnp.where(r_cc >= c_cc, 1.0, 0.0).astype(BF16)
    p0 = logf.astype(BF16)
    r1 = logf - p0.astype(F32)
    p1 = r1.astype(BF16)
    p2 = (r1 - p1.astype(F32)).astype(BF16)
    g_all = (jnp.dot(tri, p0, preferred_element_type=F32)
             + jnp.dot(tri, p1, preferred_element_type=F32)
             + jnp.dot(tri, p2, preferred_element_type=F32))
    g_scr[...] = g_all

    sub = lax.broadcasted_iota(I32, (SUBLANES, LANES), 0)
    row_c = lax.broadcasted_iota(I32, (chunk, LANES), 0)
    nw = nw_ref[...]

    for h in range(HG_HEADS):
        sl = slice(h * HG_DIM, (h + 1) * HG_DIM)
        hq = hq_ref[:, sl]
        q = hq * _sigmoid(hq)
        k = kall[:, sl]
        g = g_all[:, sl]
        v = hi_ref[:, sl]
        vb = v.astype(BF16)

        p = lax.dot_general(q.astype(BF16), k.astype(BF16), _NT, preferred_element_type=F32)
        a = jnp.where(r_cc == c_cc, p, 0.0)
        m = 1
        while m < chunk:
            ref = _level_ref(g_scr, m, sl, chunk, sub)
            e = jnp.exp(-jnp.abs(g - ref))
            odd = ((row_c // m) & 1) == 1
            x = (jnp.where(odd, q, k) * e).astype(BF16)
            p = lax.dot_general(x, x, _NT, preferred_element_type=F32)
            pair = (r_cc // (2 * m)) == (c_cc // (2 * m))
            sel = jnp.where(pair, ((r_cc // m) & 1) - ((c_cc // m) & 1), 0) == 1
            a = jnp.where(sel, p, a)
            m *= 2

        st = st_scr[h]
        o = jnp.dot(a.astype(BF16), vb, preferred_element_type=F32)
        qe = (q * jnp.exp(g)).astype(BF16)
        o = o + lax.dot_general(qe, st.astype(BF16), _NT, preferred_element_type=F32)

        ms = jnp.mean(o * o, axis=-1, keepdims=True)
        hg = hg_ref[:, sl]
        out = o * lax.rsqrt(ms + RMS_EPS) * nw * (hg * _sigmoid(hg))
        o_ref[0, :, sl] = out.astype(o_ref.dtype)

        g_last = g[chunk - 1:chunk, :]
        kd = (k * jnp.exp(g_last - g)).astype(BF16)
        st_scr[h] = st * jnp.exp(g_last) + lax.dot_general(vb, kd, _TN, preferred_element_type=F32)

    @pl.when(c == nc - 1)
    def _fin():
        for h in range(HG_HEADS):
            sout_ref[0, h] = st_scr[h].T


def _hgrn(p1, lb_logits, norm_w, s0, batch, seq, row0):
    chunk = math.gcd(seq, HG_CHUNK)
    nchunk = seq // chunk
    assert row0 % chunk == 0
    rb0 = row0 // chunk

    def col(idx):
        return pl.BlockSpec((chunk, HG_WIDTH), lambda b, c: (rb0 + b * nchunk + c, idx))

    in_specs = [pl.BlockSpec(lb_logits.shape, lambda b, c: (0, 0)),
                pl.BlockSpec((1, HG_DIM), lambda b, c: (0, 0)),
                col(0), col(1), col(2), col(3)]
    args = [lb_logits, norm_w.reshape(1, HG_DIM), p1, p1, p1, p1]
    if s0 is not None:
        in_specs.append(pl.BlockSpec((1, HG_HEADS, HG_DIM, HG_DIM), lambda b, c: (b, 0, 0, 0)))
        args.append(s0)
    return pl.pallas_call(
        functools.partial(_hgrn_kernel, chunk=chunk, has_s0=s0 is not None),
        grid=(batch, nchunk),
        in_specs=in_specs,
        out_specs=[pl.BlockSpec((1, chunk, HG_WIDTH), lambda b, c: (b, c, 0)),
                   pl.BlockSpec((1, HG_HEADS, HG_DIM, HG_DIM), lambda b, c: (b, 0, 0, 0))],
        out_shape=[jax.ShapeDtypeStruct((batch, seq, HG_WIDTH), BF16),
                   jax.ShapeDtypeStruct((batch, HG_HEADS, HG_DIM, HG_DIM), F32)],
        scratch_shapes=[pltpu.VMEM((HG_HEADS, HG_DIM, HG_DIM), F32),
                        pltpu.VMEM((chunk, HG_WIDTH), F32)],
        compiler_params=_cparams(("parallel", "arbitrary")),
        name="hgrn_seq%d" % seq,
    )(*args)


QB = 128
ATTN_UNROLL = 8


def _attn_prompt_kernel(slopes_ref, q0_ref, q1_ref, q2_ref, k_ref, v_ref, o_ref,
                        og_scr, lse_scr, bias_scr, *, seq):
    h = pl.program_id(1)
    il = lax.broadcasted_iota(I32, (QB, 2 * QB), 0)
    jl = lax.broadcasted_iota(I32, (QB, 2 * QB), 1)
    delta = il + QB - jl
    band = jnp.where(delta >= 0, delta, ATT_TAPS + 1) <= ATT_TAPS
    prev_half = jl < QB
    q_refs = (q0_ref, q1_ref, q2_ref)

    for g, (window, dil) in enumerate(DILATION_PAIRS):
        assert window // dil == ATT_TAPS
        slope = slopes_ref[g, h]
        bias_scr[g] = jnp.where(band, (-slope * dil) * delta.astype(F32), NEG)
        nbr = (seq // dil) // QB
        q_ref = q_refs[g]

        def rows(start, dil=dil):
            return pl.ds(start, QB) if dil == 1 else pl.ds(start, QB, stride=dil)

        def body(ib, carry, g=g, dil=dil, nbr=nbr, q_ref=q_ref, rows=rows):
            res = ib // nbr
            jb = ib - res * nbr
            qs = res + dil * (jb * QB)
            ps = res + dil * jnp.maximum(jb * QB - QB, 0)
            qv = q_ref[rows(qs), :].astype(BF16)
            kb = jnp.concatenate([k_ref[rows(ps), :], k_ref[rows(qs), :]], axis=0).astype(BF16)
            vb = jnp.concatenate([v_ref[rows(ps), :], v_ref[rows(qs), :]], axis=0).astype(BF16)
            s = lax.dot_general(qv, kb, _NT, preferred_element_type=F32) * ATT_SCALE + bias_scr[g]
            s = jnp.where(jnp.where(prev_half, jb, 1) == 0, NEG, s)
            mx = jnp.max(s, axis=-1, keepdims=True)
            p = jnp.exp(s - mx)
            l = jnp.sum(p, axis=-1, keepdims=True)
            o = jnp.dot(p.astype(BF16), vb, preferred_element_type=F32) / l
            og_scr[g, rows(qs), :] = o
            lse_scr[g, rows(qs), :] = jnp.broadcast_to(mx + jnp.log(l), (QB, LANES))
            return carry

        lax.fori_loop(0, seq // QB, body, 0, unroll=ATTN_UNROLL)

    def merge(tb, carry):
        r = pl.ds(pl.multiple_of(tb * QB, QB), QB)
        l0, l1, l2 = lse_scr[0, r, :], lse_scr[1, r, :], lse_scr[2, r, :]
        mx = jnp.maximum(jnp.maximum(l0, l1), l2)
        w0, w1, w2 = jnp.exp(l0 - mx), jnp.exp(l1 - mx), jnp.exp(l2 - mx)
        o = (w0 * og_scr[0, r, :] + w1 * og_scr[1, r, :] + w2 * og_scr[2, r, :]) / (w0 + w1 + w2)
        o_ref[r, :] = o.astype(o_ref.dtype)
        return carry

    lax.fori_loop(0, seq // QB, merge, 0, unroll=ATTN_UNROLL)


def _attn_prompt(p2, slopes, batch, seq):
    assert seq % (QB * max(d for _, d in DILATION_PAIRS)) == 0

    def col(fn):
        return pl.BlockSpec((seq, ATT_HEAD_DIM), lambda b, h: (b, fn(h)))

    return pl.pallas_call(
        functools.partial(_attn_prompt_kernel, seq=seq),
        grid=(batch, ATT_KV_HEADS),
        in_specs=[pl.BlockSpec(memory_space=pltpu.SMEM),
                  col(lambda h: h), col(lambda h: ATT_KV_HEADS + h), col(lambda h: 2 * ATT_KV_HEADS + h),
                  col(lambda h: ATT_Q_HEADS + h), col(lambda h: ATT_Q_HEADS + ATT_KV_HEADS + h)],
        out_specs=pl.BlockSpec((seq, ATT_HEAD_DIM), lambda b, h: (b, h)),
        out_shape=jax.ShapeDtypeStruct((batch * seq, ATT_KV_WIDTH), BF16),
        scratch_shapes=[pltpu.VMEM((N_GROUPS, seq, ATT_HEAD_DIM), F32),
                        pltpu.VMEM((N_GROUPS, seq, LANES), F32),
                        pltpu.VMEM((N_GROUPS, QB, 2 * QB), F32)],
        compiler_params=_cparams(("parallel", "parallel")),
        name="attn_prompt",
    )(slopes, p2, p2, p2, p2, p2)


def _attn_sample_kernel(slopes_ref, qa_ref, qb_ref, qc_ref, kn_ref, vn_ref, kc_ref, vc_ref, o_ref,
                        *, ctx, nq):
    rows = N_GROUPS * nq
    gi = lax.broadcasted_iota(I32, (rows, 1), 0) // nq
    dil = jnp.where(gi == 0, DILATION_PAIRS[0][1], jnp.where(gi == 1, DILATION_PAIRS[1][1], DILATION_PAIRS[2][1]))
    win = jnp.where(gi == 0, DILATION_PAIRS[0][0], jnp.where(gi == 1, DILATION_PAIRS[1][0], DILATION_PAIRS[2][0]))

    def dist_valid(shape, key0, nkeys):
        r = lax.broadcasted_iota(I32, shape, 0)
        t = r - (r // nq) * nq
        col = lax.broadcasted_iota(I32, shape, 1)
        dist = ctx + t - (col + key0)
        bad = jnp.where(dist >= 0, dist & (dil - 1), 1)
        bad = jnp.where(dist <= win, bad, 1)
        bad = jnp.where(col < nkeys, bad, 1)
        return dist.astype(F32), bad == 0

    dist_c, ok_c = dist_valid((rows, ctx), 0, ctx)
    dist_n, ok_n = dist_valid((rows, LANES), ctx, nq)
    zpad = jnp.zeros((LANES - nq, ATT_HEAD_DIM), F32)

    for h in range(ATT_KV_HEADS):
        hs = slice(h * ATT_HEAD_DIM, (h + 1) * ATT_HEAD_DIM)
        slope = jnp.where(gi == 0, slopes_ref[0, h], jnp.where(gi == 1, slopes_ref[1, h], slopes_ref[2, h]))
        qh = jnp.concatenate([qa_ref[:, hs], qb_ref[:, hs], qc_ref[:, hs]], axis=0).astype(BF16)
        kc = kc_ref[0, pl.ds(h, ctx, stride=ATT_KV_HEADS), :].astype(BF16)
        vc = vc_ref[0, pl.ds(h, ctx, stride=ATT_KV_HEADS), :].astype(BF16)
        kn = jnp.concatenate([kn_ref[:, hs], zpad], axis=0).astype(BF16)
        vn = jnp.concatenate([vn_ref[:, hs], zpad], axis=0).astype(BF16)
        sc = lax.dot_general(qh, kc, _NT, preferred_element_type=F32) * ATT_SCALE - slope * dist_c
        sn = lax.dot_general(qh, kn, _NT, preferred_element_type=F32) * ATT_SCALE - slope * dist_n
        sc = jnp.where(ok_c, sc, NEG)
        sn = jnp.where(ok_n, sn, NEG)
        mx = jnp.maximum(jnp.max(sc, axis=-1, keepdims=True), jnp.max(sn, axis=-1, keepdims=True))
        pc = jnp.exp(sc - mx)
        pn = jnp.exp(sn - mx)
        l = jnp.sum(pc, axis=-1, keepdims=True) + jnp.sum(pn, axis=-1, keepdims=True)
        o = (jnp.dot(pc.astype(BF16), vc, preferred_element_type=F32)
             + jnp.dot(pn.astype(BF16), vn, preferred_element_type=F32)) / l
        lse = mx + jnp.log(l)
        l0, l1, l2 = lse[0:nq], lse[nq:2 * nq], lse[2 * nq:3 * nq]
        m3 = jnp.maximum(jnp.maximum(l0, l1), l2)
        w0, w1, w2 = jnp.exp(l0 - m3), jnp.exp(l1 - m3), jnp.exp(l2 - m3)
        out = (w0 * o[0:nq] + w1 * o[nq:2 * nq] + w2 * o[2 * nq:3 * nq]) / (w0 + w1 + w2)
        o_ref[0, :, hs] = out.astype(o_ref.dtype)


def _attn_sample(p2, cache_k, cache_v, slopes, batch, nq, row0):
    ctx = cache_k.shape[1] // ATT_KV_HEADS
    assert row0 % nq == 0 and nq == SUBLANES and ctx >= max(w for w, _ in DILATION_PAIRS)
    rb0 = row0 // nq

    def col(idx):
        return pl.BlockSpec((nq, ATT_KV_WIDTH), lambda b: (rb0 + b, idx))

    cache_spec = pl.BlockSpec((1, ctx * ATT_KV_HEADS, ATT_HEAD_DIM), lambda b: (b, 0, 0))
    return pl.pallas_call(
        functools.partial(_attn_sample_kernel, ctx=ctx, nq=nq),
        grid=(batch,),
        in_specs=[pl.BlockSpec(memory_space=pltpu.SMEM),
                  col(0), col(1), col(2), col(3), col(4), cache_spec, cache_spec],
        out_specs=pl.BlockSpec((1, nq, ATT_KV_WIDTH), lambda b: (b, 0, 0)),
        out_shape=jax.ShapeDtypeStruct((batch, nq, ATT_KV_WIDTH), BF16),
        compiler_params=_cparams(("parallel",)),
        name="attn_sample",
    )(slopes, p2, p2, p2, p2, p2, cache_k, cache_v)


def _layer_norm(y, g, b):
    mu = jnp.mean(y, axis=-1, keepdims=True)
    yc = y - mu
    var = jnp.mean(yc * yc, axis=-1, keepdims=True)
    return yc * lax.rsqrt(var + LN_EPS) * g + b


def _merge_kernel(xp_ref, xs_ref, oap_ref, oas_ref, obp_ref, obs_ref, ga_ref, gb_ref, wa_ref, wb_ref,
                  wo_ref, g1_ref, b1_ref, wr0_ref, wr1_ref, br_ref,
                  h_ref, hp_ref, tope_ref, gate_ref, lrank_ref, cnt_ref, *, prompt_tiles):
    tm = xp_ref.shape[0]
    is_prompt = pl.program_id(0) < prompt_tiles
    x = jnp.where(is_prompt, xp_ref[...], xs_ref[...])
    oa = jnp.where(is_prompt, oap_ref[...], oas_ref[...])
    ob = jnp.where(is_prompt, obp_ref[...], obs_ref[...])
    a = jnp.dot(oa, wa_ref[...], preferred_element_type=F32)
    b = jnp.dot(ob, wb_ref[...], preferred_element_type=F32)
    merged = _sigmoid(ga_ref[...]) * a + _sigmoid(gb_ref[...]) * b
    z = jnp.dot(merged.astype(BF16), wo_ref[...], preferred_element_type=F32)
    hh = _layer_norm(DEEPNORM_ALPHA * x + z, g1_ref[...], b1_ref[...])
    h_ref[...] = hh
    hp_ref[...] = _pack_bf16_pairs(hh)

    h0 = hh.astype(BF16)
    h1 = (hh - h0.astype(F32)).astype(BF16)
    logits = (jnp.dot(h0, wr0_ref[...], preferred_element_type=F32)
              + jnp.dot(h1, wr0_ref[...], preferred_element_type=F32)
              + jnp.dot(h0, wr1_ref[...], preferred_element_type=F32)) + br_ref[...]

    lane = lax.broadcasted_iota(I32, (tm, N_EXPERTS), 1)
    work = logits
    vals, idxs = [], []
    onehot = jnp.zeros((tm, N_EXPERTS), F32)
    for _ in range(TOP_K):
        mx = jnp.max(work, axis=-1, keepdims=True)
        idx = jnp.min(jnp.where(work == mx, lane, N_EXPERTS), axis=-1, keepdims=True)
        hit = lane == idx
        vals.append(mx)
        idxs.append(idx)
        onehot = jnp.where(hit, 1.0, onehot)
        work = jnp.where(hit, -jnp.inf, work)
    ex = [jnp.exp(v - vals[0]) for v in vals]
    den = ex[0] + ex[1] + ex[2] + ex[3]

    r_tt = lax.broadcasted_iota(I32, (tm, tm), 0)
    c_tt = lax.broadcasted_iota(I32, (tm, tm), 1)
    before = jnp.where(r_tt > c_tt, 1.0, 0.0).astype(BF16)
    prefix = jnp.dot(before, onehot.astype(BF16), preferred_element_type=F32)
    for j in range(TOP_K):
        tope_ref[:, j:j + 1] = idxs[j]
        gate_ref[:, j:j + 1] = ex[j] / den
        lrank_ref[:, j:j + 1] = jnp.sum(jnp.where(lane == idxs[j], prefix, 0.0), axis=-1,
                                        keepdims=True).astype(I32)
    cnt_ref[0] = jnp.sum(onehot, axis=0, keepdims=True).astype(I32)


def _merge(xp, xs, oap, oas, obp, obs, p3, wa, wb, wo, g1, b1, wr0, wr1, br):
    tm = TOKEN_TILE
    assert xp.shape[0] % tm == 0 and xs.shape[0] % tm == 0
    ntp = xp.shape[0] // tm
    nt = ntp + xs.shape[0] // tm
    t = nt * tm

    def prow(width):
        return pl.BlockSpec((tm, width), lambda i: (jnp.minimum(i, ntp - 1), 0))

    def srow(width):
        return pl.BlockSpec((tm, width), lambda i: (jnp.maximum(i - ntp, 0), 0))

    def full(a):
        return pl.BlockSpec(a.shape, lambda i: (0,) * a.ndim)

    def rowblk(width, idx=0):
        return pl.BlockSpec((tm, width), lambda i: (i, idx))

    return pl.pallas_call(
        functools.partial(_merge_kernel, prompt_tiles=ntp),
        grid=(nt,),
        in_specs=[prow(D_MODEL), srow(D_MODEL), prow(HG_WIDTH), srow(HG_WIDTH),
                  prow(ATT_KV_WIDTH), srow(ATT_KV_WIDTH), rowblk(D_MODEL, 0),
                  rowblk(D_MODEL, 1), full(wa), full(wb), full(wo), full(g1), full(b1),
                  full(wr0), full(wr1), full(br)],
        out_specs=[rowblk(D_MODEL), rowblk(D_MODEL // 2), rowblk(TOP_K), rowblk(TOP_K), rowblk(TOP_K),
                   pl.BlockSpec((1, 1, N_EXPERTS), lambda i: (i, 0, 0))],
        out_shape=[jax.ShapeDtypeStruct((t, D_MODEL), F32),
                   jax.ShapeDtypeStruct((t, D_MODEL // 2), jnp.uint32),
                   jax.ShapeDtypeStruct((t, TOP_K), I32),
                   jax.ShapeDtypeStruct((t, TOP_K), F32),
                   jax.ShapeDtypeStruct((t, TOP_K), I32),
                   jax.ShapeDtypeStruct((nt, 1, N_EXPERTS), I32)],
        compiler_params=_cparams(("parallel",)),
        name="merge_ln1_router",
    )(xp, xs, oap, oas, obp, obs, p3, p3, wa, wb, wo, g1, b1, wr0, wr1, br)


def _pack_bf16_pairs(x):
    w = x.shape[1] // 2
    lo = lax.bitcast_convert_type(x[:, :w].astype(BF16).astype(F32), jnp.uint32)
    hi = lax.bitcast_convert_type(x[:, w:].astype(BF16).astype(F32), jnp.uint32)
    return (hi & jnp.uint32(0xFFFF0000)) | (lo >> 16)


def _unpack_pairs_f32(u):
    lo = lax.bitcast_convert_type(u << 16, F32)
    hi = lax.bitcast_convert_type(u & jnp.uint32(0xFFFF0000), F32)
    return lo, hi


def _unpack_bf16_pairs(u):
    lo, hi = _unpack_pairs_f32(u)
    return lo.astype(BF16), hi.astype(BF16)


def _gather_kernel(tok_ref, h_ref, o_ref):
    m = o_ref.shape[0]

    def body(q, carry):
        for u in range(SUBLANES):
            r = q * SUBLANES + u
            o_ref[pl.ds(r, 1), :] = h_ref[pl.ds(tok_ref[0, 0, r], 1), :]
        return carry

    lax.fori_loop(0, m // SUBLANES, body, 0)


def _moe_gather(hp, tok_pad):
    t, w = hp.shape
    p = tok_pad.shape[0]
    m = MOE_BLOCK
    return pl.pallas_call(
        _gather_kernel,
        grid=(p // m,),
        in_specs=[pl.BlockSpec((1, 1, m), lambda i: (i, 0, 0), memory_space=pltpu.SMEM),
                  pl.BlockSpec((t, w), lambda i: (0, 0), pipeline_mode=pl.Buffered(1))],
        out_specs=pl.BlockSpec((m, w), lambda i: (i, 0)),
        out_shape=jax.ShapeDtypeStruct((p, w), hp.dtype),
        compiler_params=_cparams(("arbitrary",)),
        name="moe_gather",
    )(tok_pad.reshape(p // m, 1, m), hp)


def _moe_up_kernel(blk0_ref, nblk_ref, w0_ref, w1_ref, w2_ref, w3_ref, b_ref, xs_ref, act_ref,
                   wbf_scr, x_scr, o_scr, sem_in, sem_out):
    e = pl.program_id(0)
    f = pl.program_id(1)
    n = nblk_ref[e]
    b0 = blk0_ref[e]
    m = x_scr.shape[1]
    w_refs = (w0_ref, w1_ref, w2_ref, w3_ref)
    kq, ft = w0_ref.shape[1:]
    half = ft // 2

    def rows(b):
        return pl.ds(pl.multiple_of((b0 + b) * m, m), m)

    def x_copy(b, s):
        return pltpu.make_async_copy(xs_ref.at[rows(b)], x_scr.at[s], sem_in.at[s])

    def o_copy(b, s):
        return pltpu.make_async_copy(o_scr.at[s], act_ref.at[f, rows(b)], sem_out.at[s])

    @pl.when(n > 0)
    def _():
        x_copy(0, 0).start(priority=1)
        for q, w_ref in enumerate(w_refs):
            wbf_scr[q * kq:(q + 1) * kq, :] = w_ref[0].astype(BF16)
        even = (lax.broadcasted_iota(I32, (m, LANES), 1) & 1) == 0

        def body(b, carry):
            s = b & 1
            x_copy(b, s).wait()

            @pl.when(b + 1 < n)
            def _():
                x_copy(b + 1, 1 - s).start(priority=1)

            @pl.when(b >= 2)
            def _():
                o_copy(b - 2, s).wait()

            x_lo, x_hi = _unpack_bf16_pairs(x_scr[s])
            kh = x_lo.shape[1]
            for g in range(ft // (2 * LANES)):
                cols = slice(2 * g * LANES, (2 * g + 2) * LANES)
                h = (jnp.dot(x_lo, wbf_scr[:kh, cols], preferred_element_type=F32)
                     + jnp.dot(x_hi, wbf_scr[kh:, cols], preferred_element_type=F32)
                     + b_ref[0, :, cols])
                ha, hb = h[:, :LANES], h[:, LANES:]
                glu = jnp.minimum(jnp.where(even, ha, pltpu.roll(hb, 1, 1)), SWIGLU_LIMIT)
                lin = jnp.where(even, pltpu.roll(ha, LANES - 1, 1), hb)
                lin = jnp.clip(lin, -SWIGLU_LIMIT, SWIGLU_LIMIT) + 1.0
                act = glu * _sigmoid(SWIGLU_ALPHA * glu) * lin
                o_scr[s, :, g * LANES:(g + 1) * LANES] = act.astype(o_scr.dtype)
            o_copy(b, s).start(priority=1)
            return carry

        lax.fori_loop(0, n, body, 0)

        @pl.when(n >= 2)
        def _():
            o_copy(n - 2, n & 1).wait()
        o_copy(n - 1, (n - 1) & 1).wait()

    @pl.when(e == pl.num_programs(0) - 1)
    def _():
        o_scr[0] = jnp.zeros(o_scr.shape[1:], o_scr.dtype)

        nz = act_ref.shape[1] // m - b0
        lax.fori_loop(n, nz, lambda b, c: (o_copy(b, 0).start(), c)[1], 0)
        lax.fori_loop(n, nz, lambda b, c: (o_copy(b, 0).wait(), c)[1], 0)


def _moe_up(xs, w_gate_up, b_gate_up, blk0, nblk):
    p, w = xs.shape
    d = 2 * w
    m = MOE_BLOCK
    ft = MOE_FT
    ne = w_gate_up.shape[0]
    nf = w_gate_up.shape[2] // ft

    def wslice(q, e, f, b0, nb):
        return (e, q, f)

    grid_spec = pltpu.PrefetchScalarGridSpec(
        num_scalar_prefetch=2,
        grid=(ne, nf),
        in_specs=[pl.BlockSpec((1, d // W_STREAMS, ft), functools.partial(wslice, q))
                  for q in range(W_STREAMS)] + [
                  pl.BlockSpec((1, 1, ft), lambda e, f, b0, nb: (e, 0, f)),
                  pl.BlockSpec(memory_space=pl.ANY)],
        out_specs=pl.BlockSpec(memory_space=pl.ANY),
        scratch_shapes=[pltpu.VMEM((d, ft), BF16),
                        pltpu.VMEM((2, m, w), xs.dtype),
                        pltpu.VMEM((2, m, ft // 2), BF16),
                        pltpu.SemaphoreType.DMA((2,)),
                        pltpu.SemaphoreType.DMA((2,))],
    )
    return pl.pallas_call(
        _moe_up_kernel,
        grid_spec=grid_spec,
        out_shape=jax.ShapeDtypeStruct((nf, p, ft // 2), BF16),
        compiler_params=_cparams(("arbitrary", "arbitrary")),
        name="moe_up",
    )(blk0, nblk, *([w_gate_up] * W_STREAMS), b_gate_up, xs)


def _hidden_perm():
    r = lax.broadcasted_iota(I32, (LANES, LANES), 0)
    c = lax.broadcasted_iota(I32, (LANES, LANES), 1)
    src = (r >> 1) + (r & 1) * (LANES // 2)
    return jnp.where(c == src, 1.0, 0.0).astype(BF16)


def _moe_down_kernel(be_ref, na_ref, nreal_ref, dest_ref, a_ref, wd0_ref, wd1_ref, wd2_ref, wd3_ref,
                     bd_ref, yt_ref, wbf_scr, y_scr, sem):
    i = pl.program_id(0)
    nb = pl.num_programs(0)
    nf, m, fh = a_ref.shape
    slot = i % 2

    def row_copy(s, r, d):
        return pltpu.make_async_copy(y_scr.at[s, pl.ds(r, 1)], yt_ref.at[pl.ds(d, 1)], sem.at[s])

    def drain(s, count):
        @pl.when(count == m)
        def _():
            pltpu.make_async_copy(y_scr.at[s], yt_ref.at[pl.ds(0, m)], sem.at[s]).wait()

        @pl.when(count != m)
        def _():
            def wbody(_, carry):
                row_copy(s, 0, 0).wait()
                return carry
            lax.fori_loop(0, count, wbody, 0)

    @pl.when(i >= 2)
    def _():
        drain(slot, nreal_ref[i - 2])

    changed = jnp.where(i == 0, 1, be_ref[i] - be_ref[jnp.maximum(i - 1, 0)])

    @pl.when(changed != 0)
    def _():
        perm = _hidden_perm()
        for q, wd_ref in enumerate((wd0_ref, wd1_ref, wd2_ref, wd3_ref)):
            kq = wd_ref.shape[1]
            for g in range(kq // LANES):
                rows = slice(g * LANES, (g + 1) * LANES)
                wbf_scr[q * kq + g * LANES:q * kq + (g + 1) * LANES, :] = jnp.dot(
                    perm, wd_ref[0, rows, :].astype(BF16), preferred_element_type=F32).astype(BF16)

    @pl.when(i < na_ref[0])
    def _():
        y = bd_ref[0]
        for j in range(nf):
            y = y + jnp.dot(a_ref[j], wbf_scr[j * fh:(j + 1) * fh, :], preferred_element_type=F32)
        y_scr[slot] = _pack_bf16_pairs(y)

        n = nreal_ref[i]
        unroll = 8

        def sbody8(q, carry):
            for u in range(unroll):
                r = q * unroll + u
                row_copy(slot, r, dest_ref[0, 0, r]).start()
            return carry

        def sbody1(r, carry):
            row_copy(slot, r, dest_ref[0, 0, r]).start()
            return carry

        lax.fori_loop(0, n // unroll, sbody8, 0)
        lax.fori_loop((n // unroll) * unroll, n, sbody1, 0)

    @pl.when(i == nb - 1)
    def _():
        @pl.when(i >= 1)
        def _():
            drain(1 - slot, nreal_ref[i - 1])
        drain(slot, nreal_ref[i])


def _moe_down(act, w_down, b_down, blk_expert, n_active, nreal, dest, n_rows):
    nf, p, fh = act.shape
    f = nf * fh
    m = MOE_BLOCK
    d = w_down.shape[2]

    def wslice(q, i, be, na, nr):
        return (be[i], q, 0)

    grid_spec = pltpu.PrefetchScalarGridSpec(
        num_scalar_prefetch=3,
        grid=(p // m,),
        in_specs=[pl.BlockSpec((1, 1, m), lambda i, be, na, nr: (i, 0, 0), memory_space=pltpu.SMEM),
                  pl.BlockSpec((nf, m, fh), lambda i, be, na, nr: (0, i, 0)),
                  *[pl.BlockSpec((1, f // W_STREAMS, d), functools.partial(wslice, q))
                    for q in range(W_STREAMS)],
                  pl.BlockSpec((1, 1, d), lambda i, be, na, nr: (be[i], 0, 0))],
        out_specs=pl.BlockSpec(memory_space=pl.ANY),
        scratch_shapes=[pltpu.VMEM((f, d), BF16),
                        pltpu.VMEM((2, m, d // 2), jnp.uint32),
                        pltpu.SemaphoreType.DMA((2,))],
    )
    return pl.pallas_call(
        _moe_down_kernel,
        grid_spec=grid_spec,
        out_shape=jax.ShapeDtypeStruct((n_rows, d // 2), jnp.uint32),
        compiler_params=_cparams(("arbitrary",)),
        name="moe_down",
    )(blk_expert, n_active, nreal, dest.reshape(p // m, 1, m), act, *([w_down] * W_STREAMS), b_down)


def _combine_kernel(h_ref, y0_ref, y1_ref, y2_ref, y3_ref, gate_ref, g2_ref, b2_ref, o_ref):
    lo, hi = None, None
    for j, y_ref in enumerate((y0_ref, y1_ref, y2_ref, y3_ref)):
        yl, yh = _unpack_pairs_f32(y_ref[...])
        g = gate_ref[:, j:j + 1]
        lo = g * yl if lo is None else lo + g * yl
        hi = g * yh if hi is None else hi + g * yh
    ffn = jnp.concatenate([lo, hi], axis=1)
    o_ref[...] = _layer_norm(DEEPNORM_ALPHA * h_ref[...] + ffn, g2_ref[...], b2_ref[...])


def _combine(h, yt, gate, g2, b2):
    t, d = h.shape
    tm = TOKEN_TILE
    nt = t // tm

    def choice(j):
        return pl.BlockSpec((tm, d // 2), lambda i: (j * nt + i, 0))

    return pl.pallas_call(
        _combine_kernel,
        grid=(nt,),
        in_specs=[pl.BlockSpec((tm, d), lambda i: (i, 0)),
                  choice(0), choice(1), choice(2), choice(3),
                  pl.BlockSpec((tm, TOP_K), lambda i: (i, 0)),
                  pl.BlockSpec((1, d), lambda i: (0, 0)),
                  pl.BlockSpec((1, d), lambda i: (0, 0))],
        out_specs=pl.BlockSpec((tm, d), lambda i: (i, 0)),
        out_shape=jax.ShapeDtypeStruct((t, d), F32),
        compiler_params=_cparams(("parallel",)),
        name="combine_ln2",
    )(h, yt, yt, yt, yt, gate, g2, b2)


def _route(top_e, lrank, tilecnt):
    t = top_e.shape[0]
    m = MOE_BLOCK
    n_assign = t * TOP_K
    n_blocks = n_assign // m + N_EXPERTS
    p = n_blocks * m
    tc = tilecnt.reshape(-1, N_EXPERTS)
    base = jnp.cumsum(tc, axis=0) - tc
    counts = jnp.sum(tc, axis=0)
    padded = (counts + m - 1) // m * m
    pend = jnp.cumsum(padded)
    pstart = pend - padded
    base_tok = jnp.repeat(base, TOKEN_TILE, axis=0)
    rank = lrank + jnp.take_along_axis(base_tok, top_e, axis=1)
    slot = (pstart[top_e] + rank).reshape(-1)
    a_idx = jnp.arange(n_assign, dtype=I32)
    dest = jnp.full((p,), -1, I32).at[slot].set((a_idx % TOP_K) * t + a_idx // TOP_K,
                                               unique_indices=True)
    tok_pad = jnp.where(dest >= 0, dest % t, 0)
    blk_start = jnp.arange(n_blocks, dtype=I32) * m
    blk_expert = jnp.minimum(jnp.sum((pend[None, :] <= blk_start[:, None]).astype(I32), axis=1),
                             N_EXPERTS - 1)
    n_active = (pend[-1:] // m).astype(I32)
    real_end = (pstart + counts)[blk_expert]
    nreal = jnp.where(blk_start < pend[-1], jnp.clip(real_end - blk_start, 0, m), 0).astype(I32)
    return tok_pad, dest, blk_expert, n_active, nreal, (pstart // m).astype(I32), (padded // m).astype(I32)


def kernel(x_prompt, x_sample, cache_attn_k, cache_attn_v, state_hgrn, w_in, hgrn_lb_logits,
           hgrn_norm_w, w_branch_a, w_branch_b, w_out, ln1_g, ln1_b, w_router, b_router,
           w_gate_up, b_gate_up, w_down, b_down, ln2_g, ln2_b):
    assert w_in.shape[0] == DEPTH == 1
    bp, lp, d = x_prompt.shape
    bs, ls, _ = x_sample.shape
    tp, ts = bp * lp, bs * ls
    t = tp + ts
    keep = min(max(w for w, _ in DILATION_PAIRS), lp)

    xp2, xs2 = x_prompt.reshape(tp, d), x_sample.reshape(ts, d)
    xb = jnp.concatenate([xp2.astype(BF16), xs2.astype(BF16)], axis=0)
    wi = w_in[0]
    c1, c2 = P1_WIDTH, P1_WIDTH + P2_WIDTH
    p1 = _matmul(xb, wi[:, :c1].astype(BF16), 768, 1024, F32, "inproj_hgrn")
    p2 = _matmul(xb, wi[:, c1:c2].astype(BF16), 768, 1280, F32, "inproj_attn")
    p3 = _matmul(xb, wi[:, c2:].astype(BF16), 768, 1024, F32, "inproj_gate")

    oa_p, st_p = _hgrn(p1, hgrn_lb_logits, hgrn_norm_w[0], None, bp, lp, 0)
    oa_s, st_s = _hgrn(p1, hgrn_lb_logits, hgrn_norm_w[0],
                       state_hgrn.reshape(bs, HG_HEADS, HG_DIM, HG_DIM), bs, ls, tp)

    n = jnp.arange(1, ATT_Q_HEADS + 1, dtype=F32)
    slopes = jnp.power(2.0, -8.0 * n / ATT_Q_HEADS).reshape(N_GROUPS, ATT_KV_HEADS)
    ctx = cache_attn_k.shape[2]
    ob_p = _attn_prompt(p2, slopes, bp, lp)
    ob_s = _attn_sample(p2, cache_attn_k.reshape(bs, ctx * ATT_KV_HEADS, ATT_HEAD_DIM),
                        cache_attn_v.reshape(bs, ctx * ATT_KV_HEADS, ATT_HEAD_DIM), slopes, bs, ls, tp)

    wr = w_router[0]
    wr0 = wr.astype(BF16)
    wr1 = (wr - wr0.astype(F32)).astype(BF16)
    h, hp, top_e, gate, lrank, tilecnt = _merge(
        xp2, xs2, oa_p.reshape(tp, HG_WIDTH), oa_s.reshape(ts, HG_WIDTH), ob_p,
        ob_s.reshape(ts, ATT_KV_WIDTH), p3, w_branch_a[0].astype(BF16), w_branch_b[0].astype(BF16),
        w_out[0].astype(BF16), ln1_g, ln1_b, wr0, wr1, b_router)

    tok_pad, dest, blk_expert, n_active, nreal, blk0, nblk = _route(top_e, lrank, tilecnt)
    xs = _moe_gather(hp, tok_pad)
    act = _moe_up(xs, w_gate_up.reshape(N_EXPERTS, d, 2 * D_FF),
                  b_gate_up.reshape(N_EXPERTS, 1, 2 * D_FF), blk0, nblk)
    yt = _moe_down(act, w_down.reshape(N_EXPERTS, D_FF, d), b_down.reshape(N_EXPERTS, 1, d),
                   blk_expert, n_active, nreal, dest, t * TOP_K)
    y = _combine(h, yt, gate, ln2_g, ln2_b)

    k_new = p2[:, ATT_Q_HEADS * ATT_HEAD_DIM:(ATT_Q_HEADS + ATT_KV_HEADS) * ATT_HEAD_DIM]
    v_new = p2[:, (ATT_Q_HEADS + ATT_KV_HEADS) * ATT_HEAD_DIM:]

    def kv_prompt(a):
        return a[:tp].reshape(bp, lp, ATT_KV_HEADS, ATT_HEAD_DIM)[:, lp - keep:][None]

    def kv_sample(a):
        return a[tp:].reshape(bs, ls, ATT_KV_HEADS, ATT_HEAD_DIM)[None]

    return (y[:tp].reshape(bp, lp, d), y[tp:].reshape(bs, ls, d),
            kv_prompt(k_new), kv_prompt(v_new), st_p[None],
            kv_sample(k_new), kv_sample(v_new), st_s[None])
```

```python
import functools
import math

import jax
import jax.numpy as jnp
from jax import lax
from jax.experimental import pallas as pl
from jax.experimental.pallas import tpu as pltpu

F32 = jnp.float32
BF16 = jnp.bfloat16
I32 = jnp.int32

D_MODEL = 2048
DEPTH = 1
HG_HEADS = 8
HG_DIM = 128
HG_WIDTH = HG_HEADS * HG_DIM
HG_CHUNK = 128
ATT_HEAD_DIM = 128
DILATION_PAIRS = ((128, 1), (512, 4), (2048, 16))
N_GROUPS = len(DILATION_PAIRS)
ATT_KV_HEADS = 4
ATT_Q_HEADS = N_GROUPS * ATT_KV_HEADS
ATT_TAPS = 128
ATT_KV_WIDTH = ATT_KV_HEADS * ATT_HEAD_DIM
ATT_SCALE = ATT_HEAD_DIM ** -0.5
N_EXPERTS = 32
TOP_K = 4
D_FF = D_MODEL
SWIGLU_ALPHA = 1.702
SWIGLU_LIMIT = 7.0
LN_EPS = 1e-5
RMS_EPS = 1e-5
DEEPNORM_ALPHA = (2 * DEPTH) ** 0.25

P1_WIDTH = 4 * HG_WIDTH
P2_WIDTH = (ATT_Q_HEADS + 2 * ATT_KV_HEADS) * ATT_HEAD_DIM
P3_WIDTH = 2 * D_MODEL

LANES = 128
SUBLANES = 8
VMEM_LIMIT = 60 * 1024 * 1024

TOKEN_TILE = 256
MOE_BLOCK = 256
MOE_FT = 2048
W_STREAMS = 4
X_RESIDENT = 8
NEG = -1e30

_NT = (((1,), (1,)), ((), ()))
_TN = (((0,), (0,)), ((), ()))


def _cparams(sem):
    return pltpu.CompilerParams(dimension_semantics=sem, vmem_limit_bytes=VMEM_LIMIT)


def _sigmoid(x):
    return 1.0 / (1.0 + jnp.exp(-x))


def _mm_kernel(x_ref, w_ref, o_ref):
    o_ref[...] = jnp.dot(x_ref[...], w_ref[...], preferred_element_type=F32).astype(o_ref.dtype)


def _matmul(x, w, tm, tn, out_dtype, name):
    m, k = x.shape
    n = w.shape[1]
    assert m % tm == 0 and n % tn == 0
    return pl.pallas_call(
        _mm_kernel,
        grid=(n // tn, m // tm),
        in_specs=[pl.BlockSpec((tm, k), lambda j, i: (i, 0)),
                  pl.BlockSpec((k, tn), lambda j, i: (0, j))],
        out_specs=pl.BlockSpec((tm, tn), lambda j, i: (i, j)),
        out_shape=jax.ShapeDtypeStruct((m, n), out_dtype),
        compiler_params=_cparams(("parallel", "parallel")),
        name=name,
    )(x, w)


def _level_ref(g_scr, m, sl, chunk, sub):
    pieces = []
    for j in range(chunk // SUBLANES):
        if 2 * m >= SUBLANES:
            row = ((SUBLANES * j) // (2 * m)) * (2 * m) + m - 1
            pieces.append(jnp.broadcast_to(g_scr[row:row + 1, sl], (SUBLANES, LANES)))
        else:
            acc = None
            for u in range(SUBLANES // (2 * m)):
                row = SUBLANES * j + 2 * m * u + m - 1
                b = jnp.broadcast_to(g_scr[row:row + 1, sl], (SUBLANES, LANES))
                acc = b if acc is None else jnp.where(sub // (2 * m) == u, b, acc)
            pieces.append(acc)
    return pieces[0] if len(pieces) == 1 else jnp.concatenate(pieces, axis=0)


def _hgrn_kernel(lbl_ref, nw_ref, hq_ref, hf_ref, hi_ref, hg_ref, *rest, chunk, has_s0):
    if has_s0:
        s0_ref, o_ref, sout_ref, st_scr, g_scr = rest
    else:
        o_ref, sout_ref, st_scr, g_scr = rest
    c = pl.program_id(1)
    nc = pl.num_programs(1)

    @pl.when(c == 0)
    def _init():
        for h in range(HG_HEADS):
            if has_s0:
                st_scr[h] = s0_ref[0, h].T
            else:
                st_scr[h] = jnp.zeros((HG_DIM, HG_DIM), F32)

    lbl = lbl_ref[...]
    ex = jnp.exp(lbl - jnp.max(lbl, axis=0, keepdims=True))
    lb = ex[0:1] / jnp.sum(ex, axis=0, keepdims=True)

    f = lb + (1.0 - lb) * _sigmoid(hf_ref[...])
    logf = jnp.log(f)
    kall = 1.0 - f

    r_cc = lax.broadcasted_iota(I32, (chunk, chunk), 0)
    c_cc = lax.broadcasted_iota(I32, (chunk, chunk), 1)
    tri = jnp.where(r_cc >= c_cc, 1.0, 0.0).astype(BF16)
    p0 = logf.astype(BF16)
    r1 = logf - p0.astype(F32)
    p1 = r1.astype(BF16)
    p2 = (r1 - p1.astype(F32)).astype(BF16)
    g_all = (jnp.dot(tri, p0, preferred_element_type=F32)
             + jnp.dot(tri, p1, preferred_element_type=F32)
             + jnp.dot(tri, p2, preferred_element_type=F32))
    g_scr[...] = g_all

    sub = lax.broadcasted_iota(I32, (SUBLANES, LANES), 0)
    row_c = lax.broadcasted_iota(I32, (chunk, LANES), 0)
    nw = nw_ref[...]

    for h in range(HG_HEADS):
        sl = slice(h * HG_DIM, (h + 1) * HG_DIM)
        hq = hq_ref[:, sl]
        q = hq * _sigmoid(hq)
        k = kall[:, sl]
        g = g_all[:, sl]
        v = hi_ref[:, sl]
        vb = v.astype(BF16)

        p = lax.dot_general(q.astype(BF16), k.astype(BF16), _NT, preferred_element_type=F32)
        a = jnp.where(r_cc == c_cc, p, 0.0)
        m = 1
        while m < chunk:
            ref = _level_ref(g_scr, m, sl, chunk, sub)
            e = jnp.exp(-jnp.abs(g - ref))
            odd = ((row_c // m) & 1) == 1
            x = (jnp.where(odd, q, k) * e).astype(BF16)
            p = lax.dot_general(x, x, _NT, preferred_element_type=F32)
            pair = (r_cc // (2 * m)) == (c_cc // (2 * m))
            sel = jnp.where(pair, ((r_cc // m) & 1) - ((c_cc // m) & 1), 0) == 1
            a = jnp.where(sel, p, a)
            m *= 2

        st = st_scr[h]
        o = jnp.dot(a.astype(BF16), vb, preferred_element_type=F32)
        qe = (q * jnp.exp(g)).astype(BF16)
        o = o + lax.dot_general(qe, st.astype(BF16), _NT, preferred_element_type=F32)

        ms = jnp.mean(o * o, axis=-1, keepdims=True)
        hg = hg_ref[:, sl]
        out = o * lax.rsqrt(ms + RMS_EPS) * nw * (hg * _sigmoid(hg))
        o_ref[0, :, sl] = out.astype(o_ref.dtype)

        g_last = g[chunk - 1:chunk, :]
        kd = (k * jnp.exp(g_last - g)).astype(BF16)
        st_scr[h] = st * jnp.exp(g_last) + lax.dot_general(vb, kd, _TN, preferred_element_type=F32)

    @pl.when(c == nc - 1)
    def _fin():
        for h in range(HG_HEADS):
            sout_ref[0, h] = st_scr[h].T


def _hgrn(p1, lb_logits, norm_w, s0, batch, seq, row0):
    chunk = math.gcd(seq, HG_CHUNK)
    nchunk = seq // chunk
    assert row0 % chunk == 0
    rb0 = row0 // chunk

    def col(idx):
        return pl.BlockSpec((chunk, HG_WIDTH), lambda b, c: (rb0 + b * nchunk + c, idx))

    in_specs = [pl.BlockSpec(lb_logits.shape, lambda b, c: (0, 0)),
                pl.BlockSpec((1, HG_DIM), lambda b, c: (0, 0)),
                col(0), col(1), col(2), col(3)]
    args = [lb_logits, norm_w.reshape(1, HG_DIM), p1, p1, p1, p1]
    if s0 is not None:
        in_specs.append(pl.BlockSpec((1, HG_HEADS, HG_DIM, HG_DIM), lambda b, c: (b, 0, 0, 0)))
        args.append(s0)
    return pl.pallas_call(
        functools.partial(_hgrn_kernel, chunk=chunk, has_s0=s0 is not None),
        grid=(batch, nchunk),
        in_specs=in_specs,
        out_specs=[pl.BlockSpec((1, chunk, HG_WIDTH), lambda b, c: (b, c, 0)),
                   pl.BlockSpec((1, HG_HEADS, HG_DIM, HG_DIM), lambda b, c: (b, 0, 0, 0))],
        out_shape=[jax.ShapeDtypeStruct((batch, seq, HG_WIDTH), BF16),
                   jax.ShapeDtypeStruct((batch, HG_HEADS, HG_DIM, HG_DIM), F32)],
        scratch_shapes=[pltpu.VMEM((HG_HEADS, HG_DIM, HG_DIM), F32),
                        pltpu.VMEM((chunk, HG_WIDTH), F32)],
        compiler_params=_cparams(("parallel", "arbitrary")),
        name="hgrn_seq%d" % seq,
    )(*args)


QB = 128
ATTN_UNROLL = 8


def _attn_prompt_kernel(slopes_ref, q0_ref, q1_ref, q2_ref, k_ref, v_ref, o_ref,
                        og_scr, lse_scr, bias_scr, *, seq):
    h = pl.program_id(1)
    il = lax.broadcasted_iota(I32, (QB, 2 * QB), 0)
    jl = lax.broadcasted_iota(I32, (QB, 2 * QB), 1)
    delta = il + QB - jl
    band = jnp.where(delta >= 0, delta, ATT_TAPS + 1) <= ATT_TAPS
    prev_half = jl < QB
    q_refs = (q0_ref, q1_ref, q2_ref)

    for g, (window, dil) in enumerate(DILATION_PAIRS):
        assert window // dil == ATT_TAPS
        slope = slopes_ref[g, h]
        bias_scr[g] = jnp.where(band, (-slope * dil) * delta.astype(F32), NEG)
        nbr = (seq // dil) // QB
        q_ref = q_refs[g]

        def rows(start, dil=dil):
            return pl.ds(start, QB) if dil == 1 else pl.ds(start, QB, stride=dil)

        def body(ib, carry, g=g, dil=dil, nbr=nbr, q_ref=q_ref, rows=rows):
            res = ib // nbr
            jb = ib - res * nbr
            qs = res + dil * (jb * QB)
            ps = res + dil * jnp.maximum(jb * QB - QB, 0)
            qv = q_ref[rows(qs), :].astype(BF16)
            kb = jnp.concatenate([k_ref[rows(ps), :], k_ref[rows(qs), :]], axis=0).astype(BF16)
            vb = jnp.concatenate([v_ref[rows(ps), :], v_ref[rows(qs), :]], axis=0).astype(BF16)
            s = lax.dot_general(qv, kb, _NT, preferred_element_type=F32) * ATT_SCALE + bias_scr[g]
            s = jnp.where(jnp.where(prev_half, jb, 1) == 0, NEG, s)
            mx = jnp.max(s, axis=-1, keepdims=True)
            p = jnp.exp(s - mx)
            l = jnp.sum(p, axis=-1, keepdims=True)
            o = jnp.dot(p.astype(BF16), vb, preferred_element_type=F32) / l
            og_scr[g, rows(qs), :] = o
            lse_scr[g, rows(qs), :] = jnp.broadcast_to(mx + jnp.log(l), (QB, LANES))
            return carry

        lax.fori_loop(0, seq // QB, body, 0, unroll=ATTN_UNROLL)

    def merge(tb, carry):
        r = pl.ds(pl.multiple_of(tb * QB, QB), QB)
        l0, l1, l2 = lse_scr[0, r, :], lse_scr[1, r, :], lse_scr[2, r, :]
        mx = jnp.maximum(jnp.maximum(l0, l1), l2)
        w0, w1, w2 = jnp.exp(l0 - mx), jnp.exp(l1 - mx), jnp.exp(l2 - mx)
        o = (w0 * og_scr[0, r, :] + w1 * og_scr[1, r, :] + w2 * og_scr[2, r, :]) / (w0 + w1 + w2)
        o_ref[r, :] = o.astype(o_ref.dtype)
        return carry

    lax.fori_loop(0, seq // QB, merge, 0, unroll=ATTN_UNROLL)


def _attn_prompt(p2, slopes, batch, seq):
    assert seq % (QB * max(d for _, d in DILATION_PAIRS)) == 0

    def col(fn):
        return pl.BlockSpec((seq, ATT_HEAD_DIM), lambda b, h: (b, fn(h)))

    return pl.pallas_call(
        functools.partial(_attn_prompt_kernel, seq=seq),
        grid=(batch, ATT_KV_HEADS),
        in_specs=[pl.BlockSpec(memory_space=pltpu.SMEM),
                  col(lambda h: h), col(lambda h: ATT_KV_HEADS + h), col(lambda h: 2 * ATT_KV_HEADS + h),
                  col(lambda h: ATT_Q_HEADS + h), col(lambda h: ATT_Q_HEADS + ATT_KV_HEADS + h)],
        out_specs=pl.BlockSpec((seq, ATT_HEAD_DIM), lambda b, h: (b, h)),
        out_shape=jax.ShapeDtypeStruct((batch * seq, ATT_KV_WIDTH), BF16),
        scratch_shapes=[pltpu.VMEM((N_GROUPS, seq, ATT_HEAD_DIM), F32),
                        pltpu.VMEM((N_GROUPS, seq, LANES), F32),
                        pltpu.VMEM((N_GROUPS, QB, 2 * QB), F32)],
        compiler_params=_cparams(("parallel", "parallel")),
        name="attn_prompt",
    )(slopes, p2, p2, p2, p2, p2)


def _attn_sample_kernel(slopes_ref, qa_ref, qb_ref, qc_ref, kn_ref, vn_ref, kc_ref, vc_ref, o_ref,
                        *, ctx, nq):
    rows = N_GROUPS * nq
    gi = lax.broadcasted_iota(I32, (rows, 1), 0) // nq
    dil = jnp.where(gi == 0, DILATION_PAIRS[0][1], jnp.where(gi == 1, DILATION_PAIRS[1][1], DILATION_PAIRS[2][1]))
    win = jnp.where(gi == 0, DILATION_PAIRS[0][0], jnp.where(gi == 1, DILATION_PAIRS[1][0], DILATION_PAIRS[2][0]))

    def dist_valid(shape, key0, nkeys):
        r = lax.broadcasted_iota(I32, shape, 0)
        t = r - (r // nq) * nq
        col = lax.broadcasted_iota(I32, shape, 1)
        dist = ctx + t - (col + key0)
        bad = jnp.where(dist >= 0, dist & (dil - 1), 1)
        bad = jnp.where(dist <= win, bad, 1)
        bad = jnp.where(col < nkeys, bad, 1)
        return dist.astype(F32), bad == 0

    dist_c, ok_c = dist_valid((rows, ctx), 0, ctx)
    dist_n, ok_n = dist_valid((rows, LANES), ctx, nq)
    zpad = jnp.zeros((LANES - nq, ATT_HEAD_DIM), F32)

    for h in range(ATT_KV_HEADS):
        hs = slice(h * ATT_HEAD_DIM, (h + 1) * ATT_HEAD_DIM)
        slope = jnp.where(gi == 0, slopes_ref[0, h], jnp.where(gi == 1, slopes_ref[1, h], slopes_ref[2, h]))
        qh = jnp.concatenate([qa_ref[:, hs], qb_ref[:, hs], qc_ref[:, hs]], axis=0).astype(BF16)
        kc = kc_ref[0, pl.ds(h, ctx, stride=ATT_KV_HEADS), :].astype(BF16)
        vc = vc_ref[0, pl.ds(h, ctx, stride=ATT_KV_HEADS), :].astype(BF16)
        kn = jnp.concatenate([kn_ref[:, hs], zpad], axis=0).astype(BF16)
        vn = jnp.concatenate([vn_ref[:, hs], zpad], axis=0).astype(BF16)
        sc = lax.dot_general(qh, kc, _NT, preferred_element_type=F32) * ATT_SCALE - slope * dist_c
        sn = lax.dot_general(qh, kn, _NT, preferred_element_type=F32) * ATT_SCALE - slope * dist_n
        sc = jnp.where(ok_c, sc, NEG)
        sn = jnp.where(ok_n, sn, NEG)
        mx = jnp.maximum(jnp.max(sc, axis=-1, keepdims=True), jnp.max(sn, axis=-1, keepdims=True))
        pc = jnp.exp(sc - mx)
        pn = jnp.exp(sn - mx)
        l = jnp.sum(pc, axis=-1, keepdims=True) + jnp.sum(pn, axis=-1, keepdims=True)
        o = (jnp.dot(pc.astype(BF16), vc, preferred_element_type=F32)
             + jnp.dot(pn.astype(BF16), vn, preferred_element_type=F32)) / l
        lse = mx + jnp.log(l)
        l0, l1, l2 = lse[0:nq], lse[nq:2 * nq], lse[2 * nq:3 * nq]
        m3 = jnp.maximum(jnp.maximum(l0, l1), l2)
        w0, w1, w2 = jnp.exp(l0 - m3), jnp.exp(l1 - m3), jnp.exp(l2 - m3)
        out = (w0 * o[0:nq] + w1 * o[nq:2 * nq] + w2 * o[2 * nq:3 * nq]) / (w0 + w1 + w2)
        o_ref[0, :, hs] = out.astype(o_ref.dtype)


def _attn_sample(p2, cache_k, cache_v, slopes, batch, nq, row0):
    ctx = cache_k.shape[1] // ATT_KV_HEADS
    assert row0 % nq == 0 and nq == SUBLANES and ctx >= max(w for w, _ in DILATION_PAIRS)
    rb0 = row0 // nq

    def col(idx):
        return pl.BlockSpec((nq, ATT_KV_WIDTH), lambda b: (rb0 + b, idx))

    cache_spec = pl.BlockSpec((1, ctx * ATT_KV_HEADS, ATT_HEAD_DIM), lambda b: (b, 0, 0))
    return pl.pallas_call(
        functools.partial(_attn_sample_kernel, ctx=ctx, nq=nq),
        grid=(batch,),
        in_specs=[pl.BlockSpec(memory_space=pltpu.SMEM),
                  col(0), col(1), col(2), col(3), col(4), cache_spec, cache_spec],
        out_specs=pl.BlockSpec((1, nq, ATT_KV_WIDTH), lambda b: (b, 0, 0)),
        out_shape=jax.ShapeDtypeStruct((batch, nq, ATT_KV_WIDTH), BF16),
        compiler_params=_cparams(("parallel",)),
        name="attn_sample",
    )(slopes, p2, p2, p2, p2, p2, cache_k, cache_v)


def _layer_norm(y, g, b):
    mu = jnp.mean(y, axis=-1, keepdims=True)
    yc = y - mu
    var = jnp.mean(yc * yc, axis=-1, keepdims=True)
    return yc * lax.rsqrt(var + LN_EPS) * g + b


def _merge_kernel(xp_ref, xs_ref, oap_ref, oas_ref, obp_ref, obs_ref, ga_ref, gb_ref, wa_ref, wb_ref,
                  wo_ref, g1_ref, b1_ref, wr0_ref, wr1_ref, br_ref,
                  h_ref, hp_ref, tope_ref, gate_ref, lrank_ref, cnt_ref, *, prompt_tiles):
    tm = xp_ref.shape[0]
    is_prompt = pl.program_id(0) < prompt_tiles
    x = jnp.where(is_prompt, xp_ref[...], xs_ref[...])
    oa = jnp.where(is_prompt, oap_ref[...], oas_ref[...])
    ob = jnp.where(is_prompt, obp_ref[...], obs_ref[...])
    a = jnp.dot(oa, wa_ref[...], preferred_element_type=F32)
    b = jnp.dot(ob, wb_ref[...], preferred_element_type=F32)
    merged = _sigmoid(ga_ref[...]) * a + _sigmoid(gb_ref[...]) * b
    z = jnp.dot(merged.astype(BF16), wo_ref[...], preferred_element_type=F32)
    hh = _layer_norm(DEEPNORM_ALPHA * x + z, g1_ref[...], b1_ref[...])
    h_ref[...] = hh
    hp_ref[...] = _pack_bf16_pairs(hh)

    h0 = hh.astype(BF16)
    h1 = (hh - h0.astype(F32)).astype(BF16)
    logits = (jnp.dot(h0, wr0_ref[...], preferred_element_type=F32)
              + jnp.dot(h1, wr0_ref[...], preferred_element_type=F32)
              + jnp.dot(h0, wr1_ref[...], preferred_element_type=F32)) + br_ref[...]

    lane = lax.broadcasted_iota(I32, (tm, N_EXPERTS), 1)
    work = logits
    vals, idxs = [], []
    onehot = jnp.zeros((tm, N_EXPERTS), F32)
    for _ in range(TOP_K):
        mx = jnp.max(work, axis=-1, keepdims=True)
        idx = jnp.min(jnp.where(work == mx, lane, N_EXPERTS), axis=-1, keepdims=True)
        hit = lane == idx
        vals.append(mx)
        idxs.append(idx)
        onehot = jnp.where(hit, 1.0, onehot)
        work = jnp.where(hit, -jnp.inf, work)
    ex = [jnp.exp(v - vals[0]) for v in vals]
    den = ex[0] + ex[1] + ex[2] + ex[3]

    r_tt = lax.broadcasted_iota(I32, (tm, tm), 0)
    c_tt = lax.broadcasted_iota(I32, (tm, tm), 1)
    before = jnp.where(r_tt > c_tt, 1.0, 0.0).astype(BF16)
    prefix = jnp.dot(before, onehot.astype(BF16), preferred_element_type=F32)
    for j in range(TOP_K):
        tope_ref[:, j:j + 1] = idxs[j]
        gate_ref[:, j:j + 1] = ex[j] / den
        lrank_ref[:, j:j + 1] = jnp.sum(jnp.where(lane == idxs[j], prefix, 0.0), axis=-1,
                                        keepdims=True).astype(I32)
    cnt_ref[0] = jnp.sum(onehot, axis=0, keepdims=True).astype(I32)


def _merge(xp, xs, oap, oas, obp, obs, p3, wa, wb, wo, g1, b1, wr0, wr1, br):
    tm = TOKEN_TILE
    assert xp.shape[0] % tm == 0 and xs.shape[0] % tm == 0
    ntp = xp.shape[0] // tm
    nt = ntp + xs.shape[0] // tm
    t = nt * tm

    def prow(width):
        return pl.BlockSpec((tm, width), lambda i: (jnp.minimum(i, ntp - 1), 0))

    def srow(width):
        return pl.BlockSpec((tm, width), lambda i: (jnp.maximum(i - ntp, 0), 0))

    def full(a):
        return pl.BlockSpec(a.shape, lambda i: (0,) * a.ndim)

    def rowblk(width, idx=0):
        return pl.BlockSpec((tm, width), lambda i: (i, idx))

    return pl.pallas_call(
        functools.partial(_merge_kernel, prompt_tiles=ntp),
        grid=(nt,),
        in_specs=[prow(D_MODEL), srow(D_MODEL), prow(HG_WIDTH), srow(HG_WIDTH),
                  prow(ATT_KV_WIDTH), srow(ATT_KV_WIDTH), rowblk(D_MODEL, 0),
                  rowblk(D_MODEL, 1), full(wa), full(wb), full(wo), full(g1), full(b1),
                  full(wr0), full(wr1), full(br)],
        out_specs=[rowblk(D_MODEL), rowblk(D_MODEL // 2), rowblk(TOP_K), rowblk(TOP_K), rowblk(TOP_K),
                   pl.BlockSpec((1, 1, N_EXPERTS), lambda i: (i, 0, 0))],
        out_shape=[jax.ShapeDtypeStruct((t, D_MODEL), F32),
                   jax.ShapeDtypeStruct((t, D_MODEL // 2), jnp.uint32),
                   jax.ShapeDtypeStruct((t, TOP_K), I32),
                   jax.ShapeDtypeStruct((t, TOP_K), F32),
                   jax.ShapeDtypeStruct((t, TOP_K), I32),
                   jax.ShapeDtypeStruct((nt, 1, N_EXPERTS), I32)],
        compiler_params=_cparams(("parallel",)),
        name="merge_ln1_router",
    )(xp, xs, oap, oas, obp, obs, p3, p3, wa, wb, wo, g1, b1, wr0, wr1, br)


def _pack_bf16_pairs(x):
    w = x.shape[1] // 2
    lo = lax.bitcast_convert_type(x[:, :w].astype(BF16).astype(F32), jnp.uint32)
    hi = lax.bitcast_convert_type(x[:, w:].astype(BF16).astype(F32), jnp.uint32)
    return (hi & jnp.uint32(0xFFFF0000)) | (lo >> 16)


def _unpack_pairs_f32(u):
    lo = lax.bitcast_convert_type(u << 16, F32)
    hi = lax.bitcast_convert_type(u & jnp.uint32(0xFFFF0000), F32)
    return lo, hi


def _unpack_bf16_pairs(u):
    lo, hi = _unpack_pairs_f32(u)
    return lo.astype(BF16), hi.astype(BF16)


def _gather_kernel(tok_ref, h_ref, o_ref):
    m = o_ref.shape[0]

    def body(q, carry):
        for u in range(SUBLANES):
            r = q * SUBLANES + u
            o_ref[pl.ds(r, 1), :] = h_ref[pl.ds(tok_ref[0, 0, r], 1), :]
        return carry

    lax.fori_loop(0, m // SUBLANES, body, 0)


def _moe_gather(hp, tok_pad):
    t, w = hp.shape
    p = tok_pad.shape[0]
    m = MOE_BLOCK
    return pl.pallas_call(
        _gather_kernel,
        grid=(p // m,),
        in_specs=[pl.BlockSpec((1, 1, m), lambda i: (i, 0, 0), memory_space=pltpu.SMEM),
                  pl.BlockSpec((t, w), lambda i: (0, 0), pipeline_mode=pl.Buffered(1))],
        out_specs=pl.BlockSpec((m, w), lambda i: (i, 0)),
        out_shape=jax.ShapeDtypeStruct((p, w), hp.dtype),
        compiler_params=_cparams(("arbitrary",)),
        name="moe_gather",
    )(tok_pad.reshape(p // m, 1, m), hp)


def _moe_up_kernel(blk0_ref, nblk_ref, w0_ref, w1_ref, w2_ref, w3_ref, b_ref, xs_ref, act_ref,
                   wbf_scr, x_scr, o_scr, sem_in, sem_out):
    e = pl.program_id(0)
    f = pl.program_id(1)
    n = nblk_ref[e]
    b0 = blk0_ref[e]
    m = x_scr.shape[1]
    w_refs = (w0_ref, w1_ref, w2_ref, w3_ref)
    kq, ft = w0_ref.shape[1:]
    half = ft // 2

    def rows(b):
        return pl.ds(pl.multiple_of((b0 + b) * m, m), m)

    def x_slot(b):
        return jnp.where(b < X_RESIDENT, b, X_RESIDENT + (b & 1))

    def x_needed(b):
        return jnp.logical_or(f == 0, b >= X_RESIDENT)

    def x_copy(b):
        s = x_slot(b)
        return pltpu.make_async_copy(xs_ref.at[rows(b)], x_scr.at[s], sem_in.at[s])

    def o_copy(b, s):
        return pltpu.make_async_copy(o_scr.at[s], act_ref.at[f, rows(b)], sem_out.at[s])

    @pl.when(n > 0)
    def _():
        @pl.when(x_needed(0))
        def _():
            x_copy(0).start(priority=1)

        for q, w_ref in enumerate(w_refs):
            wbf_scr[q * kq:(q + 1) * kq, :] = w_ref[0].astype(BF16)
        even = (lax.broadcasted_iota(I32, (m, LANES), 1) & 1) == 0

        def body(b, carry):
            s = b & 1
            @pl.when(x_needed(b))
            def _():
                x_copy(b).wait()

            @pl.when(jnp.logical_and(b + 1 < n, x_needed(b + 1)))
            def _():
                x_copy(b + 1).start(priority=1)

            @pl.when(b >= 2)
            def _():
                o_copy(b - 2, s).wait()

            x_lo, x_hi = _unpack_bf16_pairs(x_scr[x_slot(b)])
            kh = x_lo.shape[1]
            for g in range(ft // (2 * LANES)):
                cols = slice(2 * g * LANES, (2 * g + 2) * LANES)
                h = (jnp.dot(x_lo, wbf_scr[:kh, cols], preferred_element_type=F32)
                     + jnp.dot(x_hi, wbf_scr[kh:, cols], preferred_element_type=F32)
                     + b_ref[0, :, cols])
                ha, hb = h[:, :LANES], h[:, LANES:]
                glu = jnp.minimum(jnp.where(even, ha, pltpu.roll(hb, 1, 1)), SWIGLU_LIMIT)
                lin = jnp.where(even, pltpu.roll(ha, LANES - 1, 1), hb)
                lin = jnp.clip(lin, -SWIGLU_LIMIT, SWIGLU_LIMIT) + 1.0
                act = glu * _sigmoid(SWIGLU_ALPHA * glu) * lin
                o_scr[s, :, g * LANES:(g + 1) * LANES] = act.astype(o_scr.dtype)
            o_copy(b, s).start(priority=1)
            return carry

        lax.fori_loop(0, n, body, 0)

        @pl.when(n >= 2)
        def _():
            o_copy(n - 2, n & 1).wait()
        o_copy(n - 1, (n - 1) & 1).wait()

    @pl.when(e == pl.num_programs(0) - 1)
    def _():
        o_scr[0] = jnp.zeros(o_scr.shape[1:], o_scr.dtype)

        nz = act_ref.shape[1] // m - b0
        lax.fori_loop(n, nz, lambda b, c: (o_copy(b, 0).start(), c)[1], 0)
        lax.fori_loop(n, nz, lambda b, c: (o_copy(b, 0).wait(), c)[1], 0)


def _moe_up(xs, w_gate_up, b_gate_up, blk0, nblk):
    p, w = xs.shape
    d = 2 * w
    m = MOE_BLOCK
    ft = MOE_FT
    ne = w_gate_up.shape[0]
    nf = w_gate_up.shape[2] // ft

    def wslice(q, e, f, b0, nb):
        return (e, q, f)

    grid_spec = pltpu.PrefetchScalarGridSpec(
        num_scalar_prefetch=2,
        grid=(ne, nf),
        in_specs=[pl.BlockSpec((1, d // W_STREAMS, ft), functools.partial(wslice, q))
                  for q in range(W_STREAMS)] + [
                  pl.BlockSpec((1, 1, ft), lambda e, f, b0, nb: (e, 0, f)),
                  pl.BlockSpec(memory_space=pl.ANY)],
        out_specs=pl.BlockSpec(memory_space=pl.ANY),
        scratch_shapes=[pltpu.VMEM((d, ft), BF16),
                        pltpu.VMEM((X_RESIDENT + 2, m, w), xs.dtype),
                        pltpu.VMEM((2, m, ft // 2), BF16),
                        pltpu.SemaphoreType.DMA((X_RESIDENT + 2,)),
                        pltpu.SemaphoreType.DMA((2,))],
    )
    return pl.pallas_call(
        _moe_up_kernel,
        grid_spec=grid_spec,
        out_shape=jax.ShapeDtypeStruct((nf, p, ft // 2), BF16),
        compiler_params=_cparams(("arbitrary", "arbitrary")),
        name="moe_up",
    )(blk0, nblk, *([w_gate_up] * W_STREAMS), b_gate_up, xs)


def _hidden_perm():
    r = lax.broadcasted_iota(I32, (LANES, LANES), 0)
    c = lax.broadcasted_iota(I32, (LANES, LANES), 1)
    src = (r >> 1) + (r & 1) * (LANES // 2)
    return jnp.where(c == src, 1.0, 0.0).astype(BF16)


def _moe_down_kernel(be_ref, na_ref, nreal_ref, dest_ref, a_ref, wd0_ref, wd1_ref, wd2_ref, wd3_ref,
                     bd_ref, yt_ref, wbf_scr, y_scr, sem):
    i = pl.program_id(0)
    nb = pl.num_programs(0)
    nf, m, fh = a_ref.shape
    slot = i % 2

    def row_copy(s, r, d):
        return pltpu.make_async_copy(y_scr.at[s, pl.ds(r, 1)], yt_ref.at[pl.ds(d, 1)], sem.at[s])

    def drain(s, count):
        @pl.when(count == m)
        def _():
            pltpu.make_async_copy(y_scr.at[s], yt_ref.at[pl.ds(0, m)], sem.at[s]).wait()

        @pl.when(count != m)
        def _():
            def wbody(_, carry):
                row_copy(s, 0, 0).wait()
                return carry
            lax.fori_loop(0, count, wbody, 0)

    @pl.when(i >= 2)
    def _():
        drain(slot, nreal_ref[i - 2])

    changed = jnp.where(i == 0, 1, be_ref[i] - be_ref[jnp.maximum(i - 1, 0)])

    @pl.when(changed != 0)
    def _():
        perm = _hidden_perm()
        for q, wd_ref in enumerate((wd0_ref, wd1_ref, wd2_ref, wd3_ref)):
            kq = wd_ref.shape[1]
            for g in range(kq // LANES):
                rows = slice(g * LANES, (g + 1) * LANES)
                wbf_scr[q * kq + g * LANES:q * kq + (g + 1) * LANES, :] = jnp.dot(
                    perm, wd_ref[0, rows, :].astype(BF16), preferred_element_type=F32).astype(BF16)

    @pl.when(i < na_ref[0])
    def _():
        y = bd_ref[0]
        for j in range(nf):
            y = y + jnp.dot(a_ref[j], wbf_scr[j * fh:(j + 1) * fh, :], preferred_element_type=F32)
        y_scr[slot] = _pack_bf16_pairs(y)

        n = nreal_ref[i]
        unroll = 8

        def sbody8(q, carry):
            for u in range(unroll):
                r = q * unroll + u
                row_copy(slot, r, dest_ref[0, 0, r]).start()
            return carry

        def sbody1(r, carry):
            row_copy(slot, r, dest_ref[0, 0, r]).start()
            return carry

        lax.fori_loop(0, n // unroll, sbody8, 0)
        lax.fori_loop((n // unroll) * unroll, n, sbody1, 0)

    @pl.when(i == nb - 1)
    def _():
        @pl.when(i >= 1)
        def _():
            drain(1 - slot, nreal_ref[i - 1])
        drain(slot, nreal_ref[i])


def _moe_down(act, w_down, b_down, blk_expert, n_active, nreal, dest, n_rows):
    nf, p, fh = act.shape
    f = nf * fh
    m = MOE_BLOCK
    d = w_down.shape[2]

    def wslice(q, i, be, na, nr):
        return (be[i], q, 0)

    grid_spec = pltpu.PrefetchScalarGridSpec(
        num_scalar_prefetch=3,
        grid=(p // m,),
        in_specs=[pl.BlockSpec((1, 1, m), lambda i, be, na, nr: (i, 0, 0), memory_space=pltpu.SMEM),
                  pl.BlockSpec((nf, m, fh), lambda i, be, na, nr: (0, i, 0)),
                  *[pl.BlockSpec((1, f // W_STREAMS, d), functools.partial(wslice, q))
                    for q in range(W_STREAMS)],
                  pl.BlockSpec((1, 1, d), lambda i, be, na, nr: (be[i], 0, 0))],
        out_specs=pl.BlockSpec(memory_space=pl.ANY),
        scratch_shapes=[pltpu.VMEM((f, d), BF16),
                        pltpu.VMEM((2, m, d // 2), jnp.uint32),
                        pltpu.SemaphoreType.DMA((2,))],
    )
    return pl.pallas_call(
        _moe_down_kernel,
        grid_spec=grid_spec,
        out_shape=jax.ShapeDtypeStruct((n_rows, d // 2), jnp.uint32),
        compiler_params=_cparams(("arbitrary",)),
        name="moe_down",
    )(blk_expert, n_active, nreal, dest.reshape(p // m, 1, m), act, *([w_down] * W_STREAMS), b_down)


def _combine_kernel(h_ref, y0_ref, y1_ref, y2_ref, y3_ref, gate_ref, g2_ref, b2_ref, o_ref):
    lo, hi = None, None
    for j, y_ref in enumerate((y0_ref, y1_ref, y2_ref, y3_ref)):
        yl, yh = _unpack_pairs_f32(y_ref[...])
        g = gate_ref[:, j:j + 1]
        lo = g * yl if lo is None else lo + g * yl
        hi = g * yh if hi is None else hi + g * yh
    ffn = jnp.concatenate([lo, hi], axis=1)
    o_ref[...] = _layer_norm(DEEPNORM_ALPHA * h_ref[...] + ffn, g2_ref[...], b2_ref[...])


def _combine(h, yt, gate, g2, b2):
    t, d = h.shape
    tm = TOKEN_TILE
    nt = t // tm

    def choice(j):
        return pl.BlockSpec((tm, d // 2), lambda i: (j * nt + i, 0))

    return pl.pallas_call(
        _combine_kernel,
        grid=(nt,),
        in_specs=[pl.BlockSpec((tm, d), lambda i: (i, 0)),
                  choice(0), choice(1), choice(2), choice(3),
                  pl.BlockSpec((tm, TOP_K), lambda i: (i, 0)),
                  pl.BlockSpec((1, d), lambda i: (0, 0)),
                  pl.BlockSpec((1, d), lambda i: (0, 0))],
        out_specs=pl.BlockSpec((tm, d), lambda i: (i, 0)),
        out_shape=jax.ShapeDtypeStruct((t, d), F32),
        compiler_params=_cparams(("parallel",)),
        name="combine_ln2",
    )(h, yt, yt, yt, yt, gate, g2, b2)


def _route(top_e, lrank, tilecnt):
    t = top_e.shape[0]
    m = MOE_BLOCK
    n_assign = t * TOP_K
    n_blocks = n_assign // m + N_EXPERTS
    p = n_blocks * m
    tc = tilecnt.reshape(-1, N_EXPERTS)
    base = jnp.cumsum(tc, axis=0) - tc
    counts = jnp.sum(tc, axis=0)
    padded = (counts + m - 1) // m * m
    pend = jnp.cumsum(padded)
    pstart = pend - padded
    base_tok = jnp.repeat(base, TOKEN_TILE, axis=0)
    rank = lrank + jnp.take_along_axis(base_tok, top_e, axis=1)
    slot = (pstart[top_e] + rank).reshape(-1)
    a_idx = jnp.arange(n_assign, dtype=I32)
    dest = jnp.full((p,), -1, I32).at[slot].set((a_idx % TOP_K) * t + a_idx // TOP_K,
                                               unique_indices=True)
    tok_pad = jnp.where(dest >= 0, dest % t, 0)
    blk_start = jnp.arange(n_blocks, dtype=I32) * m
    blk_expert = jnp.minimum(jnp.sum((pend[None, :] <= blk_start[:, None]).astype(I32), axis=1),
                             N_EXPERTS - 1)
    n_active = (pend[-1:] // m).astype(I32)
    real_end = (pstart + counts)[blk_expert]
    nreal = jnp.where(blk_start < pend[-1], jnp.clip(real_end - blk_start, 0, m), 0).astype(I32)
    return tok_pad, dest, blk_expert, n_active, nreal, (pstart // m).astype(I32), (padded // m).astype(I32)


def kernel(x_prompt, x_sample, cache_attn_k, cache_attn_v, state_hgrn, w_in, hgrn_lb_logits,
           hgrn_norm_w, w_branch_a, w_branch_b, w_out, ln1_g, ln1_b, w_router, b_router,
           w_gate_up, b_gate_up, w_down, b_down, ln2_g, ln2_b):
    assert w_in.shape[0] == DEPTH == 1
    bp, lp, d = x_prompt.shape
    bs, ls, _ = x_sample.shape
    tp, ts = bp * lp, bs * ls
    t = tp + ts
    keep = min(max(w for w, _ in DILATION_PAIRS), lp)

    xp2, xs2 = x_prompt.reshape(tp, d), x_sample.reshape(ts, d)
    xb = jnp.concatenate([xp2.astype(BF16), xs2.astype(BF16)], axis=0)
    wi = w_in[0]
    c1, c2 = P1_WIDTH, P1_WIDTH + P2_WIDTH
    p1 = _matmul(xb, wi[:, :c1].astype(BF16), 768, 1024, F32, "inproj_hgrn")
    p2 = _matmul(xb, wi[:, c1:c2].astype(BF16), 768, 1280, F32, "inproj_attn")
    p3 = _matmul(xb, wi[:, c2:].astype(BF16), 768, 1024, F32, "inproj_gate")

    oa_p, st_p = _hgrn(p1, hgrn_lb_logits, hgrn_norm_w[0], None, bp, lp, 0)
    oa_s, st_s = _hgrn(p1, hgrn_lb_logits, hgrn_norm_w[0],
                       state_hgrn.reshape(bs, HG_HEADS, HG_DIM, HG_DIM), bs, ls, tp)

    n = jnp.arange(1, ATT_Q_HEADS + 1, dtype=F32)
    slopes = jnp.power(2.0, -8.0 * n / ATT_Q_HEADS).reshape(N_GROUPS, ATT_KV_HEADS)
    ctx = cache_attn_k.shape[2]
    ob_p = _attn_prompt(p2, slopes, bp, lp)
    ob_s = _attn_sample(p2, cache_attn_k.reshape(bs, ctx * ATT_KV_HEADS, ATT_HEAD_DIM),
                        cache_attn_v.reshape(bs, ctx * ATT_KV_HEADS, ATT_HEAD_DIM), slopes, bs, ls, tp)

    wr = w_router[0]
    wr0 = wr.astype(BF16)
    wr1 = (wr - wr0.astype(F32)).astype(BF16)
    h, hp, top_e, gate, lrank, tilecnt = _merge(
        xp2, xs2, oa_p.reshape(tp, HG_WIDTH), oa_s.reshape(ts, HG_WIDTH), ob_p,
        ob_s.reshape(ts, ATT_KV_WIDTH), p3, w_branch_a[0].astype(BF16), w_branch_b[0].astype(BF16),
        w_out[0].astype(BF16), ln1_g, ln1_b, wr0, wr1, b_router)

    tok_pad, dest, blk_expert, n_active, nreal, blk0, nblk = _route(top_e, lrank, tilecnt)
    xs = _moe_gather(hp, tok_pad)
    act = _moe_up(xs, w_gate_up.reshape(N_EXPERTS, d, 2 * D_FF),
                  b_gate_up.reshape(N_EXPERTS, 1, 2 * D_FF), blk0, nblk)
    yt = _moe_down(act, w_down.reshape(N_EXPERTS, D_FF, d), b_down.reshape(N_EXPERTS, 1, d),
                   blk_expert, n_active, nreal, dest, t * TOP_K)
    y = _combine(h, yt, gate, ln2_g, ln2_b)

    k_new = p2[:, ATT_Q_HEADS * ATT_HEAD_DIM:(ATT_Q_HEADS + ATT_KV_HEADS) * ATT_HEAD_DIM]
    v_new = p2[:, (ATT_Q_HEADS + ATT_KV_HEADS) * ATT_HEAD_DIM:]

    def kv_prompt(a):
        return a[:tp].reshape(bp, lp, ATT_KV_HEADS, ATT_HEAD_DIM)[:, lp - keep:][None]

    def kv_sample(a):
        return a[tp:].reshape(bs, ls, ATT_KV_HEADS, ATT_HEAD_DIM)[None]

    return (y[:tp].reshape(bp, lp, d), y[tp:].reshape(bs, ls, d),
            kv_prompt(k_new), kv_prompt(v_new), st_p[None],
            kv_sample(k_new), kv_sample(v_new), st_s[None])
```

```python
import functools
import math

import jax
import jax.numpy as jnp
from jax import lax
from jax.experimental import pallas as pl
from jax.experimental.pallas import tpu as pltpu

F32 = jnp.float32
BF16 = jnp.bfloat16
I32 = jnp.int32

D_MODEL = 2048
DEPTH = 1
HG_HEADS = 8
HG_DIM = 128
HG_WIDTH = HG_HEADS * HG_DIM
HG_CHUNK = 128
ATT_HEAD_DIM = 128
DILATION_PAIRS = ((128, 1), (512, 4), (2048, 16))
N_GROUPS = len(DILATION_PAIRS)
ATT_KV_HEADS = 4
ATT_Q_HEADS = N_GROUPS * ATT_KV_HEADS
ATT_TAPS = 128
ATT_KV_WIDTH = ATT_KV_HEADS * ATT_HEAD_DIM
ATT_SCALE = ATT_HEAD_DIM ** -0.5
N_EXPERTS = 32
TOP_K = 4
D_FF = D_MODEL
SWIGLU_ALPHA = 1.702
SWIGLU_LIMIT = 7.0
LN_EPS = 1e-5
RMS_EPS = 1e-5
DEEPNORM_ALPHA = (2 * DEPTH) ** 0.25

P1_WIDTH = 4 * HG_WIDTH
P2_WIDTH = (ATT_Q_HEADS + 2 * ATT_KV_HEADS) * ATT_HEAD_DIM
P3_WIDTH = 2 * D_MODEL

LANES = 128
SUBLANES = 8
VMEM_LIMIT = 60 * 1024 * 1024

TOKEN_TILE = 256
MOE_BLOCK = 256
MOE_FT = 2048
W_STREAMS = 4
X_RESIDENT = 8
NEG = -1e30

_NT = (((1,), (1,)), ((), ()))
_TN = (((0,), (0,)), ((), ()))


def _cparams(sem):
    return pltpu.CompilerParams(dimension_semantics=sem, vmem_limit_bytes=VMEM_LIMIT)


def _sigmoid(x):
    return 1.0 / (1.0 + jnp.exp(-x))


def _mm_kernel(x_ref, w_ref, o_ref):
    o_ref[...] = jnp.dot(x_ref[...], w_ref[...], preferred_element_type=F32).astype(o_ref.dtype)


def _matmul(x, w, tm, tn, out_dtype, name):
    m, k = x.shape
    n = w.shape[1]
    assert m % tm == 0 and n % tn == 0
    return pl.pallas_call(
        _mm_kernel,
        grid=(n // tn, m // tm),
        in_specs=[pl.BlockSpec((tm, k), lambda j, i: (i, 0)),
                  pl.BlockSpec((k, tn), lambda j, i: (0, j))],
        out_specs=pl.BlockSpec((tm, tn), lambda j, i: (i, j)),
        out_shape=jax.ShapeDtypeStruct((m, n), out_dtype),
        compiler_params=_cparams(("parallel", "parallel")),
        name=name,
    )(x, w)


def _level_ref(g_scr, m, sl, chunk, sub):
    pieces = []
    for j in range(chunk // SUBLANES):
        if 2 * m >= SUBLANES:
            row = ((SUBLANES * j) // (2 * m)) * (2 * m) + m - 1
            pieces.append(jnp.broadcast_to(g_scr[row:row + 1, sl], (SUBLANES, LANES)))
        else:
            acc = None
            for u in range(SUBLANES // (2 * m)):
                row = SUBLANES * j + 2 * m * u + m - 1
                b = jnp.broadcast_to(g_scr[row:row + 1, sl], (SUBLANES, LANES))
                acc = b if acc is None else jnp.where(sub // (2 * m) == u, b, acc)
            pieces.append(acc)
    return pieces[0] if len(pieces) == 1 else jnp.concatenate(pieces, axis=0)


def _hgrn_kernel(lbl_ref, nw_ref, hq_ref, hf_ref, hi_ref, hg_ref, *rest, chunk, has_s0):
    if has_s0:
        s0_ref, o_ref, sout_ref, st_scr, g_scr = rest
    else:
        o_ref, sout_ref, st_scr, g_scr = rest
    c = pl.program_id(1)
    nc = pl.num_programs(1)

    @pl.when(c == 0)
    def _init():
        for h in range(HG_HEADS):
            if has_s0:
                st_scr[h] = s0_ref[0, h].T
            else:
                st_scr[h] = jnp.zeros((HG_DIM, HG_DIM), F32)

    lbl = lbl_ref[...]
    ex = jnp.exp(lbl - jnp.max(lbl, axis=0, keepdims=True))
    lb = ex[0:1] / jnp.sum(ex, axis=0, keepdims=True)

    f = lb + (1.0 - lb) * _sigmoid(hf_ref[...])
    logf = jnp.log(f)
    kall = 1.0 - f

    r_cc = lax.broadcasted_iota(I32, (chunk, chunk), 0)
    c_cc = lax.broadcasted_iota(I32, (chunk, chunk), 1)
    tri = jnp.where(r_cc >= c_cc, 1.0, 0.0).astype(BF16)
    p0 = logf.astype(BF16)
    r1 = logf - p0.astype(F32)
    p1 = r1.astype(BF16)
    p2 = (r1 - p1.astype(F32)).astype(BF16)
    g_all = (jnp.dot(tri, p0, preferred_element_type=F32)
             + jnp.dot(tri, p1, preferred_element_type=F32)
             + jnp.dot(tri, p2, preferred_element_type=F32))
    g_scr[...] = g_all

    sub = lax.broadcasted_iota(I32, (SUBLANES, LANES), 0)
    row_c = lax.broadcasted_iota(I32, (chunk, LANES), 0)
    nw = nw_ref[...]

    for h in range(HG_HEADS):
        sl = slice(h * HG_DIM, (h + 1) * HG_DIM)
        hq = hq_ref[:, sl]
        q = hq * _sigmoid(hq)
        k = kall[:, sl]
        g = g_all[:, sl]
        v = hi_ref[:, sl]
        vb = v.astype(BF16)

        p = lax.dot_general(q.astype(BF16), k.astype(BF16), _NT, preferred_element_type=F32)
        a = jnp.where(r_cc == c_cc, p, 0.0)
        m = 1
        while m < chunk:
            ref = _level_ref(g_scr, m, sl, chunk, sub)
            e = jnp.exp(-jnp.abs(g - ref))
            odd = ((row_c // m) & 1) == 1
            x = (jnp.where(odd, q, k) * e).astype(BF16)
            p = lax.dot_general(x, x, _NT, preferred_element_type=F32)
            pair = (r_cc // (2 * m)) == (c_cc // (2 * m))
            sel = jnp.where(pair, ((r_cc // m) & 1) - ((c_cc // m) & 1), 0) == 1
            a = jnp.where(sel, p, a)
            m *= 2

        st = st_scr[h]
        o = jnp.dot(a.astype(BF16), vb, preferred_element_type=F32)
        qe = (q * jnp.exp(g)).astype(BF16)
        o = o + lax.dot_general(qe, st.astype(BF16), _NT, preferred_element_type=F32)

        ms = jnp.mean(o * o, axis=-1, keepdims=True)
        hg = hg_ref[:, sl]
        out = o * lax.rsqrt(ms + RMS_EPS) * nw * (hg * _sigmoid(hg))
        o_ref[0, :, sl] = out.astype(o_ref.dtype)

        g_last = g[chunk - 1:chunk, :]
        kd = (k * jnp.exp(g_last - g)).astype(BF16)
        st_scr[h] = st * jnp.exp(g_last) + lax.dot_general(vb, kd, _TN, preferred_element_type=F32)

    @pl.when(c == nc - 1)
    def _fin():
        for h in range(HG_HEADS):
            sout_ref[0, h] = st_scr[h].T


def _hgrn(p1, lb_logits, norm_w, s0, batch, seq, row0):
    chunk = math.gcd(seq, HG_CHUNK)
    nchunk = seq // chunk
    assert row0 % chunk == 0
    rb0 = row0 // chunk

    def col(idx):
        return pl.BlockSpec((chunk, HG_WIDTH), lambda b, c: (rb0 + b * nchunk + c, idx))

    in_specs = [pl.BlockSpec(lb_logits.shape, lambda b, c: (0, 0)),
                pl.BlockSpec((1, HG_DIM), lambda b, c: (0, 0)),
                col(0), col(1), col(2), col(3)]
    args = [lb_logits, norm_w.reshape(1, HG_DIM), p1, p1, p1, p1]
    if s0 is not None:
        in_specs.append(pl.BlockSpec((1, HG_HEADS, HG_DIM, HG_DIM), lambda b, c: (b, 0, 0, 0)))
        args.append(s0)
    return pl.pallas_call(
        functools.partial(_hgrn_kernel, chunk=chunk, has_s0=s0 is not None),
        grid=(batch, nchunk),
        in_specs=in_specs,
        out_specs=[pl.BlockSpec((1, chunk, HG_WIDTH), lambda b, c: (b, c, 0)),
                   pl.BlockSpec((1, HG_HEADS, HG_DIM, HG_DIM), lambda b, c: (b, 0, 0, 0))],
        out_shape=[jax.ShapeDtypeStruct((batch, seq, HG_WIDTH), BF16),
                   jax.ShapeDtypeStruct((batch, HG_HEADS, HG_DIM, HG_DIM), F32)],
        scratch_shapes=[pltpu.VMEM((HG_HEADS, HG_DIM, HG_DIM), F32),
                        pltpu.VMEM((chunk, HG_WIDTH), F32)],
        compiler_params=_cparams(("parallel", "arbitrary")),
        name="hgrn_seq%d" % seq,
    )(*args)


QB = 128
ATTN_UNROLL = 8


def _attn_prompt_kernel(slopes_ref, q0_ref, q1_ref, q2_ref, k_ref, v_ref, o_ref,
                        og_scr, lse_scr, bias_scr, *, seq):
    h = pl.program_id(1)
    il = lax.broadcasted_iota(I32, (QB, 2 * QB), 0)
    jl = lax.broadcasted_iota(I32, (QB, 2 * QB), 1)
    delta = il + QB - jl
    band = jnp.where(delta >= 0, delta, ATT_TAPS + 1) <= ATT_TAPS
    prev_half = jl < QB
    q_refs = (q0_ref, q1_ref, q2_ref)

    for g, (window, dil) in enumerate(DILATION_PAIRS):
        assert window // dil == ATT_TAPS
        slope = slopes_ref[g, h]
        bias_scr[g] = jnp.where(band, (-slope * dil) * delta.astype(F32), NEG)
        nbr = (seq // dil) // QB
        q_ref = q_refs[g]

        def rows(start, dil=dil):
            return pl.ds(start, QB) if dil == 1 else pl.ds(start, QB, stride=dil)

        def body(ib, carry, g=g, dil=dil, nbr=nbr, q_ref=q_ref, rows=rows):
            res = ib // nbr
            jb = ib - res * nbr
            qs = res + dil * (jb * QB)
            ps = res + dil * jnp.maximum(jb * QB - QB, 0)
            qv = q_ref[rows(qs), :].astype(BF16)
            kb = jnp.concatenate([k_ref[rows(ps), :], k_ref[rows(qs), :]], axis=0).astype(BF16)
            vb = jnp.concatenate([v_ref[rows(ps), :], v_ref[rows(qs), :]], axis=0).astype(BF16)
            s = lax.dot_general(qv, kb, _NT, preferred_element_type=F32) * ATT_SCALE + bias_scr[g]
            s = jnp.where(jnp.where(prev_half, jb, 1) == 0, NEG, s)
            mx = jnp.max(s, axis=-1, keepdims=True)
            p = jnp.exp(s - mx)
            l = jnp.sum(p, axis=-1, keepdims=True)
            o = jnp.dot(p.astype(BF16), vb, preferred_element_type=F32) / l
            og_scr[g, rows(qs), :] = o
            lse_scr[g, rows(qs), :] = jnp.broadcast_to(mx + jnp.log(l), (QB, LANES))
            return carry

        lax.fori_loop(0, seq // QB, body, 0, unroll=ATTN_UNROLL)

    def merge(tb, carry):
        r = pl.ds(pl.multiple_of(tb * QB, QB), QB)
        l0, l1, l2 = lse_scr[0, r, :], lse_scr[1, r, :], lse_scr[2, r, :]
        mx = jnp.maximum(jnp.maximum(l0, l1), l2)
        w0, w1, w2 = jnp.exp(l0 - mx), jnp.exp(l1 - mx), jnp.exp(l2 - mx)
        o = (w0 * og_scr[0, r, :] + w1 * og_scr[1, r, :] + w2 * og_scr[2, r, :]) / (w0 + w1 + w2)
        o_ref[r, :] = o.astype(o_ref.dtype)
        return carry

    lax.fori_loop(0, seq // QB, merge, 0, unroll=ATTN_UNROLL)


def _attn_prompt(p2, slopes, batch, seq):
    assert seq % (QB * max(d for _, d in DILATION_PAIRS)) == 0

    def col(fn):
        return pl.BlockSpec((seq, ATT_HEAD_DIM), lambda b, h: (b, fn(h)))

    return pl.pallas_call(
        functools.partial(_attn_prompt_kernel, seq=seq),
        grid=(batch, ATT_KV_HEADS),
        in_specs=[pl.BlockSpec(memory_space=pltpu.SMEM),
                  col(lambda h: h), col(lambda h: ATT_KV_HEADS + h), col(lambda h: 2 * ATT_KV_HEADS + h),
                  col(lambda h: ATT_Q_HEADS + h), col(lambda h: ATT_Q_HEADS + ATT_KV_HEADS + h)],
        out_specs=pl.BlockSpec((seq, ATT_HEAD_DIM), lambda b, h: (b, h)),
        out_shape=jax.ShapeDtypeStruct((batch * seq, ATT_KV_WIDTH), BF16),
        scratch_shapes=[pltpu.VMEM((N_GROUPS, seq, ATT_HEAD_DIM), F32),
                        pltpu.VMEM((N_GROUPS, seq, LANES), F32),
                        pltpu.VMEM((N_GROUPS, QB, 2 * QB), F32)],
        compiler_params=_cparams(("parallel", "parallel")),
        name="attn_prompt",
    )(slopes, p2, p2, p2, p2, p2)


def _attn_sample_kernel(slopes_ref, qa_ref, qb_ref, qc_ref, kn_ref, vn_ref, kc_ref, vc_ref, o_ref,
                        *, ctx, nq):
    rows = N_GROUPS * nq
    gi = lax.broadcasted_iota(I32, (rows, 1), 0) // nq
    dil = jnp.where(gi == 0, DILATION_PAIRS[0][1], jnp.where(gi == 1, DILATION_PAIRS[1][1], DILATION_PAIRS[2][1]))
    win = jnp.where(gi == 0, DILATION_PAIRS[0][0], jnp.where(gi == 1, DILATION_PAIRS[1][0], DILATION_PAIRS[2][0]))

    def dist_valid(shape, key0, nkeys):
        r = lax.broadcasted_iota(I32, shape, 0)
        t = r - (r // nq) * nq
        col = lax.broadcasted_iota(I32, shape, 1)
        dist = ctx + t - (col + key0)
        bad = jnp.where(dist >= 0, dist & (dil - 1), 1)
        bad = jnp.where(dist <= win, bad, 1)
        bad = jnp.where(col < nkeys, bad, 1)
        return dist.astype(F32), bad == 0

    dist_c, ok_c = dist_valid((rows, ctx), 0, ctx)
    dist_n, ok_n = dist_valid((rows, LANES), ctx, nq)
    zpad = jnp.zeros((LANES - nq, ATT_HEAD_DIM), F32)

    for h in range(ATT_KV_HEADS):
        hs = slice(h * ATT_HEAD_DIM, (h + 1) * ATT_HEAD_DIM)
        slope = jnp.where(gi == 0, slopes_ref[0, h], jnp.where(gi == 1, slopes_ref[1, h], slopes_ref[2, h]))
        qh = jnp.concatenate([qa_ref[:, hs], qb_ref[:, hs], qc_ref[:, hs]], axis=0).astype(BF16)
        kc = kc_ref[0, pl.ds(h, ctx, stride=ATT_KV_HEADS), :].astype(BF16)
        vc = vc_ref[0, pl.ds(h, ctx, stride=ATT_KV_HEADS), :].astype(BF16)
        kn = jnp.concatenate([kn_ref[:, hs], zpad], axis=0).astype(BF16)
        vn = jnp.concatenate([vn_ref[:, hs], zpad], axis=0).astype(BF16)
        sc = lax.dot_general(qh, kc, _NT, preferred_element_type=F32) * ATT_SCALE - slope * dist_c
        sn = lax.dot_general(qh, kn, _NT, preferred_element_type=F32) * ATT_SCALE - slope * dist_n
        sc = jnp.where(ok_c, sc, NEG)
        sn = jnp.where(ok_n, sn, NEG)
        mx = jnp.maximum(jnp.max(sc, axis=-1, keepdims=True), jnp.max(sn, axis=-1, keepdims=True))
        pc = jnp.exp(sc - mx)
        pn = jnp.exp(sn - mx)
        l = jnp.sum(pc, axis=-1, keepdims=True) + jnp.sum(pn, axis=-1, keepdims=True)
        o = (jnp.dot(pc.astype(BF16), vc, preferred_element_type=F32)
             + jnp.dot(pn.astype(BF16), vn, preferred_element_type=F32)) / l
        lse = mx + jnp.log(l)
        l0, l1, l2 = lse[0:nq], lse[nq:2 * nq], lse[2 * nq:3 * nq]
        m3 = jnp.maximum(jnp.maximum(l0, l1), l2)
        w0, w1, w2 = jnp.exp(l0 - m3), jnp.exp(l1 - m3), jnp.exp(l2 - m3)
        out = (w0 * o[0:nq] + w1 * o[nq:2 * nq] + w2 * o[2 * nq:3 * nq]) / (w0 + w1 + w2)
        o_ref[0, :, hs] = out.astype(o_ref.dtype)


def _attn_sample(p2, cache_k, cache_v, slopes, batch, nq, row0):
    ctx = cache_k.shape[1] // ATT_KV_HEADS
    assert row0 % nq == 0 and nq == SUBLANES and ctx >= max(w for w, _ in DILATION_PAIRS)
    rb0 = row0 // nq

    def col(idx):
        return pl.BlockSpec((nq, ATT_KV_WIDTH), lambda b: (rb0 + b, idx))

    cache_spec = pl.BlockSpec((1, ctx * ATT_KV_HEADS, ATT_HEAD_DIM), lambda b: (b, 0, 0))
    return pl.pallas_call(
        functools.partial(_attn_sample_kernel, ctx=ctx, nq=nq),
        grid=(batch,),
        in_specs=[pl.BlockSpec(memory_space=pltpu.SMEM),
                  col(0), col(1), col(2), col(3), col(4), cache_spec, cache_spec],
        out_specs=pl.BlockSpec((1, nq, ATT_KV_WIDTH), lambda b: (b, 0, 0)),
        out_shape=jax.ShapeDtypeStruct((batch, nq, ATT_KV_WIDTH), BF16),
        compiler_params=_cparams(("parallel",)),
        name="attn_sample",
    )(slopes, p2, p2, p2, p2, p2, cache_k, cache_v)


def _layer_norm(y, g, b):
    mu = jnp.mean(y, axis=-1, keepdims=True)
    yc = y - mu
    var = jnp.mean(yc * yc, axis=-1, keepdims=True)
    return yc * lax.rsqrt(var + LN_EPS) * g + b


def _merge_kernel(xp_ref, xs_ref, oap_ref, oas_ref, obp_ref, obs_ref, ga_ref, gb_ref, wa_ref, wb_ref,
                  wo_ref, g1_ref, b1_ref, wr0_ref, wr1_ref, br_ref,
                  h_ref, hp_ref, tope_ref, gate_ref, lrank_ref, cnt_ref, *, prompt_tiles):
    tm = xp_ref.shape[0]
    is_prompt = pl.program_id(0) < prompt_tiles
    x = jnp.where(is_prompt, xp_ref[...], xs_ref[...])
    oa = jnp.where(is_prompt, oap_ref[...], oas_ref[...])
    ob = jnp.where(is_prompt, obp_ref[...], obs_ref[...])
    a = jnp.dot(oa, wa_ref[...], preferred_element_type=F32)
    b = jnp.dot(ob, wb_ref[...], preferred_element_type=F32)
    merged = _sigmoid(ga_ref[...]) * a + _sigmoid(gb_ref[...]) * b
    z = jnp.dot(merged.astype(BF16), wo_ref[...], preferred_element_type=F32)
    hh = _layer_norm(DEEPNORM_ALPHA * x + z, g1_ref[...], b1_ref[...])
    h_ref[...] = hh
    hp_ref[...] = _pack_bf16_pairs(hh)

    h0 = hh.astype(BF16)
    h1 = (hh - h0.astype(F32)).astype(BF16)
    logits = (jnp.dot(h0, wr0_ref[...], preferred_element_type=F32)
              + jnp.dot(h1, wr0_ref[...], preferred_element_type=F32)
              + jnp.dot(h0, wr1_ref[...], preferred_element_type=F32)) + br_ref[...]

    lane = lax.broadcasted_iota(I32, (tm, N_EXPERTS), 1)
    work = logits
    vals, idxs = [], []
    onehot = jnp.zeros((tm, N_EXPERTS), F32)
    for _ in range(TOP_K):
        mx = jnp.max(work, axis=-1, keepdims=True)
        idx = jnp.min(jnp.where(work == mx, lane, N_EXPERTS), axis=-1, keepdims=True)
        hit = lane == idx
        vals.append(mx)
        idxs.append(idx)
        onehot = jnp.where(hit, 1.0, onehot)
        work = jnp.where(hit, -jnp.inf, work)
    ex = [jnp.exp(v - vals[0]) for v in vals]
    den = ex[0] + ex[1] + ex[2] + ex[3]

    r_tt = lax.broadcasted_iota(I32, (tm, tm), 0)
    c_tt = lax.broadcasted_iota(I32, (tm, tm), 1)
    before = jnp.where(r_tt > c_tt, 1.0, 0.0).astype(BF16)
    prefix = jnp.dot(before, onehot.astype(BF16), preferred_element_type=F32)
    for j in range(TOP_K):
        tope_ref[:, j:j + 1] = idxs[j]
        gate_ref[:, j:j + 1] = ex[j] / den
        lrank_ref[:, j:j + 1] = jnp.sum(jnp.where(lane == idxs[j], prefix, 0.0), axis=-1,
                                        keepdims=True).astype(I32)
    cnt_ref[0] = jnp.sum(onehot, axis=0, keepdims=True).astype(I32)


def _merge(xp, xs, oap, oas, obp, obs, p3, wa, wb, wo, g1, b1, wr0, wr1, br):
    tm = TOKEN_TILE
    assert xp.shape[0] % tm == 0 and xs.shape[0] % tm == 0
    ntp = xp.shape[0] // tm
    nt = ntp + xs.shape[0] // tm
    t = nt * tm

    def prow(width):
        return pl.BlockSpec((tm, width), lambda i: (jnp.minimum(i, ntp - 1), 0))

    def srow(width):
        return pl.BlockSpec((tm, width), lambda i: (jnp.maximum(i - ntp, 0), 0))

    def full(a):
        return pl.BlockSpec(a.shape, lambda i: (0,) * a.ndim)

    def rowblk(width, idx=0):
        return pl.BlockSpec((tm, width), lambda i: (i, idx))

    return pl.pallas_call(
        functools.partial(_merge_kernel, prompt_tiles=ntp),
        grid=(nt,),
        in_specs=[prow(D_MODEL), srow(D_MODEL), prow(HG_WIDTH), srow(HG_WIDTH),
                  prow(ATT_KV_WIDTH), srow(ATT_KV_WIDTH), rowblk(D_MODEL, 0),
                  rowblk(D_MODEL, 1), full(wa), full(wb), full(wo), full(g1), full(b1),
                  full(wr0), full(wr1), full(br)],
        out_specs=[rowblk(D_MODEL), rowblk(D_MODEL // 2), rowblk(TOP_K), rowblk(TOP_K), rowblk(TOP_K),
                   pl.BlockSpec((1, 1, N_EXPERTS), lambda i: (i, 0, 0))],
        out_shape=[jax.ShapeDtypeStruct((t, D_MODEL), F32),
                   jax.ShapeDtypeStruct((t, D_MODEL // 2), jnp.uint32),
                   jax.ShapeDtypeStruct((t, TOP_K), I32),
                   jax.ShapeDtypeStruct((t, TOP_K), F32),
                   jax.ShapeDtypeStruct((t, TOP_K), I32),
                   jax.ShapeDtypeStruct((nt, 1, N_EXPERTS), I32)],
        compiler_params=_cparams(("parallel",)),
        name="merge_ln1_router",
    )(xp, xs, oap, oas, obp, obs, p3, p3, wa, wb, wo, g1, b1, wr0, wr1, br)


def _pack_bf16_pairs(x):
    w = x.shape[1] // 2
    lo = lax.bitcast_convert_type(x[:, :w].astype(BF16).astype(F32), jnp.uint32)
    hi = lax.bitcast_convert_type(x[:, w:].astype(BF16).astype(F32), jnp.uint32)
    return (hi & jnp.uint32(0xFFFF0000)) | (lo >> 16)


def _unpack_pairs_f32(u):
    lo = lax.bitcast_convert_type(u << 16, F32)
    hi = lax.bitcast_convert_type(u & jnp.uint32(0xFFFF0000), F32)
    return lo, hi


def _unpack_bf16_pairs(u):
    lo, hi = _unpack_pairs_f32(u)
    return lo.astype(BF16), hi.astype(BF16)


def _gather_kernel(tok_ref, h_ref, o_ref):
    m = o_ref.shape[0]

    def body(q, carry):
        for u in range(SUBLANES):
            r = q * SUBLANES + u
            o_ref[pl.ds(r, 1), :] = h_ref[pl.ds(tok_ref[0, 0, r], 1), :]
        return carry

    lax.fori_loop(0, m // SUBLANES, body, 0)


def _moe_gather(hp, tok_pad):
    t, w = hp.shape
    p = tok_pad.shape[0]
    m = MOE_BLOCK
    return pl.pallas_call(
        _gather_kernel,
        grid=(p // m,),
        in_specs=[pl.BlockSpec((1, 1, m), lambda i: (i, 0, 0), memory_space=pltpu.SMEM),
                  pl.BlockSpec((t, w), lambda i: (0, 0), pipeline_mode=pl.Buffered(1))],
        out_specs=pl.BlockSpec((m, w), lambda i: (i, 0)),
        out_shape=jax.ShapeDtypeStruct((p, w), hp.dtype),
        compiler_params=_cparams(("arbitrary",)),
        name="moe_gather",
    )(tok_pad.reshape(p // m, 1, m), hp)


def _moe_up_kernel(blk0_ref, nblk_ref, w0_ref, w1_ref, w2_ref, w3_ref, b_ref, xs_ref, act_ref,
                   wbf_scr, x_scr, o_scr, sem_in, sem_out):
    e = pl.program_id(0)
    f = pl.program_id(1)
    n = nblk_ref[e]
    b0 = blk0_ref[e]
    m = x_scr.shape[1]
    w_refs = (w0_ref, w1_ref, w2_ref, w3_ref)
    kq, ft = w0_ref.shape[1:]
    half = ft // 2

    def rows(b):
        return pl.ds(pl.multiple_of((b0 + b) * m, m), m)

    def x_slot(b):
        return jnp.where(b < X_RESIDENT, b, X_RESIDENT + (b & 1))

    def x_needed(b):
        return jnp.logical_or(f == 0, b >= X_RESIDENT)

    def x_copy(b):
        s = x_slot(b)
        return pltpu.make_async_copy(xs_ref.at[rows(b)], x_scr.at[s], sem_in.at[s])

    def o_copy(b, s):
        return pltpu.make_async_copy(o_scr.at[s], act_ref.at[f, rows(b)], sem_out.at[s])

    @pl.when(n > 0)
    def _():
        @pl.when(x_needed(0))
        def _():
            x_copy(0).start(priority=1)

        for q, w_ref in enumerate(w_refs):
            wbf_scr[q * kq:(q + 1) * kq, :] = w_ref[0].astype(BF16)
        even = (lax.broadcasted_iota(I32, (m, LANES), 1) & 1) == 0

        def body(b, carry):
            s = b & 1
            @pl.when(x_needed(b))
            def _():
                x_copy(b).wait()

            @pl.when(jnp.logical_and(b + 1 < n, x_needed(b + 1)))
            def _():
                x_copy(b + 1).start(priority=1)

            @pl.when(b >= 2)
            def _():
                o_copy(b - 2, s).wait()

            x_lo, x_hi = _unpack_bf16_pairs(x_scr[x_slot(b)])
            kh = x_lo.shape[1]
            for g in range(ft // (2 * LANES)):
                cols = slice(2 * g * LANES, (2 * g + 2) * LANES)
                h = (jnp.dot(x_lo, wbf_scr[:kh, cols], preferred_element_type=F32)
                     + jnp.dot(x_hi, wbf_scr[kh:, cols], preferred_element_type=F32)
                     + b_ref[0, :, cols])
                ha, hb = h[:, :LANES], h[:, LANES:]
                glu = jnp.minimum(jnp.where(even, ha, pltpu.roll(hb, 1, 1)), SWIGLU_LIMIT)
                lin = jnp.where(even, pltpu.roll(ha, LANES - 1, 1), hb)
                lin = jnp.clip(lin, -SWIGLU_LIMIT, SWIGLU_LIMIT) + 1.0
                act = glu * _sigmoid(SWIGLU_ALPHA * glu) * lin
                o_scr[s, :, g * LANES:(g + 1) * LANES] = act.astype(o_scr.dtype)
            o_copy(b, s).start(priority=1)
            return carry

        lax.fori_loop(0, n, body, 0)

        @pl.when(n >= 2)
        def _():
            o_copy(n - 2, n & 1).wait()
        o_copy(n - 1, (n - 1) & 1).wait()

    @pl.when(e == pl.num_programs(0) - 1)
    def _():
        o_scr[0] = jnp.zeros(o_scr.shape[1:], o_scr.dtype)

        nz = act_ref.shape[1] // m - b0
        lax.fori_loop(n, nz, lambda b, c: (o_copy(b, 0).start(), c)[1], 0)
        lax.fori_loop(n, nz, lambda b, c: (o_copy(b, 0).wait(), c)[1], 0)


def _moe_up(xs, w_gate_up, b_gate_up, blk0, nblk):
    p, w = xs.shape
    d = 2 * w
    m = MOE_BLOCK
    ft = MOE_FT
    ne = w_gate_up.shape[0]
    nf = w_gate_up.shape[2] // ft

    def wslice(q, e, f, b0, nb):
        return (e, q, f)

    grid_spec = pltpu.PrefetchScalarGridSpec(
        num_scalar_prefetch=2,
        grid=(ne, nf),
        in_specs=[pl.BlockSpec((1, d // W_STREAMS, ft), functools.partial(wslice, q))
                  for q in range(W_STREAMS)] + [
                  pl.BlockSpec((1, 1, ft), lambda e, f, b0, nb: (e, 0, f)),
                  pl.BlockSpec(memory_space=pl.ANY)],
        out_specs=pl.BlockSpec(memory_space=pl.ANY),
        scratch_shapes=[pltpu.VMEM((d, ft), BF16),
                        pltpu.VMEM((X_RESIDENT + 2, m, w), xs.dtype),
                        pltpu.VMEM((2, m, ft // 2), BF16),
                        pltpu.SemaphoreType.DMA((X_RESIDENT + 2,)),
                        pltpu.SemaphoreType.DMA((2,))],
    )
    return pl.pallas_call(
        _moe_up_kernel,
        grid_spec=grid_spec,
        out_shape=jax.ShapeDtypeStruct((nf, p, ft // 2), BF16),
        compiler_params=_cparams(("arbitrary", "arbitrary")),
        name="moe_up",
    )(blk0, nblk, *([w_gate_up] * W_STREAMS), b_gate_up, xs)


def _hidden_perm():
    r = lax.broadcasted_iota(I32, (LANES, LANES), 0)
    c = lax.broadcasted_iota(I32, (LANES, LANES), 1)
    src = (r >> 1) + (r & 1) * (LANES // 2)
    return jnp.where(c == src, 1.0, 0.0).astype(BF16)


def _moe_down_kernel(blk0_ref, nblk_ref, nreal_ref, wd0_ref, wd1_ref, wd2_ref, wd3_ref, bd_ref,
                     act_ref, dest_ref, yt_ref, wbf_scr, a_scr, d_scr, y_scr, sem_a, sem_d, sem_y):
    e = pl.program_id(0)
    n = nblk_ref[e]
    b0 = blk0_ref[e]
    nf, m, fh = a_scr.shape[1:]

    def a_copy(b, s):
        rows = pl.ds(pl.multiple_of((b0 + b) * m, m), m)
        return pltpu.make_async_copy(act_ref.at[:, rows], a_scr.at[s], sem_a.at[s])

    def d_copy(b, s):
        return pltpu.make_async_copy(dest_ref.at[b0 + b], d_scr.at[s], sem_d.at[s])

    def row_copy(s, r, d):
        return pltpu.make_async_copy(y_scr.at[s, pl.ds(r, 1)], yt_ref.at[pl.ds(d, 1)], sem_y.at[s])

    def drain(s, count):
        @pl.when(count == m)
        def _():
            pltpu.make_async_copy(y_scr.at[s], yt_ref.at[pl.ds(0, m)], sem_y.at[s]).wait()

        @pl.when(count != m)
        def _():
            def wbody(_, carry):
                row_copy(s, 0, 0).wait()
                return carry
            lax.fori_loop(0, count, wbody, 0)

    @pl.when(n > 0)
    def _():
        a_copy(0, 0).start(priority=1)
        d_copy(0, 0).start(priority=1)

        perm = _hidden_perm()
        for q, wd_ref in enumerate((wd0_ref, wd1_ref, wd2_ref, wd3_ref)):
            kq = wd_ref.shape[1]
            for g in range(kq // LANES):
                rows = slice(g * LANES, (g + 1) * LANES)
                wbf_scr[q * kq + g * LANES:q * kq + (g + 1) * LANES, :] = jnp.dot(
                    perm, wd_ref[0, rows, :].astype(BF16), preferred_element_type=F32).astype(BF16)

        def body(b, carry):
            s = b & 1
            a_copy(b, s).wait()
            d_copy(b, s).wait()

            @pl.when(b + 1 < n)
            def _():
                a_copy(b + 1, 1 - s).start(priority=1)
                d_copy(b + 1, 1 - s).start(priority=1)

            @pl.when(b >= 2)
            def _():
                drain(s, nreal_ref[b0 + b - 2])

            y = bd_ref[0]
            for j in range(nf):
                y = y + jnp.dot(a_scr[s, j], wbf_scr[j * fh:(j + 1) * fh, :],
                                preferred_element_type=F32)
            y_scr[s] = _pack_bf16_pairs(y)

            cnt = nreal_ref[b0 + b]
            unroll = 8

            def sbody8(q, c):
                for u in range(unroll):
                    r = q * unroll + u
                    row_copy(s, r, d_scr[s, r]).start()
                return c

            def sbody1(r, c):
                row_copy(s, r, d_scr[s, r]).start()
                return c

            lax.fori_loop(0, cnt // unroll, sbody8, 0)
            lax.fori_loop((cnt // unroll) * unroll, cnt, sbody1, 0)
            return carry

        lax.fori_loop(0, n, body, 0)

        @pl.when(n >= 2)
        def _():
            drain(n & 1, nreal_ref[b0 + n - 2])
        drain((n - 1) & 1, nreal_ref[b0 + n - 1])


def _moe_down(act, w_down, b_down, blk0, nblk, nreal, dest, n_rows):
    nf, p, fh = act.shape
    f = nf * fh
    m = MOE_BLOCK
    ne = w_down.shape[0]
    d = w_down.shape[2]

    def wslice(q, e, b0, nb, nr):
        return (e, q, 0)

    grid_spec = pltpu.PrefetchScalarGridSpec(
        num_scalar_prefetch=3,
        grid=(ne,),
        in_specs=[*[pl.BlockSpec((1, f // W_STREAMS, d), functools.partial(wslice, q))
                    for q in range(W_STREAMS)],
                  pl.BlockSpec((1, 1, d), lambda e, b0, nb, nr: (e, 0, 0)),
                  pl.BlockSpec(memory_space=pl.ANY),
                  pl.BlockSpec(memory_space=pl.ANY)],
        out_specs=pl.BlockSpec(memory_space=pl.ANY),
        scratch_shapes=[pltpu.VMEM((f, d), BF16),
                        pltpu.VMEM((2, nf, m, fh), BF16),
                        pltpu.SMEM((2, m), I32),
                        pltpu.VMEM((2, m, d // 2), jnp.uint32),
                        pltpu.SemaphoreType.DMA((2,)),
                        pltpu.SemaphoreType.DMA((2,)),
                        pltpu.SemaphoreType.DMA((2,))],
    )
    return pl.pallas_call(
        _moe_down_kernel,
        grid_spec=grid_spec,
        out_shape=jax.ShapeDtypeStruct((n_rows, d // 2), jnp.uint32),
        compiler_params=_cparams(("arbitrary",)),
        name="moe_down",
    )(blk0, nblk, nreal, *([w_down] * W_STREAMS), b_down, act, dest.reshape(p // m, m))


def _combine_kernel(h_ref, y0_ref, y1_ref, y2_ref, y3_ref, gate_ref, g2_ref, b2_ref, o_ref):
    lo, hi = None, None
    for j, y_ref in enumerate((y0_ref, y1_ref, y2_ref, y3_ref)):
        yl, yh = _unpack_pairs_f32(y_ref[...])
        g = gate_ref[:, j:j + 1]
        lo = g * yl if lo is None else lo + g * yl
        hi = g * yh if hi is None else hi + g * yh
    ffn = jnp.concatenate([lo, hi], axis=1)
    o_ref[...] = _layer_norm(DEEPNORM_ALPHA * h_ref[...] + ffn, g2_ref[...], b2_ref[...])


def _combine(h, yt, gate, g2, b2):
    t, d = h.shape
    tm = TOKEN_TILE
    nt = t // tm

    def choice(j):
        return pl.BlockSpec((tm, d // 2), lambda i: (j * nt + i, 0))

    return pl.pallas_call(
        _combine_kernel,
        grid=(nt,),
        in_specs=[pl.BlockSpec((tm, d), lambda i: (i, 0)),
                  choice(0), choice(1), choice(2), choice(3),
                  pl.BlockSpec((tm, TOP_K), lambda i: (i, 0)),
                  pl.BlockSpec((1, d), lambda i: (0, 0)),
                  pl.BlockSpec((1, d), lambda i: (0, 0))],
        out_specs=pl.BlockSpec((tm, d), lambda i: (i, 0)),
        out_shape=jax.ShapeDtypeStruct((t, d), F32),
        compiler_params=_cparams(("parallel",)),
        name="combine_ln2",
    )(h, yt, yt, yt, yt, gate, g2, b2)


def _route(top_e, lrank, tilecnt):
    t = top_e.shape[0]
    m = MOE_BLOCK
    n_assign = t * TOP_K
    n_blocks = n_assign // m + N_EXPERTS
    p = n_blocks * m
    tc = tilecnt.reshape(-1, N_EXPERTS)
    base = jnp.cumsum(tc, axis=0) - tc
    counts = jnp.sum(tc, axis=0)
    padded = (counts + m - 1) // m * m
    pend = jnp.cumsum(padded)
    pstart = pend - padded
    base_tok = jnp.repeat(base, TOKEN_TILE, axis=0)
    rank = lrank + jnp.take_along_axis(base_tok, top_e, axis=1)
    slot = (pstart[top_e] + rank).reshape(-1)
    a_idx = jnp.arange(n_assign, dtype=I32)
    dest = jnp.full((p,), -1, I32).at[slot].set((a_idx % TOP_K) * t + a_idx // TOP_K,
                                               unique_indices=True)
    tok_pad = jnp.where(dest >= 0, dest % t, 0)
    blk_start = jnp.arange(n_blocks, dtype=I32) * m
    blk_expert = jnp.minimum(jnp.sum((pend[None, :] <= blk_start[:, None]).astype(I32), axis=1),
                             N_EXPERTS - 1)
    real_end = (pstart + counts)[blk_expert]
    nreal = jnp.where(blk_start < pend[-1], jnp.clip(real_end - blk_start, 0, m), 0).astype(I32)
    return tok_pad, dest, nreal, (pstart // m).astype(I32), (padded // m).astype(I32)


def kernel(x_prompt, x_sample, cache_attn_k, cache_attn_v, state_hgrn, w_in, hgrn_lb_logits,
           hgrn_norm_w, w_branch_a, w_branch_b, w_out, ln1_g, ln1_b, w_router, b_router,
           w_gate_up, b_gate_up, w_down, b_down, ln2_g, ln2_b):
    assert w_in.shape[0] == DEPTH == 1
    bp, lp, d = x_prompt.shape
    bs, ls, _ = x_sample.shape
    tp, ts = bp * lp, bs * ls
    t = tp + ts
    keep = min(max(w for w, _ in DILATION_PAIRS), lp)

    xp2, xs2 = x_prompt.reshape(tp, d), x_sample.reshape(ts, d)
    xb = jnp.concatenate([xp2.astype(BF16), xs2.astype(BF16)], axis=0)
    wi = w_in[0]
    c1, c2 = P1_WIDTH, P1_WIDTH + P2_WIDTH
    p1 = _matmul(xb, wi[:, :c1].astype(BF16), 768, 1024, F32, "inproj_hgrn")
    p2 = _matmul(xb, wi[:, c1:c2].astype(BF16), 768, 1280, F32, "inproj_attn")
    p3 = _matmul(xb, wi[:, c2:].astype(BF16), 768, 1024, F32, "inproj_gate")

    oa_p, st_p = _hgrn(p1, hgrn_lb_logits, hgrn_norm_w[0], None, bp, lp, 0)
    oa_s, st_s = _hgrn(p1, hgrn_lb_logits, hgrn_norm_w[0],
                       state_hgrn.reshape(bs, HG_HEADS, HG_DIM, HG_DIM), bs, ls, tp)

    n = jnp.arange(1, ATT_Q_HEADS + 1, dtype=F32)
    slopes = jnp.power(2.0, -8.0 * n / ATT_Q_HEADS).reshape(N_GROUPS, ATT_KV_HEADS)
    ctx = cache_attn_k.shape[2]
    ob_p = _attn_prompt(p2, slopes, bp, lp)
    ob_s = _attn_sample(p2, cache_attn_k.reshape(bs, ctx * ATT_KV_HEADS, ATT_HEAD_DIM),
                        cache_attn_v.reshape(bs, ctx * ATT_KV_HEADS, ATT_HEAD_DIM), slopes, bs, ls, tp)

    wr = w_router[0]
    wr0 = wr.astype(BF16)
    wr1 = (wr - wr0.astype(F32)).astype(BF16)
    h, hp, top_e, gate, lrank, tilecnt = _merge(
        xp2, xs2, oa_p.reshape(tp, HG_WIDTH), oa_s.reshape(ts, HG_WIDTH), ob_p,
        ob_s.reshape(ts, ATT_KV_WIDTH), p3, w_branch_a[0].astype(BF16), w_branch_b[0].astype(BF16),
        w_out[0].astype(BF16), ln1_g, ln1_b, wr0, wr1, b_router)

    tok_pad, dest, nreal, blk0, nblk = _route(top_e, lrank, tilecnt)
    xs = _moe_gather(hp, tok_pad)
    act = _moe_up(xs, w_gate_up.reshape(N_EXPERTS, d, 2 * D_FF),
                  b_gate_up.reshape(N_EXPERTS, 1, 2 * D_FF), blk0, nblk)
    yt = _moe_down(act, w_down.reshape(N_EXPERTS, D_FF, d), b_down.reshape(N_EXPERTS, 1, d),
                   blk0, nblk, nreal, dest, t * TOP_K)
    y = _combine(h, yt, gate, ln2_g, ln2_b)

    k_new = p2[:, ATT_Q_HEADS * ATT_HEAD_DIM:(ATT_Q_HEADS + ATT_KV_HEADS) * ATT_HEAD_DIM]
    v_new = p2[:, (ATT_Q_HEADS + ATT_KV_HEADS) * ATT_HEAD_DIM:]

    def kv_prompt(a):
        return a[:tp].reshape(bp, lp, ATT_KV_HEADS, ATT_HEAD_DIM)[:, lp - keep:][None]

    def kv_sample(a):
        return a[tp:].reshape(bs, ls, ATT_KV_HEADS, ATT_HEAD_DIM)[None]

    return (y[:tp].reshape(bp, lp, d), y[tp:].reshape(bs, ls, d),
            kv_prompt(k_new), kv_prompt(v_new), st_p[None],
            kv_sample(k_new), kv_sample(v_new), st_s[None])
```

```python
import functools
import math

import jax
import jax.numpy as jnp
from jax import lax
from jax.experimental import pallas as pl
from jax.experimental.pallas import tpu as pltpu

F32 = jnp.float32
BF16 = jnp.bfloat16
I32 = jnp.int32

D_MODEL = 2048
DEPTH = 1
HG_HEADS = 8
HG_DIM = 128
HG_WIDTH = HG_HEADS * HG_DIM
HG_CHUNK = 128
ATT_HEAD_DIM = 128
DILATION_PAIRS = ((128, 1), (512, 4), (2048, 16))
N_GROUPS = len(DILATION_PAIRS)
ATT_KV_HEADS = 4
ATT_Q_HEADS = N_GROUPS * ATT_KV_HEADS
ATT_TAPS = 128
ATT_KV_WIDTH = ATT_KV_HEADS * ATT_HEAD_DIM
ATT_SCALE = ATT_HEAD_DIM ** -0.5
N_EXPERTS = 32
TOP_K = 4
D_FF = D_MODEL
SWIGLU_ALPHA = 1.702
SWIGLU_LIMIT = 7.0
LN_EPS = 1e-5
RMS_EPS = 1e-5
DEEPNORM_ALPHA = (2 * DEPTH) ** 0.25

P1_WIDTH = 4 * HG_WIDTH
P2_WIDTH = (ATT_Q_HEADS + 2 * ATT_KV_HEADS) * ATT_HEAD_DIM
P3_WIDTH = 2 * D_MODEL

LANES = 128
SUBLANES = 8
VMEM_LIMIT = 60 * 1024 * 1024

TOKEN_TILE = 256
MOE_BLOCK = 256
MOE_FT = 2048
W_STREAMS = 4
X_RESIDENT = 8
NEG = -1e30

_NT = (((1,), (1,)), ((), ()))
_TN = (((0,), (0,)), ((), ()))


def _cparams(sem):
    return pltpu.CompilerParams(dimension_semantics=sem, vmem_limit_bytes=VMEM_LIMIT)


def _sigmoid(x):
    return 1.0 / (1.0 + jnp.exp(-x))


def _mm_kernel(x_ref, w_ref, o_ref):
    o_ref[...] = jnp.dot(x_ref[...], w_ref[...], preferred_element_type=F32).astype(o_ref.dtype)


def _matmul(x, w, tm, tn, out_dtype, name):
    m, k = x.shape
    n = w.shape[1]
    assert m % tm == 0 and n % tn == 0
    return pl.pallas_call(
        _mm_kernel,
        grid=(n // tn, m // tm),
        in_specs=[pl.BlockSpec((tm, k), lambda j, i: (i, 0)),
                  pl.BlockSpec((k, tn), lambda j, i: (0, j))],
        out_specs=pl.BlockSpec((tm, tn), lambda j, i: (i, j)),
        out_shape=jax.ShapeDtypeStruct((m, n), out_dtype),
        compiler_params=_cparams(("parallel", "parallel")),
        name=name,
    )(x, w)


def _level_ref(g_scr, m, sl, chunk, sub):
    pieces = []
    for j in range(chunk // SUBLANES):
        if 2 * m >= SUBLANES:
            row = ((SUBLANES * j) // (2 * m)) * (2 * m) + m - 1
            pieces.append(jnp.broadcast_to(g_scr[row:row + 1, sl], (SUBLANES, LANES)))
        else:
            acc = None
            for u in range(SUBLANES // (2 * m)):
                row = SUBLANES * j + 2 * m * u + m - 1
                b = jnp.broadcast_to(g_scr[row:row + 1, sl], (SUBLANES, LANES))
                acc = b if acc is None else jnp.where(sub // (2 * m) == u, b, acc)
            pieces.append(acc)
    return pieces[0] if len(pieces) == 1 else jnp.concatenate(pieces, axis=0)


def _hgrn_kernel(lbl_ref, nw_ref, hq_ref, hf_ref, hi_ref, hg_ref, *rest, chunk, has_s0):
    if has_s0:
        s0_ref, o_ref, sout_ref, st_scr, g_scr = rest
    else:
        o_ref, sout_ref, st_scr, g_scr = rest
    c = pl.program_id(1)
    nc = pl.num_programs(1)

    @pl.when(c == 0)
    def _init():
        for h in range(HG_HEADS):
            if has_s0:
                st_scr[h] = s0_ref[0, h].T
            else:
                st_scr[h] = jnp.zeros((HG_DIM, HG_DIM), F32)

    lbl = lbl_ref[...]
    ex = jnp.exp(lbl - jnp.max(lbl, axis=0, keepdims=True))
    lb = ex[0:1] / jnp.sum(ex, axis=0, keepdims=True)

    f = lb + (1.0 - lb) * _sigmoid(hf_ref[...])
    logf = jnp.log(f)
    kall = 1.0 - f

    r_cc = lax.broadcasted_iota(I32, (chunk, chunk), 0)
    c_cc = lax.broadcasted_iota(I32, (chunk, chunk), 1)
    tri = jnp.where(r_cc >= c_cc, 1.0, 0.0).astype(BF16)
    p0 = logf.astype(BF16)
    r1 = logf - p0.astype(F32)
    p1 = r1.astype(BF16)
    p2 = (r1 - p1.astype(F32)).astype(BF16)
    g_all = (jnp.dot(tri, p0, preferred_element_type=F32)
             + jnp.dot(tri, p1, preferred_element_type=F32)
             + jnp.dot(tri, p2, preferred_element_type=F32))
    g_scr[...] = g_all

    sub = lax.broadcasted_iota(I32, (SUBLANES, LANES), 0)
    row_c = lax.broadcasted_iota(I32, (chunk, LANES), 0)
    nw = nw_ref[...]

    for h in range(HG_HEADS):
        sl = slice(h * HG_DIM, (h + 1) * HG_DIM)
        hq = hq_ref[:, sl]
        q = hq * _sigmoid(hq)
        k = kall[:, sl]
        g = g_all[:, sl]
        v = hi_ref[:, sl]
        vb = v.astype(BF16)

        p = lax.dot_general(q.astype(BF16), k.astype(BF16), _NT, preferred_element_type=F32)
        a = jnp.where(r_cc == c_cc, p, 0.0)
        m = 1
        while m < chunk:
            ref = _level_ref(g_scr, m, sl, chunk, sub)
            e = jnp.exp(-jnp.abs(g - ref))
            odd = ((row_c // m) & 1) == 1
            x = (jnp.where(odd, q, k) * e).astype(BF16)
            p = lax.dot_general(x, x, _NT, preferred_element_type=F32)
            pair = (r_cc // (2 * m)) == (c_cc // (2 * m))
            sel = jnp.where(pair, ((r_cc // m) & 1) - ((c_cc // m) & 1), 0) == 1
            a = jnp.where(sel, p, a)
            m *= 2

        st = st_scr[h]
        o = jnp.dot(a.astype(BF16), vb, preferred_element_type=F32)
        qe = (q * jnp.exp(g)).astype(BF16)
        o = o + lax.dot_general(qe, st.astype(BF16), _NT, preferred_element_type=F32)

        ms = jnp.mean(o * o, axis=-1, keepdims=True)
        hg = hg_ref[:, sl]
        out = o * lax.rsqrt(ms + RMS_EPS) * nw * (hg * _sigmoid(hg))
        o_ref[0, :, sl] = out.astype(o_ref.dtype)

        g_last = g[chunk - 1:chunk, :]
        kd = (k * jnp.exp(g_last - g)).astype(BF16)
        st_scr[h] = st * jnp.exp(g_last) + lax.dot_general(vb, kd, _TN, preferred_element_type=F32)

    @pl.when(c == nc - 1)
    def _fin():
        for h in range(HG_HEADS):
            sout_ref[0, h] = st_scr[h].T


def _hgrn(p1, lb_logits, norm_w, s0, batch, seq, row0):
    chunk = math.gcd(seq, HG_CHUNK)
    nchunk = seq // chunk
    assert row0 % chunk == 0
    rb0 = row0 // chunk

    def col(idx):
        return pl.BlockSpec((chunk, HG_WIDTH), lambda b, c: (rb0 + b * nchunk + c, idx))

    in_specs = [pl.BlockSpec(lb_logits.shape, lambda b, c: (0, 0)),
                pl.BlockSpec((1, HG_DIM), lambda b, c: (0, 0)),
                col(0), col(1), col(2), col(3)]
    args = [lb_logits, norm_w.reshape(1, HG_DIM), p1, p1, p1, p1]
    if s0 is not None:
        in_specs.append(pl.BlockSpec((1, HG_HEADS, HG_DIM, HG_DIM), lambda b, c: (b, 0, 0, 0)))
        args.append(s0)
    return pl.pallas_call(
        functools.partial(_hgrn_kernel, chunk=chunk, has_s0=s0 is not None),
        grid=(batch, nchunk),
        in_specs=in_specs,
        out_specs=[pl.BlockSpec((1, chunk, HG_WIDTH), lambda b, c: (b, c, 0)),
                   pl.BlockSpec((1, HG_HEADS, HG_DIM, HG_DIM), lambda b, c: (b, 0, 0, 0))],
        out_shape=[jax.ShapeDtypeStruct((batch, seq, HG_WIDTH), BF16),
                   jax.ShapeDtypeStruct((batch, HG_HEADS, HG_DIM, HG_DIM), F32)],
        scratch_shapes=[pltpu.VMEM((HG_HEADS, HG_DIM, HG_DIM), F32),
                        pltpu.VMEM((chunk, HG_WIDTH), F32)],
        compiler_params=_cparams(("parallel", "arbitrary")),
        name="hgrn_seq%d" % seq,
    )(*args)


QB = 128
ATTN_UNROLL = 8


def _attn_prompt_kernel(slopes_ref, q0_ref, q1_ref, q2_ref, k_ref, v_ref, o_ref,
                        og_scr, lse_scr, bias_scr, *, seq):
    h = pl.program_id(1)
    il = lax.broadcasted_iota(I32, (QB, 2 * QB), 0)
    jl = lax.broadcasted_iota(I32, (QB, 2 * QB), 1)
    delta = il + QB - jl
    band = jnp.where(delta >= 0, delta, ATT_TAPS + 1) <= ATT_TAPS
    prev_half = jl < QB
    q_refs = (q0_ref, q1_ref, q2_ref)

    for g, (window, dil) in enumerate(DILATION_PAIRS):
        assert window // dil == ATT_TAPS
        slope = slopes_ref[g, h]
        bias_scr[g] = jnp.where(band, (-slope * dil) * delta.astype(F32), NEG)
        nbr = (seq // dil) // QB
        q_ref = q_refs[g]

        def rows(start, dil=dil):
            return pl.ds(start, QB) if dil == 1 else pl.ds(start, QB, stride=dil)

        def body(ib, carry, g=g, dil=dil, nbr=nbr, q_ref=q_ref, rows=rows):
            res = ib // nbr
            jb = ib - res * nbr
            qs = res + dil * (jb * QB)
            ps = res + dil * jnp.maximum(jb * QB - QB, 0)
            qv = q_ref[rows(qs), :].astype(BF16)
            kb = jnp.concatenate([k_ref[rows(ps), :], k_ref[rows(qs), :]], axis=0).astype(BF16)
            vb = jnp.concatenate([v_ref[rows(ps), :], v_ref[rows(qs), :]], axis=0).astype(BF16)
            s = lax.dot_general(qv, kb, _NT, preferred_element_type=F32) * ATT_SCALE + bias_scr[g]
            s = jnp.where(jnp.where(prev_half, jb, 1) == 0, NEG, s)
            mx = jnp.max(s, axis=-1, keepdims=True)
            p = jnp.exp(s - mx)
            l = jnp.sum(p, axis=-1, keepdims=True)
            o = jnp.dot(p.astype(BF16), vb, preferred_element_type=F32) / l
            og_scr[g, rows(qs), :] = o
            lse_scr[g, rows(qs), :] = jnp.broadcast_to(mx + jnp.log(l), (QB, LANES))
            return carry

        lax.fori_loop(0, seq // QB, body, 0, unroll=ATTN_UNROLL)

    def merge(tb, carry):
        r = pl.ds(pl.multiple_of(tb * QB, QB), QB)
        l0, l1, l2 = lse_scr[0, r, :], lse_scr[1, r, :], lse_scr[2, r, :]
        mx = jnp.maximum(jnp.maximum(l0, l1), l2)
        w0, w1, w2 = jnp.exp(l0 - mx), jnp.exp(l1 - mx), jnp.exp(l2 - mx)
        o = (w0 * og_scr[0, r, :] + w1 * og_scr[1, r, :] + w2 * og_scr[2, r, :]) / (w0 + w1 + w2)
        o_ref[r, :] = o.astype(o_ref.dtype)
        return carry

    lax.fori_loop(0, seq // QB, merge, 0, unroll=ATTN_UNROLL)


def _attn_prompt(p2, slopes, batch, seq):
    assert seq % (QB * max(d for _, d in DILATION_PAIRS)) == 0

    def col(fn):
        return pl.BlockSpec((seq, ATT_HEAD_DIM), lambda b, h: (b, fn(h)))

    return pl.pallas_call(
        functools.partial(_attn_prompt_kernel, seq=seq),
        grid=(batch, ATT_KV_HEADS),
        in_specs=[pl.BlockSpec(memory_space=pltpu.SMEM),
                  col(lambda h: h), col(lambda h: ATT_KV_HEADS + h), col(lambda h: 2 * ATT_KV_HEADS + h),
                  col(lambda h: ATT_Q_HEADS + h), col(lambda h: ATT_Q_HEADS + ATT_KV_HEADS + h)],
        out_specs=pl.BlockSpec((seq, ATT_HEAD_DIM), lambda b, h: (b, h)),
        out_shape=jax.ShapeDtypeStruct((batch * seq, ATT_KV_WIDTH), BF16),
        scratch_shapes=[pltpu.VMEM((N_GROUPS, seq, ATT_HEAD_DIM), F32),
                        pltpu.VMEM((N_GROUPS, seq, LANES), F32),
                        pltpu.VMEM((N_GROUPS, QB, 2 * QB), F32)],
        compiler_params=_cparams(("parallel", "parallel")),
        name="attn_prompt",
    )(slopes, p2, p2, p2, p2, p2)


def _attn_sample_kernel(slopes_ref, qa_ref, qb_ref, qc_ref, kn_ref, vn_ref, kc_ref, vc_ref, o_ref,
                        *, ctx, nq):
    rows = N_GROUPS * nq
    gi = lax.broadcasted_iota(I32, (rows, 1), 0) // nq
    dil = jnp.where(gi == 0, DILATION_PAIRS[0][1], jnp.where(gi == 1, DILATION_PAIRS[1][1], DILATION_PAIRS[2][1]))
    win = jnp.where(gi == 0, DILATION_PAIRS[0][0], jnp.where(gi == 1, DILATION_PAIRS[1][0], DILATION_PAIRS[2][0]))

    def dist_valid(shape, key0, nkeys):
        r = lax.broadcasted_iota(I32, shape, 0)
        t = r - (r // nq) * nq
        col = lax.broadcasted_iota(I32, shape, 1)
        dist = ctx + t - (col + key0)
        bad = jnp.where(dist >= 0, dist & (dil - 1), 1)
        bad = jnp.where(dist <= win, bad, 1)
        bad = jnp.where(col < nkeys, bad, 1)
        return dist.astype(F32), bad == 0

    dist_c, ok_c = dist_valid((rows, ctx), 0, ctx)
    dist_n, ok_n = dist_valid((rows, LANES), ctx, nq)
    zpad = jnp.zeros((LANES - nq, ATT_HEAD_DIM), F32)

    for h in range(ATT_KV_HEADS):
        hs = slice(h * ATT_HEAD_DIM, (h + 1) * ATT_HEAD_DIM)
        slope = jnp.where(gi == 0, slopes_ref[0, h], jnp.where(gi == 1, slopes_ref[1, h], slopes_ref[2, h]))
        qh = jnp.concatenate([qa_ref[:, hs], qb_ref[:, hs], qc_ref[:, hs]], axis=0).astype(BF16)
        kc = kc_ref[0, pl.ds(h, ctx, stride=ATT_KV_HEADS), :].astype(BF16)
        vc = vc_ref[0, pl.ds(h, ctx, stride=ATT_KV_HEADS), :].astype(BF16)
        kn = jnp.concatenate([kn_ref[:, hs], zpad], axis=0).astype(BF16)
        vn = jnp.concatenate([vn_ref[:, hs], zpad], axis=0).astype(BF16)
        sc = lax.dot_general(qh, kc, _NT, preferred_element_type=F32) * ATT_SCALE - slope * dist_c
        sn = lax.dot_general(qh, kn, _NT, preferred_element_type=F32) * ATT_SCALE - slope * dist_n
        sc = jnp.where(ok_c, sc, NEG)
        sn = jnp.where(ok_n, sn, NEG)
        mx = jnp.maximum(jnp.max(sc, axis=-1, keepdims=True), jnp.max(sn, axis=-1, keepdims=True))
        pc = jnp.exp(sc - mx)
        pn = jnp.exp(sn - mx)
        l = jnp.sum(pc, axis=-1, keepdims=True) + jnp.sum(pn, axis=-1, keepdims=True)
        o = (jnp.dot(pc.astype(BF16), vc, preferred_element_type=F32)
             + jnp.dot(pn.astype(BF16), vn, preferred_element_type=F32)) / l
        lse = mx + jnp.log(l)
        l0, l1, l2 = lse[0:nq], lse[nq:2 * nq], lse[2 * nq:3 * nq]
        m3 = jnp.maximum(jnp.maximum(l0, l1), l2)
        w0, w1, w2 = jnp.exp(l0 - m3), jnp.exp(l1 - m3), jnp.exp(l2 - m3)
        out = (w0 * o[0:nq] + w1 * o[nq:2 * nq] + w2 * o[2 * nq:3 * nq]) / (w0 + w1 + w2)
        o_ref[0, :, hs] = out.astype(o_ref.dtype)


def _attn_sample(p2, cache_k, cache_v, slopes, batch, nq, row0):
    ctx = cache_k.shape[1] // ATT_KV_HEADS
    assert row0 % nq == 0 and nq == SUBLANES and ctx >= max(w for w, _ in DILATION_PAIRS)
    rb0 = row0 // nq

    def col(idx):
        return pl.BlockSpec((nq, ATT_KV_WIDTH), lambda b: (rb0 + b, idx))

    cache_spec = pl.BlockSpec((1, ctx * ATT_KV_HEADS, ATT_HEAD_DIM), lambda b: (b, 0, 0))
    return pl.pallas_call(
        functools.partial(_attn_sample_kernel, ctx=ctx, nq=nq),
        grid=(batch,),
        in_specs=[pl.BlockSpec(memory_space=pltpu.SMEM),
                  col(0), col(1), col(2), col(3), col(4), cache_spec, cache_spec],
        out_specs=pl.BlockSpec((1, nq, ATT_KV_WIDTH), lambda b: (b, 0, 0)),
        out_shape=jax.ShapeDtypeStruct((batch, nq, ATT_KV_WIDTH), BF16),
        compiler_params=_cparams(("parallel",)),
        name="attn_sample",
    )(slopes, p2, p2, p2, p2, p2, cache_k, cache_v)


def _layer_norm(y, g, b):
    mu = jnp.mean(y, axis=-1, keepdims=True)
    yc = y - mu
    var = jnp.mean(yc * yc, axis=-1, keepdims=True)
    return yc * lax.rsqrt(var + LN_EPS) * g + b


def _merge_kernel(xp_ref, xs_ref, oap_ref, oas_ref, obp_ref, obs_ref, ga_ref, gb_ref, wa_ref, wb_ref,
                  wo_ref, g1_ref, b1_ref, wr0_ref, wr1_ref, br_ref,
                  h_ref, hp_ref, tope_ref, gate_ref, lrank_ref, cnt_ref, *, prompt_tiles):
    tm = xp_ref.shape[0]
    is_prompt = pl.program_id(0) < prompt_tiles
    x = jnp.where(is_prompt, xp_ref[...], xs_ref[...])
    oa = jnp.where(is_prompt, oap_ref[...], oas_ref[...])
    ob = jnp.where(is_prompt, obp_ref[...], obs_ref[...])
    a = jnp.dot(oa, wa_ref[...], preferred_element_type=F32)
    b = jnp.dot(ob, wb_ref[...], preferred_element_type=F32)
    merged = _sigmoid(ga_ref[...]) * a + _sigmoid(gb_ref[...]) * b
    z = jnp.dot(merged.astype(BF16), wo_ref[...], preferred_element_type=F32)
    hh = _layer_norm(DEEPNORM_ALPHA * x + z, g1_ref[...], b1_ref[...])
    h_ref[...] = hh
    hp_ref[...] = _pack_bf16_pairs(hh)

    h0 = hh.astype(BF16)
    h1 = (hh - h0.astype(F32)).astype(BF16)
    logits = (jnp.dot(h0, wr0_ref[...], preferred_element_type=F32)
              + jnp.dot(h1, wr0_ref[...], preferred_element_type=F32)
              + jnp.dot(h0, wr1_ref[...], preferred_element_type=F32)) + br_ref[...]

    lane = lax.broadcasted_iota(I32, (tm, N_EXPERTS), 1)
    work = logits
    vals, idxs = [], []
    onehot = jnp.zeros((tm, N_EXPERTS), F32)
    for _ in range(TOP_K):
        mx = jnp.max(work, axis=-1, keepdims=True)
        idx = jnp.min(jnp.where(work == mx, lane, N_EXPERTS), axis=-1, keepdims=True)
        hit = lane == idx
        vals.append(mx)
        idxs.append(idx)
        onehot = jnp.where(hit, 1.0, onehot)
        work = jnp.where(hit, -jnp.inf, work)
    ex = [jnp.exp(v - vals[0]) for v in vals]
    den = ex[0] + ex[1] + ex[2] + ex[3]

    r_tt = lax.broadcasted_iota(I32, (tm, tm), 0)
    c_tt = lax.broadcasted_iota(I32, (tm, tm), 1)
    before = jnp.where(r_tt > c_tt, 1.0, 0.0).astype(BF16)
    prefix = jnp.dot(before, onehot.astype(BF16), preferred_element_type=F32)
    for j in range(TOP_K):
        tope_ref[:, j:j + 1] = idxs[j]
        gate_ref[:, j:j + 1] = ex[j] / den
        lrank_ref[:, j:j + 1] = jnp.sum(jnp.where(lane == idxs[j], prefix, 0.0), axis=-1,
                                        keepdims=True).astype(I32)
    cnt_ref[0] = jnp.sum(onehot, axis=0, keepdims=True).astype(I32)


def _merge(xp, xs, oap, oas, obp, obs, p3, wa, wb, wo, g1, b1, wr0, wr1, br):
    tm = TOKEN_TILE
    assert xp.shape[0] % tm == 0 and xs.shape[0] % tm == 0
    ntp = xp.shape[0] // tm
    nt = ntp + xs.shape[0] // tm
    t = nt * tm

    def prow(width):
        return pl.BlockSpec((tm, width), lambda i: (jnp.minimum(i, ntp - 1), 0))

    def srow(width):
        return pl.BlockSpec((tm, width), lambda i: (jnp.maximum(i - ntp, 0), 0))

    def full(a):
        return pl.BlockSpec(a.shape, lambda i: (0,) * a.ndim)

    def rowblk(width, idx=0):
        return pl.BlockSpec((tm, width), lambda i: (i, idx))

    return pl.pallas_call(
        functools.partial(_merge_kernel, prompt_tiles=ntp),
        grid=(nt,),
        in_specs=[prow(D_MODEL), srow(D_MODEL), prow(HG_WIDTH), srow(HG_WIDTH),
                  prow(ATT_KV_WIDTH), srow(ATT_KV_WIDTH), rowblk(D_MODEL, 0),
                  rowblk(D_MODEL, 1), full(wa), full(wb), full(wo), full(g1), full(b1),
                  full(wr0), full(wr1), full(br)],
        out_specs=[rowblk(D_MODEL), rowblk(D_MODEL // 2), rowblk(TOP_K), rowblk(TOP_K), rowblk(TOP_K),
                   pl.BlockSpec((1, 1, N_EXPERTS), lambda i: (i, 0, 0))],
        out_shape=[jax.ShapeDtypeStruct((t, D_MODEL), F32),
                   jax.ShapeDtypeStruct((t, D_MODEL // 2), jnp.uint32),
                   jax.ShapeDtypeStruct((t, TOP_K), I32),
                   jax.ShapeDtypeStruct((t, TOP_K), F32),
                   jax.ShapeDtypeStruct((t, TOP_K), I32),
                   jax.ShapeDtypeStruct((nt, 1, N_EXPERTS), I32)],
        compiler_params=_cparams(("parallel",)),
        name="merge_ln1_router",
    )(xp, xs, oap, oas, obp, obs, p3, p3, wa, wb, wo, g1, b1, wr0, wr1, br)


def _pack_bf16_pairs(x):
    w = x.shape[1] // 2
    lo = lax.bitcast_convert_type(x[:, :w].astype(BF16).astype(F32), jnp.uint32)
    hi = lax.bitcast_convert_type(x[:, w:].astype(BF16).astype(F32), jnp.uint32)
    return (hi & jnp.uint32(0xFFFF0000)) | (lo >> 16)


def _unpack_pairs_f32(u):
    lo = lax.bitcast_convert_type(u << 16, F32)
    hi = lax.bitcast_convert_type(u & jnp.uint32(0xFFFF0000), F32)
    return lo, hi


def _unpack_bf16_pairs(u):
    lo, hi = _unpack_pairs_f32(u)
    return lo.astype(BF16), hi.astype(BF16)


def _gather_kernel(tok_ref, h_ref, o_ref):
    m = o_ref.shape[0]

    def body(q, carry):
        for u in range(SUBLANES):
            r = q * SUBLANES + u
            o_ref[pl.ds(r, 1), :] = h_ref[pl.ds(tok_ref[0, 0, r], 1), :]
        return carry

    lax.fori_loop(0, m // SUBLANES, body, 0)


def _moe_gather(hp, tok_pad):
    t, w = hp.shape
    p = tok_pad.shape[0]
    m = MOE_BLOCK
    return pl.pallas_call(
        _gather_kernel,
        grid=(p // m,),
        in_specs=[pl.BlockSpec((1, 1, m), lambda i: (i, 0, 0), memory_space=pltpu.SMEM),
                  pl.BlockSpec((t, w), lambda i: (0, 0), pipeline_mode=pl.Buffered(1))],
        out_specs=pl.BlockSpec((m, w), lambda i: (i, 0)),
        out_shape=jax.ShapeDtypeStruct((p, w), hp.dtype),
        compiler_params=_cparams(("arbitrary",)),
        name="moe_gather",
    )(tok_pad.reshape(p // m, 1, m), hp)


def _moe_up_kernel(blk0_ref, nblk_ref, w0_ref, w1_ref, w2_ref, w3_ref, b_ref, xs_ref, act_ref,
                   wbf_scr, x_scr, o_scr, sem_in, sem_out):
    e = pl.program_id(0)
    f = pl.program_id(1)
    n = nblk_ref[e]
    b0 = blk0_ref[e]
    m = x_scr.shape[1]
    w_refs = (w0_ref, w1_ref, w2_ref, w3_ref)
    kq, ft = w0_ref.shape[1:]
    half = ft // 2

    def rows(b):
        return pl.ds(pl.multiple_of((b0 + b) * m, m), m)

    def x_slot(b):
        return jnp.where(b < X_RESIDENT, b, X_RESIDENT + (b & 1))

    def x_needed(b):
        return jnp.logical_or(f == 0, b >= X_RESIDENT)

    def x_copy(b):
        s = x_slot(b)
        return pltpu.make_async_copy(xs_ref.at[rows(b)], x_scr.at[s], sem_in.at[s])

    def o_copy(b, s):
        return pltpu.make_async_copy(o_scr.at[s], act_ref.at[f, rows(b)], sem_out.at[s])

    @pl.when(n > 0)
    def _():
        @pl.when(x_needed(0))
        def _():
            x_copy(0).start(priority=1)

        for q, w_ref in enumerate(w_refs):
            wbf_scr[q * kq:(q + 1) * kq, :] = w_ref[0].astype(BF16)
        even = (lax.broadcasted_iota(I32, (m, LANES), 1) & 1) == 0

        def body(b, carry):
            s = b & 1
            @pl.when(x_needed(b))
            def _():
                x_copy(b).wait()

            @pl.when(jnp.logical_and(b + 1 < n, x_needed(b + 1)))
            def _():
                x_copy(b + 1).start(priority=1)

            @pl.when(b >= 2)
            def _():
                o_copy(b - 2, s).wait()

            x_lo, x_hi = _unpack_bf16_pairs(x_scr[x_slot(b)])
            kh = x_lo.shape[1]
            for g in range(ft // (2 * LANES)):
                cols = slice(2 * g * LANES, (2 * g + 2) * LANES)
                h = (jnp.dot(x_lo, wbf_scr[:kh, cols], preferred_element_type=F32)
                     + jnp.dot(x_hi, wbf_scr[kh:, cols], preferred_element_type=F32)
                     + b_ref[0, :, cols])
                ha, hb = h[:, :LANES], h[:, LANES:]
                glu = jnp.minimum(jnp.where(even, ha, pltpu.roll(hb, 1, 1)), SWIGLU_LIMIT)
                lin = jnp.where(even, pltpu.roll(ha, LANES - 1, 1), hb)
                lin = jnp.clip(lin, -SWIGLU_LIMIT, SWIGLU_LIMIT) + 1.0
                act = glu * _sigmoid(SWIGLU_ALPHA * glu) * lin
                o_scr[s, :, g * LANES:(g + 1) * LANES] = act.astype(o_scr.dtype)
            o_copy(b, s).start(priority=1)
            return carry

        lax.fori_loop(0, n, body, 0)

        @pl.when(n >= 2)
        def _():
            o_copy(n - 2, n & 1).wait()
        o_copy(n - 1, (n - 1) & 1).wait()

    @pl.when(e == pl.num_programs(0) - 1)
    def _():
        o_scr[0] = jnp.zeros(o_scr.shape[1:], o_scr.dtype)

        nz = act_ref.shape[1] // m - b0
        lax.fori_loop(n, nz, lambda b, c: (o_copy(b, 0).start(), c)[1], 0)
        lax.fori_loop(n, nz, lambda b, c: (o_copy(b, 0).wait(), c)[1], 0)


def _moe_up(xs, w_gate_up, b_gate_up, blk0, nblk):
    p, w = xs.shape
    d = 2 * w
    m = MOE_BLOCK
    ft = MOE_FT
    ne = w_gate_up.shape[0]
    nf = w_gate_up.shape[2] // ft

    def wslice(q, e, f, b0, nb):
        return (e, q, f)

    grid_spec = pltpu.PrefetchScalarGridSpec(
        num_scalar_prefetch=2,
        grid=(ne, nf),
        in_specs=[pl.BlockSpec((1, d // W_STREAMS, ft), functools.partial(wslice, q))
                  for q in range(W_STREAMS)] + [
                  pl.BlockSpec((1, 1, ft), lambda e, f, b0, nb: (e, 0, f)),
                  pl.BlockSpec(memory_space=pl.ANY)],
        out_specs=pl.BlockSpec(memory_space=pl.ANY),
        scratch_shapes=[pltpu.VMEM((d, ft), BF16),
                        pltpu.VMEM((X_RESIDENT + 2, m, w), xs.dtype),
                        pltpu.VMEM((2, m, ft // 2), BF16),
                        pltpu.SemaphoreType.DMA((X_RESIDENT + 2,)),
                        pltpu.SemaphoreType.DMA((2,))],
    )
    return pl.pallas_call(
        _moe_up_kernel,
        grid_spec=grid_spec,
        out_shape=jax.ShapeDtypeStruct((nf, p, ft // 2), BF16),
        compiler_params=_cparams(("arbitrary", "arbitrary")),
        name="moe_up",
    )(blk0, nblk, *([w_gate_up] * W_STREAMS), b_gate_up, xs)


def _hidden_perm():
    r = lax.broadcasted_iota(I32, (LANES, LANES), 0)
    c = lax.broadcasted_iota(I32, (LANES, LANES), 1)
    src = (r >> 1) + (r & 1) * (LANES // 2)
    return jnp.where(c == src, 1.0, 0.0).astype(BF16)


def _moe_down_kernel(be_ref, na_ref, nreal_ref, dest_ref, a_ref, wd0_ref, wd1_ref, wd2_ref, wd3_ref,
                     bd_ref, yt_ref, wbf_scr, y_scr, sem):
    i = pl.program_id(0)
    nb = pl.num_programs(0)
    nf, m, fh = a_ref.shape
    slot = i % 2

    def row_copy(s, r, d):
        return pltpu.make_async_copy(y_scr.at[s, pl.ds(r, 1)], yt_ref.at[pl.ds(d, 1)], sem.at[s])

    def drain(s, count):
        @pl.when(count == m)
        def _():
            pltpu.make_async_copy(y_scr.at[s], yt_ref.at[pl.ds(0, m)], sem.at[s]).wait()

        @pl.when(count != m)
        def _():
            def wbody(_, carry):
                row_copy(s, 0, 0).wait()
                return carry
            lax.fori_loop(0, count, wbody, 0)

    @pl.when(i >= 2)
    def _():
        drain(slot, nreal_ref[i - 2])

    changed = jnp.where(i == 0, 1, be_ref[i] - be_ref[jnp.maximum(i - 1, 0)])

    @pl.when(changed != 0)
    def _():
        perm = _hidden_perm()
        for q, wd_ref in enumerate((wd0_ref, wd1_ref, wd2_ref, wd3_ref)):
            kq = wd_ref.shape[1]
            for g in range(kq // LANES):
                rows = slice(g * LANES, (g + 1) * LANES)
                wbf_scr[q * kq + g * LANES:q * kq + (g + 1) * LANES, :] = jnp.dot(
                    perm, wd_ref[0, rows, :].astype(BF16), preferred_element_type=F32).astype(BF16)

    @pl.when(i < na_ref[0])
    def _():
        y = bd_ref[0]
        for j in range(nf):
            y = y + jnp.dot(a_ref[j], wbf_scr[j * fh:(j + 1) * fh, :], preferred_element_type=F32)
        y_scr[slot] = _pack_bf16_pairs(y)

        n = nreal_ref[i]
        unroll = 8

        def sbody8(q, carry):
            for u in range(unroll):
                r = q * unroll + u
                row_copy(slot, r, dest_ref[0, 0, r]).start(priority=u % 2)
            return carry

        def sbody1(r, carry):
            row_copy(slot, r, dest_ref[0, 0, r]).start()
            return carry

        lax.fori_loop(0, n // unroll, sbody8, 0)
        lax.fori_loop((n // unroll) * unroll, n, sbody1, 0)

    @pl.when(i == nb - 1)
    def _():
        @pl.when(i >= 1)
        def _():
            drain(1 - slot, nreal_ref[i - 1])
        drain(slot, nreal_ref[i])


def _moe_down(act, w_down, b_down, blk_expert, n_active, nreal, dest, n_rows):
    nf, p, fh = act.shape
    f = nf * fh
    m = MOE_BLOCK
    d = w_down.shape[2]

    def wslice(q, i, be, na, nr):
        return (be[i], q, 0)

    grid_spec = pltpu.PrefetchScalarGridSpec(
        num_scalar_prefetch=3,
        grid=(p // m,),
        in_specs=[pl.BlockSpec((1, 1, m), lambda i, be, na, nr: (i, 0, 0), memory_space=pltpu.SMEM),
                  pl.BlockSpec((nf, m, fh), lambda i, be, na, nr: (0, i, 0)),
                  *[pl.BlockSpec((1, f // W_STREAMS, d), functools.partial(wslice, q))
                    for q in range(W_STREAMS)],
                  pl.BlockSpec((1, 1, d), lambda i, be, na, nr: (be[i], 0, 0))],
        out_specs=pl.BlockSpec(memory_space=pl.ANY),
        scratch_shapes=[pltpu.VMEM((f, d), BF16),
                        pltpu.VMEM((2, m, d // 2), jnp.uint32),
                        pltpu.SemaphoreType.DMA((2,))],
    )
    return pl.pallas_call(
        _moe_down_kernel,
        grid_spec=grid_spec,
        out_shape=jax.ShapeDtypeStruct((n_rows, d // 2), jnp.uint32),
        compiler_params=_cparams(("arbitrary",)),
        name="moe_down",
    )(blk_expert, n_active, nreal, dest.reshape(p // m, 1, m), act, *([w_down] * W_STREAMS), b_down)


def _combine_kernel(h_ref, y0_ref, y1_ref, y2_ref, y3_ref, gate_ref, g2_ref, b2_ref, o_ref):
    lo, hi = None, None
    for j, y_ref in enumerate((y0_ref, y1_ref, y2_ref, y3_ref)):
        yl, yh = _unpack_pairs_f32(y_ref[...])
        g = gate_ref[:, j:j + 1]
        lo = g * yl if lo is None else lo + g * yl
        hi = g * yh if hi is None else hi + g * yh
    ffn = jnp.concatenate([lo, hi], axis=1)
    o_ref[...] = _layer_norm(DEEPNORM_ALPHA * h_ref[...] + ffn, g2_ref[...], b2_ref[...])


def _combine(h, yt, gate, g2, b2):
    t, d = h.shape
    tm = TOKEN_TILE
    nt = t // tm

    def choice(j):
        return pl.BlockSpec((tm, d // 2), lambda i: (j * nt + i, 0))

    return pl.pallas_call(
        _combine_kernel,
        grid=(nt,),
        in_specs=[pl.BlockSpec((tm, d), lambda i: (i, 0)),
                  choice(0), choice(1), choice(2), choice(3),
                  pl.BlockSpec((tm, TOP_K), lambda i: (i, 0)),
                  pl.BlockSpec((1, d), lambda i: (0, 0)),
                  pl.BlockSpec((1, d), lambda i: (0, 0))],
        out_specs=pl.BlockSpec((tm, d), lambda i: (i, 0)),
        out_shape=jax.ShapeDtypeStruct((t, d), F32),
        compiler_params=_cparams(("parallel",)),
        name="combine_ln2",
    )(h, yt, yt, yt, yt, gate, g2, b2)


def _route(top_e, lrank, tilecnt):
    t = top_e.shape[0]
    m = MOE_BLOCK
    n_assign = t * TOP_K
    n_blocks = n_assign // m + N_EXPERTS
    p = n_blocks * m
    tc = tilecnt.reshape(-1, N_EXPERTS)
    base = jnp.cumsum(tc, axis=0) - tc
    counts = jnp.sum(tc, axis=0)
    padded = (counts + m - 1) // m * m
    pend = jnp.cumsum(padded)
    pstart = pend - padded
    base_tok = jnp.repeat(base, TOKEN_TILE, axis=0)
    rank = lrank + jnp.take_along_axis(base_tok, top_e, axis=1)
    slot = (pstart[top_e] + rank).reshape(-1)
    a_idx = jnp.arange(n_assign, dtype=I32)
    dest = jnp.full((p,), -1, I32).at[slot].set((a_idx % TOP_K) * t + a_idx // TOP_K,
                                               unique_indices=True)
    tok_pad = jnp.where(dest >= 0, dest % t, 0)
    blk_start = jnp.arange(n_blocks, dtype=I32) * m
    blk_expert = jnp.minimum(jnp.sum((pend[None, :] <= blk_start[:, None]).astype(I32), axis=1),
                             N_EXPERTS - 1)
    n_active = (pend[-1:] // m).astype(I32)
    real_end = (pstart + counts)[blk_expert]
    nreal = jnp.where(blk_start < pend[-1], jnp.clip(real_end - blk_start, 0, m), 0).astype(I32)
    return tok_pad, dest, blk_expert, n_active, nreal, (pstart // m).astype(I32), (padded // m).astype(I32)


def kernel(x_prompt, x_sample, cache_attn_k, cache_attn_v, state_hgrn, w_in, hgrn_lb_logits,
           hgrn_norm_w, w_branch_a, w_branch_b, w_out, ln1_g, ln1_b, w_router, b_router,
           w_gate_up, b_gate_up, w_down, b_down, ln2_g, ln2_b):
    assert w_in.shape[0] == DEPTH == 1
    bp, lp, d = x_prompt.shape
    bs, ls, _ = x_sample.shape
    tp, ts = bp * lp, bs * ls
    t = tp + ts
    keep = min(max(w for w, _ in DILATION_PAIRS), lp)

    xp2, xs2 = x_prompt.reshape(tp, d), x_sample.reshape(ts, d)
    xb = jnp.concatenate([xp2.astype(BF16), xs2.astype(BF16)], axis=0)
    wi = w_in[0]
    c1, c2 = P1_WIDTH, P1_WIDTH + P2_WIDTH
    p1 = _matmul(xb, wi[:, :c1].astype(BF16), 768, 1024, F32, "inproj_hgrn")
    p2 = _matmul(xb, wi[:, c1:c2].astype(BF16), 768, 1280, F32, "inproj_attn")
    p3 = _matmul(xb, wi[:, c2:].astype(BF16), 768, 1024, F32, "inproj_gate")

    oa_p, st_p = _hgrn(p1, hgrn_lb_logits, hgrn_norm_w[0], None, bp, lp, 0)
    oa_s, st_s = _hgrn(p1, hgrn_lb_logits, hgrn_norm_w[0],
                       state_hgrn.reshape(bs, HG_HEADS, HG_DIM, HG_DIM), bs, ls, tp)

    n = jnp.arange(1, ATT_Q_HEADS + 1, dtype=F32)
    slopes = jnp.power(2.0, -8.0 * n / ATT_Q_HEADS).reshape(N_GROUPS, ATT_KV_HEADS)
    ctx = cache_attn_k.shape[2]
    ob_p = _attn_prompt(p2, slopes, bp, lp)
    ob_s = _attn_sample(p2, cache_attn_k.reshape(bs, ctx * ATT_KV_HEADS, ATT_HEAD_DIM),
                        cache_attn_v.reshape(bs, ctx * ATT_KV_HEADS, ATT_HEAD_DIM), slopes, bs, ls, tp)

    wr = w_router[0]
    wr0 = wr.astype(BF16)
    wr1 = (wr - wr0.astype(F32)).astype(BF16)
    h, hp, top_e, gate, lrank, tilecnt = _merge(
        xp2, xs2, oa_p.reshape(tp, HG_WIDTH), oa_s.reshape(ts, HG_WIDTH), ob_p,
        ob_s.reshape(ts, ATT_KV_WIDTH), p3, w_branch_a[0].astype(BF16), w_branch_b[0].astype(BF16),
        w_out[0].astype(BF16), ln1_g, ln1_b, wr0, wr1, b_router)

    tok_pad, dest, blk_expert, n_active, nreal, blk0, nblk = _route(top_e, lrank, tilecnt)
    xs = _moe_gather(hp, tok_pad)
    act = _moe_up(xs, w_gate_up.reshape(N_EXPERTS, d, 2 * D_FF),
                  b_gate_up.reshape(N_EXPERTS, 1, 2 * D_FF), blk0, nblk)
    yt = _moe_down(act, w_down.reshape(N_EXPERTS, D_FF, d), b_down.reshape(N_EXPERTS, 1, d),
                   blk_expert, n_active, nreal, dest, t * TOP_K)
    y = _combine(h, yt, gate, ln2_g, ln2_b)

    k_new = p2[:, ATT_Q_HEADS * ATT_HEAD_DIM:(ATT_Q_HEADS + ATT_KV_HEADS) * ATT_HEAD_DIM]
    v_new = p2[:, (ATT_Q_HEADS + ATT_KV_HEADS) * ATT_HEAD_DIM:]

    def kv_prompt(a):
        return a[:tp].reshape(bp, lp, ATT_KV_HEADS, ATT_HEAD_DIM)[:, lp - keep:][None]

    def kv_sample(a):
        return a[tp:].reshape(bs, ls, ATT_KV_HEADS, ATT_HEAD_DIM)[None]

    return (y[:tp].reshape(bp, lp, d), y[tp:].reshape(bs, ls, d),
            kv_prompt(k_new), kv_prompt(v_new), st_p[None],
            kv_sample(k_new), kv_sample(v_new), st_s[None])
```
